```python
import jax
import jax.numpy as jnp
from jax import lax
import numpy as np

D_MODEL = 1024
BATCH = 4
SEQ = 8192
DEPTH = 1

GRID_W = 64
CTX_LEN = 256

RW_HEAD_DIM = 64
RW_WIDTH = D_MODEL // 2
RW_HEADS = RW_WIDTH // RW_HEAD_DIM
RW_DECAY_RANK = 32
RW_AAA_RANK = 32
RW_GATE_RANK = 96
RW_GN_EPS = 64e-5
RW_SPLITS = (RW_WIDTH, RW_WIDTH, RW_WIDTH, RW_DECAY_RANK, RW_AAA_RANK, RW_GATE_RANK)
RW_COLS = 3 * RW_WIDTH + RW_DECAY_RANK + RW_AAA_RANK + RW_GATE_RANK

GLA_HEADS = 4
GLA_KEY_WIDTH = D_MODEL // 4
GLA_VAL_WIDTH = D_MODEL // 2
GLA_KEY_DIM = GLA_KEY_WIDTH // GLA_HEADS
GLA_VAL_DIM = GLA_VAL_WIDTH // GLA_HEADS
GLA_GATE_RANK = 16
GLA_TAU = 16.0
GLA_CHUNK = 64
CONV_SIZE = 3
GLA_QKV_COLS = 2 * GLA_KEY_WIDTH + GLA_VAL_WIDTH
GLA_COLS = GLA_QKV_COLS + GLA_GATE_RANK + GLA_VAL_WIDTH

MIX_COLS = RW_COLS + GLA_COLS
IN_COLS = MIX_COLS + 2 * D_MODEL

N_EXPERTS = 256
TOP_K = 8
N_GROUPS = 8
TOPK_GROUPS = 4
EXPERT_DIM = D_MODEL // 4
SHARED_DIM = EXPERT_DIM
ROUTED_SCALE = 2.5
EXPERT_BLOCK = 128

LN_EPS = 1e-5
ALPHA = (2 * DEPTH) ** 0.25
BETA = (8 * DEPTH) ** -0.25

kernel_name = 'hybrid_rwkv7_gla_moe_flow_block'


def _split(t, sizes):
    idx, acc = [], 0
    for s in sizes[:-1]:
        acc += s
        idx.append(acc)
    return jnp.split(t, idx, axis=-1)


def _heads(t, n):
    return t.reshape(t.shape[:-1] + (n, t.shape[-1] // n))


def _layer_norm(x, w, b):
    xf = x.astype(jnp.float32)
    mu = jnp.mean(xf, -1, keepdims=True)
    var = jnp.mean(jnp.square(xf - mu), -1, keepdims=True)
    return ((xf - mu) * lax.rsqrt(var + LN_EPS)).astype(x.dtype) * w + b


def _head_norm(y, n_heads, eps):
    yh = _heads(y, n_heads).astype(jnp.float32)
    mu = jnp.mean(yh, -1, keepdims=True)
    var = jnp.mean(jnp.square(yh - mu), -1, keepdims=True)
    return ((yh - mu) * lax.rsqrt(var + eps)).reshape(y.shape)


def _qshift_grid(p, rows):
    B, L, C = p.shape
    g = p.reshape(B, rows, GRID_W, C // 4, 4)
    zc = jnp.zeros_like(g[:, :, :1, :, 0])
    zr = jnp.zeros_like(g[:, :1, :, :, 0])
    left = jnp.concatenate([zc, g[:, :, :-1, :, 0]], axis=2)
    right = jnp.concatenate([g[:, :, 1:, :, 1], zc], axis=2)
    up = jnp.concatenate([zr, g[:, :-1, :, :, 2]], axis=1)
    down = jnp.concatenate([g[:, 1:, :, :, 3], zr], axis=1)
    return jnp.stack([left, right, up, down], axis=-1).reshape(B, L, C)


def _qshift_seq(p):
    B, L, C = p.shape
    g = p.reshape(B, L, C // 4, 4)
    z = jnp.zeros_like(g[:, :1, :, 0])
    prev = lambda a: jnp.concatenate([z, a[:, :-1]], axis=1)
    nxt = lambda a: jnp.concatenate([a[:, 1:], z], axis=1)
    return jnp.stack([prev(g[..., 0]), nxt(g[..., 1]), prev(g[..., 2]), nxt(g[..., 3])], axis=-1).reshape(B, L, C)


def _dwconv(g, w):
    return lax.conv_general_dilated(g, w[:, :, None, :].astype(g.dtype), (1, 1), 'SAME',
                                    dimension_numbers=('NHWC', 'HWIO', 'NHWC'),
                                    feature_group_count=g.shape[-1])


def _stream_features(p_mix, shift_fn, conv_fn, rw_mu):
    p_rw, p_gla = p_mix[..., :RW_COLS], p_mix[..., RW_COLS:]
    p_rw = p_rw + rw_mu * (shift_fn(p_rw) - p_rw)
    r, k, v, pw, pa, pg = _split(p_rw, RW_SPLITS)
    qkv = jax.nn.silu(conv_fn(p_gla[..., :GLA_QKV_COLS]))
    q, kg, vg = _split(qkv, (GLA_KEY_WIDTH, GLA_KEY_WIDTH, GLA_VAL_WIDTH))
    pgl, og = _split(p_gla[..., GLA_QKV_COLS:], (GLA_GATE_RANK, GLA_VAL_WIDTH))
    return {'r': r, 'k': k, 'v': v, 'pw': pw, 'pa': pa, 'pg': pg,
            'q': q, 'kg': kg, 'vg': vg, 'pgl': pgl, 'og': og}


def _rwkv_dir(f, w0, w2, a0, a2, k_k, k_a):
    f32 = jnp.float32
    w_log = -jax.nn.softplus(-(w0 + jnp.tanh(f['pw']) @ w2)) - 0.5
    decay = jnp.exp(-jnp.exp(w_log.astype(f32)))
    a = jax.nn.sigmoid(a0 + f['pa'] @ a2)
    kk = _heads(f['k'] * k_k, RW_HEADS).astype(f32)
    kk = kk / jnp.maximum(jnp.sqrt(jnp.sum(kk * kk, -1, keepdims=True)), 1e-12)
    k_mod = _heads(f['k'] * (1.0 + (a - 1.0) * k_a), RW_HEADS).astype(f32)
    a_h = _heads(a, RW_HEADS).astype(f32)
    return _heads(decay, RW_HEADS), k_mod, -kk, kk * a_h


def _rwkv7_scan(state, r, w, k, v, a, b, emit):
    def step(S, inp):
        r_t, w_t, k_t, v_t, a_t, b_t = inp
        sa = jnp.einsum('bhvk,bhk->bhv', S, a_t)
        S = S * w_t[:, :, None, :] + sa[..., None] * b_t[:, :, None, :] + v_t[..., None] * k_t[:, :, None, :]
        return S, (jnp.einsum('bhvk,bhk->bhv', S, r_t) if emit else None)
    xs = tuple(jnp.moveaxis(t, 1, 0) for t in (r, w, k, v, a, b))
    state, ys = lax.scan(step, state, xs)
    return state, (jnp.moveaxis(ys, 0, 1) if emit else None)


def _gla_scan(state, q, k, v, log_a, emit):
    B, L, H, _ = q.shape
    n = L // GLA_CHUNK
    chunks = lambda t: jnp.moveaxis(t.reshape(B, n, GLA_CHUNK, H, t.shape[-1]), 1, 0)
    causal = jnp.tril(jnp.ones((GLA_CHUNK, GLA_CHUNK), dtype=bool))[None, :, :, None, None]

    def step(S, inp):
        q_c, k_c, v_c, g_c = inp
        b = jnp.cumsum(g_c, axis=1)
        b_last = b[:, -1]
        out = None
        if emit:
            inter = jnp.einsum('bihk,bhkv->bihv', q_c * jnp.exp(b), S)
            decay_ij = jnp.exp(jnp.where(causal, b[:, :, None] - b[:, None, :], -jnp.inf))
            att = jnp.einsum('bihk,bjhk,bijhk->bhij', q_c, k_c, decay_ij)
            out = inter + jnp.einsum('bhij,bjhv->bihv', att, v_c)
        S = S * jnp.exp(b_last)[..., None] + jnp.einsum('bjhk,bjhv->bhkv', k_c * jnp.exp(b_last[:, None] - b), v_c)
        return S, out

    state, o = lax.scan(step, state, tuple(chunks(t) for t in (q, k, v, log_a)))
    return state, (jnp.moveaxis(o, 0, 1).reshape(B, L, H, v.shape[-1]) if emit else None)


def _token_mix(h_lat, h_ctx, rows, need_ctx, lp):
    f32 = jnp.float32
    B, L, _ = h_lat.shape
    w_mix, w_gate = lp['w_in'][:, :MIX_COLS], lp['w_in'][:, MIX_COLS:]
    conv_w = lp['gla_conv']
    lat_conv = lambda t: _dwconv(t.reshape(B, rows, GRID_W, t.shape[-1]), conv_w).reshape(B, L, t.shape[-1])
    ctx_conv = lambda t: _dwconv(t[:, None], conv_w)[:, 0]
    f_lat = _stream_features(h_lat @ w_mix, lambda t: _qshift_grid(t, rows), lat_conv, lp['rw_mu'])
    f_ctx = _stream_features(h_ctx @ w_mix, _qshift_seq, ctx_conv, lp['rw_mu'])

    outs = {'lat': {'rw': [], 'bonus': [], 'gla': []}, 'ctx': {'rw': [], 'bonus': [], 'gla': []}}
    for d in range(2):
        flip = (lambda t: t[:, ::-1]) if d == 1 else (lambda t: t)
        s_rw = jnp.zeros((B, RW_HEADS, RW_HEAD_DIM, RW_HEAD_DIM), f32)
        s_gla = jnp.zeros((B, GLA_HEADS, GLA_KEY_DIM, GLA_VAL_DIM), f32)
        for name, f, emit in (('ctx', f_ctx, need_ctx), ('lat', f_lat, True)):
            decay, k_mod, a_vec, b_vec = _rwkv_dir(f, lp['rw_w0'][d], lp['rw_w2'][d], lp['rw_a0'][d],
                                                   lp['rw_a2'][d], lp['rw_k_k'], lp['rw_k_a'])
            r_h = _heads(f['r'], RW_HEADS).astype(f32)
            v_h = _heads(f['v'], RW_HEADS).astype(f32)
            s_rw, y_rw = _rwkv7_scan(s_rw, *[flip(t) for t in (r_h, decay, k_mod, v_h, a_vec, b_vec)], emit)
            q_h = _heads(f['q'], GLA_HEADS).astype(f32) * GLA_KEY_DIM ** -0.5
            k_h = _heads(f['kg'], GLA_HEADS).astype(f32)
            vg_h = _heads(f['vg'], GLA_HEADS).astype(f32)
            log_a = _heads(jax.nn.log_sigmoid((f['pgl'] @ lp['gla_g2'][d] + lp['gla_gb'][d]).astype(f32)) / GLA_TAU,
                           GLA_HEADS)
            s_gla, y_gla = _gla_scan(s_gla, *[flip(t) for t in (q_h, k_h, vg_h, log_a)], emit)
            if emit:
                outs[name]['rw'].append(flip(y_rw))
                outs[name]['bonus'].append(jnp.sum(r_h * k_mod * lp['rw_r_k'], -1, keepdims=True) * v_h)
                outs[name]['gla'].append(flip(y_gla))

    def merge(h, f, o):
        lead = h.shape[:-1]
        y_rw = _head_norm((o['rw'][0] + o['rw'][1]).reshape(lead + (RW_WIDTH,)), RW_HEADS, RW_GN_EPS)
        y_rw = y_rw * lp['rw_gn_w'] + lp['rw_gn_b'] + (o['bonus'][0] + o['bonus'][1]).reshape(lead + (RW_WIDTH,))
        y_rw = (y_rw * (jax.nn.sigmoid(f['pg']) @ lp['rw_g2'])).astype(h.dtype) @ lp['w_br_rw']
        y_gla = _head_norm((o['gla'][0] + o['gla'][1]).reshape(lead + (GLA_VAL_WIDTH,)), GLA_HEADS, LN_EPS)
        y_gla = ((y_gla * lp['gla_gn_w'] + lp['gla_gn_b']) * jax.nn.silu(f['og'])).astype(h.dtype) @ lp['w_br_gla']
        gate_rw, gate_gla = jnp.split(jax.nn.sigmoid(h @ w_gate), 2, axis=-1)
        return (gate_rw * y_rw + gate_gla * y_gla) @ lp['w_out']

    return merge(h_lat, f_lat, outs['lat']), (merge(h_ctx, f_ctx, outs['ctx']) if need_ctx else None)


def _swiglu(t, w_gu, w_d):
    a, b = jnp.split(t @ w_gu, 2, axis=-1)
    return (jax.nn.silu(a) * b) @ w_d


def _moe(h, router, router_bias, w_gate_up, w_down, sh_gate_up, sh_down):
    shape = h.shape
    t = h.reshape(-1, shape[-1])
    n = t.shape[0]
    f32 = jnp.float32
    per_group = N_EXPERTS // N_GROUPS
    scores = jax.nn.sigmoid((t @ router).astype(f32))
    sel = scores + router_bias.astype(f32)
    group_score = lax.top_k(sel.reshape(n, N_GROUPS, per_group), 2)[0].sum(-1)
    _, top_groups = lax.top_k(group_score, TOPK_GROUPS)
    group_keep = jnp.any(top_groups[:, :, None] == jnp.arange(N_GROUPS), axis=1)
    sel = jnp.where(jnp.repeat(group_keep, per_group, axis=1), sel, -jnp.inf)
    _, experts = lax.top_k(sel, TOP_K)
    gate = jnp.take_along_axis(scores, experts, axis=-1)
    gate = gate / jnp.sum(gate, -1, keepdims=True) * ROUTED_SCALE

    flat_e = experts.reshape(-1)
    nk = flat_e.shape[0]
    order = jnp.argsort(flat_e)
    sorted_e = flat_e[order]
    sizes = jnp.bincount(flat_e, length=N_EXPERTS)
    padded = (sizes + EXPERT_BLOCK - 1) // EXPERT_BLOCK * EXPERT_BLOCK
    start = jnp.cumsum(sizes) - sizes
    pad_end = jnp.cumsum(padded)
    pad_start = pad_end - padded
    dest = pad_start[sorted_e] + jnp.arange(nk) - start[sorted_e]
    n_blocks = (nk + N_EXPERTS * (EXPERT_BLOCK - 1) + EXPERT_BLOCK - 1) // EXPERT_BLOCK
    cap = n_blocks * EXPERT_BLOCK
    row_tok = jnp.full((cap,), n, jnp.int32).at[dest].set((order // TOP_K).astype(jnp.int32))
    row_gate = jnp.zeros((cap,), t.dtype).at[dest].set(gate.reshape(-1)[order].astype(t.dtype))
    block_e = jnp.minimum(jnp.searchsorted(pad_end, jnp.arange(n_blocks) * EXPERT_BLOCK, side='right'),
                          N_EXPERTS - 1)
    t_pad = jnp.concatenate([t, jnp.zeros((1, t.shape[1]), t.dtype)], axis=0)

    def expert_block(args):
        tok, e = args
        return _swiglu(t_pad[tok], w_gate_up[e], w_down[e])

    y = lax.map(expert_block, (row_tok.reshape(n_blocks, EXPERT_BLOCK), block_e)).reshape(cap, -1)
    routed = jax.ops.segment_sum(y * row_gate[:, None], row_tok, num_segments=n + 1)[:n]
    return (routed + _swiglu(t, sh_gate_up, sh_down)).reshape(shape)


def setup_inputs(seed: int = 0) -> dict:
    key = jax.random.key(seed)
    ks = iter(jax.random.split(key, 48))

    def nrm(shape, scale):
        return scale * jax.random.normal(next(ks), shape, jnp.float32)

    def uni(shape, lo, hi):
        return jax.random.uniform(next(ks), shape, jnp.float32, lo, hi)

    Ld, E, F = DEPTH, N_EXPERTS, EXPERT_DIM
    d_inv = D_MODEL ** -0.5
    gla_v0 = RW_COLS + 2 * GLA_KEY_WIDTH
    v_cols = jnp.zeros((IN_COLS,), bool).at[2 * RW_WIDTH:3 * RW_WIDTH].set(True).at[gla_v0:gla_v0 + GLA_VAL_WIDTH].set(True)
    col_scale = jnp.where(v_cols, BETA, 1.0)
    w0_ramp = -6.0 + 5.0 * jnp.linspace(0.0, 1.0, RW_WIDTH) ** 1.5
    return {
        'x': nrm((BATCH, SEQ, D_MODEL), 1.0),
        'c': nrm((BATCH, D_MODEL), 1.0),
        'ctx': nrm((BATCH, CTX_LEN, D_MODEL), 1.0),
        'c_ctx': nrm((D_MODEL,), 1.0),
        'w_ada': nrm((Ld, D_MODEL, 6 * D_MODEL), 0.5 * d_inv),
        'b_ada': nrm((Ld, 6 * D_MODEL), 0.02),
        'w_in': nrm((Ld, D_MODEL, IN_COLS), d_inv) * col_scale,
        'rw_mu': uni((Ld, RW_COLS), 0.0, 1.0),
        'rw_w0': w0_ramp + nrm((Ld, 2, RW_WIDTH), 0.3),
        'rw_w2': nrm((Ld, 2, RW_DECAY_RANK, RW_WIDTH), 0.5 * RW_DECAY_RANK ** -0.5),
        'rw_a0': nrm((Ld, 2, RW_WIDTH), 0.3),
        'rw_a2': nrm((Ld, 2, RW_AAA_RANK, RW_WIDTH), RW_AAA_RANK ** -0.5),
        'rw_g2': nrm((Ld, RW_GATE_RANK, RW_WIDTH), RW_GATE_RANK ** -0.5),
        'rw_k_k': 0.85 + nrm((Ld, RW_WIDTH), 0.05),
        'rw_k_a': 1.0 + nrm((Ld, RW_WIDTH), 0.05),
        'rw_r_k': nrm((Ld, RW_HEADS, RW_HEAD_DIM), 0.1),
        'rw_gn_w': 1.0 + nrm((Ld, RW_WIDTH), 0.02),
        'rw_gn_b': nrm((Ld, RW_WIDTH), 0.02),
        'gla_conv': nrm((Ld, CONV_SIZE, CONV_SIZE, GLA_QKV_COLS), 1.0 / CONV_SIZE),
        'gla_g2': nrm((Ld, 2, GLA_GATE_RANK, GLA_KEY_WIDTH), GLA_GATE_RANK ** -0.5),
        'gla_gb': 1.0 + nrm((Ld, 2, GLA_KEY_WIDTH), 1.0),
        'gla_gn_w': 1.0 + nrm((Ld, GLA_VAL_WIDTH), 0.02),
        'gla_gn_b': nrm((Ld, GLA_VAL_WIDTH), 0.02),
        'w_br_rw': nrm((Ld, RW_WIDTH, D_MODEL), RW_WIDTH ** -0.5),
        'w_br_gla': nrm((Ld, GLA_VAL_WIDTH, D_MODEL), GLA_VAL_WIDTH ** -0.5),
        'w_out': nrm((Ld, D_MODEL, D_MODEL), BETA * d_inv),
        'ln1_w': 1.0 + nrm((Ld, D_MODEL), 0.02),
        'ln1_b': nrm((Ld, D_MODEL), 0.02),
        'router': nrm((Ld, D_MODEL, E), d_inv),
        'router_bias': nrm((Ld, E), 0.01),
        'w_gate_up': nrm((Ld, E, D_MODEL, 2 * F), d_inv),
        'w_down': nrm((Ld, E, F, D_MODEL), BETA * F ** -0.5),
        'sh_gate_up': nrm((Ld, D_MODEL, 2 * SHARED_DIM), d_inv),
        'sh_down': nrm((Ld, SHARED_DIM, D_MODEL), BETA * SHARED_DIM ** -0.5),
        'ln2_w': 1.0 + nrm((Ld, D_MODEL), 0.02),
        'ln2_b': nrm((Ld, D_MODEL), 0.02),
    }


def reference(x, c, ctx, c_ctx, w_ada, b_ada, w_in, rw_mu, rw_w0, rw_w2, rw_a0, rw_a2, rw_g2,
              rw_k_k, rw_k_a, rw_r_k, rw_gn_w, rw_gn_b, gla_conv, gla_g2, gla_gb, gla_gn_w, gla_gn_b,
              w_br_rw, w_br_gla, w_out, ln1_w, ln1_b, router, router_bias, w_gate_up, w_down,
              sh_gate_up, sh_down, ln2_w, ln2_b):
    rows = x.shape[1] // GRID_W
    ctx_s = ctx
    for l in range(DEPTH):
        need_ctx = l < DEPTH - 1
        sh1, sc1, g1, sh2, sc2, g2 = [m[:, None, :] for m in
                                      _split(jax.nn.silu(c) @ w_ada[l] + b_ada[l], (D_MODEL,) * 6)]
        csh1, csc1, cg1, csh2, csc2, cg2 = _split(jax.nn.silu(c_ctx) @ w_ada[l] + b_ada[l], (D_MODEL,) * 6)
        lp = {'w_in': w_in[l], 'rw_mu': rw_mu[l], 'rw_w0': rw_w0[l], 'rw_w2': rw_w2[l], 'rw_a0': rw_a0[l],
              'rw_a2': rw_a2[l], 'rw_g2': rw_g2[l], 'rw_k_k': rw_k_k[l], 'rw_k_a': rw_k_a[l],
              'rw_r_k': rw_r_k[l], 'rw_gn_w': rw_gn_w[l], 'rw_gn_b': rw_gn_b[l], 'gla_conv': gla_conv[l],
              'gla_g2': gla_g2[l], 'gla_gb': gla_gb[l], 'gla_gn_w': gla_gn_w[l], 'gla_gn_b': gla_gn_b[l],
              'w_br_rw': w_br_rw[l], 'w_br_gla': w_br_gla[l], 'w_out': w_out[l]}
        moe_p = (router[l], router_bias[l], w_gate_up[l], w_down[l], sh_gate_up[l], sh_down[l])

        h = x * (1.0 + sc1) + sh1
        hc = ctx_s * (1.0 + csc1) + csh1
        mix, mix_c = _token_mix(h, hc, rows, need_ctx, lp)
        x = _layer_norm(ALPHA * x + g1 * mix, ln1_w[l], ln1_b[l])

        h = x * (1.0 + sc2) + sh2
        x = _layer_norm(ALPHA * x + g2 * _moe(h, *moe_p), ln2_w[l], ln2_b[l])

        if need_ctx:
            ctx_s = _layer_norm(ALPHA * ctx_s + cg1 * mix_c, ln1_w[l], ln1_b[l])
            hc = ctx_s * (1.0 + csc2) + csh2
            ctx_s = _layer_norm(ALPHA * ctx_s + cg2 * _moe(hc, *moe_p), ln2_w[l], ln2_b[l])
    return x
```

```python
import functools
import math

import jax
import jax.numpy as jnp
from jax import lax
from jax.experimental import pallas as pl
from jax.experimental.pallas import tpu as pltpu

F32 = jnp.float32
BF16 = jnp.bfloat16

D_MODEL = 1024
GRID_W = 64
RW_WIDTH = 512
RW_HEADS = 8
RW_HEAD_DIM = 64
RW_DECAY_RANK = 32
RW_AAA_RANK = 32
RW_GATE_RANK = 96
RW_GN_EPS = 64e-5
RW_COLS = 1696
RW_SEG = 1792
GLA_HEADS = 4
GLA_KEY_WIDTH = 256
GLA_VAL_WIDTH = 512
GLA_KEY_DIM = 64
GLA_VAL_DIM = 128
GLA_GATE_RANK = 16
GLA_TAU = 16.0
GLA_QKV_COLS = 1024
GLA_COLS = 1552
GLA_SEG = 1664
MIX_COLS = RW_COLS + GLA_COLS
N_EXPERTS = 256
TOP_K = 8
N_GROUPS = 8
TOPK_GROUPS = 4
EXPERT_DIM = 256
ROUTED_SCALE = 2.5
LN_EPS = 1e-5
DEPTH = 1
ALPHA = (2 * DEPTH) ** 0.25

CHUNK = 64
ROW_BLOCK = 256
VMEM_LIMIT = 48 * 1024 * 1024

N_ADA = 3
N_PROJ = 2
N_LORA = 2
N_SEG = 3
N_CUM = 3
N_RW = 2
N_GLA = 2
N_GLA_INTRA = 2
N_MERGE = 2
N_ROUTER = 3
N_EXPERT = 1
N_SHARED = 1


def _split(x, n):
    parts = []
    r = x
    for i in range(n):
        p = r.astype(BF16)
        parts.append(p)
        if i < n - 1:
            r = r - p.astype(F32)
    return parts


def _dot(a, b, ta=False, tb=False):
    dn = (((0 if ta else 1,), (1 if tb else 0,)), ((), ()))
    return lax.dot_general(a, b, dn, preferred_element_type=F32)


def _mmp(ap, bp, ta=False, tb=False):
    n = max(len(ap), len(bp))
    out = None
    for i in range(len(ap)):
        for j in range(len(bp)):
            if i + j <= n - 1:
                t = _dot(ap[i], bp[j], ta, tb)
                out = t if out is None else out + t
    return out


def _mm(a, b, n, ta=False, tb=False):
    return _mmp(_split(a, n), _split(b, n), ta, tb)


def _mm_exact_l(m_bf16, x, n):
    return _mmp([m_bf16], _split(x, n))


def _mm_exact_r(x, m_bf16, n):
    return _mmp(_split(x, n), [m_bf16])


def _sigmoid(x):
    return 1.0 / (1.0 + jnp.exp(-x))


def _silu(x):
    return x * _sigmoid(x)


def _log_sigmoid(x):
    return jnp.minimum(x, 0.0) - jnp.log(1.0 + jnp.exp(-jnp.abs(x)))


def _cparams(sem):
    return pltpu.CompilerParams(dimension_semantics=sem, vmem_limit_bytes=VMEM_LIMIT)


def _full(shape):
    nd = len(shape)
    return pl.BlockSpec(shape, lambda *a: (0,) * nd)


def _ada_body(c_ref, w_ref, b_ref, o_ref):
    s = _silu(c_ref[...])
    o_ref[...] = _mm(s, w_ref[...], N_ADA) + b_ref[...]


def _ada(cc, w, b):
    rows, d = cc.shape
    n = w.shape[1] // d
    return pl.pallas_call(
        _ada_body,
        grid=(n,),
        in_specs=[pl.BlockSpec((rows, d), lambda j: (0, 0)),
                  pl.BlockSpec((d, d), lambda j: (0, j)),
                  pl.BlockSpec((1, d), lambda j: (0, j))],
        out_specs=pl.BlockSpec((rows, d), lambda j: (0, j)),
        out_shape=jax.ShapeDtypeStruct((rows, w.shape[1]), F32),
        compiler_params=_cparams(("arbitrary",)),
        name="ada",
    )(cc, w, b.reshape(1, -1))


def _proj_body(act, x_ref, mod_ref, *refs):
    w_refs, o_ref = refs[:-1], refs[-1]
    sh = mod_ref[0, 0:1, :]
    sc = mod_ref[0, 1:2, :]
    h = x_ref[0] * (1.0 + sc) + sh
    y = _mmp(_split(h, N_PROJ), [w[...] for w in w_refs])
    if act:
        y = _sigmoid(y)
    o_ref[0] = y


def _proj(x, mod, w, act, tm):
    B, L, D = x.shape
    n = w.shape[1]
    wp = _split(w, N_PROJ)
    return pl.pallas_call(
        functools.partial(_proj_body, act),
        grid=(B, L // tm),
        in_specs=[pl.BlockSpec((1, tm, D), lambda b, i: (b, i, 0)),
                  pl.BlockSpec((1, 6, D), lambda b, i: (b, 0, 0))]
                 + [pl.BlockSpec((D, n), lambda b, i: (0, 0))] * N_PROJ,
        out_specs=pl.BlockSpec((1, tm, n), lambda b, i: (b, i, 0)),
        out_shape=jax.ShapeDtypeStruct((B, L, n), F32),
        compiler_params=_cparams(("parallel", "arbitrary")),
        name="proj",
    )(x, mod, *wp)


def _neighbours(vertical, width, tm, cur, prev_ref, next_ref):
    if vertical:
        i = pl.program_id(1)
        n = pl.num_programs(1)
        prev = jnp.where(i > 0, prev_ref[0], 0.0)
        nxt = jnp.where(i < n - 1, next_ref[0], 0.0)
        pad = jnp.zeros((8, cur.shape[1]), F32)
        ext = jnp.concatenate([pad, prev, cur, nxt, pad], axis=0)
        off = GRID_W + 8
    else:
        pad = jnp.zeros((8, cur.shape[1]), F32)
        ext = jnp.concatenate([pad, cur, pad], axis=0)
        off = 8
    col = lax.broadcasted_iota(jnp.int32, (tm, 1), 0) % width

    def get(dr, dc):
        s = off + GRID_W * dr + dc
        v = ext[s:s + tm]
        if dc == -1:
            v = jnp.where(col == 0, 0.0, v)
        elif dc == 1:
            v = jnp.where(col == width - 1, 0.0, v)
        return v

    return get


def _feat_rw_body(vertical, width, tm, *refs):
    if vertical:
        prev_ref, cur_ref, next_ref = refs[:3]
        refs = refs[3:]
    else:
        cur_ref = refs[0]
        prev_ref = next_ref = None
        refs = refs[1:]
    (mu_ref, lw_hi_ref, lw_lo_ref, w0_ref, a0_ref, kk_ref, ka_ref, rk_ref, bd_ref,
     r_out, k_out, v_out, kkn_out, g_out, bonus_out, lwd_out, a_out) = refs
    cur = cur_ref[0]
    get = _neighbours(vertical, width, tm, cur, prev_ref, next_ref)
    left, right = get(0, -1), get(0, 1)
    up, down = (get(-1, 0), get(1, 0)) if vertical else (left, right)
    l4 = lax.broadcasted_iota(jnp.int32, (1, cur.shape[1]), 1) % 4
    shifted = jnp.where(l4 == 0, left, jnp.where(l4 == 1, right, jnp.where(l4 == 2, up, down)))
    p = cur + mu_ref[...] * (shifted - cur)

    r = p[:, 0:RW_WIDTH]
    k = p[:, RW_WIDTH:2 * RW_WIDTH]
    v = p[:, 2 * RW_WIDTH:3 * RW_WIDTH]
    slab = p[:, 3 * RW_WIDTH:3 * RW_WIDTH + 256]
    ln = lax.broadcasted_iota(jnp.int32, (1, 256), 1)
    e1 = RW_DECAY_RANK
    e2 = e1 + RW_AAA_RANK
    e3 = e2 + RW_GATE_RANK
    slab = jnp.where(ln < e1, jnp.tanh(slab),
                     jnp.where(ln < e2, slab, jnp.where(ln < e3, _sigmoid(slab), 0.0)))
    lo = _mmp(_split(slab, N_LORA), [lw_hi_ref[...], lw_lo_ref[...]][:N_LORA])
    W = RW_WIDTH
    a_sum = None
    for d in range(2):
        z = w0_ref[:, d * W:(d + 1) * W] + lo[:, d * W:(d + 1) * W]
        lwd_out[d, 0] = -_sigmoid(z) * math.exp(-0.5)
        a = _sigmoid(a0_ref[:, d * W:(d + 1) * W] + lo[:, (2 + d) * W:(3 + d) * W])
        a_out[d, 0] = a
        a_sum = a if a_sum is None else a_sum + a
    g_out[0] = lo[:, 4 * W:5 * W]
    bd = bd_ref[...]
    kk = k * kk_ref[...]
    ss = _mm_exact_r(kk * kk, bd, N_SEG)
    kkn_out[0] = kk / jnp.maximum(jnp.sqrt(ss), 1e-12)
    kmod_sum = k * (2.0 + (a_sum - 2.0) * ka_ref[...])
    bonus_out[0] = _mm_exact_r(r * kmod_sum * rk_ref[...], bd, N_SEG) * v
    r_out[0] = r
    k_out[0] = k
    v_out[0] = v


def _feat_rw(p, vertical, tm, consts):
    B, L, S = p.shape
    width = GRID_W if vertical else tm
    W = RW_WIDTH
    hb = tm // GRID_W
    nhb = L // GRID_W
    tile = pl.BlockSpec((1, tm, S), lambda b, i: (b, i, 0))
    if vertical:
        in_specs = [pl.BlockSpec((1, GRID_W, S), lambda b, i: (b, jnp.maximum(i * hb - 1, 0), 0)),
                    tile,
                    pl.BlockSpec((1, GRID_W, S), lambda b, i: (b, jnp.minimum((i + 1) * hb, nhb - 1), 0))]
        args = [p, p, p]
    else:
        in_specs = [tile]
        args = [p]
    in_specs += [_full(c.shape) for c in consts]
    o1 = pl.BlockSpec((1, tm, W), lambda b, i: (b, i, 0))
    o2 = pl.BlockSpec((2, 1, tm, W), lambda b, i: (0, b, i, 0))
    s1 = jax.ShapeDtypeStruct((B, L, W), F32)
    s2 = jax.ShapeDtypeStruct((2, B, L, W), F32)
    return pl.pallas_call(
        functools.partial(_feat_rw_body, vertical, width, tm),
        grid=(B, L // tm),
        in_specs=in_specs,
        out_specs=[o1] * 6 + [o2] * 2,
        out_shape=[s1] * 6 + [s2] * 2,
        compiler_params=_cparams(("parallel", "arbitrary")),
        name="feat_rw",
    )(*args, *consts)


def _feat_gla_body(vertical, width, tm, *refs):
    if vertical:
        prev_ref, cur_ref, next_ref = refs[:3]
        refs = refs[3:]
    else:
        cur_ref = refs[0]
        prev_ref = next_ref = None
        refs = refs[1:]
    cw_ref, g2_hi_ref, g2_lo_ref, gb_ref, qk_out, v_out, la_out = refs
    cur = cur_ref[0][:, 0:GLA_QKV_COLS]
    if vertical:
        class _Slice:
            def __init__(self, ref):
                self.ref = ref

            def __getitem__(self, idx):
                return self.ref[idx][:, 0:GLA_QKV_COLS]
        get = _neighbours(True, width, tm, cur, _Slice(prev_ref), _Slice(next_ref))
    else:
        get = _neighbours(False, width, tm, cur, None, None)
    acc = None
    for dr in ((-1, 0, 1) if vertical else (0,)):
        for dc in (-1, 0, 1):
            t = get(dr, dc) * cw_ref[(dr + 1) * 3 + (dc + 1):(dr + 1) * 3 + (dc + 1) + 1, :]
            acc = t if acc is None else acc + t
    qkv = _silu(acc)
    kw = GLA_KEY_WIDTH
    lane = lax.broadcasted_iota(jnp.int32, (1, 2 * kw), 1)
    qk_out[0] = qkv[:, 0:2 * kw] * jnp.where(lane < kw, GLA_KEY_DIM ** -0.5, 1.0)
    v_out[0] = qkv[:, 2 * kw:]
    pgl = cur_ref[0][:, GLA_QKV_COLS + GLA_VAL_WIDTH:GLA_SEG]
    z = _mmp(_split(pgl, N_LORA), [g2_hi_ref[...], g2_lo_ref[...]][:N_LORA]) + gb_ref[...]
    la = _log_sigmoid(z) * (1.0 / GLA_TAU)
    la_out[0, 0] = la[:, 0:kw]
    la_out[1, 0] = la[:, kw:]


def _feat_gla(p, vertical, tm, consts):
    B, L, S = p.shape
    width = GRID_W if vertical else tm
    hb = tm // GRID_W
    nhb = L // GRID_W
    tile = pl.BlockSpec((1, tm, S), lambda b, i: (b, i, 0))
    if vertical:
        in_specs = [pl.BlockSpec((1, GRID_W, S), lambda b, i: (b, jnp.maximum(i * hb - 1, 0), 0)),
                    tile,
                    pl.BlockSpec((1, GRID_W, S), lambda b, i: (b, jnp.minimum((i + 1) * hb, nhb - 1), 0))]
        args = [p, p, p]
    else:
        in_specs = [tile]
        args = [p]
    in_specs += [_full(c.shape) for c in consts]
    kw, vw = GLA_KEY_WIDTH, GLA_VAL_WIDTH
    return pl.pallas_call(
        functools.partial(_feat_gla_body, vertical, width, tm),
        grid=(B, L // tm),
        in_specs=in_specs,
        out_specs=[pl.BlockSpec((1, tm, 2 * kw), lambda b, i: (b, i, 0)),
                   pl.BlockSpec((1, tm, vw), lambda b, i: (b, i, 0)),
                   pl.BlockSpec((2, 1, tm, kw), lambda b, i: (0, b, i, 0))],
        out_shape=[jax.ShapeDtypeStruct((B, L, 2 * kw), F32),
                   jax.ShapeDtypeStruct((B, L, vw), F32),
                   jax.ShapeDtypeStruct((2, B, L, kw), F32)],
        compiler_params=_cparams(("parallel", "arbitrary")),
        name="feat_gla",
    )(*args, *consts)


def _order_masks(n, sgn):
    ti = lax.broadcasted_iota(jnp.int32, (n, n), 0)
    si = lax.broadcasted_iota(jnp.int32, (n, n), 1)
    rel = (si - ti) * sgn
    return rel < 0, rel <= 0


def _chunk_index(nc):
    return lambda b, d, c: c + d * (nc - 1 - 2 * c)


def _rwkv_body(r_ref, k_ref, v_ref, kk_ref, lw_ref, a_ref, ka_ref, s0_ref, y_ref, st_ref, s_scr):
    C = CHUNK
    d = pl.program_id(1)
    c = pl.program_id(2)

    @pl.when(c == 0)
    def _():
        s_scr[...] = s0_ref[0, 0]

    sgn = 1 - 2 * d
    r, k, v, kk = r_ref[0], k_ref[0], v_ref[0], kk_ref[0]
    lw, a = lw_ref[0, 0], a_ref[0, 0]
    before, upto = _order_masks(C, sgn)
    cum = _mm_exact_l(upto.astype(BF16), lw, N_CUM)
    tot = jnp.where(d == 0, cum[C - 1:C], cum[0:1])
    p_in = jnp.exp(cum)
    p_ex = jnp.exp(cum - lw)
    p_inv = jnp.exp(-cum)
    p_rem = jnp.exp(tot - cum)
    p_all = jnp.exp(tot)
    bvec = kk * a
    kmod = k * (1.0 + (a - 1.0) * ka_ref[...])
    a_t = -kk * p_ex
    r_t = r * p_in
    b_t = bvec * p_inv
    k_t = kmod * p_inv
    b_p = bvec * p_rem
    k_p = kmod * p_rem

    P = 2 * RW_HEAD_DIM
    lane = lax.broadcasted_iota(jnp.int32, (C, P), 1)
    h0 = lane < RW_HEAD_DIM
    ri = lax.broadcasted_iota(jnp.int32, (P, P), 0)
    ci = lax.broadcasted_iota(jnp.int32, (P, P), 1)
    same = (ri // C) == (ci // C)
    rel = ((ci % C) - (ri % C)) * sgn
    strict = same & (rel < 0)
    incl = same & (rel <= 0)
    eye = (ri == ci).astype(F32)

    def stack2(x):
        return jnp.concatenate([jnp.where(h0, x, 0.0), jnp.where(h0, 0.0, x)], axis=0)

    for p in range(RW_HEADS // 2):
        sl = slice(p * P, (p + 1) * P)
        a_st, r_st = stack2(a_t[:, sl]), stack2(r_t[:, sl])
        b_st, k_st = stack2(b_t[:, sl]), stack2(k_t[:, sl])
        v_st = stack2(v[:, sl])
        g = _mm(jnp.concatenate([a_st, r_st], axis=0), jnp.concatenate([b_st, k_st], axis=0), N_RW, tb=True)
        nmat = jnp.where(strict, g[0:P, 0:P], 0.0)
        a_ak = jnp.where(strict, g[0:P, P:], 0.0)
        a_rb = jnp.where(incl, g[P:, 0:P], 0.0)
        a_rk = jnp.where(incl, g[P:, P:], 0.0)
        t = eye + nmat
        npow = nmat
        steps = max(C.bit_length() - 2, 0)
        for _ in range(steps):
            npow = _mm(npow, npow, N_RW)
            t = t + _mm(t, npow, N_RW)
        rhs = jnp.concatenate([a_st, _mm(a_ak, v_st, N_RW)], axis=1)
        x = _mm(t, rhs, N_RW)
        a_hat = x[0:C, 0:P] + x[C:, 0:P]
        v_hat = x[0:C, P:] + x[C:, P:]
        s = s_scr[p]
        ur = _mm(jnp.concatenate([a_hat, r_t[:, sl]], axis=0), s, N_RW, tb=True)
        u = ur[0:C] + v_hat
        y_st = _mm(jnp.concatenate([a_rb, a_rk], axis=1),
                   jnp.concatenate([stack2(u), v_st], axis=0), N_RW)
        y_ref[0, 0, :, sl] = ur[C:] + y_st[0:C] + y_st[C:]
        uv_t = jnp.concatenate([u, v[:, sl]], axis=0).T
        upd = _mm(uv_t, jnp.concatenate([b_p[:, sl], k_p[:, sl]], axis=0), N_RW)
        s_new = s * p_all[:, sl] + jnp.where(same, upd, 0.0)
        s_scr[p] = s_new
        st_ref[0, 0, p] = s_new


def _rwkv_scan(r, k, v, kk, lw, a, ka, s0):
    B, L, W = r.shape
    nc = L // CHUNK
    cidx = _chunk_index(nc)
    shared = pl.BlockSpec((1, CHUNK, W), lambda b, d, c: (b, cidx(b, d, c), 0))
    perdir = pl.BlockSpec((1, 1, CHUNK, W), lambda b, d, c: (d, b, cidx(b, d, c), 0))
    npair = RW_HEADS // 2
    P = 2 * RW_HEAD_DIM
    sspec = pl.BlockSpec((1, 1, npair, P, P), lambda b, d, c: (b, d, 0, 0, 0))
    return pl.pallas_call(
        _rwkv_body,
        grid=(B, 2, nc),
        in_specs=[shared, shared, shared, shared, perdir, perdir,
                  pl.BlockSpec((1, W), lambda b, d, c: (0, 0)), sspec],
        out_specs=[pl.BlockSpec((1, 1, CHUNK, W), lambda b, d, c: (d, b, cidx(b, d, c), 0)), sspec],
        out_shape=[jax.ShapeDtypeStruct((2, B, L, W), F32),
                   jax.ShapeDtypeStruct((B, 2, npair, P, P), F32)],
        scratch_shapes=[pltpu.VMEM((npair, P, P), F32)],
        compiler_params=_cparams(("parallel", "parallel", "arbitrary")),
        name="rwkv_scan",
    )(r, k, v, kk, lw, a, ka, s0)


def _gla_body(qk_ref, v_ref, g_ref, hm_ref, s0_ref, o_ref, st_ref, s_scr, b_scr, o_scr):
    C = CHUNK
    d = pl.program_id(1)
    c = pl.program_id(2)

    @pl.when(c == 0)
    def _():
        s_scr[...] = s0_ref[0, 0]

    sgn = 1 - 2 * d
    kw = GLA_KEY_WIDTH
    q = qk_ref[0][:, 0:kw]
    k = qk_ref[0][:, kw:]
    v = v_ref[0]
    g = g_ref[0, 0]
    _, upto = _order_masks(C, sgn)
    b = _mm_exact_l(upto.astype(BF16), g, N_CUM)
    tot = jnp.where(d == 0, b[C - 1:C], b[0:1])
    s = s_scr[...]
    inter = _mm(q * jnp.exp(b), s, N_GLA, tb=True)
    b_scr[...] = b
    hm = hm_ref[...]
    jrow = lax.broadcasted_iota(jnp.int32, (C, 1), 0)

    def row(i, carry):
        bi = b_scr[pl.ds(i, 1), :]
        qi = qk_ref[0, pl.ds(i, 1), 0:kw]
        valid = (jrow - i) * sgn <= 0
        e = jnp.where(valid, jnp.exp(jnp.minimum(bi - b, 0.0)) * k * qi, 0.0)
        att = _mm_exact_r(e, hm, N_GLA_INTRA)
        o_scr[pl.ds(i, 1), :] = jnp.sum(att * v, axis=0, keepdims=True)
        return carry

    lax.fori_loop(0, C, row, 0)
    o_ref[0, 0] = inter + o_scr[...]
    upd = _mm(v.T, k * jnp.exp(tot - b), N_GLA)
    ri = lax.broadcasted_iota(jnp.int32, upd.shape, 0) // GLA_VAL_DIM
    ci = lax.broadcasted_iota(jnp.int32, upd.shape, 1) // GLA_KEY_DIM
    s_new = s * jnp.exp(tot) + jnp.where(ri == ci, upd, 0.0)
    s_scr[...] = s_new
    st_ref[0, 0] = s_new


def _gla_scan(qk, v, la, hm, s0):
    B, L, _ = qk.shape
    kw, vw = GLA_KEY_WIDTH, GLA_VAL_WIDTH
    nc = L // CHUNK
    cidx = _chunk_index(nc)
    sspec = pl.BlockSpec((1, 1, vw, kw), lambda b, d, c: (b, d, 0, 0))
    return pl.pallas_call(
        _gla_body,
        grid=(B, 2, nc),
        in_specs=[pl.BlockSpec((1, CHUNK, 2 * kw), lambda b, d, c: (b, cidx(b, d, c), 0)),
                  pl.BlockSpec((1, CHUNK, vw), lambda b, d, c: (b, cidx(b, d, c), 0)),
                  pl.BlockSpec((1, 1, CHUNK, kw), lambda b, d, c: (d, b, cidx(b, d, c), 0)),
                  pl.BlockSpec((kw, vw), lambda b, d, c: (0, 0)), sspec],
        out_specs=[pl.BlockSpec((1, 1, CHUNK, vw), lambda b, d, c: (d, b, cidx(b, d, c), 0)), sspec],
        out_shape=[jax.ShapeDtypeStruct((2, B, L, vw), F32),
                   jax.ShapeDtypeStruct((B, 2, vw, kw), F32)],
        scratch_shapes=[pltpu.VMEM((vw, kw), F32), pltpu.VMEM((CHUNK, kw), F32),
                        pltpu.VMEM((CHUNK, vw), F32)],
        compiler_params=_cparams(("parallel", "parallel", "arbitrary")),
        name="gla_scan",
    )(qk, v, la, hm, s0)


def _seg_norm(y, bd, dim, eps):
    mu = _mm_exact_r(y, bd, N_SEG) * (1.0 / dim)
    dlt = y - mu
    var = _mm_exact_r(dlt * dlt, bd, N_SEG) * (1.0 / dim)
    return dlt * lax.rsqrt(var + eps)


def _layer_norm(x, w, b):
    mu = jnp.mean(x, axis=-1, keepdims=True)
    dlt = x - mu
    var = jnp.mean(dlt * dlt, axis=-1, keepdims=True)
    return dlt * lax.rsqrt(var + LN_EPS) * w + b


def _merge_body(x_ref, mod_ref, yrw_ref, bonus_ref, g_ref, ygla_ref, og_ref, gate_ref,
                bd64_ref, bd128_ref, rgw_ref, rgb_ref, ggw_ref, ggb_ref, *refs):
    n = N_MERGE
    wrw = [r[...] for r in refs[0:n]]
    wgla = [r[...] for r in refs[n:2 * n]]
    wout = [r[...] for r in refs[2 * n:3 * n]]
    ln_w_ref, ln_b_ref, x1_ref, h2_ref = refs[3 * n:]
    y = _seg_norm(yrw_ref[0, 0] + yrw_ref[1, 0], bd64_ref[...], RW_HEAD_DIM, RW_GN_EPS)
    y = (y * rgw_ref[...] + rgb_ref[...] + bonus_ref[0]) * g_ref[0]
    y_rw = _mmp(_split(y, n), wrw)
    y = _seg_norm(ygla_ref[0, 0] + ygla_ref[1, 0], bd128_ref[...], GLA_VAL_DIM, LN_EPS)
    y = (y * ggw_ref[...] + ggb_ref[...]) * _silu(og_ref[0])
    y_gla = _mmp(_split(y, n), wgla)
    gate = gate_ref[0]
    mixed = gate[:, 0:D_MODEL] * y_rw + gate[:, D_MODEL:] * y_gla
    mix = _mmp(_split(mixed, n), wout)
    g1 = mod_ref[0, 2:3, :]
    x1 = _layer_norm(ALPHA * x_ref[0] + g1 * mix, ln_w_ref[...], ln_b_ref[...])
    x1_ref[0] = x1
    h2_ref[0] = x1 * (1.0 + mod_ref[0, 4:5, :]) + mod_ref[0, 3:4, :]


def _merge(x, mod, yrw, bonus, g, ygla, p_gla, gate, consts, weights, ln_w, ln_b, tm):
    B, L, D = x.shape
    W = RW_WIDTH
    tok = lambda w: pl.BlockSpec((1, tm, w), lambda b, i: (b, i, 0))
    dirs = pl.BlockSpec((2, 1, tm, W), lambda b, i: (0, b, i, 0))
    wl = [w for ws in weights for w in ws]
    return pl.pallas_call(
        _merge_body,
        grid=(B, L // tm),
        in_specs=[tok(D), pl.BlockSpec((1, 6, D), lambda b, i: (b, 0, 0)), dirs, tok(W), tok(W), dirs,
                  pl.BlockSpec((1, tm, W), lambda b, i: (b, i, GLA_QKV_COLS // W)), tok(2 * D)]
                 + [_full(c.shape) for c in consts] + [_full(w.shape) for w in wl]
                 + [_full(ln_w.shape), _full(ln_b.shape)],
        out_specs=[tok(D), tok(D)],
        out_shape=[jax.ShapeDtypeStruct((B, L, D), F32)] * 2,
        compiler_params=_cparams(("parallel", "arbitrary")),
        name="merge",
    )(x, mod, yrw, bonus, g, ygla, p_gla, gate, *consts, *wl, ln_w, ln_b)


def _router_body(h_ref, *refs):
    n = N_ROUTER
    rt = [r[...] for r in refs[0:n]]
    bias_ref, e_out, g_out = refs[n:]
    tm = h_ref.shape[0]
    E, G, PG = N_EXPERTS, N_GROUPS, N_EXPERTS // N_GROUPS
    logits = _mmp(rt, _split(h_ref[...], n), tb=True)
    scores = _sigmoid(logits)
    sel = scores + bias_ref[:, 0:tm]
    NEG = -jnp.inf
    ip = lax.broadcasted_iota(jnp.int32, (PG, tm), 0)
    group_rows = []
    for gidx in range(G):
        sg = sel[gidx * PG:(gidx + 1) * PG]
        m1 = jnp.max(sg, axis=0, keepdims=True)
        first = jnp.min(jnp.where(sg == m1, ip, PG), axis=0, keepdims=True)
        m2 = jnp.max(jnp.where(ip == first, NEG, sg), axis=0, keepdims=True)
        group_rows.append(m1 + m2)
    gs = jnp.concatenate(group_rows, axis=0)
    gi = lax.broadcasted_iota(jnp.int32, (G, tm), 0)
    keep = jnp.zeros((G, tm), F32)
    for _ in range(TOPK_GROUPS):
        m = jnp.max(gs, axis=0, keepdims=True)
        idx = jnp.min(jnp.where(gs == m, gi, G), axis=0, keepdims=True)
        hit = gi == idx
        keep = jnp.where(hit, 1.0, keep)
        gs = jnp.where(hit, NEG, gs)
    cur = jnp.concatenate(
        [jnp.where(keep[gidx:gidx + 1] > 0.5, sel[gidx * PG:(gidx + 1) * PG], NEG) for gidx in range(G)],
        axis=0)
    ei = lax.broadcasted_iota(jnp.int32, (E, tm), 0)
    idxs, gates = [], []
    for _ in range(TOP_K):
        m = jnp.max(cur, axis=0, keepdims=True)
        idx = jnp.min(jnp.where(cur == m, ei, E), axis=0, keepdims=True)
        hit = ei == idx
        idxs.append(idx)
        gates.append(jnp.sum(jnp.where(hit, scores, 0.0), axis=0, keepdims=True))
        cur = jnp.where(hit, NEG, cur)
    gate = jnp.concatenate(gates, axis=0)
    e_out[...] = jnp.concatenate(idxs, axis=0)
    g_out[...] = gate / jnp.sum(gate, axis=0, keepdims=True) * ROUTED_SCALE


def _router(h, router_t_parts, bias_b, tm):
    n, D = h.shape
    E = N_EXPERTS
    return pl.pallas_call(
        _router_body,
        grid=(n // tm,),
        in_specs=[pl.BlockSpec((tm, D), lambda i: (i, 0))]
                 + [pl.BlockSpec((E, D), lambda i: (0, 0))] * len(router_t_parts)
                 + [pl.BlockSpec(bias_b.shape, lambda i: (0, 0))],
        out_specs=[pl.BlockSpec((TOP_K, tm), lambda i: (0, i))] * 2,
        out_shape=[jax.ShapeDtypeStruct((TOP_K, n), jnp.int32), jax.ShapeDtypeStruct((TOP_K, n), F32)],
        compiler_params=_cparams(("arbitrary",)),
        name="router",
    )(h, *router_t_parts, bias_b)


def _row_gather(idx_ref, src_hbm, dst_ref, sem, rows):
    def issue(r, carry):
        pltpu.make_async_copy(src_hbm.at[pl.ds(idx_ref[r], 1)], dst_ref.at[pl.ds(r, 1)], sem).start()
        return carry
    lax.fori_loop(0, rows, issue, 0)


def _row_gather_wait(src_hbm, dst_ref, sem, rows):
    pltpu.make_async_copy(src_hbm.at[pl.ds(0, rows)], dst_ref, sem).wait()


def _experts_body(be_ref, nu_ref, idx_ref, idx_next_ref, h_hbm, wgu_ref, wd_ref, y_ref, xbuf, sem):
    i = pl.program_id(0)
    nb = pl.num_programs(0)
    nused = nu_ref[0]
    slot = i % 2
    R = ROW_BLOCK

    @pl.when(i == 0)
    def _():
        _row_gather(idx_ref.at[0, 0], h_hbm, xbuf.at[0], sem.at[0], R)

    @pl.when((i + 1 < nb) & (i + 1 < nused))
    def _():
        _row_gather(idx_next_ref.at[0, 0], h_hbm, xbuf.at[1 - slot], sem.at[1 - slot], R)

    @pl.when((i < nused) | (i == 0))
    def _():
        _row_gather_wait(h_hbm, xbuf.at[slot], sem.at[slot], R)

    @pl.when(i < nused)
    def _():
        xb = xbuf[slot]
        F = EXPERT_DIM
        gu = _mm(xb, wgu_ref[0], N_EXPERT)
        act = _silu(gu[:, 0:F]) * gu[:, F:]
        y_ref[...] = _mm(act, wd_ref[0], N_EXPERT)

    @pl.when(i >= nused)
    def _():
        y_ref[...] = jnp.zeros(y_ref.shape, F32)


def _experts(block_e, nused, row_tok, h, w_gate_up, w_down):
    nb = block_e.shape[0]
    n, D = h.shape
    R = ROW_BLOCK
    F2 = w_gate_up.shape[2]
    idx3 = row_tok.reshape(nb, 1, R)
    grid_spec = pltpu.PrefetchScalarGridSpec(
        num_scalar_prefetch=2,
        grid=(nb,),
        in_specs=[pl.BlockSpec((1, 1, R), lambda i, be, nu: (i, 0, 0), memory_space=pltpu.SMEM),
                  pl.BlockSpec((1, 1, R), lambda i, be, nu: (jnp.minimum(i + 1, nb - 1), 0, 0),
                               memory_space=pltpu.SMEM),
                  pl.BlockSpec(memory_space=pl.ANY),
                  pl.BlockSpec((1, D, F2), lambda i, be, nu: (be[i], 0, 0)),
                  pl.BlockSpec((1, F2 // 2, D), lambda i, be, nu: (be[i], 0, 0))],
        out_specs=pl.BlockSpec((R, D), lambda i, be, nu: (i, 0)),
        scratch_shapes=[pltpu.VMEM((2, R, D), F32), pltpu.SemaphoreType.DMA((2,))],
    )
    return pl.pallas_call(
        _experts_body,
        grid_spec=grid_spec,
        out_shape=jax.ShapeDtypeStruct((nb * R, D), F32),
        compiler_params=_cparams(("arbitrary",)),
        name="experts",
    )(block_e, nused, idx3, idx3, h, w_gate_up, w_down)


def _final_body(pos_ref, pos_next_ref, y_hbm, x1_ref, h2_ref, mod_ref, gate_ref, *refs):
    n = N_SHARED
    sgu = [r[...] for r in refs[0:n]]
    sd = [r[...] for r in refs[n:2 * n]]
    ln_w_ref, ln_b_ref, o_ref, ybuf, sem = refs[2 * n:]
    b = pl.program_id(0)
    i = pl.program_id(1)
    nb, ni = pl.num_programs(0), pl.num_programs(1)
    step = b * ni + i
    last = nb * ni - 1
    slot = step % 2
    tm = x1_ref.shape[1]
    rows = tm * TOP_K

    @pl.when(step == 0)
    def _():
        _row_gather(pos_ref.at[0, 0], y_hbm, ybuf.at[0], sem.at[0], rows)

    @pl.when(step < last)
    def _():
        _row_gather(pos_next_ref.at[0, 0], y_hbm, ybuf.at[1 - slot], sem.at[1 - slot], rows)

    _row_gather_wait(y_hbm, ybuf.at[slot], sem.at[slot], rows)
    gate = gate_ref[0]
    routed = None
    for kk in range(TOP_K):
        t = ybuf[slot, kk * tm:(kk + 1) * tm, :] * gate[:, kk:kk + 1]
        routed = t if routed is None else routed + t
    h2 = h2_ref[0]
    F = sgu[0].shape[1] // 2
    gu = _mmp(_split(h2, n), sgu)
    act = _silu(gu[:, 0:F]) * gu[:, F:]
    shared = _mmp(_split(act, n), sd)
    g2 = mod_ref[0, 5:6, :]
    o_ref[0] = _layer_norm(ALPHA * x1_ref[0] + g2 * (routed + shared), ln_w_ref[...], ln_b_ref[...])


def _final(pos, y, x1, h2, mod, gate, sgu, sd, ln_w, ln_b, tm):
    B, L, D = x1.shape
    ni = L // tm
    nt = B * ni
    rows = tm * TOP_K
    pos3 = pos.reshape(nt, 1, rows)
    tok = lambda w: pl.BlockSpec((1, tm, w), lambda b, i: (b, i, 0))
    ws = list(sgu) + list(sd)
    return pl.pallas_call(
        _final_body,
        grid=(B, ni),
        in_specs=[pl.BlockSpec((1, 1, rows), lambda b, i: (b * ni + i, 0, 0), memory_space=pltpu.SMEM),
                  pl.BlockSpec((1, 1, rows), lambda b, i: (jnp.minimum(b * ni + i + 1, nt - 1), 0, 0),
                               memory_space=pltpu.SMEM),
                  pl.BlockSpec(memory_space=pl.ANY),
                  tok(D), tok(D), pl.BlockSpec((1, 6, D), lambda b, i: (b, 0, 0)), tok(TOP_K)]
                 + [_full(w.shape) for w in ws] + [_full(ln_w.shape), _full(ln_b.shape)],
        out_specs=tok(D),
        out_shape=jax.ShapeDtypeStruct((B, L, D), F32),
        scratch_shapes=[pltpu.VMEM((2, rows, D), F32), pltpu.SemaphoreType.DMA((2,))],
        compiler_params=_cparams(("arbitrary", "arbitrary")),
        name="final",
    )(pos3, pos3, y, x1, h2, mod, gate, *ws, ln_w, ln_b)


def _dispatch_plan(experts, n):
    R = ROW_BLOCK
    flat_e = experts.reshape(-1)
    nk = flat_e.shape[0]
    order = jnp.argsort(flat_e)
    sorted_e = flat_e[order]
    sizes = jnp.bincount(flat_e, length=N_EXPERTS)
    padded = (sizes + R - 1) // R * R
    start = jnp.cumsum(sizes) - sizes
    pad_end = jnp.cumsum(padded)
    pad_start = pad_end - padded
    dest = (pad_start[sorted_e] + jnp.arange(nk) - start[sorted_e]).astype(jnp.int32)
    nb = (nk + N_EXPERTS * (R - 1) + R - 1) // R
    row_tok = jnp.zeros((nb * R,), jnp.int32).at[dest].set((order // TOP_K).astype(jnp.int32))
    pos = jnp.zeros((nk,), jnp.int32).at[order].set(dest)
    block_e = jnp.minimum(jnp.searchsorted(pad_end, jnp.arange(nb) * R, side='right'),
                          N_EXPERTS - 1).astype(jnp.int32)
    nused = (pad_end[-1] // R).astype(jnp.int32).reshape(1)
    return row_tok, pos.reshape(n, TOP_K), block_e, nused


def _block_diag_ones(n, blk):
    i = jnp.arange(n) // blk
    return (i[:, None] == i[None, :]).astype(BF16)


def kernel(x, c, ctx, c_ctx, w_ada, b_ada, w_in, rw_mu, rw_w0, rw_w2, rw_a0, rw_a2, rw_g2, rw_k_k, rw_k_a,
           rw_r_k, rw_gn_w, rw_gn_b, gla_conv, gla_g2, gla_gb, gla_gn_w, gla_gn_b, w_br_rw, w_br_gla, w_out,
           ln1_w, ln1_b, router, router_bias, w_gate_up, w_down, sh_gate_up, sh_down, ln2_w, ln2_b):
    B, L, D = x.shape
    CT = ctx.shape[1]
    l = 0
    W = RW_WIDTH
    row = lambda t: t.reshape(1, -1)

    rows = -(-(B + 1) // 8) * 8
    cc = jnp.zeros((rows, D), F32).at[:B].set(c).at[B].set(c_ctx)
    mod = _ada(cc, w_ada[l], b_ada[l])
    mod_lat = mod[:B].reshape(B, 6, D)
    mod_ctx = jnp.broadcast_to(mod[B].reshape(1, 6, D), (B, 6, D))

    w = w_in[l]
    g0 = RW_COLS
    w_rw = jnp.pad(w[:, :RW_COLS], ((0, 0), (0, RW_SEG - RW_COLS)))
    w_gla = jnp.concatenate([w[:, g0:g0 + GLA_QKV_COLS],
                             w[:, g0 + GLA_QKV_COLS + GLA_GATE_RANK:g0 + GLA_COLS],
                             w[:, g0 + GLA_QKV_COLS:g0 + GLA_QKV_COLS + GLA_GATE_RANK],
                             jnp.zeros((D, GLA_SEG - GLA_COLS), F32)], axis=1)
    w_gate = w[:, MIX_COLS:]
    tm_p = min(512, L)
    p_rw = _proj(x, mod_lat, w_rw, False, tm_p)
    p_gla = _proj(x, mod_lat, w_gla, False, tm_p)
    gate = _proj(x, mod_lat, w_gate, True, tm_p)
    pc_rw = _proj(ctx, mod_ctx, w_rw, False, CT)
    pc_gla = _proj(ctx, mod_ctx, w_gla, False, CT)

    mu = jnp.pad(rw_mu[l], (0, RW_SEG - RW_COLS)).reshape(1, -1)
    lora = jnp.zeros((256, 5 * W), F32)
    e1 = RW_DECAY_RANK
    e2 = e1 + RW_AAA_RANK
    e3 = e2 + RW_GATE_RANK
    for d in range(2):
        lora = lora.at[0:e1, d * W:(d + 1) * W].set(rw_w2[l, d])
        lora = lora.at[e1:e2, (2 + d) * W:(3 + d) * W].set(rw_a2[l, d])
    lora = lora.at[e2:e3, 4 * W:].set(rw_g2[l])
    lora_p = (_split(lora, N_LORA) + [jnp.zeros_like(lora, BF16)])[:2]
    bd64 = _block_diag_ones(W, RW_HEAD_DIM)
    rw_consts = [mu, lora_p[0], lora_p[1], rw_w0[l].reshape(1, -1), rw_a0[l].reshape(1, -1),
                 row(rw_k_k[l]), row(rw_k_a[l]), row(rw_r_k[l]), bd64]
    tm_f = min(256, L)
    r, k, v, kkn, g, bonus, lwd, a = _feat_rw(p_rw, True, tm_f, rw_consts)
    rc, kc, vc, kknc, _, _, lwdc, ac = _feat_rw(pc_rw, False, CT, rw_consts)

    g2 = jnp.zeros((GLA_SEG - GLA_QKV_COLS - GLA_VAL_WIDTH, 2 * GLA_KEY_WIDTH), F32)
    g2 = g2.at[:GLA_GATE_RANK].set(jnp.concatenate([gla_g2[l, 0], gla_g2[l, 1]], axis=1))
    g2_p = (_split(g2, N_LORA) + [jnp.zeros_like(g2, BF16)])[:2]
    cw = jnp.pad(gla_conv[l].reshape(9, GLA_QKV_COLS), ((0, 7), (0, 0)))
    gla_consts = [cw, g2_p[0], g2_p[1], gla_gb[l].reshape(1, -1)]
    qk, vg, la = _feat_gla(p_gla, True, tm_f, gla_consts)
    qkc, vgc, lac = _feat_gla(pc_gla, False, CT, gla_consts)

    ka = row(rw_k_a[l])
    P = 2 * RW_HEAD_DIM
    s0 = jnp.zeros((B, 2, RW_HEADS // 2, P, P), F32)
    _, s_ctx = _rwkv_scan(rc, kc, vc, kknc, lwdc, ac, ka, s0)
    y_rw, _ = _rwkv_scan(r, k, v, kkn, lwd, a, ka, s_ctx)
    hi = jnp.arange(GLA_KEY_WIDTH) // GLA_KEY_DIM
    hj = jnp.arange(GLA_VAL_WIDTH) // GLA_VAL_DIM
    hm = (hi[:, None] == hj[None, :]).astype(BF16)
    g0s = jnp.zeros((B, 2, GLA_VAL_WIDTH, GLA_KEY_WIDTH), F32)
    _, gs_ctx = _gla_scan(qkc, vgc, lac, hm, g0s)
    y_gla, _ = _gla_scan(qk, vg, la, hm, gs_ctx)

    bd128 = _block_diag_ones(GLA_VAL_WIDTH, GLA_VAL_DIM)
    m_consts = [bd64, bd128, row(rw_gn_w[l]), row(rw_gn_b[l]), row(gla_gn_w[l]), row(gla_gn_b[l])]
    m_weights = [_split(w_br_rw[l], N_MERGE), _split(w_br_gla[l], N_MERGE), _split(w_out[l], N_MERGE)]
    x1, h2 = _merge(x, mod_lat, y_rw, bonus, g, y_gla, p_gla, gate, m_consts, m_weights,
                    row(ln1_w[l]), row(ln1_b[l]), min(256, L))

    n = B * L
    h2f = h2.reshape(n, D)
    tm_r = min(256, n)
    bias_b = jnp.broadcast_to(router_bias[l].reshape(-1, 1), (N_EXPERTS, tm_r))
    e_t, g_t = _router(h2f, _split(router[l].T, N_ROUTER), bias_b, tm_r)
    row_tok, pos, block_e, nused = _dispatch_plan(e_t.T, n)
    y = _experts(block_e, nused, row_tok, h2f, w_gate_up[l], w_down[l])
    tm_c = min(64, L)
    ni = L // tm_c
    pos_t = pos.reshape(B * ni, tm_c, TOP_K).transpose(0, 2, 1).reshape(-1)
    gate_tok = g_t.T.reshape(B, L, TOP_K)
    return _final(pos_t, y, x1, h2, mod_lat, gate_tok, _split(sh_gate_up[l], N_SHARED),
                  _split(sh_down[l], N_SHARED), row(ln2_w[l]), row(ln2_b[l]), tm_c)
```

```python
import functools
import math

import jax
import jax.numpy as jnp
from jax import lax
from jax.experimental import pallas as pl
from jax.experimental.pallas import tpu as pltpu

F32 = jnp.float32
BF16 = jnp.bfloat16

D_MODEL = 1024
GRID_W = 64
RW_WIDTH = 512
RW_HEADS = 8
RW_HEAD_DIM = 64
RW_DECAY_RANK = 32
RW_AAA_RANK = 32
RW_GATE_RANK = 96
RW_GN_EPS = 64e-5
RW_COLS = 1696
RW_SEG = 1792
GLA_HEADS = 4
GLA_KEY_WIDTH = 256
GLA_VAL_WIDTH = 512
GLA_KEY_DIM = 64
GLA_VAL_DIM = 128
GLA_GATE_RANK = 16
GLA_TAU = 16.0
GLA_QKV_COLS = 1024
GLA_COLS = 1552
GLA_SEG = 1664
MIX_COLS = RW_COLS + GLA_COLS
N_EXPERTS = 256
TOP_K = 8
N_GROUPS = 8
TOPK_GROUPS = 4
EXPERT_DIM = 256
ROUTED_SCALE = 2.5
LN_EPS = 1e-5
DEPTH = 1
ALPHA = (2 * DEPTH) ** 0.25

CHUNK = 64
GLA_SUB = 16
ROW_BLOCK = 256
VMEM_LIMIT = 48 * 1024 * 1024

N_ADA = 3
N_PROJ = 2
N_LORA = 2
N_SEG = 3
N_CUM = 3
N_RW = 2
N_GLA = 2
N_GLA_INTRA = 2
N_MERGE = 2
N_ROUTER = 3
N_EXPERT = 1
N_SHARED = 1


def _split(x, n):
    parts = []
    r = x
    for i in range(n):
        p = r.astype(BF16)
        parts.append(p)
        if i < n - 1:
            r = r - p.astype(F32)
    return parts


def _dot(a, b, ta=False, tb=False):
    dn = (((0 if ta else 1,), (1 if tb else 0,)), ((), ()))
    return lax.dot_general(a, b, dn, preferred_element_type=F32)


def _mmp(ap, bp, ta=False, tb=False):
    n = max(len(ap), len(bp))
    out = None
    for i in range(len(ap)):
        for j in range(len(bp)):
            if i + j <= n - 1:
                t = _dot(ap[i], bp[j], ta, tb)
                out = t if out is None else out + t
    return out


def _mm(a, b, n, ta=False, tb=False):
    return _mmp(_split(a, n), _split(b, n), ta, tb)


def _mm_exact_l(m_bf16, x, n):
    return _mmp([m_bf16], _split(x, n))


def _mm_exact_r(x, m_bf16, n):
    return _mmp(_split(x, n), [m_bf16])


def _sigmoid(x):
    return 1.0 / (1.0 + jnp.exp(-x))


def _silu(x):
    return x * _sigmoid(x)


def _log_sigmoid(x):
    return jnp.minimum(x, 0.0) - jnp.log(1.0 + jnp.exp(-jnp.abs(x)))


def _cparams(sem):
    return pltpu.CompilerParams(dimension_semantics=sem, vmem_limit_bytes=VMEM_LIMIT)


def _full(shape):
    nd = len(shape)
    return pl.BlockSpec(shape, lambda *a: (0,) * nd)


def _ada_body(c_ref, w_ref, b_ref, o_ref):
    s = _silu(c_ref[...])
    o_ref[...] = _mm(s, w_ref[...], N_ADA) + b_ref[...]


def _ada(cc, w, b):
    rows, d = cc.shape
    n = w.shape[1] // d
    return pl.pallas_call(
        _ada_body,
        grid=(n,),
        in_specs=[pl.BlockSpec((rows, d), lambda j: (0, 0)),
                  pl.BlockSpec((d, d), lambda j: (0, j)),
                  pl.BlockSpec((1, d), lambda j: (0, j))],
        out_specs=pl.BlockSpec((rows, d), lambda j: (0, j)),
        out_shape=jax.ShapeDtypeStruct((rows, w.shape[1]), F32),
        compiler_params=_cparams(("arbitrary",)),
        name="ada",
    )(cc, w, b.reshape(1, -1))


def _proj_body(act, x_ref, mod_ref, *refs):
    w_refs, o_ref = refs[:-1], refs[-1]
    sh = mod_ref[0, 0:1, :]
    sc = mod_ref[0, 1:2, :]
    h = x_ref[0] * (1.0 + sc) + sh
    y = _mmp(_split(h, N_PROJ), [w[...] for w in w_refs])
    if act:
        y = _sigmoid(y)
    o_ref[0] = y


def _proj(x, mod, w, act, tm):
    B, L, D = x.shape
    n = w.shape[1]
    wp = _split(w, N_PROJ)
    return pl.pallas_call(
        functools.partial(_proj_body, act),
        grid=(B, L // tm),
        in_specs=[pl.BlockSpec((1, tm, D), lambda b, i: (b, i, 0)),
                  pl.BlockSpec((1, 6, D), lambda b, i: (b, 0, 0))]
                 + [pl.BlockSpec((D, n), lambda b, i: (0, 0))] * N_PROJ,
        out_specs=pl.BlockSpec((1, tm, n), lambda b, i: (b, i, 0)),
        out_shape=jax.ShapeDtypeStruct((B, L, n), F32),
        compiler_params=_cparams(("parallel", "arbitrary")),
        name="proj",
    )(x, mod, *wp)


def _neighbours(vertical, width, tm, cur, prev_ref, next_ref):
    if vertical:
        i = pl.program_id(1)
        n = pl.num_programs(1)
        prev = jnp.where(i > 0, prev_ref[0], 0.0)
        nxt = jnp.where(i < n - 1, next_ref[0], 0.0)
        pad = jnp.zeros((8, cur.shape[1]), F32)
        ext = jnp.concatenate([pad, prev, cur, nxt, pad], axis=0)
        off = GRID_W + 8
    else:
        pad = jnp.zeros((8, cur.shape[1]), F32)
        ext = jnp.concatenate([pad, cur, pad], axis=0)
        off = 8
    col = lax.broadcasted_iota(jnp.int32, (tm, 1), 0) % width

    def get(dr, dc):
        s = off + GRID_W * dr + dc
        v = ext[s:s + tm]
        if dc == -1:
            v = jnp.where(col == 0, 0.0, v)
        elif dc == 1:
            v = jnp.where(col == width - 1, 0.0, v)
        return v

    return get


def _feat_rw_body(vertical, width, tm, *refs):
    if vertical:
        prev_ref, cur_ref, next_ref = refs[:3]
        refs = refs[3:]
    else:
        cur_ref = refs[0]
        prev_ref = next_ref = None
        refs = refs[1:]
    (mu_ref, lw_hi_ref, lw_lo_ref, w0_ref, a0_ref, kk_ref, ka_ref, rk_ref, bd_ref,
     r_out, k_out, v_out, kkn_out, g_out, bonus_out, lwd_out, a_out) = refs
    cur = cur_ref[0]
    get = _neighbours(vertical, width, tm, cur, prev_ref, next_ref)
    left, right = get(0, -1), get(0, 1)
    up, down = (get(-1, 0), get(1, 0)) if vertical else (left, right)
    l4 = lax.broadcasted_iota(jnp.int32, (1, cur.shape[1]), 1) % 4
    shifted = jnp.where(l4 == 0, left, jnp.where(l4 == 1, right, jnp.where(l4 == 2, up, down)))
    p = cur + mu_ref[...] * (shifted - cur)

    r = p[:, 0:RW_WIDTH]
    k = p[:, RW_WIDTH:2 * RW_WIDTH]
    v = p[:, 2 * RW_WIDTH:3 * RW_WIDTH]
    slab = p[:, 3 * RW_WIDTH:3 * RW_WIDTH + 256]
    ln = lax.broadcasted_iota(jnp.int32, (1, 256), 1)
    e1 = RW_DECAY_RANK
    e2 = e1 + RW_AAA_RANK
    e3 = e2 + RW_GATE_RANK
    slab = jnp.where(ln < e1, jnp.tanh(slab),
                     jnp.where(ln < e2, slab, jnp.where(ln < e3, _sigmoid(slab), 0.0)))
    lo = _mmp(_split(slab, N_LORA), [lw_hi_ref[...], lw_lo_ref[...]][:N_LORA])
    W = RW_WIDTH
    a_sum = None
    for d in range(2):
        z = w0_ref[:, d * W:(d + 1) * W] + lo[:, d * W:(d + 1) * W]
        lwd_out[d, 0] = -_sigmoid(z) * math.exp(-0.5)
        a = _sigmoid(a0_ref[:, d * W:(d + 1) * W] + lo[:, (2 + d) * W:(3 + d) * W])
        a_out[d, 0] = a
        a_sum = a if a_sum is None else a_sum + a
    g_out[0] = lo[:, 4 * W:5 * W]
    bd = bd_ref[...]
    kk = k * kk_ref[...]
    ss = _mm_exact_r(kk * kk, bd, N_SEG)
    kkn_out[0] = kk / jnp.maximum(jnp.sqrt(ss), 1e-12)
    kmod_sum = k * (2.0 + (a_sum - 2.0) * ka_ref[...])
    bonus_out[0] = _mm_exact_r(r * kmod_sum * rk_ref[...], bd, N_SEG) * v
    r_out[0] = r
    k_out[0] = k
    v_out[0] = v


def _feat_rw(p, vertical, tm, consts):
    B, L, S = p.shape
    width = GRID_W if vertical else tm
    W = RW_WIDTH
    hb = tm // GRID_W
    nhb = L // GRID_W
    tile = pl.BlockSpec((1, tm, S), lambda b, i: (b, i, 0))
    if vertical:
        in_specs = [pl.BlockSpec((1, GRID_W, S), lambda b, i: (b, jnp.maximum(i * hb - 1, 0), 0)),
                    tile,
                    pl.BlockSpec((1, GRID_W, S), lambda b, i: (b, jnp.minimum((i + 1) * hb, nhb - 1), 0))]
        args = [p, p, p]
    else:
        in_specs = [tile]
        args = [p]
    in_specs += [_full(c.shape) for c in consts]
    o1 = pl.BlockSpec((1, tm, W), lambda b, i: (b, i, 0))
    o2 = pl.BlockSpec((2, 1, tm, W), lambda b, i: (0, b, i, 0))
    s1 = jax.ShapeDtypeStruct((B, L, W), F32)
    s2 = jax.ShapeDtypeStruct((2, B, L, W), F32)
    return pl.pallas_call(
        functools.partial(_feat_rw_body, vertical, width, tm),
        grid=(B, L // tm),
        in_specs=in_specs,
        out_specs=[o1] * 6 + [o2] * 2,
        out_shape=[s1] * 6 + [s2] * 2,
        compiler_params=_cparams(("parallel", "arbitrary")),
        name="feat_rw",
    )(*args, *consts)


def _feat_gla_body(vertical, width, tm, *refs):
    if vertical:
        prev_ref, cur_ref, next_ref = refs[:3]
        refs = refs[3:]
    else:
        cur_ref = refs[0]
        prev_ref = next_ref = None
        refs = refs[1:]
    cw_ref, g2_hi_ref, g2_lo_ref, gb_ref, qk_out, v_out, la_out = refs
    cur = cur_ref[0][:, 0:GLA_QKV_COLS]
    if vertical:
        class _Slice:
            def __init__(self, ref):
                self.ref = ref

            def __getitem__(self, idx):
                return self.ref[idx][:, 0:GLA_QKV_COLS]
        get = _neighbours(True, width, tm, cur, _Slice(prev_ref), _Slice(next_ref))
    else:
        get = _neighbours(False, width, tm, cur, None, None)
    acc = None
    for dr in ((-1, 0, 1) if vertical else (0,)):
        for dc in (-1, 0, 1):
            t = get(dr, dc) * cw_ref[(dr + 1) * 3 + (dc + 1):(dr + 1) * 3 + (dc + 1) + 1, :]
            acc = t if acc is None else acc + t
    qkv = _silu(acc)
    kw = GLA_KEY_WIDTH
    lane = lax.broadcasted_iota(jnp.int32, (1, 2 * kw), 1)
    qk_out[0] = qkv[:, 0:2 * kw] * jnp.where(lane < kw, GLA_KEY_DIM ** -0.5, 1.0)
    v_out[0] = qkv[:, 2 * kw:]
    pgl = cur_ref[0][:, GLA_QKV_COLS + GLA_VAL_WIDTH:GLA_SEG]
    z = _mmp(_split(pgl, N_LORA), [g2_hi_ref[...], g2_lo_ref[...]][:N_LORA]) + gb_ref[...]
    la = _log_sigmoid(z) * (1.0 / GLA_TAU)
    la_out[0, 0] = la[:, 0:kw]
    la_out[1, 0] = la[:, kw:]


def _feat_gla(p, vertical, tm, consts):
    B, L, S = p.shape
    width = GRID_W if vertical else tm
    hb = tm // GRID_W
    nhb = L // GRID_W
    tile = pl.BlockSpec((1, tm, S), lambda b, i: (b, i, 0))
    if vertical:
        in_specs = [pl.BlockSpec((1, GRID_W, S), lambda b, i: (b, jnp.maximum(i * hb - 1, 0), 0)),
                    tile,
                    pl.BlockSpec((1, GRID_W, S), lambda b, i: (b, jnp.minimum((i + 1) * hb, nhb - 1), 0))]
        args = [p, p, p]
    else:
        in_specs = [tile]
        args = [p]
    in_specs += [_full(c.shape) for c in consts]
    kw, vw = GLA_KEY_WIDTH, GLA_VAL_WIDTH
    return pl.pallas_call(
        functools.partial(_feat_gla_body, vertical, width, tm),
        grid=(B, L // tm),
        in_specs=in_specs,
        out_specs=[pl.BlockSpec((1, tm, 2 * kw), lambda b, i: (b, i, 0)),
                   pl.BlockSpec((1, tm, vw), lambda b, i: (b, i, 0)),
                   pl.BlockSpec((2, 1, tm, kw), lambda b, i: (0, b, i, 0))],
        out_shape=[jax.ShapeDtypeStruct((B, L, 2 * kw), F32),
                   jax.ShapeDtypeStruct((B, L, vw), F32),
                   jax.ShapeDtypeStruct((2, B, L, kw), F32)],
        compiler_params=_cparams(("parallel", "arbitrary")),
        name="feat_gla",
    )(*args, *consts)


def _order_masks(n, sgn):
    ti = lax.broadcasted_iota(jnp.int32, (n, n), 0)
    si = lax.broadcasted_iota(jnp.int32, (n, n), 1)
    rel = (si - ti) * sgn
    return rel < 0, rel <= 0


def _chunk_index(nc):
    return lambda b, d, c: c + d * (nc - 1 - 2 * c)


def _rwkv_body(r_ref, k_ref, v_ref, kk_ref, lw_ref, a_ref, ka_ref, s0_ref, y_ref, st_ref, s_scr):
    C = CHUNK
    d = pl.program_id(1)
    c = pl.program_id(2)

    @pl.when(c == 0)
    def _():
        s_scr[...] = s0_ref[0, 0]

    sgn = 1 - 2 * d
    r, k, v, kk = r_ref[0], k_ref[0], v_ref[0], kk_ref[0]
    lw, a = lw_ref[0, 0], a_ref[0, 0]
    before, upto = _order_masks(C, sgn)
    cum = _mm_exact_l(upto.astype(BF16), lw, N_CUM)
    tot = jnp.where(d == 0, cum[C - 1:C], cum[0:1])
    p_in = jnp.exp(cum)
    p_ex = jnp.exp(cum - lw)
    p_inv = jnp.exp(-cum)
    p_rem = jnp.exp(tot - cum)
    p_all = jnp.exp(tot)
    bvec = kk * a
    kmod = k * (1.0 + (a - 1.0) * ka_ref[...])
    a_t = -kk * p_ex
    r_t = r * p_in
    b_t = bvec * p_inv
    k_t = kmod * p_inv
    b_p = bvec * p_rem
    k_p = kmod * p_rem

    P = 2 * RW_HEAD_DIM
    lane = lax.broadcasted_iota(jnp.int32, (C, P), 1)
    h0 = lane < RW_HEAD_DIM
    ri = lax.broadcasted_iota(jnp.int32, (P, P), 0)
    ci = lax.broadcasted_iota(jnp.int32, (P, P), 1)
    same = (ri // C) == (ci // C)
    rel = ((ci % C) - (ri % C)) * sgn
    strict = same & (rel < 0)
    eye = (ri == ci).astype(F32)

    def stack2(x):
        return jnp.concatenate([jnp.where(h0, x, 0.0), jnp.where(h0, 0.0, x)], axis=0)

    pairs = range(RW_HEADS // 2)
    sls = [slice(p * P, (p + 1) * P) for p in pairs]
    a_st = [stack2(a_t[:, sl]) for sl in sls]
    v_st = [stack2(v[:, sl]) for sl in sls]
    g = [_mm(jnp.concatenate([a_st[p], stack2(r_t[:, sls[p]])], axis=0),
             jnp.concatenate([stack2(b_t[:, sls[p]]), stack2(k_t[:, sls[p]])], axis=0), N_RW, tb=True)
         for p in pairs]
    nmat = [jnp.where(strict, g[p][0:P, 0:P], 0.0) for p in pairs]
    a_ak = [jnp.where(strict, g[p][0:P, P:], 0.0) for p in pairs]
    ri2 = lax.broadcasted_iota(jnp.int32, (P, 2 * P), 0)
    ci2 = lax.broadcasted_iota(jnp.int32, (P, 2 * P), 1)
    incl2 = ((ri2 // C) == ((ci2 % P) // C)) & ((((ci2 % C) - (ri2 % C)) * sgn) <= 0)
    a_r = [jnp.where(incl2, g[p][P:, :], 0.0) for p in pairs]
    akv = [_mm(a_ak[p], v_st[p], N_RW) for p in pairs]
    t = [eye + nmat[p] for p in pairs]
    npow = nmat
    for _ in range(max(C.bit_length() - 2, 0)):
        npow = [_mm(npow[p], npow[p], N_RW) for p in pairs]
        t = [t[p] + _mm(t[p], npow[p], N_RW) for p in pairs]
    x = [_mm(t[p], jnp.concatenate([a_st[p], akv[p]], axis=1), N_RW) for p in pairs]
    s = [s_scr[p] for p in pairs]
    ur = [_mm(jnp.concatenate([x[p][0:C, 0:P] + x[p][C:, 0:P], r_t[:, sls[p]]], axis=0), s[p], N_RW, tb=True)
          for p in pairs]
    u = [ur[p][0:C] + x[p][0:C, P:] + x[p][C:, P:] for p in pairs]
    y_st = [_mm(a_r[p], jnp.concatenate([stack2(u[p]), v_st[p]], axis=0), N_RW) for p in pairs]
    for p in pairs:
        y_ref[0, 0, :, sls[p]] = ur[p][C:] + y_st[p][0:C] + y_st[p][C:]
    upd = [_mm(jnp.concatenate([u[p], v[:, sls[p]]], axis=0).T,
               jnp.concatenate([b_p[:, sls[p]], k_p[:, sls[p]]], axis=0), N_RW) for p in pairs]
    for p in pairs:
        s_new = s[p] * p_all[:, sls[p]] + jnp.where(same, upd[p], 0.0)
        s_scr[p] = s_new
        st_ref[0, 0, p] = s_new


def _rwkv_scan(r, k, v, kk, lw, a, ka, s0):
    B, L, W = r.shape
    nc = L // CHUNK
    cidx = _chunk_index(nc)
    shared = pl.BlockSpec((1, CHUNK, W), lambda b, d, c: (b, cidx(b, d, c), 0))
    perdir = pl.BlockSpec((1, 1, CHUNK, W), lambda b, d, c: (d, b, cidx(b, d, c), 0))
    npair = RW_HEADS // 2
    P = 2 * RW_HEAD_DIM
    sspec = pl.BlockSpec((1, 1, npair, P, P), lambda b, d, c: (b, d, 0, 0, 0))
    return pl.pallas_call(
        _rwkv_body,
        grid=(B, 2, nc),
        in_specs=[shared, shared, shared, shared, perdir, perdir,
                  pl.BlockSpec((1, W), lambda b, d, c: (0, 0)), sspec],
        out_specs=[pl.BlockSpec((1, 1, CHUNK, W), lambda b, d, c: (d, b, cidx(b, d, c), 0)), sspec],
        out_shape=[jax.ShapeDtypeStruct((2, B, L, W), F32),
                   jax.ShapeDtypeStruct((B, 2, npair, P, P), F32)],
        scratch_shapes=[pltpu.VMEM((npair, P, P), F32)],
        compiler_params=_cparams(("parallel", "parallel", "arbitrary")),
        name="rwkv_scan",
    )(r, k, v, kk, lw, a, ka, s0)


def _gla_body(qk_ref, v_ref, g_ref, hm_ref, s0_ref, o_ref, st_ref, s_scr):
    C = CHUNK
    SB = GLA_SUB
    H = GLA_HEADS
    d = pl.program_id(1)
    c = pl.program_id(2)

    @pl.when(c == 0)
    def _():
        s_scr[...] = s0_ref[0, 0]

    sgn = 1 - 2 * d
    kw, vw = GLA_KEY_WIDTH, GLA_VAL_WIDTH
    q = qk_ref[0][:, 0:kw]
    k = qk_ref[0][:, kw:]
    v = v_ref[0]
    g = g_ref[0, 0]
    _, upto = _order_masks(C, sgn)
    b = _mm_exact_l(upto.astype(BF16), g, N_CUM)
    pos = lax.broadcasted_iota(jnp.int32, (C, 1), 0) * sgn + d * (C - 1)

    def b_at(p):
        return jnp.sum(jnp.where(pos == p, b, 0.0), axis=0, keepdims=True)

    NEG = -jnp.inf
    tot = b_at(C - 1)
    s = s_scr[...]
    inter = _mm(q * jnp.exp(b), s, N_GLA, tb=True)

    half = pos >= C // 2
    odd = (pos // SB) % 2 == 1
    r1 = b_at(C // 2 - 1)
    r2 = jnp.where(half, b_at(3 * SB - 1), b_at(SB - 1))
    q1 = q * jnp.exp(jnp.where(half, b - r1, NEG))
    k1 = k * jnp.exp(jnp.where(half, NEG, r1 - b))
    q2 = q * jnp.exp(jnp.where(odd, b - r2, NEG))
    k2 = k * jnp.exp(jnp.where(odd, NEG, r2 - b))
    lane_h = lax.broadcasted_iota(jnp.int32, (C, kw), 1) // GLA_KEY_DIM

    def stack_heads(x):
        return jnp.concatenate([jnp.where(lane_h == h, x, 0.0) for h in range(H)], axis=0)

    att1 = _mm(stack_heads(q1), k1, N_GLA, tb=True)
    att2 = _mm(stack_heads(q2), k2, N_GLA, tb=True)
    pr = (lax.broadcasted_iota(jnp.int32, (H * C, C), 0) % C) * sgn + d * (C - 1)
    pc = lax.broadcasted_iota(jnp.int32, (H * C, C), 1) * sgn + d * (C - 1)
    att = att1 + jnp.where((pr >= C // 2) == (pc >= C // 2), att2, 0.0)
    res = _mm(att, v, N_GLA)
    lane_hv = lax.broadcasted_iota(jnp.int32, (C, vw), 1) // GLA_VAL_DIM
    off = None
    for h in range(H):
        t = jnp.where(lane_hv == h, res[h * C:(h + 1) * C], 0.0)
        off = t if off is None else off + t

    hm = hm_ref[...]
    il = lax.broadcasted_iota(jnp.int32, (SB, 1), 0)
    diag = []
    for blk in range(C // SB):
        rs = slice(blk * SB, (blk + 1) * SB)
        bb, qb, kb, vb = b[rs], q[rs], k[rs], v[rs]
        pieces = []
        for j in range(SB):
            valid = (il - j) * sgn >= 0
            pieces.append(jnp.exp(jnp.where(valid, bb - bb[j:j + 1], NEG)) * qb * kb[j:j + 1])
        a = _mm_exact_r(jnp.concatenate(pieces, axis=0), hm, N_GLA_INTRA)
        acc = None
        for j in range(SB):
            t = a[j * SB:(j + 1) * SB] * vb[j:j + 1]
            acc = t if acc is None else acc + t
        diag.append(acc)
    o_ref[0, 0] = inter + off + jnp.concatenate(diag, axis=0)

    upd = _mm(v.T, k * jnp.exp(tot - b), N_GLA)
    ri = lax.broadcasted_iota(jnp.int32, upd.shape, 0) // GLA_VAL_DIM
    ci = lax.broadcasted_iota(jnp.int32, upd.shape, 1) // GLA_KEY_DIM
    s_new = s * jnp.exp(tot) + jnp.where(ri == ci, upd, 0.0)
    s_scr[...] = s_new
    st_ref[0, 0] = s_new


def _gla_scan(qk, v, la, hm, s0):
    B, L, _ = qk.shape
    kw, vw = GLA_KEY_WIDTH, GLA_VAL_WIDTH
    nc = L // CHUNK
    cidx = _chunk_index(nc)
    sspec = pl.BlockSpec((1, 1, vw, kw), lambda b, d, c: (b, d, 0, 0))
    return pl.pallas_call(
        _gla_body,
        grid=(B, 2, nc),
        in_specs=[pl.BlockSpec((1, CHUNK, 2 * kw), lambda b, d, c: (b, cidx(b, d, c), 0)),
                  pl.BlockSpec((1, CHUNK, vw), lambda b, d, c: (b, cidx(b, d, c), 0)),
                  pl.BlockSpec((1, 1, CHUNK, kw), lambda b, d, c: (d, b, cidx(b, d, c), 0)),
                  pl.BlockSpec((kw, vw), lambda b, d, c: (0, 0)), sspec],
        out_specs=[pl.BlockSpec((1, 1, CHUNK, vw), lambda b, d, c: (d, b, cidx(b, d, c), 0)), sspec],
        out_shape=[jax.ShapeDtypeStruct((2, B, L, vw), F32),
                   jax.ShapeDtypeStruct((B, 2, vw, kw), F32)],
        scratch_shapes=[pltpu.VMEM((vw, kw), F32)],
        compiler_params=_cparams(("parallel", "parallel", "arbitrary")),
        name="gla_scan",
    )(qk, v, la, hm, s0)


def _seg_norm(y, bd, dim, eps):
    mu = _mm_exact_r(y, bd, N_SEG) * (1.0 / dim)
    dlt = y - mu
    var = _mm_exact_r(dlt * dlt, bd, N_SEG) * (1.0 / dim)
    return dlt * lax.rsqrt(var + eps)


def _layer_norm(x, w, b):
    mu = jnp.mean(x, axis=-1, keepdims=True)
    dlt = x - mu
    var = jnp.mean(dlt * dlt, axis=-1, keepdims=True)
    return dlt * lax.rsqrt(var + LN_EPS) * w + b


def _merge_body(x_ref, mod_ref, yrw_ref, bonus_ref, g_ref, ygla_ref, og_ref, gate_ref,
                bd64_ref, bd128_ref, rgw_ref, rgb_ref, ggw_ref, ggb_ref, *refs):
    n = N_MERGE
    wrw = [r[...] for r in refs[0:n]]
    wgla = [r[...] for r in refs[n:2 * n]]
    wout = [r[...] for r in refs[2 * n:3 * n]]
    ln_w_ref, ln_b_ref, x1_ref, h2_ref = refs[3 * n:]
    y = _seg_norm(yrw_ref[0, 0] + yrw_ref[1, 0], bd64_ref[...], RW_HEAD_DIM, RW_GN_EPS)
    y = (y * rgw_ref[...] + rgb_ref[...] + bonus_ref[0]) * g_ref[0]
    y_rw = _mmp(_split(y, n), wrw)
    y = _seg_norm(ygla_ref[0, 0] + ygla_ref[1, 0], bd128_ref[...], GLA_VAL_DIM, LN_EPS)
    y = (y * ggw_ref[...] + ggb_ref[...]) * _silu(og_ref[0])
    y_gla = _mmp(_split(y, n), wgla)
    gate = gate_ref[0]
    mixed = gate[:, 0:D_MODEL] * y_rw + gate[:, D_MODEL:] * y_gla
    mix = _mmp(_split(mixed, n), wout)
    g1 = mod_ref[0, 2:3, :]
    x1 = _layer_norm(ALPHA * x_ref[0] + g1 * mix, ln_w_ref[...], ln_b_ref[...])
    x1_ref[0] = x1
    h2_ref[0] = x1 * (1.0 + mod_ref[0, 4:5, :]) + mod_ref[0, 3:4, :]


def _merge(x, mod, yrw, bonus, g, ygla, p_gla, gate, consts, weights, ln_w, ln_b, tm):
    B, L, D = x.shape
    W = RW_WIDTH
    tok = lambda w: pl.BlockSpec((1, tm, w), lambda b, i: (b, i, 0))
    dirs = pl.BlockSpec((2, 1, tm, W), lambda b, i: (0, b, i, 0))
    wl = [w for ws in weights for w in ws]
    return pl.pallas_call(
        _merge_body,
        grid=(B, L // tm),
        in_specs=[tok(D), pl.BlockSpec((1, 6, D), lambda b, i: (b, 0, 0)), dirs, tok(W), tok(W), dirs,
                  pl.BlockSpec((1, tm, W), lambda b, i: (b, i, GLA_QKV_COLS // W)), tok(2 * D)]
                 + [_full(c.shape) for c in consts] + [_full(w.shape) for w in wl]
                 + [_full(ln_w.shape), _full(ln_b.shape)],
        out_specs=[tok(D), tok(D)],
        out_shape=[jax.ShapeDtypeStruct((B, L, D), F32)] * 2,
        compiler_params=_cparams(("parallel", "arbitrary")),
        name="merge",
    )(x, mod, yrw, bonus, g, ygla, p_gla, gate, *consts, *wl, ln_w, ln_b)


def _router_body(h_ref, *refs):
    n = N_ROUTER
    rt = [r[...] for r in refs[0:n]]
    bias_ref, e_out, g_out, cnt_out = refs[n:]
    tm = h_ref.shape[0]
    E, G, PG = N_EXPERTS, N_GROUPS, N_EXPERTS // N_GROUPS
    logits = _mmp(rt, _split(h_ref[...], n), tb=True)
    scores = _sigmoid(logits)
    sel = scores + bias_ref[:, 0:tm]
    NEG = -jnp.inf
    ip = lax.broadcasted_iota(jnp.int32, (PG, tm), 0)
    group_rows = []
    for gidx in range(G):
        sg = sel[gidx * PG:(gidx + 1) * PG]
        m1 = jnp.max(sg, axis=0, keepdims=True)
        first = jnp.min(jnp.where(sg == m1, ip, PG), axis=0, keepdims=True)
        m2 = jnp.max(jnp.where(ip == first, NEG, sg), axis=0, keepdims=True)
        group_rows.append(m1 + m2)
    gs = jnp.concatenate(group_rows, axis=0)
    gi = lax.broadcasted_iota(jnp.int32, (G, tm), 0)
    keep = jnp.zeros((G, tm), F32)
    for _ in range(TOPK_GROUPS):
        m = jnp.max(gs, axis=0, keepdims=True)
        idx = jnp.min(jnp.where(gs == m, gi, G), axis=0, keepdims=True)
        hit = gi == idx
        keep = jnp.where(hit, 1.0, keep)
        gs = jnp.where(hit, NEG, gs)
    cur = jnp.concatenate(
        [jnp.where(keep[gidx:gidx + 1] > 0.5, sel[gidx * PG:(gidx + 1) * PG], NEG) for gidx in range(G)],
        axis=0)
    ei = lax.broadcasted_iota(jnp.int32, (E, tm), 0)
    idxs, gates = [], []
    picked = jnp.zeros((E, tm), F32)
    for _ in range(TOP_K):
        m = jnp.max(cur, axis=0, keepdims=True)
        idx = jnp.min(jnp.where(cur == m, ei, E), axis=0, keepdims=True)
        hit = ei == idx
        idxs.append(idx)
        gates.append(jnp.sum(jnp.where(hit, scores, 0.0), axis=0, keepdims=True))
        cur = jnp.where(hit, NEG, cur)
        picked = jnp.where(hit, 1.0, picked)
    gate = jnp.concatenate(gates, axis=0)
    e_out[...] = jnp.concatenate(idxs, axis=0)
    g_out[...] = gate / jnp.sum(gate, axis=0, keepdims=True) * ROUTED_SCALE

    @pl.when(pl.program_id(0) == 0)
    def _():
        cnt_out[...] = jnp.zeros(cnt_out.shape, F32)

    part = picked[:, 0:128]
    for j in range(1, tm // 128):
        part = part + picked[:, j * 128:(j + 1) * 128]
    cnt_out[...] += part


def _router(h, router_t_parts, bias_b, tm):
    n, D = h.shape
    E = N_EXPERTS
    return pl.pallas_call(
        _router_body,
        grid=(n // tm,),
        in_specs=[pl.BlockSpec((tm, D), lambda i: (i, 0))]
                 + [pl.BlockSpec((E, D), lambda i: (0, 0))] * len(router_t_parts)
                 + [pl.BlockSpec(bias_b.shape, lambda i: (0, 0))],
        out_specs=[pl.BlockSpec((TOP_K, tm), lambda i: (0, i))] * 2
                  + [pl.BlockSpec((E, 128), lambda i: (0, 0))],
        out_shape=[jax.ShapeDtypeStruct((TOP_K, n), jnp.int32), jax.ShapeDtypeStruct((TOP_K, n), F32),
                   jax.ShapeDtypeStruct((E, 128), F32)],
        compiler_params=_cparams(("arbitrary",)),
        name="router",
    )(h, *router_t_parts, bias_b)


def _row_gather(idx_ref, src_hbm, dst_ref, sem, rows):
    def issue(r, carry):
        pltpu.make_async_copy(src_hbm.at[pl.ds(idx_ref[r], 1)], dst_ref.at[pl.ds(r, 1)], sem).start()
        return carry
    lax.fori_loop(0, rows, issue, 0, unroll=8)


def _row_gather_wait(src_hbm, dst_ref, sem, rows):
    pltpu.make_async_copy(src_hbm.at[pl.ds(0, rows)], dst_ref, sem).wait()


def _experts_body(be_ref, nu_ref, idx_ref, idx_next_ref, h_hbm, wgu_ref, wd_ref, y_ref, xbuf, sem):
    i = pl.program_id(0)
    nb = pl.num_programs(0)
    nused = nu_ref[0]
    slot = i % 2
    R = ROW_BLOCK

    @pl.when(i == 0)
    def _():
        _row_gather(idx_ref.at[0, 0], h_hbm, xbuf.at[0], sem.at[0], R)

    @pl.when((i + 1 < nb) & (i + 1 < nused))
    def _():
        _row_gather(idx_next_ref.at[0, 0], h_hbm, xbuf.at[1 - slot], sem.at[1 - slot], R)

    @pl.when((i < nused) | (i == 0))
    def _():
        _row_gather_wait(h_hbm, xbuf.at[slot], sem.at[slot], R)

    @pl.when(i < nused)
    def _():
        xb = xbuf[slot]
        F = EXPERT_DIM
        gu = _mm(xb, wgu_ref[0], N_EXPERT)
        act = _silu(gu[:, 0:F]) * gu[:, F:]
        y_ref[...] = _mm(act, wd_ref[0], N_EXPERT)

    @pl.when(i >= nused)
    def _():
        y_ref[...] = jnp.zeros(y_ref.shape, F32)


def _experts(block_e, nused, row_tok, h, w_gate_up, w_down):
    nb = block_e.shape[0]
    n, D = h.shape
    R = ROW_BLOCK
    F2 = w_gate_up.shape[2]
    idx3 = row_tok.reshape(nb, 1, R)
    grid_spec = pltpu.PrefetchScalarGridSpec(
        num_scalar_prefetch=2,
        grid=(nb,),
        in_specs=[pl.BlockSpec((1, 1, R), lambda i, be, nu: (i, 0, 0), memory_space=pltpu.SMEM),
                  pl.BlockSpec((1, 1, R), lambda i, be, nu: (jnp.minimum(i + 1, nb - 1), 0, 0),
                               memory_space=pltpu.SMEM),
                  pl.BlockSpec(memory_space=pl.ANY),
                  pl.BlockSpec((1, D, F2), lambda i, be, nu: (be[i], 0, 0)),
                  pl.BlockSpec((1, F2 // 2, D), lambda i, be, nu: (be[i], 0, 0))],
        out_specs=pl.BlockSpec((R, D), lambda i, be, nu: (i, 0)),
        scratch_shapes=[pltpu.VMEM((2, R, D), F32), pltpu.SemaphoreType.DMA((2,))],
    )
    return pl.pallas_call(
        _experts_body,
        grid_spec=grid_spec,
        out_shape=jax.ShapeDtypeStruct((nb * R, D), F32),
        compiler_params=_cparams(("arbitrary",)),
        name="experts",
    )(block_e, nused, idx3, idx3, h, w_gate_up, w_down)


def _final_body(pos_ref, pos_next_ref, y_hbm, x1_ref, h2_ref, mod_ref, gate_ref, *refs):
    n = N_SHARED
    sgu = [r[...] for r in refs[0:n]]
    sd = [r[...] for r in refs[n:2 * n]]
    ln_w_ref, ln_b_ref, o_ref, ybuf, sem = refs[2 * n:]
    b = pl.program_id(0)
    i = pl.program_id(1)
    nb, ni = pl.num_programs(0), pl.num_programs(1)
    step = b * ni + i
    last = nb * ni - 1
    slot = step % 2
    tm = x1_ref.shape[1]
    rows = tm * TOP_K

    @pl.when(step == 0)
    def _():
        _row_gather(pos_ref.at[0, 0], y_hbm, ybuf.at[0], sem.at[0], rows)

    @pl.when(step < last)
    def _():
        _row_gather(pos_next_ref.at[0, 0], y_hbm, ybuf.at[1 - slot], sem.at[1 - slot], rows)

    _row_gather_wait(y_hbm, ybuf.at[slot], sem.at[slot], rows)
    gate = gate_ref[0]
    routed = None
    for kk in range(TOP_K):
        t = ybuf[slot, kk * tm:(kk + 1) * tm, :] * gate[:, kk:kk + 1]
        routed = t if routed is None else routed + t
    h2 = h2_ref[0]
    F = sgu[0].shape[1] // 2
    gu = _mmp(_split(h2, n), sgu)
    act = _silu(gu[:, 0:F]) * gu[:, F:]
    shared = _mmp(_split(act, n), sd)
    g2 = mod_ref[0, 5:6, :]
    o_ref[0] = _layer_norm(ALPHA * x1_ref[0] + g2 * (routed + shared), ln_w_ref[...], ln_b_ref[...])


def _final(pos, y, x1, h2, mod, gate, sgu, sd, ln_w, ln_b, tm):
    B, L, D = x1.shape
    ni = L // tm
    nt = B * ni
    rows = tm * TOP_K
    pos3 = pos.reshape(nt, 1, rows)
    tok = lambda w: pl.BlockSpec((1, tm, w), lambda b, i: (b, i, 0))
    ws = list(sgu) + list(sd)
    return pl.pallas_call(
        _final_body,
        grid=(B, ni),
        in_specs=[pl.BlockSpec((1, 1, rows), lambda b, i: (b * ni + i, 0, 0), memory_space=pltpu.SMEM),
                  pl.BlockSpec((1, 1, rows), lambda b, i: (jnp.minimum(b * ni + i + 1, nt - 1), 0, 0),
                               memory_space=pltpu.SMEM),
                  pl.BlockSpec(memory_space=pl.ANY),
                  tok(D), tok(D), pl.BlockSpec((1, 6, D), lambda b, i: (b, 0, 0)), tok(TOP_K)]
                 + [_full(w.shape) for w in ws] + [_full(ln_w.shape), _full(ln_b.shape)],
        out_specs=tok(D),
        out_shape=jax.ShapeDtypeStruct((B, L, D), F32),
        scratch_shapes=[pltpu.VMEM((2, rows, D), F32), pltpu.SemaphoreType.DMA((2,))],
        compiler_params=_cparams(("arbitrary", "arbitrary")),
        name="final",
    )(pos3, pos3, y, x1, h2, mod, gate, *ws, ln_w, ln_b)


def _dispatch_plan(experts_t, sizes, tm):
    R, E = ROW_BLOCK, N_EXPERTS
    K, n = experts_t.shape
    nk = K * n
    nb = (nk + E * (R - 1) + R - 1) // R
    ncand = nb * R - nk
    padded = (sizes + R - 1) // R * R
    pad_end = jnp.cumsum(padded)
    tok = jnp.broadcast_to(jnp.arange(n, dtype=jnp.int32)[None, :], (K, n))
    slot = jnp.arange(K, dtype=jnp.int32)[:, None]
    aid = (tok // tm) * (tm * K) + slot * tm + tok % tm
    cand = jnp.arange(ncand, dtype=jnp.int32)
    ce, cp = cand // (R - 1), cand % (R - 1)
    need = jnp.repeat(padded - sizes, R - 1, total_repeat_length=E * (R - 1))
    need = jnp.pad(need, (0, ncand - E * (R - 1)))
    ckey = jnp.where(cp < need, 2 * ce + 1, 2 * E + 1)
    keys = jnp.concatenate([2 * experts_t.reshape(-1), ckey])
    toks = jnp.concatenate([tok.reshape(-1), jnp.zeros((ncand,), jnp.int32)])
    aids = jnp.concatenate([aid.reshape(-1), nk + cand])
    _, row_tok, row_aid = lax.sort((keys, toks, aids), num_keys=1)
    _, pos = lax.sort((row_aid, jnp.arange(nb * R, dtype=jnp.int32)), num_keys=1)
    block_e = jnp.minimum(jnp.searchsorted(pad_end, jnp.arange(nb) * R, side='right'), E - 1).astype(jnp.int32)
    nused = (pad_end[-1] // R).astype(jnp.int32).reshape(1)
    return row_tok, pos[:nk], block_e, nused


def _block_diag_ones(n, blk):
    i = jnp.arange(n) // blk
    return (i[:, None] == i[None, :]).astype(BF16)


def kernel(x, c, ctx, c_ctx, w_ada, b_ada, w_in, rw_mu, rw_w0, rw_w2, rw_a0, rw_a2, rw_g2, rw_k_k, rw_k_a,
           rw_r_k, rw_gn_w, rw_gn_b, gla_conv, gla_g2, gla_gb, gla_gn_w, gla_gn_b, w_br_rw, w_br_gla, w_out,
           ln1_w, ln1_b, router, router_bias, w_gate_up, w_down, sh_gate_up, sh_down, ln2_w, ln2_b):
    B, L, D = x.shape
    CT = ctx.shape[1]
    l = 0
    W = RW_WIDTH
    row = lambda t: t.reshape(1, -1)

    rows = -(-(B + 1) // 8) * 8
    cc = jnp.zeros((rows, D), F32).at[:B].set(c).at[B].set(c_ctx)
    mod = _ada(cc, w_ada[l], b_ada[l])
    mod_lat = mod[:B].reshape(B, 6, D)
    mod_ctx = jnp.broadcast_to(mod[B].reshape(1, 6, D), (B, 6, D))

    w = w_in[l]
    g0 = RW_COLS
    w_rw = jnp.pad(w[:, :RW_COLS], ((0, 0), (0, RW_SEG - RW_COLS)))
    w_gla = jnp.concatenate([w[:, g0:g0 + GLA_QKV_COLS],
                             w[:, g0 + GLA_QKV_COLS + GLA_GATE_RANK:g0 + GLA_COLS],
                             w[:, g0 + GLA_QKV_COLS:g0 + GLA_QKV_COLS + GLA_GATE_RANK],
                             jnp.zeros((D, GLA_SEG - GLA_COLS), F32)], axis=1)
    w_gate = w[:, MIX_COLS:]
    tm_p = min(512, L)
    p_rw = _proj(x, mod_lat, w_rw, False, tm_p)
    p_gla = _proj(x, mod_lat, w_gla, False, tm_p)
    gate = _proj(x, mod_lat, w_gate, True, tm_p)
    pc_rw = _proj(ctx, mod_ctx, w_rw, False, CT)
    pc_gla = _proj(ctx, mod_ctx, w_gla, False, CT)

    mu = jnp.pad(rw_mu[l], (0, RW_SEG - RW_COLS)).reshape(1, -1)
    lora = jnp.zeros((256, 5 * W), F32)
    e1 = RW_DECAY_RANK
    e2 = e1 + RW_AAA_RANK
    e3 = e2 + RW_GATE_RANK
    for d in range(2):
        lora = lora.at[0:e1, d * W:(d + 1) * W].set(rw_w2[l, d])
        lora = lora.at[e1:e2, (2 + d) * W:(3 + d) * W].set(rw_a2[l, d])
    lora = lora.at[e2:e3, 4 * W:].set(rw_g2[l])
    lora_p = (_split(lora, N_LORA) + [jnp.zeros_like(lora, BF16)])[:2]
    bd64 = _block_diag_ones(W, RW_HEAD_DIM)
    rw_consts = [mu, lora_p[0], lora_p[1], rw_w0[l].reshape(1, -1), rw_a0[l].reshape(1, -1),
                 row(rw_k_k[l]), row(rw_k_a[l]), row(rw_r_k[l]), bd64]
    tm_f = min(256, L)
    r, k, v, kkn, g, bonus, lwd, a = _feat_rw(p_rw, True, tm_f, rw_consts)
    rc, kc, vc, kknc, _, _, lwdc, ac = _feat_rw(pc_rw, False, CT, rw_consts)

    g2 = jnp.zeros((GLA_SEG - GLA_QKV_COLS - GLA_VAL_WIDTH, 2 * GLA_KEY_WIDTH), F32)
    g2 = g2.at[:GLA_GATE_RANK].set(jnp.concatenate([gla_g2[l, 0], gla_g2[l, 1]], axis=1))
    g2_p = (_split(g2, N_LORA) + [jnp.zeros_like(g2, BF16)])[:2]
    cw = jnp.pad(gla_conv[l].reshape(9, GLA_QKV_COLS), ((0, 7), (0, 0)))
    gla_consts = [cw, g2_p[0], g2_p[1], gla_gb[l].reshape(1, -1)]
    qk, vg, la = _feat_gla(p_gla, True, tm_f, gla_consts)
    qkc, vgc, lac = _feat_gla(pc_gla, False, CT, gla_consts)

    ka = row(rw_k_a[l])
    P = 2 * RW_HEAD_DIM
    s0 = jnp.zeros((B, 2, RW_HEADS // 2, P, P), F32)
    _, s_ctx = _rwkv_scan(rc, kc, vc, kknc, lwdc, ac, ka, s0)
    y_rw, _ = _rwkv_scan(r, k, v, kkn, lwd, a, ka, s_ctx)
    hi = jnp.arange(GLA_KEY_WIDTH) // GLA_KEY_DIM
    hj = jnp.arange(GLA_VAL_WIDTH) // GLA_VAL_DIM
    hm = (hi[:, None] == hj[None, :]).astype(BF16)
    g0s = jnp.zeros((B, 2, GLA_VAL_WIDTH, GLA_KEY_WIDTH), F32)
    _, gs_ctx = _gla_scan(qkc, vgc, lac, hm, g0s)
    y_gla, _ = _gla_scan(qk, vg, la, hm, gs_ctx)

    bd128 = _block_diag_ones(GLA_VAL_WIDTH, GLA_VAL_DIM)
    m_consts = [bd64, bd128, row(rw_gn_w[l]), row(rw_gn_b[l]), row(gla_gn_w[l]), row(gla_gn_b[l])]
    m_weights = [_split(w_br_rw[l], N_MERGE), _split(w_br_gla[l], N_MERGE), _split(w_out[l], N_MERGE)]
    x1, h2 = _merge(x, mod_lat, y_rw, bonus, g, y_gla, p_gla, gate, m_consts, m_weights,
                    row(ln1_w[l]), row(ln1_b[l]), min(256, L))

    n = B * L
    h2f = h2.reshape(n, D)
    tm_r = min(256, n)
    bias_b = jnp.broadcast_to(router_bias[l].reshape(-1, 1), (N_EXPERTS, tm_r))
    e_t, g_t, cnt = _router(h2f, _split(router[l].T, N_ROUTER), bias_b, tm_r)
    sizes = jnp.sum(cnt, axis=1).astype(jnp.int32)
    tm_c = min(64, L)
    row_tok, pos_t, block_e, nused = _dispatch_plan(e_t, sizes, tm_c)
    y = _experts(block_e, nused, row_tok, h2f, w_gate_up[l], w_down[l])
    gate_tok = g_t.T.reshape(B, L, TOP_K)
    return _final(pos_t, y, x1, h2, mod_lat, gate_tok, _split(sh_gate_up[l], N_SHARED),
                  _split(sh_down[l], N_SHARED), row(ln2_w[l]), row(ln2_b[l]), tm_c)
```

```python
import functools
import math

import jax
import jax.numpy as jnp
from jax import lax
from jax.experimental import pallas as pl
from jax.experimental.pallas import tpu as pltpu

F32 = jnp.float32
BF16 = jnp.bfloat16

D_MODEL = 1024
GRID_W = 64
RW_WIDTH = 512
RW_HEADS = 8
RW_HEAD_DIM = 64
RW_DECAY_RANK = 32
RW_AAA_RANK = 32
RW_GATE_RANK = 96
RW_GN_EPS = 64e-5
RW_COLS = 1696
RW_SEG = 1792
GLA_HEADS = 4
GLA_KEY_WIDTH = 256
GLA_VAL_WIDTH = 512
GLA_KEY_DIM = 64
GLA_VAL_DIM = 128
GLA_GATE_RANK = 16
GLA_TAU = 16.0
GLA_QKV_COLS = 1024
GLA_COLS = 1552
GLA_SEG = 1664
MIX_COLS = RW_COLS + GLA_COLS
N_EXPERTS = 256
TOP_K = 8
N_GROUPS = 8
TOPK_GROUPS = 4
EXPERT_DIM = 256
ROUTED_SCALE = 2.5
LN_EPS = 1e-5
DEPTH = 1
ALPHA = (2 * DEPTH) ** 0.25

CHUNK = 64
GLA_SUB = 16
ROW_BLOCK = 256
VMEM_LIMIT = 48 * 1024 * 1024

N_ADA = 3
N_PROJ = 1
N_LORA = 1
N_SEG = 2
N_CUM = 2
N_RW = 1
N_GLA = 1
N_GLA_INTRA = 1
N_MERGE = 1
N_ROUTER = 3
N_EXPERT = 1
N_SHARED = 1


def _split(x, n):
    parts = []
    r = x
    for i in range(n):
        p = r.astype(BF16)
        parts.append(p)
        if i < n - 1:
            r = r - p.astype(F32)
    return parts


def _dot(a, b, ta=False, tb=False):
    dn = (((0 if ta else 1,), (1 if tb else 0,)), ((), ()))
    return lax.dot_general(a, b, dn, preferred_element_type=F32)


def _mmp(ap, bp, ta=False, tb=False):
    n = max(len(ap), len(bp))
    out = None
    for i in range(len(ap)):
        for j in range(len(bp)):
            if i + j <= n - 1:
                t = _dot(ap[i], bp[j], ta, tb)
                out = t if out is None else out + t
    return out


def _mm(a, b, n, ta=False, tb=False):
    return _mmp(_split(a, n), _split(b, n), ta, tb)


def _mm_exact_l(m_bf16, x, n):
    return _mmp([m_bf16], _split(x, n))


def _mm_exact_r(x, m_bf16, n):
    return _mmp(_split(x, n), [m_bf16])


def _sigmoid(x):
    return 1.0 / (1.0 + jnp.exp(-x))


def _silu(x):
    return x * _sigmoid(x)


def _log_sigmoid(x):
    return jnp.minimum(x, 0.0) - jnp.log(1.0 + jnp.exp(-jnp.abs(x)))


def _cparams(sem):
    return pltpu.CompilerParams(dimension_semantics=sem, vmem_limit_bytes=VMEM_LIMIT)


def _full(shape):
    nd = len(shape)
    return pl.BlockSpec(shape, lambda *a: (0,) * nd)


def _ada_body(c_ref, w_ref, b_ref, o_ref):
    s = _silu(c_ref[...])
    o_ref[...] = _mm(s, w_ref[...], N_ADA) + b_ref[...]


def _ada(cc, w, b):
    rows, d = cc.shape
    n = w.shape[1] // d
    return pl.pallas_call(
        _ada_body,
        grid=(n,),
        in_specs=[pl.BlockSpec((rows, d), lambda j: (0, 0)),
                  pl.BlockSpec((d, d), lambda j: (0, j)),
                  pl.BlockSpec((1, d), lambda j: (0, j))],
        out_specs=pl.BlockSpec((rows, d), lambda j: (0, j)),
        out_shape=jax.ShapeDtypeStruct((rows, w.shape[1]), F32),
        compiler_params=_cparams(("arbitrary",)),
        name="ada",
    )(cc, w, b.reshape(1, -1))


def _proj_body(act, x_ref, mod_ref, *refs):
    w_refs, o_ref = refs[:-1], refs[-1]
    sh = mod_ref[0, 0:1, :]
    sc = mod_ref[0, 1:2, :]
    h = x_ref[0] * (1.0 + sc) + sh
    y = _mmp(_split(h, N_PROJ), [w[...] for w in w_refs])
    if act:
        y = _sigmoid(y)
    o_ref[0] = y


def _proj(x, mod, w, act, tm):
    B, L, D = x.shape
    n = w.shape[1]
    wp = _split(w, N_PROJ)
    return pl.pallas_call(
        functools.partial(_proj_body, act),
        grid=(B, L // tm),
        in_specs=[pl.BlockSpec((1, tm, D), lambda b, i: (b, i, 0)),
                  pl.BlockSpec((1, 6, D), lambda b, i: (b, 0, 0))]
                 + [pl.BlockSpec((D, n), lambda b, i: (0, 0))] * N_PROJ,
        out_specs=pl.BlockSpec((1, tm, n), lambda b, i: (b, i, 0)),
        out_shape=jax.ShapeDtypeStruct((B, L, n), F32),
        compiler_params=_cparams(("parallel", "arbitrary")),
        name="proj",
    )(x, mod, *wp)


def _neighbours(vertical, width, tm, cur, prev_ref, next_ref):
    if vertical:
        i = pl.program_id(1)
        n = pl.num_programs(1)
        prev = jnp.where(i > 0, prev_ref[0], 0.0)
        nxt = jnp.where(i < n - 1, next_ref[0], 0.0)
        pad = jnp.zeros((8, cur.shape[1]), F32)
        ext = jnp.concatenate([pad, prev, cur, nxt, pad], axis=0)
        off = GRID_W + 8
    else:
        pad = jnp.zeros((8, cur.shape[1]), F32)
        ext = jnp.concatenate([pad, cur, pad], axis=0)
        off = 8
    col = lax.broadcasted_iota(jnp.int32, (tm, 1), 0) % width

    def get(dr, dc):
        s = off + GRID_W * dr + dc
        v = ext[s:s + tm]
        if dc == -1:
            v = jnp.where(col == 0, 0.0, v)
        elif dc == 1:
            v = jnp.where(col == width - 1, 0.0, v)
        return v

    return get


def _feat_rw_body(vertical, width, tm, *refs):
    if vertical:
        prev_ref, cur_ref, next_ref = refs[:3]
        refs = refs[3:]
    else:
        cur_ref = refs[0]
        prev_ref = next_ref = None
        refs = refs[1:]
    (mu_ref, lw_hi_ref, lw_lo_ref, w0_ref, a0_ref, kk_ref, ka_ref, rk_ref, bd_ref,
     r_out, k_out, v_out, kkn_out, g_out, bonus_out, lwd_out, a_out) = refs
    cur = cur_ref[0]
    get = _neighbours(vertical, width, tm, cur, prev_ref, next_ref)
    left, right = get(0, -1), get(0, 1)
    up, down = (get(-1, 0), get(1, 0)) if vertical else (left, right)
    l4 = lax.broadcasted_iota(jnp.int32, (1, cur.shape[1]), 1) % 4
    shifted = jnp.where(l4 == 0, left, jnp.where(l4 == 1, right, jnp.where(l4 == 2, up, down)))
    p = cur + mu_ref[...] * (shifted - cur)

    r = p[:, 0:RW_WIDTH]
    k = p[:, RW_WIDTH:2 * RW_WIDTH]
    v = p[:, 2 * RW_WIDTH:3 * RW_WIDTH]
    slab = p[:, 3 * RW_WIDTH:3 * RW_WIDTH + 256]
    ln = lax.broadcasted_iota(jnp.int32, (1, 256), 1)
    e1 = RW_DECAY_RANK
    e2 = e1 + RW_AAA_RANK
    e3 = e2 + RW_GATE_RANK
    slab = jnp.where(ln < e1, jnp.tanh(slab),
                     jnp.where(ln < e2, slab, jnp.where(ln < e3, _sigmoid(slab), 0.0)))
    lo = _mmp(_split(slab, N_LORA), [lw_hi_ref[...], lw_lo_ref[...]][:N_LORA])
    W = RW_WIDTH
    a_sum = None
    for d in range(2):
        z = w0_ref[:, d * W:(d + 1) * W] + lo[:, d * W:(d + 1) * W]
        lwd_out[d, 0] = -_sigmoid(z) * math.exp(-0.5)
        a = _sigmoid(a0_ref[:, d * W:(d + 1) * W] + lo[:, (2 + d) * W:(3 + d) * W])
        a_out[d, 0] = a
        a_sum = a if a_sum is None else a_sum + a
    g_out[0] = lo[:, 4 * W:5 * W]
    bd = bd_ref[...]
    kk = k * kk_ref[...]
    ss = _mm_exact_r(kk * kk, bd, N_SEG)
    kkn_out[0] = kk / jnp.maximum(jnp.sqrt(ss), 1e-12)
    kmod_sum = k * (2.0 + (a_sum - 2.0) * ka_ref[...])
    bonus_out[0] = _mm_exact_r(r * kmod_sum * rk_ref[...], bd, N_SEG) * v
    r_out[0] = r
    k_out[0] = k
    v_out[0] = v


def _feat_rw(p, vertical, tm, consts):
    B, L, S = p.shape
    width = GRID_W if vertical else tm
    W = RW_WIDTH
    hb = tm // GRID_W
    nhb = L // GRID_W
    tile = pl.BlockSpec((1, tm, S), lambda b, i: (b, i, 0))
    if vertical:
        in_specs = [pl.BlockSpec((1, GRID_W, S), lambda b, i: (b, jnp.maximum(i * hb - 1, 0), 0)),
                    tile,
                    pl.BlockSpec((1, GRID_W, S), lambda b, i: (b, jnp.minimum((i + 1) * hb, nhb - 1), 0))]
        args = [p, p, p]
    else:
        in_specs = [tile]
        args = [p]
    in_specs += [_full(c.shape) for c in consts]
    o1 = pl.BlockSpec((1, tm, W), lambda b, i: (b, i, 0))
    o2 = pl.BlockSpec((2, 1, tm, W), lambda b, i: (0, b, i, 0))
    s1 = jax.ShapeDtypeStruct((B, L, W), F32)
    s2 = jax.ShapeDtypeStruct((2, B, L, W), F32)
    return pl.pallas_call(
        functools.partial(_feat_rw_body, vertical, width, tm),
        grid=(B, L // tm),
        in_specs=in_specs,
        out_specs=[o1] * 6 + [o2] * 2,
        out_shape=[s1] * 6 + [s2] * 2,
        compiler_params=_cparams(("parallel", "arbitrary")),
        name="feat_rw",
    )(*args, *consts)


def _feat_gla_body(vertical, width, tm, *refs):
    if vertical:
        prev_ref, cur_ref, next_ref = refs[:3]
        refs = refs[3:]
    else:
        cur_ref = refs[0]
        prev_ref = next_ref = None
        refs = refs[1:]
    cw_ref, g2_hi_ref, g2_lo_ref, gb_ref, qk_out, v_out, la_out = refs
    cur = cur_ref[0][:, 0:GLA_QKV_COLS]
    if vertical:
        class _Slice:
            def __init__(self, ref):
                self.ref = ref

            def __getitem__(self, idx):
                return self.ref[idx][:, 0:GLA_QKV_COLS]
        get = _neighbours(True, width, tm, cur, _Slice(prev_ref), _Slice(next_ref))
    else:
        get = _neighbours(False, width, tm, cur, None, None)
    acc = None
    for dr in ((-1, 0, 1) if vertical else (0,)):
        for dc in (-1, 0, 1):
            t = get(dr, dc) * cw_ref[(dr + 1) * 3 + (dc + 1):(dr + 1) * 3 + (dc + 1) + 1, :]
            acc = t if acc is None else acc + t
    qkv = _silu(acc)
    kw = GLA_KEY_WIDTH
    lane = lax.broadcasted_iota(jnp.int32, (1, 2 * kw), 1)
    qk_out[0] = qkv[:, 0:2 * kw] * jnp.where(lane < kw, GLA_KEY_DIM ** -0.5, 1.0)
    v_out[0] = qkv[:, 2 * kw:]
    pgl = cur_ref[0][:, GLA_QKV_COLS + GLA_VAL_WIDTH:GLA_SEG]
    z = _mmp(_split(pgl, N_LORA), [g2_hi_ref[...], g2_lo_ref[...]][:N_LORA]) + gb_ref[...]
    la = _log_sigmoid(z) * (1.0 / GLA_TAU)
    la_out[0, 0] = la[:, 0:kw]
    la_out[1, 0] = la[:, kw:]


def _feat_gla(p, vertical, tm, consts):
    B, L, S = p.shape
    width = GRID_W if vertical else tm
    hb = tm // GRID_W
    nhb = L // GRID_W
    tile = pl.BlockSpec((1, tm, S), lambda b, i: (b, i, 0))
    if vertical:
        in_specs = [pl.BlockSpec((1, GRID_W, S), lambda b, i: (b, jnp.maximum(i * hb - 1, 0), 0)),
                    tile,
                    pl.BlockSpec((1, GRID_W, S), lambda b, i: (b, jnp.minimum((i + 1) * hb, nhb - 1), 0))]
        args = [p, p, p]
    else:
        in_specs = [tile]
        args = [p]
    in_specs += [_full(c.shape) for c in consts]
    kw, vw = GLA_KEY_WIDTH, GLA_VAL_WIDTH
    return pl.pallas_call(
        functools.partial(_feat_gla_body, vertical, width, tm),
        grid=(B, L // tm),
        in_specs=in_specs,
        out_specs=[pl.BlockSpec((1, tm, 2 * kw), lambda b, i: (b, i, 0)),
                   pl.BlockSpec((1, tm, vw), lambda b, i: (b, i, 0)),
                   pl.BlockSpec((2, 1, tm, kw), lambda b, i: (0, b, i, 0))],
        out_shape=[jax.ShapeDtypeStruct((B, L, 2 * kw), F32),
                   jax.ShapeDtypeStruct((B, L, vw), F32),
                   jax.ShapeDtypeStruct((2, B, L, kw), F32)],
        compiler_params=_cparams(("parallel", "arbitrary")),
        name="feat_gla",
    )(*args, *consts)


def _order_masks(n, sgn):
    ti = lax.broadcasted_iota(jnp.int32, (n, n), 0)
    si = lax.broadcasted_iota(jnp.int32, (n, n), 1)
    rel = (si - ti) * sgn
    return rel < 0, rel <= 0


def _chunk_index(nc):
    return lambda b, d, c: c + d * (nc - 1 - 2 * c)


def _rwkv_body(r_ref, k_ref, v_ref, kk_ref, lw_ref, a_ref, ka_ref, s0_ref, y_ref, st_ref, s_scr):
    C = CHUNK
    d = pl.program_id(1)
    c = pl.program_id(2)

    @pl.when(c == 0)
    def _():
        s_scr[...] = s0_ref[0, 0]

    sgn = 1 - 2 * d
    r, k, v, kk = r_ref[0], k_ref[0], v_ref[0], kk_ref[0]
    lw, a = lw_ref[0, 0], a_ref[0, 0]
    before, upto = _order_masks(C, sgn)
    cum = _mm_exact_l(upto.astype(BF16), lw, N_CUM)
    tot = jnp.where(d == 0, cum[C - 1:C], cum[0:1])
    p_in = jnp.exp(cum)
    p_ex = jnp.exp(cum - lw)
    p_inv = jnp.exp(-cum)
    p_rem = jnp.exp(tot - cum)
    p_all = jnp.exp(tot)
    bvec = kk * a
    kmod = k * (1.0 + (a - 1.0) * ka_ref[...])
    a_t = -kk * p_ex
    r_t = r * p_in
    b_t = bvec * p_inv
    k_t = kmod * p_inv
    b_p = bvec * p_rem
    k_p = kmod * p_rem

    P = 2 * RW_HEAD_DIM
    lane = lax.broadcasted_iota(jnp.int32, (C, P), 1)
    h0 = lane < RW_HEAD_DIM
    ri = lax.broadcasted_iota(jnp.int32, (P, P), 0)
    ci = lax.broadcasted_iota(jnp.int32, (P, P), 1)
    same = (ri // C) == (ci // C)
    rel = ((ci % C) - (ri % C)) * sgn
    strict = same & (rel < 0)
    eye = (ri == ci).astype(F32)

    def stack2(x):
        return jnp.concatenate([jnp.where(h0, x, 0.0), jnp.where(h0, 0.0, x)], axis=0)

    pairs = range(RW_HEADS // 2)
    sls = [slice(p * P, (p + 1) * P) for p in pairs]
    a_st = [stack2(a_t[:, sl]) for sl in sls]
    v_st = [stack2(v[:, sl]) for sl in sls]
    g = [_mm(jnp.concatenate([a_st[p], stack2(r_t[:, sls[p]])], axis=0),
             jnp.concatenate([stack2(b_t[:, sls[p]]), stack2(k_t[:, sls[p]])], axis=0), N_RW, tb=True)
         for p in pairs]
    nmat = [jnp.where(strict, g[p][0:P, 0:P], 0.0) for p in pairs]
    a_ak = [jnp.where(strict, g[p][0:P, P:], 0.0) for p in pairs]
    ri2 = lax.broadcasted_iota(jnp.int32, (P, 2 * P), 0)
    ci2 = lax.broadcasted_iota(jnp.int32, (P, 2 * P), 1)
    incl2 = ((ri2 // C) == ((ci2 % P) // C)) & ((((ci2 % C) - (ri2 % C)) * sgn) <= 0)
    a_r = [jnp.where(incl2, g[p][P:, :], 0.0) for p in pairs]
    akv = [_mm(a_ak[p], v_st[p], N_RW) for p in pairs]
    t = [eye + nmat[p] for p in pairs]
    npow = nmat
    for _ in range(max(C.bit_length() - 2, 0)):
        npow = [_mm(npow[p], npow[p], N_RW) for p in pairs]
        t = [t[p] + _mm(t[p], npow[p], N_RW) for p in pairs]
    x = [_mm(t[p], jnp.concatenate([a_st[p], akv[p]], axis=1), N_RW) for p in pairs]
    s = [s_scr[p] for p in pairs]
    ur = [_mm(jnp.concatenate([x[p][0:C, 0:P] + x[p][C:, 0:P], r_t[:, sls[p]]], axis=0), s[p], N_RW, tb=True)
          for p in pairs]
    u = [ur[p][0:C] + x[p][0:C, P:] + x[p][C:, P:] for p in pairs]
    y_st = [_mm(a_r[p], jnp.concatenate([stack2(u[p]), v_st[p]], axis=0), N_RW) for p in pairs]
    for p in pairs:
        y_ref[0, 0, :, sls[p]] = ur[p][C:] + y_st[p][0:C] + y_st[p][C:]
    upd = [_mm(jnp.concatenate([u[p], v[:, sls[p]]], axis=0).T,
               jnp.concatenate([b_p[:, sls[p]], k_p[:, sls[p]]], axis=0), N_RW) for p in pairs]
    for p in pairs:
        s_new = s[p] * p_all[:, sls[p]] + jnp.where(same, upd[p], 0.0)
        s_scr[p] = s_new
        st_ref[0, 0, p] = s_new


def _rwkv_scan(r, k, v, kk, lw, a, ka, s0):
    B, L, W = r.shape
    nc = L // CHUNK
    cidx = _chunk_index(nc)
    shared = pl.BlockSpec((1, CHUNK, W), lambda b, d, c: (b, cidx(b, d, c), 0))
    perdir = pl.BlockSpec((1, 1, CHUNK, W), lambda b, d, c: (d, b, cidx(b, d, c), 0))
    npair = RW_HEADS // 2
    P = 2 * RW_HEAD_DIM
    sspec = pl.BlockSpec((1, 1, npair, P, P), lambda b, d, c: (b, d, 0, 0, 0))
    return pl.pallas_call(
        _rwkv_body,
        grid=(B, 2, nc),
        in_specs=[shared, shared, shared, shared, perdir, perdir,
                  pl.BlockSpec((1, W), lambda b, d, c: (0, 0)), sspec],
        out_specs=[pl.BlockSpec((1, 1, CHUNK, W), lambda b, d, c: (d, b, cidx(b, d, c), 0)), sspec],
        out_shape=[jax.ShapeDtypeStruct((2, B, L, W), F32),
                   jax.ShapeDtypeStruct((B, 2, npair, P, P), F32)],
        scratch_shapes=[pltpu.VMEM((npair, P, P), F32)],
        compiler_params=_cparams(("parallel", "parallel", "arbitrary")),
        name="rwkv_scan",
    )(r, k, v, kk, lw, a, ka, s0)


def _gla_body(qk_ref, v_ref, g_ref, hm_ref, s0_ref, o_ref, st_ref, s_scr):
    C = CHUNK
    SB = GLA_SUB
    H = GLA_HEADS
    d = pl.program_id(1)
    c = pl.program_id(2)

    @pl.when(c == 0)
    def _():
        s_scr[...] = s0_ref[0, 0]

    sgn = 1 - 2 * d
    kw, vw = GLA_KEY_WIDTH, GLA_VAL_WIDTH
    q = qk_ref[0][:, 0:kw]
    k = qk_ref[0][:, kw:]
    v = v_ref[0]
    g = g_ref[0, 0]
    _, upto = _order_masks(C, sgn)
    b = _mm_exact_l(upto.astype(BF16), g, N_CUM)
    pos = lax.broadcasted_iota(jnp.int32, (C, 1), 0) * sgn + d * (C - 1)

    def b_at(p):
        return jnp.sum(jnp.where(pos == p, b, 0.0), axis=0, keepdims=True)

    NEG = -jnp.inf
    tot = b_at(C - 1)
    s = s_scr[...]
    inter = _mm(q * jnp.exp(b), s, N_GLA, tb=True)

    half = pos >= C // 2
    odd = (pos // SB) % 2 == 1
    r1 = b_at(C // 2 - 1)
    r2 = jnp.where(half, b_at(3 * SB - 1), b_at(SB - 1))
    q1 = q * jnp.exp(jnp.where(half, b - r1, NEG))
    k1 = k * jnp.exp(jnp.where(half, NEG, r1 - b))
    q2 = q * jnp.exp(jnp.where(odd, b - r2, NEG))
    k2 = k * jnp.exp(jnp.where(odd, NEG, r2 - b))
    lane_h = lax.broadcasted_iota(jnp.int32, (C, kw), 1) // GLA_KEY_DIM

    def stack_heads(x):
        return jnp.concatenate([jnp.where(lane_h == h, x, 0.0) for h in range(H)], axis=0)

    att1 = _mm(stack_heads(q1), k1, N_GLA, tb=True)
    att2 = _mm(stack_heads(q2), k2, N_GLA, tb=True)
    pr = (lax.broadcasted_iota(jnp.int32, (H * C, C), 0) % C) * sgn + d * (C - 1)
    pc = lax.broadcasted_iota(jnp.int32, (H * C, C), 1) * sgn + d * (C - 1)
    att = att1 + jnp.where((pr >= C // 2) == (pc >= C // 2), att2, 0.0)
    res = _mm(att, v, N_GLA)
    lane_hv = lax.broadcasted_iota(jnp.int32, (C, vw), 1) // GLA_VAL_DIM
    off = None
    for h in range(H):
        t = jnp.where(lane_hv == h, res[h * C:(h + 1) * C], 0.0)
        off = t if off is None else off + t

    hm = hm_ref[...]
    il = lax.broadcasted_iota(jnp.int32, (SB, 1), 0)
    diag = []
    for blk in range(C // SB):
        rs = slice(blk * SB, (blk + 1) * SB)
        bb, qb, kb, vb = b[rs], q[rs], k[rs], v[rs]
        pieces = []
        for j in range(SB):
            valid = (il - j) * sgn >= 0
            pieces.append(jnp.exp(jnp.where(valid, bb - bb[j:j + 1], NEG)) * qb * kb[j:j + 1])
        a = _mm_exact_r(jnp.concatenate(pieces, axis=0), hm, N_GLA_INTRA)
        acc = None
        for j in range(SB):
            t = a[j * SB:(j + 1) * SB] * vb[j:j + 1]
            acc = t if acc is None else acc + t
        diag.append(acc)
    o_ref[0, 0] = inter + off + jnp.concatenate(diag, axis=0)

    upd = _mm(v.T, k * jnp.exp(tot - b), N_GLA)
    ri = lax.broadcasted_iota(jnp.int32, upd.shape, 0) // GLA_VAL_DIM
    ci = lax.broadcasted_iota(jnp.int32, upd.shape, 1) // GLA_KEY_DIM
    s_new = s * jnp.exp(tot) + jnp.where(ri == ci, upd, 0.0)
    s_scr[...] = s_new
    st_ref[0, 0] = s_new


def _gla_scan(qk, v, la, hm, s0):
    B, L, _ = qk.shape
    kw, vw = GLA_KEY_WIDTH, GLA_VAL_WIDTH
    nc = L // CHUNK
    cidx = _chunk_index(nc)
    sspec = pl.BlockSpec((1, 1, vw, kw), lambda b, d, c: (b, d, 0, 0))
    return pl.pallas_call(
        _gla_body,
        grid=(B, 2, nc),
        in_specs=[pl.BlockSpec((1, CHUNK, 2 * kw), lambda b, d, c: (b, cidx(b, d, c), 0)),
                  pl.BlockSpec((1, CHUNK, vw), lambda b, d, c: (b, cidx(b, d, c), 0)),
                  pl.BlockSpec((1, 1, CHUNK, kw), lambda b, d, c: (d, b, cidx(b, d, c), 0)),
                  pl.BlockSpec((kw, vw), lambda b, d, c: (0, 0)), sspec],
        out_specs=[pl.BlockSpec((1, 1, CHUNK, vw), lambda b, d, c: (d, b, cidx(b, d, c), 0)), sspec],
        out_shape=[jax.ShapeDtypeStruct((2, B, L, vw), F32),
                   jax.ShapeDtypeStruct((B, 2, vw, kw), F32)],
        scratch_shapes=[pltpu.VMEM((vw, kw), F32)],
        compiler_params=_cparams(("parallel", "parallel", "arbitrary")),
        name="gla_scan",
    )(qk, v, la, hm, s0)


def _seg_norm(y, bd, dim, eps):
    mu = _mm_exact_r(y, bd, N_SEG) * (1.0 / dim)
    dlt = y - mu
    var = _mm_exact_r(dlt * dlt, bd, N_SEG) * (1.0 / dim)
    return dlt * lax.rsqrt(var + eps)


def _layer_norm(x, w, b):
    mu = jnp.mean(x, axis=-1, keepdims=True)
    dlt = x - mu
    var = jnp.mean(dlt * dlt, axis=-1, keepdims=True)
    return dlt * lax.rsqrt(var + LN_EPS) * w + b


def _merge_body(x_ref, mod_ref, yrw_ref, bonus_ref, g_ref, ygla_ref, og_ref, gate_ref,
                bd64_ref, bd128_ref, rgw_ref, rgb_ref, ggw_ref, ggb_ref, *refs):
    n = N_MERGE
    wrw = [r[...] for r in refs[0:n]]
    wgla = [r[...] for r in refs[n:2 * n]]
    wout = [r[...] for r in refs[2 * n:3 * n]]
    ln_w_ref, ln_b_ref, x1_ref, h2_ref = refs[3 * n:]
    y = _seg_norm(yrw_ref[0, 0] + yrw_ref[1, 0], bd64_ref[...], RW_HEAD_DIM, RW_GN_EPS)
    y = (y * rgw_ref[...] + rgb_ref[...] + bonus_ref[0]) * g_ref[0]
    y_rw = _mmp(_split(y, n), wrw)
    y = _seg_norm(ygla_ref[0, 0] + ygla_ref[1, 0], bd128_ref[...], GLA_VAL_DIM, LN_EPS)
    y = (y * ggw_ref[...] + ggb_ref[...]) * _silu(og_ref[0])
    y_gla = _mmp(_split(y, n), wgla)
    gate = gate_ref[0]
    mixed = gate[:, 0:D_MODEL] * y_rw + gate[:, D_MODEL:] * y_gla
    mix = _mmp(_split(mixed, n), wout)
    g1 = mod_ref[0, 2:3, :]
    x1 = _layer_norm(ALPHA * x_ref[0] + g1 * mix, ln_w_ref[...], ln_b_ref[...])
    x1_ref[0] = x1
    h2_ref[0] = x1 * (1.0 + mod_ref[0, 4:5, :]) + mod_ref[0, 3:4, :]


def _merge(x, mod, yrw, bonus, g, ygla, p_gla, gate, consts, weights, ln_w, ln_b, tm):
    B, L, D = x.shape
    W = RW_WIDTH
    tok = lambda w: pl.BlockSpec((1, tm, w), lambda b, i: (b, i, 0))
    dirs = pl.BlockSpec((2, 1, tm, W), lambda b, i: (0, b, i, 0))
    wl = [w for ws in weights for w in ws]
    return pl.pallas_call(
        _merge_body,
        grid=(B, L // tm),
        in_specs=[tok(D), pl.BlockSpec((1, 6, D), lambda b, i: (b, 0, 0)), dirs, tok(W), tok(W), dirs,
                  pl.BlockSpec((1, tm, W), lambda b, i: (b, i, GLA_QKV_COLS // W)), tok(2 * D)]
                 + [_full(c.shape) for c in consts] + [_full(w.shape) for w in wl]
                 + [_full(ln_w.shape), _full(ln_b.shape)],
        out_specs=[tok(D), tok(D)],
        out_shape=[jax.ShapeDtypeStruct((B, L, D), F32)] * 2,
        compiler_params=_cparams(("parallel", "arbitrary")),
        name="merge",
    )(x, mod, yrw, bonus, g, ygla, p_gla, gate, *consts, *wl, ln_w, ln_b)


def _router_body(h_ref, *refs):
    n = N_ROUTER
    rt = [r[...] for r in refs[0:n]]
    bias_ref, e_out, g_out, cnt_out = refs[n:]
    tm = h_ref.shape[0]
    E, G, PG = N_EXPERTS, N_GROUPS, N_EXPERTS // N_GROUPS
    logits = _mmp(rt, _split(h_ref[...], n), tb=True)
    scores = _sigmoid(logits)
    sel = scores + bias_ref[:, 0:tm]
    NEG = -jnp.inf
    ip = lax.broadcasted_iota(jnp.int32, (PG, tm), 0)
    group_rows = []
    for gidx in range(G):
        sg = sel[gidx * PG:(gidx + 1) * PG]
        m1 = jnp.max(sg, axis=0, keepdims=True)
        first = jnp.min(jnp.where(sg == m1, ip, PG), axis=0, keepdims=True)
        m2 = jnp.max(jnp.where(ip == first, NEG, sg), axis=0, keepdims=True)
        group_rows.append(m1 + m2)
    gs = jnp.concatenate(group_rows, axis=0)
    gi = lax.broadcasted_iota(jnp.int32, (G, tm), 0)
    keep = jnp.zeros((G, tm), F32)
    for _ in range(TOPK_GROUPS):
        m = jnp.max(gs, axis=0, keepdims=True)
        idx = jnp.min(jnp.where(gs == m, gi, G), axis=0, keepdims=True)
        hit = gi == idx
        keep = jnp.where(hit, 1.0, keep)
        gs = jnp.where(hit, NEG, gs)
    cur = jnp.concatenate(
        [jnp.where(keep[gidx:gidx + 1] > 0.5, sel[gidx * PG:(gidx + 1) * PG], NEG) for gidx in range(G)],
        axis=0)
    ei = lax.broadcasted_iota(jnp.int32, (E, tm), 0)
    idxs, gates = [], []
    picked = jnp.zeros((E, tm), F32)
    for _ in range(TOP_K):
        m = jnp.max(cur, axis=0, keepdims=True)
        idx = jnp.min(jnp.where(cur == m, ei, E), axis=0, keepdims=True)
        hit = ei == idx
        idxs.append(idx)
        gates.append(jnp.sum(jnp.where(hit, scores, 0.0), axis=0, keepdims=True))
        cur = jnp.where(hit, NEG, cur)
        picked = jnp.where(hit, 1.0, picked)
    gate = jnp.concatenate(gates, axis=0)
    e_out[...] = jnp.concatenate(idxs, axis=0)
    g_out[...] = gate / jnp.sum(gate, axis=0, keepdims=True) * ROUTED_SCALE

    @pl.when(pl.program_id(0) == 0)
    def _():
        cnt_out[...] = jnp.zeros(cnt_out.shape, F32)

    part = picked[:, 0:128]
    for j in range(1, tm // 128):
        part = part + picked[:, j * 128:(j + 1) * 128]
    cnt_out[...] += part


def _router(h, router_t_parts, bias_b, tm):
    n, D = h.shape
    E = N_EXPERTS
    return pl.pallas_call(
        _router_body,
        grid=(n // tm,),
        in_specs=[pl.BlockSpec((tm, D), lambda i: (i, 0))]
                 + [pl.BlockSpec((E, D), lambda i: (0, 0))] * len(router_t_parts)
                 + [pl.BlockSpec(bias_b.shape, lambda i: (0, 0))],
        out_specs=[pl.BlockSpec((TOP_K, tm), lambda i: (0, i))] * 2
                  + [pl.BlockSpec((E, 128), lambda i: (0, 0))],
        out_shape=[jax.ShapeDtypeStruct((TOP_K, n), jnp.int32), jax.ShapeDtypeStruct((TOP_K, n), F32),
                   jax.ShapeDtypeStruct((E, 128), F32)],
        compiler_params=_cparams(("arbitrary",)),
        name="router",
    )(h, *router_t_parts, bias_b)


def _row_gather(idx_ref, src_hbm, dst_ref, sem, rows):
    def issue(r, carry):
        pltpu.make_async_copy(src_hbm.at[pl.ds(idx_ref[r], 1)], dst_ref.at[pl.ds(r, 1)], sem).start()
        return carry
    lax.fori_loop(0, rows, issue, 0, unroll=8)


def _row_gather_wait(src_hbm, dst_ref, sem, rows):
    pltpu.make_async_copy(src_hbm.at[pl.ds(0, rows)], dst_ref, sem).wait()


def _experts_body(be_ref, nu_ref, idx_ref, idx_next_ref, h_hbm, wgu_ref, wd_ref, y_ref, xbuf, sem):
    i = pl.program_id(0)
    nb = pl.num_programs(0)
    nused = nu_ref[0]
    slot = i % 2
    R = ROW_BLOCK

    @pl.when(i == 0)
    def _():
        _row_gather(idx_ref.at[0, 0], h_hbm, xbuf.at[0], sem.at[0], R)

    @pl.when((i + 1 < nb) & (i + 1 < nused))
    def _():
        _row_gather(idx_next_ref.at[0, 0], h_hbm, xbuf.at[1 - slot], sem.at[1 - slot], R)

    @pl.when((i < nused) | (i == 0))
    def _():
        _row_gather_wait(h_hbm, xbuf.at[slot], sem.at[slot], R)

    @pl.when(i < nused)
    def _():
        xb = xbuf[slot]
        F = EXPERT_DIM
        gu = _mm(xb, wgu_ref[0], N_EXPERT)
        act = _silu(gu[:, 0:F]) * gu[:, F:]
        y_ref[...] = _mm(act, wd_ref[0], N_EXPERT)

    @pl.when(i >= nused)
    def _():
        y_ref[...] = jnp.zeros(y_ref.shape, F32)


def _experts(block_e, nused, row_tok, h, w_gate_up, w_down):
    nb = block_e.shape[0]
    n, D = h.shape
    R = ROW_BLOCK
    F2 = w_gate_up.shape[2]
    idx3 = row_tok.reshape(nb, 1, R)
    grid_spec = pltpu.PrefetchScalarGridSpec(
        num_scalar_prefetch=2,
        grid=(nb,),
        in_specs=[pl.BlockSpec((1, 1, R), lambda i, be, nu: (i, 0, 0), memory_space=pltpu.SMEM),
                  pl.BlockSpec((1, 1, R), lambda i, be, nu: (jnp.minimum(i + 1, nb - 1), 0, 0),
                               memory_space=pltpu.SMEM),
                  pl.BlockSpec(memory_space=pl.ANY),
                  pl.BlockSpec((1, D, F2), lambda i, be, nu: (be[i], 0, 0)),
                  pl.BlockSpec((1, F2 // 2, D), lambda i, be, nu: (be[i], 0, 0))],
        out_specs=pl.BlockSpec((R, D), lambda i, be, nu: (i, 0)),
        scratch_shapes=[pltpu.VMEM((2, R, D), F32), pltpu.SemaphoreType.DMA((2,))],
    )
    return pl.pallas_call(
        _experts_body,
        grid_spec=grid_spec,
        out_shape=jax.ShapeDtypeStruct((nb * R, D), F32),
        compiler_params=_cparams(("arbitrary",)),
        name="experts",
    )(block_e, nused, idx3, idx3, h, w_gate_up, w_down)


def _final_body(pos_ref, pos_next_ref, y_hbm, x1_ref, h2_ref, mod_ref, gate_ref, *refs):
    n = N_SHARED
    sgu = [r[...] for r in refs[0:n]]
    sd = [r[...] for r in refs[n:2 * n]]
    ln_w_ref, ln_b_ref, o_ref, ybuf, sem = refs[2 * n:]
    b = pl.program_id(0)
    i = pl.program_id(1)
    nb, ni = pl.num_programs(0), pl.num_programs(1)
    step = b * ni + i
    last = nb * ni - 1
    slot = step % 2
    tm = x1_ref.shape[1]
    rows = tm * TOP_K

    @pl.when(step == 0)
    def _():
        _row_gather(pos_ref.at[0, 0], y_hbm, ybuf.at[0], sem.at[0], rows)

    @pl.when(step < last)
    def _():
        _row_gather(pos_next_ref.at[0, 0], y_hbm, ybuf.at[1 - slot], sem.at[1 - slot], rows)

    _row_gather_wait(y_hbm, ybuf.at[slot], sem.at[slot], rows)
    gate = gate_ref[0]
    routed = None
    for kk in range(TOP_K):
        t = ybuf[slot, kk * tm:(kk + 1) * tm, :] * gate[:, kk:kk + 1]
        routed = t if routed is None else routed + t
    h2 = h2_ref[0]
    F = sgu[0].shape[1] // 2
    gu = _mmp(_split(h2, n), sgu)
    act = _silu(gu[:, 0:F]) * gu[:, F:]
    shared = _mmp(_split(act, n), sd)
    g2 = mod_ref[0, 5:6, :]
    o_ref[0] = _layer_norm(ALPHA * x1_ref[0] + g2 * (routed + shared), ln_w_ref[...], ln_b_ref[...])


def _final(pos, y, x1, h2, mod, gate, sgu, sd, ln_w, ln_b, tm):
    B, L, D = x1.shape
    ni = L // tm
    nt = B * ni
    rows = tm * TOP_K
    pos3 = pos.reshape(nt, 1, rows)
    tok = lambda w: pl.BlockSpec((1, tm, w), lambda b, i: (b, i, 0))
    ws = list(sgu) + list(sd)
    return pl.pallas_call(
        _final_body,
        grid=(B, ni),
        in_specs=[pl.BlockSpec((1, 1, rows), lambda b, i: (b * ni + i, 0, 0), memory_space=pltpu.SMEM),
                  pl.BlockSpec((1, 1, rows), lambda b, i: (jnp.minimum(b * ni + i + 1, nt - 1), 0, 0),
                               memory_space=pltpu.SMEM),
                  pl.BlockSpec(memory_space=pl.ANY),
                  tok(D), tok(D), pl.BlockSpec((1, 6, D), lambda b, i: (b, 0, 0)), tok(TOP_K)]
                 + [_full(w.shape) for w in ws] + [_full(ln_w.shape), _full(ln_b.shape)],
        out_specs=tok(D),
        out_shape=jax.ShapeDtypeStruct((B, L, D), F32),
        scratch_shapes=[pltpu.VMEM((2, rows, D), F32), pltpu.SemaphoreType.DMA((2,))],
        compiler_params=_cparams(("arbitrary", "arbitrary")),
        name="final",
    )(pos3, pos3, y, x1, h2, mod, gate, *ws, ln_w, ln_b)


def _dispatch_plan(experts_t, sizes, tm):
    R, E = ROW_BLOCK, N_EXPERTS
    K, n = experts_t.shape
    nk = K * n
    nb = (nk + E * (R - 1) + R - 1) // R
    ncand = nb * R - nk
    padded = (sizes + R - 1) // R * R
    pad_end = jnp.cumsum(padded)
    tok = jnp.broadcast_to(jnp.arange(n, dtype=jnp.int32)[None, :], (K, n))
    slot = jnp.arange(K, dtype=jnp.int32)[:, None]
    aid = (tok // tm) * (tm * K) + slot * tm + tok % tm
    cand = jnp.arange(ncand, dtype=jnp.int32)
    ce, cp = cand // (R - 1), cand % (R - 1)
    need = jnp.repeat(padded - sizes, R - 1, total_repeat_length=E * (R - 1))
    need = jnp.pad(need, (0, ncand - E * (R - 1)))
    ckey = jnp.where(cp < need, 2 * ce + 1, 2 * E + 1)
    keys = jnp.concatenate([2 * experts_t.reshape(-1), ckey])
    toks = jnp.concatenate([tok.reshape(-1), jnp.zeros((ncand,), jnp.int32)])
    aids = jnp.concatenate([aid.reshape(-1), nk + cand])
    _, row_tok, row_aid = lax.sort((keys, toks, aids), num_keys=1)
    _, pos = lax.sort((row_aid, jnp.arange(nb * R, dtype=jnp.int32)), num_keys=1)
    block_e = jnp.minimum(jnp.searchsorted(pad_end, jnp.arange(nb) * R, side='right'), E - 1).astype(jnp.int32)
    nused = (pad_end[-1] // R).astype(jnp.int32).reshape(1)
    return row_tok, pos[:nk], block_e, nused


def _block_diag_ones(n, blk):
    i = jnp.arange(n) // blk
    return (i[:, None] == i[None, :]).astype(BF16)


def kernel(x, c, ctx, c_ctx, w_ada, b_ada, w_in, rw_mu, rw_w0, rw_w2, rw_a0, rw_a2, rw_g2, rw_k_k, rw_k_a,
           rw_r_k, rw_gn_w, rw_gn_b, gla_conv, gla_g2, gla_gb, gla_gn_w, gla_gn_b, w_br_rw, w_br_gla, w_out,
           ln1_w, ln1_b, router, router_bias, w_gate_up, w_down, sh_gate_up, sh_down, ln2_w, ln2_b):
    B, L, D = x.shape
    CT = ctx.shape[1]
    l = 0
    W = RW_WIDTH
    row = lambda t: t.reshape(1, -1)

    rows = -(-(B + 1) // 8) * 8
    cc = jnp.zeros((rows, D), F32).at[:B].set(c).at[B].set(c_ctx)
    mod = _ada(cc, w_ada[l], b_ada[l])
    mod_lat = mod[:B].reshape(B, 6, D)
    mod_ctx = jnp.broadcast_to(mod[B].reshape(1, 6, D), (B, 6, D))

    w = w_in[l]
    g0 = RW_COLS
    w_rw = jnp.pad(w[:, :RW_COLS], ((0, 0), (0, RW_SEG - RW_COLS)))
    w_gla = jnp.concatenate([w[:, g0:g0 + GLA_QKV_COLS],
                             w[:, g0 + GLA_QKV_COLS + GLA_GATE_RANK:g0 + GLA_COLS],
                             w[:, g0 + GLA_QKV_COLS:g0 + GLA_QKV_COLS + GLA_GATE_RANK],
                             jnp.zeros((D, GLA_SEG - GLA_COLS), F32)], axis=1)
    w_gate = w[:, MIX_COLS:]
    tm_p = min(512, L)
    p_rw = _proj(x, mod_lat, w_rw, False, tm_p)
    p_gla = _proj(x, mod_lat, w_gla, False, tm_p)
    gate = _proj(x, mod_lat, w_gate, True, tm_p)
    pc_rw = _proj(ctx, mod_ctx, w_rw, False, CT)
    pc_gla = _proj(ctx, mod_ctx, w_gla, False, CT)

    mu = jnp.pad(rw_mu[l], (0, RW_SEG - RW_COLS)).reshape(1, -1)
    lora = jnp.zeros((256, 5 * W), F32)
    e1 = RW_DECAY_RANK
    e2 = e1 + RW_AAA_RANK
    e3 = e2 + RW_GATE_RANK
    for d in range(2):
        lora = lora.at[0:e1, d * W:(d + 1) * W].set(rw_w2[l, d])
        lora = lora.at[e1:e2, (2 + d) * W:(3 + d) * W].set(rw_a2[l, d])
    lora = lora.at[e2:e3, 4 * W:].set(rw_g2[l])
    lora_p = (_split(lora, N_LORA) + [jnp.zeros_like(lora, BF16)])[:2]
    bd64 = _block_diag_ones(W, RW_HEAD_DIM)
    rw_consts = [mu, lora_p[0], lora_p[1], rw_w0[l].reshape(1, -1), rw_a0[l].reshape(1, -1),
                 row(rw_k_k[l]), row(rw_k_a[l]), row(rw_r_k[l]), bd64]
    tm_f = min(256, L)
    r, k, v, kkn, g, bonus, lwd, a = _feat_rw(p_rw, True, tm_f, rw_consts)
    rc, kc, vc, kknc, _, _, lwdc, ac = _feat_rw(pc_rw, False, CT, rw_consts)

    g2 = jnp.zeros((GLA_SEG - GLA_QKV_COLS - GLA_VAL_WIDTH, 2 * GLA_KEY_WIDTH), F32)
    g2 = g2.at[:GLA_GATE_RANK].set(jnp.concatenate([gla_g2[l, 0], gla_g2[l, 1]], axis=1))
    g2_p = (_split(g2, N_LORA) + [jnp.zeros_like(g2, BF16)])[:2]
    cw = jnp.pad(gla_conv[l].reshape(9, GLA_QKV_COLS), ((0, 7), (0, 0)))
    gla_consts = [cw, g2_p[0], g2_p[1], gla_gb[l].reshape(1, -1)]
    qk, vg, la = _feat_gla(p_gla, True, tm_f, gla_consts)
    qkc, vgc, lac = _feat_gla(pc_gla, False, CT, gla_consts)

    ka = row(rw_k_a[l])
    P = 2 * RW_HEAD_DIM
    s0 = jnp.zeros((B, 2, RW_HEADS // 2, P, P), F32)
    _, s_ctx = _rwkv_scan(rc, kc, vc, kknc, lwdc, ac, ka, s0)
    y_rw, _ = _rwkv_scan(r, k, v, kkn, lwd, a, ka, s_ctx)
    hi = jnp.arange(GLA_KEY_WIDTH) // GLA_KEY_DIM
    hj = jnp.arange(GLA_VAL_WIDTH) // GLA_VAL_DIM
    hm = (hi[:, None] == hj[None, :]).astype(BF16)
    g0s = jnp.zeros((B, 2, GLA_VAL_WIDTH, GLA_KEY_WIDTH), F32)
    _, gs_ctx = _gla_scan(qkc, vgc, lac, hm, g0s)
    y_gla, _ = _gla_scan(qk, vg, la, hm, gs_ctx)

    bd128 = _block_diag_ones(GLA_VAL_WIDTH, GLA_VAL_DIM)
    m_consts = [bd64, bd128, row(rw_gn_w[l]), row(rw_gn_b[l]), row(gla_gn_w[l]), row(gla_gn_b[l])]
    m_weights = [_split(w_br_rw[l], N_MERGE), _split(w_br_gla[l], N_MERGE), _split(w_out[l], N_MERGE)]
    x1, h2 = _merge(x, mod_lat, y_rw, bonus, g, y_gla, p_gla, gate, m_consts, m_weights,
                    row(ln1_w[l]), row(ln1_b[l]), min(256, L))

    n = B * L
    h2f = h2.reshape(n, D)
    tm_r = min(256, n)
    bias_b = jnp.broadcast_to(router_bias[l].reshape(-1, 1), (N_EXPERTS, tm_r))
    e_t, g_t, cnt = _router(h2f, _split(router[l].T, N_ROUTER), bias_b, tm_r)
    sizes = jnp.sum(cnt, axis=1).astype(jnp.int32)
    tm_c = min(64, L)
    row_tok, pos_t, block_e, nused = _dispatch_plan(e_t, sizes, tm_c)
    y = _experts(block_e, nused, row_tok, h2f, w_gate_up[l], w_down[l])
    gate_tok = g_t.T.reshape(B, L, TOP_K)
    return _final(pos_t, y, x1, h2, mod_lat, gate_tok, _split(sh_gate_up[l], N_SHARED),
                  _split(sh_down[l], N_SHARED), row(ln2_w[l]), row(ln2_b[l]), tm_c)
```

```python
import functools
import math

import jax
import jax.numpy as jnp
from jax import lax
from jax.experimental import pallas as pl
from jax.experimental.pallas import tpu as pltpu
from jax.experimental.pallas import tpu_sc as plsc

F32 = jnp.float32
BF16 = jnp.bfloat16

D_MODEL = 1024
GRID_W = 64
RW_WIDTH = 512
RW_HEADS = 8
RW_HEAD_DIM = 64
RW_DECAY_RANK = 32
RW_AAA_RANK = 32
RW_GATE_RANK = 96
RW_GN_EPS = 64e-5
RW_COLS = 1696
RW_SEG = 1792
GLA_HEADS = 4
GLA_KEY_WIDTH = 256
GLA_VAL_WIDTH = 512
GLA_KEY_DIM = 64
GLA_VAL_DIM = 128
GLA_GATE_RANK = 16
GLA_TAU = 16.0
GLA_QKV_COLS = 1024
GLA_COLS = 1552
GLA_SEG = 1664
MIX_COLS = RW_COLS + GLA_COLS
N_EXPERTS = 256
TOP_K = 8
N_GROUPS = 8
TOPK_GROUPS = 4
EXPERT_DIM = 256
ROUTED_SCALE = 2.5
LN_EPS = 1e-5
DEPTH = 1
ALPHA = (2 * DEPTH) ** 0.25

CHUNK = 64
GLA_SUB = 16
ROW_BLOCK = 256
VMEM_LIMIT = 48 * 1024 * 1024
SC_CORES = 2
SC_SUBCORES = 16
SC_WORKERS = SC_CORES * SC_SUBCORES
SC_WINDOW = 32
SC_ROW_ALIGN = 2 * SC_WORKERS * SC_WINDOW

N_ADA = 3
N_PROJ = 1
N_LORA = 1
N_SEG = 2
N_CUM = 2
N_RW = 1
N_GLA = 1
N_GLA_INTRA = 1
N_MERGE = 1
N_ROUTER = 3
N_EXPERT = 1
N_SHARED = 1


def _split(x, n):
    parts = []
    r = x
    for i in range(n):
        p = r.astype(BF16)
        parts.append(p)
        if i < n - 1:
            r = r - p.astype(F32)
    return parts


def _dot(a, b, ta=False, tb=False):
    dn = (((0 if ta else 1,), (1 if tb else 0,)), ((), ()))
    return lax.dot_general(a, b, dn, preferred_element_type=F32)


def _mmp(ap, bp, ta=False, tb=False):
    n = max(len(ap), len(bp))
    out = None
    for i in range(len(ap)):
        for j in range(len(bp)):
            if i + j <= n - 1:
                t = _dot(ap[i], bp[j], ta, tb)
                out = t if out is None else out + t
    return out


def _mm(a, b, n, ta=False, tb=False):
    return _mmp(_split(a, n), _split(b, n), ta, tb)


def _mm_exact_l(m_bf16, x, n):
    return _mmp([m_bf16], _split(x, n))


def _mm_exact_r(x, m_bf16, n):
    return _mmp(_split(x, n), [m_bf16])


def _sigmoid(x):
    return 1.0 / (1.0 + jnp.exp(-x))


def _silu(x):
    return x * _sigmoid(x)


def _log_sigmoid(x):
    return jnp.minimum(x, 0.0) - jnp.log(1.0 + jnp.exp(-jnp.abs(x)))


def _cparams(sem):
    return pltpu.CompilerParams(dimension_semantics=sem, vmem_limit_bytes=VMEM_LIMIT)


def _full(shape):
    nd = len(shape)
    return pl.BlockSpec(shape, lambda *a: (0,) * nd)


def _ada_body(c_ref, w_ref, b_ref, o_ref):
    s = _silu(c_ref[...])
    o_ref[...] = _mm(s, w_ref[...], N_ADA) + b_ref[...]


def _ada(cc, w, b):
    rows, d = cc.shape
    n = w.shape[1] // d
    return pl.pallas_call(
        _ada_body,
        grid=(n,),
        in_specs=[pl.BlockSpec((rows, d), lambda j: (0, 0)),
                  pl.BlockSpec((d, d), lambda j: (0, j)),
                  pl.BlockSpec((1, d), lambda j: (0, j))],
        out_specs=pl.BlockSpec((rows, d), lambda j: (0, j)),
        out_shape=jax.ShapeDtypeStruct((rows, w.shape[1]), F32),
        compiler_params=_cparams(("arbitrary",)),
        name="ada",
    )(cc, w, b.reshape(1, -1))


def _proj_body(act, x_ref, mod_ref, *refs):
    w_refs, o_ref = refs[:-1], refs[-1]
    sh = mod_ref[0, 0:1, :]
    sc = mod_ref[0, 1:2, :]
    h = x_ref[0] * (1.0 + sc) + sh
    y = _mmp(_split(h, N_PROJ), [w[...] for w in w_refs])
    if act:
        y = _sigmoid(y)
    o_ref[0] = y


def _proj(x, mod, w, act, tm):
    B, L, D = x.shape
    n = w.shape[1]
    wp = _split(w, N_PROJ)
    return pl.pallas_call(
        functools.partial(_proj_body, act),
        grid=(B, L // tm),
        in_specs=[pl.BlockSpec((1, tm, D), lambda b, i: (b, i, 0)),
                  pl.BlockSpec((1, 6, D), lambda b, i: (b, 0, 0))]
                 + [pl.BlockSpec((D, n), lambda b, i: (0, 0))] * N_PROJ,
        out_specs=pl.BlockSpec((1, tm, n), lambda b, i: (b, i, 0)),
        out_shape=jax.ShapeDtypeStruct((B, L, n), F32),
        compiler_params=_cparams(("parallel", "arbitrary")),
        name="proj",
    )(x, mod, *wp)


def _neighbours(vertical, width, tm, cur, prev_ref, next_ref):
    if vertical:
        i = pl.program_id(1)
        n = pl.num_programs(1)
        prev = jnp.where(i > 0, prev_ref[0], 0.0)
        nxt = jnp.where(i < n - 1, next_ref[0], 0.0)
        pad = jnp.zeros((8, cur.shape[1]), F32)
        ext = jnp.concatenate([pad, prev, cur, nxt, pad], axis=0)
        off = GRID_W + 8
    else:
        pad = jnp.zeros((8, cur.shape[1]), F32)
        ext = jnp.concatenate([pad, cur, pad], axis=0)
        off = 8
    col = lax.broadcasted_iota(jnp.int32, (tm, 1), 0) % width

    def get(dr, dc):
        s = off + GRID_W * dr + dc
        v = ext[s:s + tm]
        if dc == -1:
            v = jnp.where(col == 0, 0.0, v)
        elif dc == 1:
            v = jnp.where(col == width - 1, 0.0, v)
        return v

    return get


def _feat_rw_body(vertical, width, tm, *refs):
    if vertical:
        prev_ref, cur_ref, next_ref = refs[:3]
        refs = refs[3:]
    else:
        cur_ref = refs[0]
        prev_ref = next_ref = None
        refs = refs[1:]
    (mu_ref, lw_hi_ref, lw_lo_ref, w0_ref, a0_ref, kk_ref, ka_ref, rk_ref, bd_ref,
     r_out, k_out, v_out, kkn_out, g_out, bonus_out, lwd_out, a_out) = refs
    cur = cur_ref[0]
    get = _neighbours(vertical, width, tm, cur, prev_ref, next_ref)
    left, right = get(0, -1), get(0, 1)
    up, down = (get(-1, 0), get(1, 0)) if vertical else (left, right)
    l4 = lax.broadcasted_iota(jnp.int32, (1, cur.shape[1]), 1) % 4
    shifted = jnp.where(l4 == 0, left, jnp.where(l4 == 1, right, jnp.where(l4 == 2, up, down)))
    p = cur + mu_ref[...] * (shifted - cur)

    r = p[:, 0:RW_WIDTH]
    k = p[:, RW_WIDTH:2 * RW_WIDTH]
    v = p[:, 2 * RW_WIDTH:3 * RW_WIDTH]
    slab = p[:, 3 * RW_WIDTH:3 * RW_WIDTH + 256]
    ln = lax.broadcasted_iota(jnp.int32, (1, 256), 1)
    e1 = RW_DECAY_RANK
    e2 = e1 + RW_AAA_RANK
    e3 = e2 + RW_GATE_RANK
    slab = jnp.where(ln < e1, jnp.tanh(slab),
                     jnp.where(ln < e2, slab, jnp.where(ln < e3, _sigmoid(slab), 0.0)))
    lo = _mmp(_split(slab, N_LORA), [lw_hi_ref[...], lw_lo_ref[...]][:N_LORA])
    W = RW_WIDTH
    a_sum = None
    for d in range(2):
        z = w0_ref[:, d * W:(d + 1) * W] + lo[:, d * W:(d + 1) * W]
        lwd_out[d, 0] = -_sigmoid(z) * math.exp(-0.5)
        a = _sigmoid(a0_ref[:, d * W:(d + 1) * W] + lo[:, (2 + d) * W:(3 + d) * W])
        a_out[d, 0] = a
        a_sum = a if a_sum is None else a_sum + a
    g_out[0] = lo[:, 4 * W:5 * W]
    bd = bd_ref[...]
    kk = k * kk_ref[...]
    ss = _mm_exact_r(kk * kk, bd, N_SEG)
    kkn_out[0] = kk / jnp.maximum(jnp.sqrt(ss), 1e-12)
    kmod_sum = k * (2.0 + (a_sum - 2.0) * ka_ref[...])
    bonus_out[0] = _mm_exact_r(r * kmod_sum * rk_ref[...], bd, N_SEG) * v
    r_out[0] = r
    k_out[0] = k
    v_out[0] = v


def _feat_rw(p, vertical, tm, consts):
    B, L, S = p.shape
    width = GRID_W if vertical else tm
    W = RW_WIDTH
    hb = tm // GRID_W
    nhb = L // GRID_W
    tile = pl.BlockSpec((1, tm, S), lambda b, i: (b, i, 0))
    if vertical:
        in_specs = [pl.BlockSpec((1, GRID_W, S), lambda b, i: (b, jnp.maximum(i * hb - 1, 0), 0)),
                    tile,
                    pl.BlockSpec((1, GRID_W, S), lambda b, i: (b, jnp.minimum((i + 1) * hb, nhb - 1), 0))]
        args = [p, p, p]
    else:
        in_specs = [tile]
        args = [p]
    in_specs += [_full(c.shape) for c in consts]
    o1 = pl.BlockSpec((1, tm, W), lambda b, i: (b, i, 0))
    o2 = pl.BlockSpec((2, 1, tm, W), lambda b, i: (0, b, i, 0))
    s1 = jax.ShapeDtypeStruct((B, L, W), F32)
    s2 = jax.ShapeDtypeStruct((2, B, L, W), F32)
    return pl.pallas_call(
        functools.partial(_feat_rw_body, vertical, width, tm),
        grid=(B, L // tm),
        in_specs=in_specs,
        out_specs=[o1] * 6 + [o2] * 2,
        out_shape=[s1] * 6 + [s2] * 2,
        compiler_params=_cparams(("parallel", "arbitrary")),
        name="feat_rw",
    )(*args, *consts)


def _feat_gla_body(vertical, width, tm, *refs):
    if vertical:
        prev_ref, cur_ref, next_ref = refs[:3]
        refs = refs[3:]
    else:
        cur_ref = refs[0]
        prev_ref = next_ref = None
        refs = refs[1:]
    cw_ref, g2_hi_ref, g2_lo_ref, gb_ref, qk_out, v_out, la_out = refs
    cur = cur_ref[0][:, 0:GLA_QKV_COLS]
    if vertical:
        class _Slice:
            def __init__(self, ref):
                self.ref = ref

            def __getitem__(self, idx):
                return self.ref[idx][:, 0:GLA_QKV_COLS]
        get = _neighbours(True, width, tm, cur, _Slice(prev_ref), _Slice(next_ref))
    else:
        get = _neighbours(False, width, tm, cur, None, None)
    acc = None
    for dr in ((-1, 0, 1) if vertical else (0,)):
        for dc in (-1, 0, 1):
            t = get(dr, dc) * cw_ref[(dr + 1) * 3 + (dc + 1):(dr + 1) * 3 + (dc + 1) + 1, :]
            acc = t if acc is None else acc + t
    qkv = _silu(acc)
    kw = GLA_KEY_WIDTH
    lane = lax.broadcasted_iota(jnp.int32, (1, 2 * kw), 1)
    qk_out[0] = qkv[:, 0:2 * kw] * jnp.where(lane < kw, GLA_KEY_DIM ** -0.5, 1.0)
    v_out[0] = qkv[:, 2 * kw:]
    pgl = cur_ref[0][:, GLA_QKV_COLS + GLA_VAL_WIDTH:GLA_SEG]
    z = _mmp(_split(pgl, N_LORA), [g2_hi_ref[...], g2_lo_ref[...]][:N_LORA]) + gb_ref[...]
    la = _log_sigmoid(z) * (1.0 / GLA_TAU)
    la_out[0, 0] = la[:, 0:kw]
    la_out[1, 0] = la[:, kw:]


def _feat_gla(p, vertical, tm, consts):
    B, L, S = p.shape
    width = GRID_W if vertical else tm
    hb = tm // GRID_W
    nhb = L // GRID_W
    tile = pl.BlockSpec((1, tm, S), lambda b, i: (b, i, 0))
    if vertical:
        in_specs = [pl.BlockSpec((1, GRID_W, S), lambda b, i: (b, jnp.maximum(i * hb - 1, 0), 0)),
                    tile,
                    pl.BlockSpec((1, GRID_W, S), lambda b, i: (b, jnp.minimum((i + 1) * hb, nhb - 1), 0))]
        args = [p, p, p]
    else:
        in_specs = [tile]
        args = [p]
    in_specs += [_full(c.shape) for c in consts]
    kw, vw = GLA_KEY_WIDTH, GLA_VAL_WIDTH
    return pl.pallas_call(
        functools.partial(_feat_gla_body, vertical, width, tm),
        grid=(B, L // tm),
        in_specs=in_specs,
        out_specs=[pl.BlockSpec((1, tm, 2 * kw), lambda b, i: (b, i, 0)),
                   pl.BlockSpec((1, tm, vw), lambda b, i: (b, i, 0)),
                   pl.BlockSpec((2, 1, tm, kw), lambda b, i: (0, b, i, 0))],
        out_shape=[jax.ShapeDtypeStruct((B, L, 2 * kw), F32),
                   jax.ShapeDtypeStruct((B, L, vw), F32),
                   jax.ShapeDtypeStruct((2, B, L, kw), F32)],
        compiler_params=_cparams(("parallel", "arbitrary")),
        name="feat_gla",
    )(*args, *consts)


def _order_masks(n, sgn):
    ti = lax.broadcasted_iota(jnp.int32, (n, n), 0)
    si = lax.broadcasted_iota(jnp.int32, (n, n), 1)
    rel = (si - ti) * sgn
    return rel < 0, rel <= 0


def _chunk_index(nc):
    return lambda b, d, c: c + d * (nc - 1 - 2 * c)


def _rwkv_body(r_ref, k_ref, v_ref, kk_ref, lw_ref, a_ref, ka_ref, s0_ref, y_ref, st_ref, s_scr):
    C = CHUNK
    d = pl.program_id(1)
    c = pl.program_id(2)

    @pl.when(c == 0)
    def _():
        s_scr[...] = s0_ref[0, 0]

    sgn = 1 - 2 * d
    r, k, v, kk = r_ref[0], k_ref[0], v_ref[0], kk_ref[0]
    lw, a = lw_ref[0, 0], a_ref[0, 0]
    before, upto = _order_masks(C, sgn)
    cum = _mm_exact_l(upto.astype(BF16), lw, N_CUM)
    tot = jnp.where(d == 0, cum[C - 1:C], cum[0:1])
    p_in = jnp.exp(cum)
    p_ex = jnp.exp(cum - lw)
    p_inv = jnp.exp(-cum)
    p_rem = jnp.exp(tot - cum)
    p_all = jnp.exp(tot)
    bvec = kk * a
    kmod = k * (1.0 + (a - 1.0) * ka_ref[...])
    a_t = -kk * p_ex
    r_t = r * p_in
    b_t = bvec * p_inv
    k_t = kmod * p_inv
    b_p = bvec * p_rem
    k_p = kmod * p_rem

    P = 2 * RW_HEAD_DIM
    lane = lax.broadcasted_iota(jnp.int32, (C, P), 1)
    h0 = lane < RW_HEAD_DIM
    ri = lax.broadcasted_iota(jnp.int32, (P, P), 0)
    ci = lax.broadcasted_iota(jnp.int32, (P, P), 1)
    same = (ri // C) == (ci // C)
    rel = ((ci % C) - (ri % C)) * sgn
    strict = same & (rel < 0)
    eye = (ri == ci).astype(F32)

    def stack2(x):
        return jnp.concatenate([jnp.where(h0, x, 0.0), jnp.where(h0, 0.0, x)], axis=0)

    pairs = range(RW_HEADS // 2)
    sls = [slice(p * P, (p + 1) * P) for p in pairs]
    a_st = [stack2(a_t[:, sl]) for sl in sls]
    v_st = [stack2(v[:, sl]) for sl in sls]
    g = [_mm(jnp.concatenate([a_st[p], stack2(r_t[:, sls[p]])], axis=0),
             jnp.concatenate([stack2(b_t[:, sls[p]]), stack2(k_t[:, sls[p]])], axis=0), N_RW, tb=True)
         for p in pairs]
    nmat = [jnp.where(strict, g[p][0:P, 0:P], 0.0) for p in pairs]
    a_ak = [jnp.where(strict, g[p][0:P, P:], 0.0) for p in pairs]
    ri2 = lax.broadcasted_iota(jnp.int32, (P, 2 * P), 0)
    ci2 = lax.broadcasted_iota(jnp.int32, (P, 2 * P), 1)
    incl2 = ((ri2 // C) == ((ci2 % P) // C)) & ((((ci2 % C) - (ri2 % C)) * sgn) <= 0)
    a_r = [jnp.where(incl2, g[p][P:, :], 0.0) for p in pairs]
    akv = [_mm(a_ak[p], v_st[p], N_RW) for p in pairs]
    t = [eye + nmat[p] for p in pairs]
    npow = nmat
    for _ in range(max(C.bit_length() - 2, 0)):
        npow = [_mm(npow[p], npow[p], N_RW) for p in pairs]
        t = [t[p] + _mm(t[p], npow[p], N_RW) for p in pairs]
    x = [_mm(t[p], jnp.concatenate([a_st[p], akv[p]], axis=1), N_RW) for p in pairs]
    s = [s_scr[p] for p in pairs]
    ur = [_mm(jnp.concatenate([x[p][0:C, 0:P] + x[p][C:, 0:P], r_t[:, sls[p]]], axis=0), s[p], N_RW, tb=True)
          for p in pairs]
    u = [ur[p][0:C] + x[p][0:C, P:] + x[p][C:, P:] for p in pairs]
    y_st = [_mm(a_r[p], jnp.concatenate([stack2(u[p]), v_st[p]], axis=0), N_RW) for p in pairs]
    for p in pairs:
        y_ref[0, 0, :, sls[p]] = ur[p][C:] + y_st[p][0:C] + y_st[p][C:]
    upd = [_mm(jnp.concatenate([u[p], v[:, sls[p]]], axis=0).T,
               jnp.concatenate([b_p[:, sls[p]], k_p[:, sls[p]]], axis=0), N_RW) for p in pairs]
    for p in pairs:
        s_new = s[p] * p_all[:, sls[p]] + jnp.where(same, upd[p], 0.0)
        s_scr[p] = s_new
        st_ref[0, 0, p] = s_new


def _rwkv_scan(r, k, v, kk, lw, a, ka, s0):
    B, L, W = r.shape
    nc = L // CHUNK
    cidx = _chunk_index(nc)
    shared = pl.BlockSpec((1, CHUNK, W), lambda b, d, c: (b, cidx(b, d, c), 0))
    perdir = pl.BlockSpec((1, 1, CHUNK, W), lambda b, d, c: (d, b, cidx(b, d, c), 0))
    npair = RW_HEADS // 2
    P = 2 * RW_HEAD_DIM
    sspec = pl.BlockSpec((1, 1, npair, P, P), lambda b, d, c: (b, d, 0, 0, 0))
    return pl.pallas_call(
        _rwkv_body,
        grid=(B, 2, nc),
        in_specs=[shared, shared, shared, shared, perdir, perdir,
                  pl.BlockSpec((1, W), lambda b, d, c: (0, 0)), sspec],
        out_specs=[pl.BlockSpec((1, 1, CHUNK, W), lambda b, d, c: (d, b, cidx(b, d, c), 0)), sspec],
        out_shape=[jax.ShapeDtypeStruct((2, B, L, W), F32),
                   jax.ShapeDtypeStruct((B, 2, npair, P, P), F32)],
        scratch_shapes=[pltpu.VMEM((npair, P, P), F32)],
        compiler_params=_cparams(("parallel", "parallel", "arbitrary")),
        name="rwkv_scan",
    )(r, k, v, kk, lw, a, ka, s0)


def _gla_body(qk_ref, v_ref, g_ref, hm_ref, s0_ref, o_ref, st_ref, s_scr):
    C = CHUNK
    SB = GLA_SUB
    H = GLA_HEADS
    d = pl.program_id(1)
    c = pl.program_id(2)

    @pl.when(c == 0)
    def _():
        s_scr[...] = s0_ref[0, 0]

    sgn = 1 - 2 * d
    kw, vw = GLA_KEY_WIDTH, GLA_VAL_WIDTH
    q = qk_ref[0][:, 0:kw]
    k = qk_ref[0][:, kw:]
    v = v_ref[0]
    g = g_ref[0, 0]
    _, upto = _order_masks(C, sgn)
    b = _mm_exact_l(upto.astype(BF16), g, N_CUM)
    pos = lax.broadcasted_iota(jnp.int32, (C, 1), 0) * sgn + d * (C - 1)

    def b_at(p):
        return jnp.sum(jnp.where(pos == p, b, 0.0), axis=0, keepdims=True)

    NEG = -jnp.inf
    tot = b_at(C - 1)
    s = s_scr[...]
    inter = _mm(q * jnp.exp(b), s, N_GLA, tb=True)

    half = pos >= C // 2
    odd = (pos // SB) % 2 == 1
    r1 = b_at(C // 2 - 1)
    r2 = jnp.where(half, b_at(3 * SB - 1), b_at(SB - 1))
    q1 = q * jnp.exp(jnp.where(half, b - r1, NEG))
    k1 = k * jnp.exp(jnp.where(half, NEG, r1 - b))
    q2 = q * jnp.exp(jnp.where(odd, b - r2, NEG))
    k2 = k * jnp.exp(jnp.where(odd, NEG, r2 - b))
    lane_h = lax.broadcasted_iota(jnp.int32, (C, kw), 1) // GLA_KEY_DIM

    def stack_heads(x):
        return jnp.concatenate([jnp.where(lane_h == h, x, 0.0) for h in range(H)], axis=0)

    att1 = _mm(stack_heads(q1), k1, N_GLA, tb=True)
    att2 = _mm(stack_heads(q2), k2, N_GLA, tb=True)
    pr = (lax.broadcasted_iota(jnp.int32, (H * C, C), 0) % C) * sgn + d * (C - 1)
    pc = lax.broadcasted_iota(jnp.int32, (H * C, C), 1) * sgn + d * (C - 1)
    att = att1 + jnp.where((pr >= C // 2) == (pc >= C // 2), att2, 0.0)
    res = _mm(att, v, N_GLA)
    lane_hv = lax.broadcasted_iota(jnp.int32, (C, vw), 1) // GLA_VAL_DIM
    off = None
    for h in range(H):
        t = jnp.where(lane_hv == h, res[h * C:(h + 1) * C], 0.0)
        off = t if off is None else off + t

    hm = hm_ref[...]
    il = lax.broadcasted_iota(jnp.int32, (SB, 1), 0)
    diag = []
    for blk in range(C // SB):
        rs = slice(blk * SB, (blk + 1) * SB)
        bb, qb, kb, vb = b[rs], q[rs], k[rs], v[rs]
        pieces = []
        for j in range(SB):
            valid = (il - j) * sgn >= 0
            pieces.append(jnp.exp(jnp.where(valid, bb - bb[j:j + 1], NEG)) * qb * kb[j:j + 1])
        a = _mm_exact_r(jnp.concatenate(pieces, axis=0), hm, N_GLA_INTRA)
        acc = None
        for j in range(SB):
            t = a[j * SB:(j + 1) * SB] * vb[j:j + 1]
            acc = t if acc is None else acc + t
        diag.append(acc)
    o_ref[0, 0] = inter + off + jnp.concatenate(diag, axis=0)

    upd = _mm(v.T, k * jnp.exp(tot - b), N_GLA)
    ri = lax.broadcasted_iota(jnp.int32, upd.shape, 0) // GLA_VAL_DIM
    ci = lax.broadcasted_iota(jnp.int32, upd.shape, 1) // GLA_KEY_DIM
    s_new = s * jnp.exp(tot) + jnp.where(ri == ci, upd, 0.0)
    s_scr[...] = s_new
    st_ref[0, 0] = s_new


def _gla_scan(qk, v, la, hm, s0):
    B, L, _ = qk.shape
    kw, vw = GLA_KEY_WIDTH, GLA_VAL_WIDTH
    nc = L // CHUNK
    cidx = _chunk_index(nc)
    sspec = pl.BlockSpec((1, 1, vw, kw), lambda b, d, c: (b, d, 0, 0))
    return pl.pallas_call(
        _gla_body,
        grid=(B, 2, nc),
        in_specs=[pl.BlockSpec((1, CHUNK, 2 * kw), lambda b, d, c: (b, cidx(b, d, c), 0)),
                  pl.BlockSpec((1, CHUNK, vw), lambda b, d, c: (b, cidx(b, d, c), 0)),
                  pl.BlockSpec((1, 1, CHUNK, kw), lambda b, d, c: (d, b, cidx(b, d, c), 0)),
                  pl.BlockSpec((kw, vw), lambda b, d, c: (0, 0)), sspec],
        out_specs=[pl.BlockSpec((1, 1, CHUNK, vw), lambda b, d, c: (d, b, cidx(b, d, c), 0)), sspec],
        out_shape=[jax.ShapeDtypeStruct((2, B, L, vw), F32),
                   jax.ShapeDtypeStruct((B, 2, vw, kw), F32)],
        scratch_shapes=[pltpu.VMEM((vw, kw), F32)],
        compiler_params=_cparams(("parallel", "parallel", "arbitrary")),
        name="gla_scan",
    )(qk, v, la, hm, s0)


def _seg_norm(y, bd, dim, eps):
    mu = _mm_exact_r(y, bd, N_SEG) * (1.0 / dim)
    dlt = y - mu
    var = _mm_exact_r(dlt * dlt, bd, N_SEG) * (1.0 / dim)
    return dlt * lax.rsqrt(var + eps)


def _layer_norm(x, w, b):
    mu = jnp.mean(x, axis=-1, keepdims=True)
    dlt = x - mu
    var = jnp.mean(dlt * dlt, axis=-1, keepdims=True)
    return dlt * lax.rsqrt(var + LN_EPS) * w + b


def _merge_body(x_ref, mod_ref, yrw_ref, bonus_ref, g_ref, ygla_ref, og_ref, gate_ref,
                bd64_ref, bd128_ref, rgw_ref, rgb_ref, ggw_ref, ggb_ref, *refs):
    n = N_MERGE
    wrw = [r[...] for r in refs[0:n]]
    wgla = [r[...] for r in refs[n:2 * n]]
    wout = [r[...] for r in refs[2 * n:3 * n]]
    ln_w_ref, ln_b_ref, x1_ref, h2_ref = refs[3 * n:]
    y = _seg_norm(yrw_ref[0, 0] + yrw_ref[1, 0], bd64_ref[...], RW_HEAD_DIM, RW_GN_EPS)
    y = (y * rgw_ref[...] + rgb_ref[...] + bonus_ref[0]) * g_ref[0]
    y_rw = _mmp(_split(y, n), wrw)
    y = _seg_norm(ygla_ref[0, 0] + ygla_ref[1, 0], bd128_ref[...], GLA_VAL_DIM, LN_EPS)
    y = (y * ggw_ref[...] + ggb_ref[...]) * _silu(og_ref[0])
    y_gla = _mmp(_split(y, n), wgla)
    gate = gate_ref[0]
    mixed = gate[:, 0:D_MODEL] * y_rw + gate[:, D_MODEL:] * y_gla
    mix = _mmp(_split(mixed, n), wout)
    g1 = mod_ref[0, 2:3, :]
    x1 = _layer_norm(ALPHA * x_ref[0] + g1 * mix, ln_w_ref[...], ln_b_ref[...])
    x1_ref[0] = x1
    h2_ref[0] = x1 * (1.0 + mod_ref[0, 4:5, :]) + mod_ref[0, 3:4, :]


def _merge(x, mod, yrw, bonus, g, ygla, p_gla, gate, consts, weights, ln_w, ln_b, tm):
    B, L, D = x.shape
    W = RW_WIDTH
    tok = lambda w: pl.BlockSpec((1, tm, w), lambda b, i: (b, i, 0))
    dirs = pl.BlockSpec((2, 1, tm, W), lambda b, i: (0, b, i, 0))
    wl = [w for ws in weights for w in ws]
    return pl.pallas_call(
        _merge_body,
        grid=(B, L // tm),
        in_specs=[tok(D), pl.BlockSpec((1, 6, D), lambda b, i: (b, 0, 0)), dirs, tok(W), tok(W), dirs,
                  pl.BlockSpec((1, tm, W), lambda b, i: (b, i, GLA_QKV_COLS // W)), tok(2 * D)]
                 + [_full(c.shape) for c in consts] + [_full(w.shape) for w in wl]
                 + [_full(ln_w.shape), _full(ln_b.shape)],
        out_specs=[tok(D), tok(D)],
        out_shape=[jax.ShapeDtypeStruct((B, L, D), F32)] * 2,
        compiler_params=_cparams(("parallel", "arbitrary")),
        name="merge",
    )(x, mod, yrw, bonus, g, ygla, p_gla, gate, *consts, *wl, ln_w, ln_b)


def _router_body(h_ref, *refs):
    n = N_ROUTER
    rt = [r[...] for r in refs[0:n]]
    bias_ref, e_out, g_out, cnt_out = refs[n:]
    tm = h_ref.shape[0]
    E, G, PG = N_EXPERTS, N_GROUPS, N_EXPERTS // N_GROUPS
    logits = _mmp(rt, _split(h_ref[...], n), tb=True)
    scores = _sigmoid(logits)
    sel = scores + bias_ref[:, 0:tm]
    NEG = -jnp.inf
    ip = lax.broadcasted_iota(jnp.int32, (PG, tm), 0)
    group_rows = []
    for gidx in range(G):
        sg = sel[gidx * PG:(gidx + 1) * PG]
        m1 = jnp.max(sg, axis=0, keepdims=True)
        first = jnp.min(jnp.where(sg == m1, ip, PG), axis=0, keepdims=True)
        m2 = jnp.max(jnp.where(ip == first, NEG, sg), axis=0, keepdims=True)
        group_rows.append(m1 + m2)
    gs = jnp.concatenate(group_rows, axis=0)
    gi = lax.broadcasted_iota(jnp.int32, (G, tm), 0)
    keep = jnp.zeros((G, tm), F32)
    for _ in range(TOPK_GROUPS):
        m = jnp.max(gs, axis=0, keepdims=True)
        idx = jnp.min(jnp.where(gs == m, gi, G), axis=0, keepdims=True)
        hit = gi == idx
        keep = jnp.where(hit, 1.0, keep)
        gs = jnp.where(hit, NEG, gs)
    cur = jnp.concatenate(
        [jnp.where(keep[gidx:gidx + 1] > 0.5, sel[gidx * PG:(gidx + 1) * PG], NEG) for gidx in range(G)],
        axis=0)
    ei = lax.broadcasted_iota(jnp.int32, (E, tm), 0)
    idxs, gates = [], []
    picked = jnp.zeros((E, tm), F32)
    for _ in range(TOP_K):
        m = jnp.max(cur, axis=0, keepdims=True)
        idx = jnp.min(jnp.where(cur == m, ei, E), axis=0, keepdims=True)
        hit = ei == idx
        idxs.append(idx)
        gates.append(jnp.sum(jnp.where(hit, scores, 0.0), axis=0, keepdims=True))
        cur = jnp.where(hit, NEG, cur)
        picked = jnp.where(hit, 1.0, picked)
    gate = jnp.concatenate(gates, axis=0)
    e_out[...] = jnp.concatenate(idxs, axis=0)
    g_out[...] = gate / jnp.sum(gate, axis=0, keepdims=True) * ROUTED_SCALE

    @pl.when(pl.program_id(0) == 0)
    def _():
        cnt_out[...] = jnp.zeros(cnt_out.shape, F32)

    part = picked[:, 0:128]
    for j in range(1, tm // 128):
        part = part + picked[:, j * 128:(j + 1) * 128]
    cnt_out[...] += part


def _router(h, router_t_parts, bias_b, tm):
    n, D = h.shape
    E = N_EXPERTS
    return pl.pallas_call(
        _router_body,
        grid=(n // tm,),
        in_specs=[pl.BlockSpec((tm, D), lambda i: (i, 0))]
                 + [pl.BlockSpec((E, D), lambda i: (0, 0))] * len(router_t_parts)
                 + [pl.BlockSpec(bias_b.shape, lambda i: (0, 0))],
        out_specs=[pl.BlockSpec((TOP_K, tm), lambda i: (0, i))] * 2
                  + [pl.BlockSpec((E, 128), lambda i: (0, 0))],
        out_shape=[jax.ShapeDtypeStruct((TOP_K, n), jnp.int32), jax.ShapeDtypeStruct((TOP_K, n), F32),
                   jax.ShapeDtypeStruct((E, 128), F32)],
        compiler_params=_cparams(("arbitrary",)),
        name="router",
    )(h, *router_t_parts, bias_b)


def _sc_gather_body(per_w, table_hbm, idx_hbm, out_hbm, idx_v, rows_v, sem_g, sem_o):
    wid = lax.axis_index("s") * SC_CORES + lax.axis_index("c")
    base = wid * per_w
    win = SC_WINDOW
    pltpu.sync_copy(idx_hbm.at[wid], idx_v)

    @pl.loop(0, per_w // win, step=2)
    def _(j):
        g0 = pltpu.async_copy(table_hbm.at[idx_v.at[j]], rows_v.at[0], sem_g.at[0])
        g1 = pltpu.async_copy(table_hbm.at[idx_v.at[j + 1]], rows_v.at[1], sem_g.at[1])
        g0.wait()
        o0 = pltpu.async_copy(rows_v.at[0], out_hbm.at[pl.ds(base + j * win, win)], sem_o.at[0])
        g1.wait()
        o1 = pltpu.async_copy(rows_v.at[1], out_hbm.at[pl.ds(base + (j + 1) * win, win)], sem_o.at[1])
        o0.wait()
        o1.wait()


def _sc_gather(table, idx):
    nrows = idx.shape[0]
    D = table.shape[1]
    per_w = nrows // SC_WORKERS
    mesh = plsc.VectorSubcoreMesh(core_axis_name="c", subcore_axis_name="s",
                                  num_cores=SC_CORES, num_subcores=SC_SUBCORES)
    return pl.kernel(
        functools.partial(_sc_gather_body, per_w),
        out_type=jax.ShapeDtypeStruct((nrows, D), table.dtype),
        mesh=mesh,
        scratch_types=[pltpu.VMEM((per_w // SC_WINDOW, SC_WINDOW), jnp.int32),
                       pltpu.VMEM((2, SC_WINDOW, D), table.dtype),
                       pltpu.SemaphoreType.DMA((2,)), pltpu.SemaphoreType.DMA((2,))],
        name="sc_gather",
    )(table, idx.reshape(SC_WORKERS, per_w // SC_WINDOW, SC_WINDOW))


def _experts_body(be_ref, nu_ref, x_ref, wgu_ref, wd_ref, y_ref):
    i = pl.program_id(0)

    @pl.when(i < nu_ref[0])
    def _():
        F = EXPERT_DIM
        gu = _mm(x_ref[...], wgu_ref[0], N_EXPERT)
        act = _silu(gu[:, 0:F]) * gu[:, F:]
        y_ref[...] = _mm(act, wd_ref[0], N_EXPERT)

    @pl.when(i >= nu_ref[0])
    def _():
        y_ref[...] = jnp.zeros(y_ref.shape, F32)


def _experts(block_e, nused, xs, w_gate_up, w_down):
    nb = block_e.shape[0]
    D = xs.shape[1]
    R = ROW_BLOCK
    F2 = w_gate_up.shape[2]
    last = lambda i, nu: jnp.minimum(i, nu[0] - 1)
    grid_spec = pltpu.PrefetchScalarGridSpec(
        num_scalar_prefetch=2,
        grid=(nb,),
        in_specs=[pl.BlockSpec((R, D), lambda i, be, nu: (last(i, nu), 0)),
                  pl.BlockSpec((1, D, F2), lambda i, be, nu: (be[last(i, nu)], 0, 0)),
                  pl.BlockSpec((1, F2 // 2, D), lambda i, be, nu: (be[last(i, nu)], 0, 0))],
        out_specs=pl.BlockSpec((R, D), lambda i, be, nu: (i, 0)),
    )
    return pl.pallas_call(
        _experts_body,
        grid_spec=grid_spec,
        out_shape=jax.ShapeDtypeStruct((nb * R, D), F32),
        compiler_params=_cparams(("arbitrary",)),
        name="experts",
    )(block_e, nused, xs, w_gate_up, w_down)


def _final_body(yg_ref, x1_ref, h2_ref, mod_ref, gate_ref, *refs):
    n = N_SHARED
    sgu = [r[...] for r in refs[0:n]]
    sd = [r[...] for r in refs[n:2 * n]]
    ln_w_ref, ln_b_ref, o_ref = refs[2 * n:]
    tm = x1_ref.shape[1]
    gate = gate_ref[0]
    routed = None
    for kk in range(TOP_K):
        t = yg_ref[kk * tm:(kk + 1) * tm, :] * gate[:, kk:kk + 1]
        routed = t if routed is None else routed + t
    h2 = h2_ref[0]
    F = sgu[0].shape[1] // 2
    gu = _mmp(_split(h2, n), sgu)
    act = _silu(gu[:, 0:F]) * gu[:, F:]
    shared = _mmp(_split(act, n), sd)
    g2 = mod_ref[0, 5:6, :]
    o_ref[0] = _layer_norm(ALPHA * x1_ref[0] + g2 * (routed + shared), ln_w_ref[...], ln_b_ref[...])


def _final(yg, x1, h2, mod, gate, sgu, sd, ln_w, ln_b, tm):
    B, L, D = x1.shape
    ni = L // tm
    rows = tm * TOP_K
    tok = lambda w: pl.BlockSpec((1, tm, w), lambda b, i: (b, i, 0))
    ws = list(sgu) + list(sd)
    return pl.pallas_call(
        _final_body,
        grid=(B, ni),
        in_specs=[pl.BlockSpec((rows, D), lambda b, i: (b * ni + i, 0)),
                  tok(D), tok(D), pl.BlockSpec((1, 6, D), lambda b, i: (b, 0, 0)), tok(TOP_K)]
                 + [_full(w.shape) for w in ws] + [_full(ln_w.shape), _full(ln_b.shape)],
        out_specs=tok(D),
        out_shape=jax.ShapeDtypeStruct((B, L, D), F32),
        compiler_params=_cparams(("parallel", "arbitrary")),
        name="final",
    )(yg, x1, h2, mod, gate, *ws, ln_w, ln_b)


def _dispatch_plan(experts_t, sizes, tm):
    R, E = ROW_BLOCK, N_EXPERTS
    K, n = experts_t.shape
    nk = K * n
    rows = nk + E * (R - 1)
    rows = -(-rows // math.lcm(R, SC_ROW_ALIGN)) * math.lcm(R, SC_ROW_ALIGN)
    nb = rows // R
    ncand = nb * R - nk
    padded = (sizes + R - 1) // R * R
    pad_end = jnp.cumsum(padded)
    tok = jnp.broadcast_to(jnp.arange(n, dtype=jnp.int32)[None, :], (K, n))
    slot = jnp.arange(K, dtype=jnp.int32)[:, None]
    aid = (tok // tm) * (tm * K) + slot * tm + tok % tm
    cand = jnp.arange(ncand, dtype=jnp.int32)
    ce, cp = cand // (R - 1), cand % (R - 1)
    need = jnp.broadcast_to((padded - sizes)[:, None], (E, R - 1)).reshape(-1)
    need = jnp.pad(need, (0, ncand - E * (R - 1)))
    ckey = jnp.where(cp < need, 2 * ce + 1, 2 * E + 1)
    keys = jnp.concatenate([2 * experts_t.reshape(-1), ckey])
    toks = jnp.concatenate([tok.reshape(-1), jnp.zeros((ncand,), jnp.int32)])
    aids = jnp.concatenate([aid.reshape(-1), nk + cand])
    _, row_tok, row_aid = lax.sort((keys, toks, aids), num_keys=1)
    _, pos = lax.sort((row_aid, jnp.arange(nb * R, dtype=jnp.int32)), num_keys=1)
    block_e = jnp.minimum(jnp.searchsorted(pad_end, jnp.arange(nb) * R, side='right'), E - 1).astype(jnp.int32)
    nused = (pad_end[-1] // R).astype(jnp.int32).reshape(1)
    return row_tok, pos[:nk], block_e, nused


def _block_diag_ones(n, blk):
    i = jnp.arange(n) // blk
    return (i[:, None] == i[None, :]).astype(BF16)


def kernel(x, c, ctx, c_ctx, w_ada, b_ada, w_in, rw_mu, rw_w0, rw_w2, rw_a0, rw_a2, rw_g2, rw_k_k, rw_k_a,
           rw_r_k, rw_gn_w, rw_gn_b, gla_conv, gla_g2, gla_gb, gla_gn_w, gla_gn_b, w_br_rw, w_br_gla, w_out,
           ln1_w, ln1_b, router, router_bias, w_gate_up, w_down, sh_gate_up, sh_down, ln2_w, ln2_b):
    B, L, D = x.shape
    CT = ctx.shape[1]
    l = 0
    W = RW_WIDTH
    row = lambda t: t.reshape(1, -1)

    rows = -(-(B + 1) // 8) * 8
    cc = jnp.zeros((rows, D), F32).at[:B].set(c).at[B].set(c_ctx)
    mod = _ada(cc, w_ada[l], b_ada[l])
    mod_lat = mod[:B].reshape(B, 6, D)
    mod_ctx = jnp.broadcast_to(mod[B].reshape(1, 6, D), (B, 6, D))

    w = w_in[l]
    g0 = RW_COLS
    w_rw = jnp.pad(w[:, :RW_COLS], ((0, 0), (0, RW_SEG - RW_COLS)))
    w_gla = jnp.concatenate([w[:, g0:g0 + GLA_QKV_COLS],
                             w[:, g0 + GLA_QKV_COLS + GLA_GATE_RANK:g0 + GLA_COLS],
                             w[:, g0 + GLA_QKV_COLS:g0 + GLA_QKV_COLS + GLA_GATE_RANK],
                             jnp.zeros((D, GLA_SEG - GLA_COLS), F32)], axis=1)
    w_gate = w[:, MIX_COLS:]
    tm_p = min(512, L)
    p_rw = _proj(x, mod_lat, w_rw, False, tm_p)
    p_gla = _proj(x, mod_lat, w_gla, False, tm_p)
    gate = _proj(x, mod_lat, w_gate, True, tm_p)
    pc_rw = _proj(ctx, mod_ctx, w_rw, False, CT)
    pc_gla = _proj(ctx, mod_ctx, w_gla, False, CT)

    mu = jnp.pad(rw_mu[l], (0, RW_SEG - RW_COLS)).reshape(1, -1)
    lora = jnp.zeros((256, 5 * W), F32)
    e1 = RW_DECAY_RANK
    e2 = e1 + RW_AAA_RANK
    e3 = e2 + RW_GATE_RANK
    for d in range(2):
        lora = lora.at[0:e1, d * W:(d + 1) * W].set(rw_w2[l, d])
        lora = lora.at[e1:e2, (2 + d) * W:(3 + d) * W].set(rw_a2[l, d])
    lora = lora.at[e2:e3, 4 * W:].set(rw_g2[l])
    lora_p = (_split(lora, N_LORA) + [jnp.zeros_like(lora, BF16)])[:2]
    bd64 = _block_diag_ones(W, RW_HEAD_DIM)
    rw_consts = [mu, lora_p[0], lora_p[1], rw_w0[l].reshape(1, -1), rw_a0[l].reshape(1, -1),
                 row(rw_k_k[l]), row(rw_k_a[l]), row(rw_r_k[l]), bd64]
    tm_f = min(256, L)
    r, k, v, kkn, g, bonus, lwd, a = _feat_rw(p_rw, True, tm_f, rw_consts)
    rc, kc, vc, kknc, _, _, lwdc, ac = _feat_rw(pc_rw, False, CT, rw_consts)

    g2 = jnp.zeros((GLA_SEG - GLA_QKV_COLS - GLA_VAL_WIDTH, 2 * GLA_KEY_WIDTH), F32)
    g2 = g2.at[:GLA_GATE_RANK].set(jnp.concatenate([gla_g2[l, 0], gla_g2[l, 1]], axis=1))
    g2_p = (_split(g2, N_LORA) + [jnp.zeros_like(g2, BF16)])[:2]
    cw = jnp.pad(gla_conv[l].reshape(9, GLA_QKV_COLS), ((0, 7), (0, 0)))
    gla_consts = [cw, g2_p[0], g2_p[1], gla_gb[l].reshape(1, -1)]
    qk, vg, la = _feat_gla(p_gla, True, tm_f, gla_consts)
    qkc, vgc, lac = _feat_gla(pc_gla, False, CT, gla_consts)

    ka = row(rw_k_a[l])
    P = 2 * RW_HEAD_DIM
    s0 = jnp.zeros((B, 2, RW_HEADS // 2, P, P), F32)
    _, s_ctx = _rwkv_scan(rc, kc, vc, kknc, lwdc, ac, ka, s0)
    y_rw, _ = _rwkv_scan(r, k, v, kkn, lwd, a, ka, s_ctx)
    hi = jnp.arange(GLA_KEY_WIDTH) // GLA_KEY_DIM
    hj = jnp.arange(GLA_VAL_WIDTH) // GLA_VAL_DIM
    hm = (hi[:, None] == hj[None, :]).astype(BF16)
    g0s = jnp.zeros((B, 2, GLA_VAL_WIDTH, GLA_KEY_WIDTH), F32)
    _, gs_ctx = _gla_scan(qkc, vgc, lac, hm, g0s)
    y_gla, _ = _gla_scan(qk, vg, la, hm, gs_ctx)

    bd128 = _block_diag_ones(GLA_VAL_WIDTH, GLA_VAL_DIM)
    m_consts = [bd64, bd128, row(rw_gn_w[l]), row(rw_gn_b[l]), row(gla_gn_w[l]), row(gla_gn_b[l])]
    m_weights = [_split(w_br_rw[l], N_MERGE), _split(w_br_gla[l], N_MERGE), _split(w_out[l], N_MERGE)]
    x1, h2 = _merge(x, mod_lat, y_rw, bonus, g, y_gla, p_gla, gate, m_consts, m_weights,
                    row(ln1_w[l]), row(ln1_b[l]), min(256, L))

    n = B * L
    h2f = h2.reshape(n, D)
    tm_r = min(256, n)
    bias_b = jnp.broadcast_to(router_bias[l].reshape(-1, 1), (N_EXPERTS, tm_r))
    e_t, g_t, cnt = _router(h2f, _split(router[l].T, N_ROUTER), bias_b, tm_r)
    sizes = jnp.sum(cnt, axis=1).astype(jnp.int32)
    tm_c = min(64, L)
    row_tok, pos_t, block_e, nused = _dispatch_plan(e_t, sizes, tm_c)
    xs = _sc_gather(h2f, row_tok)
    y = _experts(block_e, nused, xs, w_gate_up[l], w_down[l])
    yg = _sc_gather(y, pos_t)
    gate_tok = g_t.T.reshape(B, L, TOP_K)
    return _final(yg, x1, h2, mod_lat, gate_tok, _split(sh_gate_up[l], N_SHARED),
                  _split(sh_down[l], N_SHARED), row(ln2_w[l]), row(ln2_b[l]), tm_c)
```

```python
import functools
import math

import jax
import jax.numpy as jnp
from jax import lax
from jax.experimental import pallas as pl
from jax.experimental.pallas import tpu as pltpu
from jax.experimental.pallas import tpu_sc as plsc

F32 = jnp.float32
BF16 = jnp.bfloat16

D_MODEL = 1024
GRID_W = 64
RW_WIDTH = 512
RW_HEADS = 8
RW_HEAD_DIM = 64
RW_DECAY_RANK = 32
RW_AAA_RANK = 32
RW_GATE_RANK = 96
RW_GN_EPS = 64e-5
RW_COLS = 1696
RW_SEG = 1792
GLA_HEADS = 4
GLA_KEY_WIDTH = 256
GLA_VAL_WIDTH = 512
GLA_KEY_DIM = 64
GLA_VAL_DIM = 128
GLA_GATE_RANK = 16
GLA_TAU = 16.0
GLA_QKV_COLS = 1024
GLA_COLS = 1552
GLA_SEG = 1664
MIX_COLS = RW_COLS + GLA_COLS
N_EXPERTS = 256
TOP_K = 8
N_GROUPS = 8
TOPK_GROUPS = 4
EXPERT_DIM = 256
ROUTED_SCALE = 2.5
LN_EPS = 1e-5
DEPTH = 1
ALPHA = (2 * DEPTH) ** 0.25

CHUNK = 64
GLA_SUB = 16
ROW_BLOCK = 256
VMEM_LIMIT = 48 * 1024 * 1024
SC_CORES = 2
SC_SUBCORES = 16
SC_WORKERS = SC_CORES * SC_SUBCORES
SC_WINDOW = 32
SC_ROW_ALIGN = 2 * SC_WORKERS * SC_WINDOW

N_ADA = 3
N_PROJ = 1
N_LORA = 1
N_SEG = 2
N_CUM = 2
N_RW = 1
N_GLA = 1
N_GLA_INTRA = 1
N_MERGE = 1
N_ROUTER = 3
N_EXPERT = 1
N_SHARED = 1


def _split(x, n):
    parts = []
    r = x
    for i in range(n):
        p = r.astype(BF16)
        parts.append(p)
        if i < n - 1:
            r = r - p.astype(F32)
    return parts


def _dot(a, b, ta=False, tb=False):
    dn = (((0 if ta else 1,), (1 if tb else 0,)), ((), ()))
    return lax.dot_general(a, b, dn, preferred_element_type=F32)


def _mmp(ap, bp, ta=False, tb=False):
    n = max(len(ap), len(bp))
    out = None
    for i in range(len(ap)):
        for j in range(len(bp)):
            if i + j <= n - 1:
                t = _dot(ap[i], bp[j], ta, tb)
                out = t if out is None else out + t
    return out


def _mm(a, b, n, ta=False, tb=False):
    return _mmp(_split(a, n), _split(b, n), ta, tb)


def _mm_exact_l(m_bf16, x, n):
    return _mmp([m_bf16], _split(x, n))


def _mm_exact_r(x, m_bf16, n):
    return _mmp(_split(x, n), [m_bf16])


def _pack_bf16_pairs(x):
    w = x.shape[1] // 2
    hi = lax.bitcast_convert_type(x[:, :w].astype(BF16).astype(F32), jnp.int32)
    lo = lax.bitcast_convert_type(x[:, w:].astype(BF16).astype(F32), jnp.int32)
    return hi | lax.shift_right_logical(lo, 16)


def _unpack_bf16_pairs(p):
    hi = lax.bitcast_convert_type(p & jnp.int32(-65536), F32)
    lo = lax.bitcast_convert_type(lax.shift_left(p, 16), F32)
    return jnp.concatenate([hi, lo], axis=1)


def _sigmoid(x):
    return 1.0 / (1.0 + jnp.exp(-x))


def _silu(x):
    return x * _sigmoid(x)


def _log_sigmoid(x):
    return jnp.minimum(x, 0.0) - jnp.log(1.0 + jnp.exp(-jnp.abs(x)))


def _cparams(sem):
    return pltpu.CompilerParams(dimension_semantics=sem, vmem_limit_bytes=VMEM_LIMIT)


def _full(shape):
    nd = len(shape)
    return pl.BlockSpec(shape, lambda *a: (0,) * nd)


def _ada_body(c_ref, w_ref, b_ref, o_ref):
    s = _silu(c_ref[...])
    o_ref[...] = _mm(s, w_ref[...], N_ADA) + b_ref[...]


def _ada(cc, w, b):
    rows, d = cc.shape
    n = w.shape[1] // d
    return pl.pallas_call(
        _ada_body,
        grid=(n,),
        in_specs=[pl.BlockSpec((rows, d), lambda j: (0, 0)),
                  pl.BlockSpec((d, d), lambda j: (0, j)),
                  pl.BlockSpec((1, d), lambda j: (0, j))],
        out_specs=pl.BlockSpec((rows, d), lambda j: (0, j)),
        out_shape=jax.ShapeDtypeStruct((rows, w.shape[1]), F32),
        compiler_params=_cparams(("arbitrary",)),
        name="ada",
    )(cc, w, b.reshape(1, -1))


def _proj_body(act, x_ref, mod_ref, *refs):
    w_refs, o_ref = refs[:-1], refs[-1]
    sh = mod_ref[0, 0:1, :]
    sc = mod_ref[0, 1:2, :]
    h = x_ref[0] * (1.0 + sc) + sh
    y = _mmp(_split(h, N_PROJ), [w[...] for w in w_refs])
    if act:
        y = _sigmoid(y)
    o_ref[0] = y


def _proj(x, mod, w, act, tm):
    B, L, D = x.shape
    n = w.shape[1]
    wp = _split(w, N_PROJ)
    return pl.pallas_call(
        functools.partial(_proj_body, act),
        grid=(B, L // tm),
        in_specs=[pl.BlockSpec((1, tm, D), lambda b, i: (b, i, 0)),
                  pl.BlockSpec((1, 6, D), lambda b, i: (b, 0, 0))]
                 + [pl.BlockSpec((D, n), lambda b, i: (0, 0))] * N_PROJ,
        out_specs=pl.BlockSpec((1, tm, n), lambda b, i: (b, i, 0)),
        out_shape=jax.ShapeDtypeStruct((B, L, n), F32),
        compiler_params=_cparams(("parallel", "arbitrary")),
        name="proj",
    )(x, mod, *wp)


def _neighbours(vertical, width, tm, cur, prev_ref, next_ref):
    if vertical:
        i = pl.program_id(1)
        n = pl.num_programs(1)
        prev = jnp.where(i > 0, prev_ref[0], 0.0)
        nxt = jnp.where(i < n - 1, next_ref[0], 0.0)
        pad = jnp.zeros((8, cur.shape[1]), F32)
        ext = jnp.concatenate([pad, prev, cur, nxt, pad], axis=0)
        off = GRID_W + 8
    else:
        pad = jnp.zeros((8, cur.shape[1]), F32)
        ext = jnp.concatenate([pad, cur, pad], axis=0)
        off = 8
    col = lax.broadcasted_iota(jnp.int32, (tm, 1), 0) % width

    def get(dr, dc):
        s = off + GRID_W * dr + dc
        v = ext[s:s + tm]
        if dc == -1:
            v = jnp.where(col == 0, 0.0, v)
        elif dc == 1:
            v = jnp.where(col == width - 1, 0.0, v)
        return v

    return get


def _feat_rw_body(vertical, width, tm, *refs):
    if vertical:
        prev_ref, cur_ref, next_ref = refs[:3]
        refs = refs[3:]
    else:
        cur_ref = refs[0]
        prev_ref = next_ref = None
        refs = refs[1:]
    (mu_ref, lw_hi_ref, lw_lo_ref, w0_ref, a0_ref, kk_ref, ka_ref, rk_ref, bd_ref,
     r_out, k_out, v_out, kkn_out, g_out, bonus_out, lwd_out, a_out) = refs
    cur = cur_ref[0]
    get = _neighbours(vertical, width, tm, cur, prev_ref, next_ref)
    left, right = get(0, -1), get(0, 1)
    up, down = (get(-1, 0), get(1, 0)) if vertical else (left, right)
    l4 = lax.broadcasted_iota(jnp.int32, (1, cur.shape[1]), 1) % 4
    shifted = jnp.where(l4 == 0, left, jnp.where(l4 == 1, right, jnp.where(l4 == 2, up, down)))
    p = cur + mu_ref[...] * (shifted - cur)

    r = p[:, 0:RW_WIDTH]
    k = p[:, RW_WIDTH:2 * RW_WIDTH]
    v = p[:, 2 * RW_WIDTH:3 * RW_WIDTH]
    slab = p[:, 3 * RW_WIDTH:3 * RW_WIDTH + 256]
    ln = lax.broadcasted_iota(jnp.int32, (1, 256), 1)
    e1 = RW_DECAY_RANK
    e2 = e1 + RW_AAA_RANK
    e3 = e2 + RW_GATE_RANK
    slab = jnp.where(ln < e1, jnp.tanh(slab),
                     jnp.where(ln < e2, slab, jnp.where(ln < e3, _sigmoid(slab), 0.0)))
    lo = _mmp(_split(slab, N_LORA), [lw_hi_ref[...], lw_lo_ref[...]][:N_LORA])
    W = RW_WIDTH
    a_sum = None
    for d in range(2):
        z = w0_ref[:, d * W:(d + 1) * W] + lo[:, d * W:(d + 1) * W]
        lwd_out[d, 0] = -_sigmoid(z) * math.exp(-0.5)
        a = _sigmoid(a0_ref[:, d * W:(d + 1) * W] + lo[:, (2 + d) * W:(3 + d) * W])
        a_out[d, 0] = a
        a_sum = a if a_sum is None else a_sum + a
    g_out[0] = lo[:, 4 * W:5 * W]
    bd = bd_ref[...]
    kk = k * kk_ref[...]
    ss = _mm_exact_r(kk * kk, bd, N_SEG)
    kkn_out[0] = kk / jnp.maximum(jnp.sqrt(ss), 1e-12)
    kmod_sum = k * (2.0 + (a_sum - 2.0) * ka_ref[...])
    bonus_out[0] = _mm_exact_r(r * kmod_sum * rk_ref[...], bd, N_SEG) * v
    r_out[0] = r
    k_out[0] = k
    v_out[0] = v


def _feat_rw(p, vertical, tm, consts):
    B, L, S = p.shape
    width = GRID_W if vertical else tm
    W = RW_WIDTH
    hb = tm // GRID_W
    nhb = L // GRID_W
    tile = pl.BlockSpec((1, tm, S), lambda b, i: (b, i, 0))
    if vertical:
        in_specs = [pl.BlockSpec((1, GRID_W, S), lambda b, i: (b, jnp.maximum(i * hb - 1, 0), 0)),
                    tile,
                    pl.BlockSpec((1, GRID_W, S), lambda b, i: (b, jnp.minimum((i + 1) * hb, nhb - 1), 0))]
        args = [p, p, p]
    else:
        in_specs = [tile]
        args = [p]
    in_specs += [_full(c.shape) for c in consts]
    o1 = pl.BlockSpec((1, tm, W), lambda b, i: (b, i, 0))
    o2 = pl.BlockSpec((2, 1, tm, W), lambda b, i: (0, b, i, 0))
    s1 = jax.ShapeDtypeStruct((B, L, W), F32)
    s2 = jax.ShapeDtypeStruct((2, B, L, W), F32)
    return pl.pallas_call(
        functools.partial(_feat_rw_body, vertical, width, tm),
        grid=(B, L // tm),
        in_specs=in_specs,
        out_specs=[o1] * 6 + [o2] * 2,
        out_shape=[s1] * 6 + [s2] * 2,
        compiler_params=_cparams(("parallel", "arbitrary")),
        name="feat_rw",
    )(*args, *consts)


def _feat_gla_body(vertical, width, tm, *refs):
    if vertical:
        prev_ref, cur_ref, next_ref = refs[:3]
        refs = refs[3:]
    else:
        cur_ref = refs[0]
        prev_ref = next_ref = None
        refs = refs[1:]
    cw_ref, g2_hi_ref, g2_lo_ref, gb_ref, qk_out, v_out, la_out = refs
    cur = cur_ref[0][:, 0:GLA_QKV_COLS]
    if vertical:
        class _Slice:
            def __init__(self, ref):
                self.ref = ref

            def __getitem__(self, idx):
                return self.ref[idx][:, 0:GLA_QKV_COLS]
        get = _neighbours(True, width, tm, cur, _Slice(prev_ref), _Slice(next_ref))
    else:
        get = _neighbours(False, width, tm, cur, None, None)
    acc = None
    for dr in ((-1, 0, 1) if vertical else (0,)):
        for dc in (-1, 0, 1):
            t = get(dr, dc) * cw_ref[(dr + 1) * 3 + (dc + 1):(dr + 1) * 3 + (dc + 1) + 1, :]
            acc = t if acc is None else acc + t
    qkv = _silu(acc)
    kw = GLA_KEY_WIDTH
    lane = lax.broadcasted_iota(jnp.int32, (1, 2 * kw), 1)
    qk_out[0] = qkv[:, 0:2 * kw] * jnp.where(lane < kw, GLA_KEY_DIM ** -0.5, 1.0)
    v_out[0] = qkv[:, 2 * kw:]
    pgl = cur_ref[0][:, GLA_QKV_COLS + GLA_VAL_WIDTH:GLA_SEG]
    z = _mmp(_split(pgl, N_LORA), [g2_hi_ref[...], g2_lo_ref[...]][:N_LORA]) + gb_ref[...]
    la = _log_sigmoid(z) * (1.0 / GLA_TAU)
    la_out[0, 0] = la[:, 0:kw]
    la_out[1, 0] = la[:, kw:]


def _feat_gla(p, vertical, tm, consts):
    B, L, S = p.shape
    width = GRID_W if vertical else tm
    hb = tm // GRID_W
    nhb = L // GRID_W
    tile = pl.BlockSpec((1, tm, S), lambda b, i: (b, i, 0))
    if vertical:
        in_specs = [pl.BlockSpec((1, GRID_W, S), lambda b, i: (b, jnp.maximum(i * hb - 1, 0), 0)),
                    tile,
                    pl.BlockSpec((1, GRID_W, S), lambda b, i: (b, jnp.minimum((i + 1) * hb, nhb - 1), 0))]
        args = [p, p, p]
    else:
        in_specs = [tile]
        args = [p]
    in_specs += [_full(c.shape) for c in consts]
    kw, vw = GLA_KEY_WIDTH, GLA_VAL_WIDTH
    return pl.pallas_call(
        functools.partial(_feat_gla_body, vertical, width, tm),
        grid=(B, L // tm),
        in_specs=in_specs,
        out_specs=[pl.BlockSpec((1, tm, 2 * kw), lambda b, i: (b, i, 0)),
                   pl.BlockSpec((1, tm, vw), lambda b, i: (b, i, 0)),
                   pl.BlockSpec((2, 1, tm, kw), lambda b, i: (0, b, i, 0))],
        out_shape=[jax.ShapeDtypeStruct((B, L, 2 * kw), F32),
                   jax.ShapeDtypeStruct((B, L, vw), F32),
                   jax.ShapeDtypeStruct((2, B, L, kw), F32)],
        compiler_params=_cparams(("parallel", "arbitrary")),
        name="feat_gla",
    )(*args, *consts)


def _order_masks(n, sgn):
    ti = lax.broadcasted_iota(jnp.int32, (n, n), 0)
    si = lax.broadcasted_iota(jnp.int32, (n, n), 1)
    rel = (si - ti) * sgn
    return rel < 0, rel <= 0


def _chunk_index(nc):
    return lambda b, d, c: c + d * (nc - 1 - 2 * c)


def _rwkv_body(r_ref, k_ref, v_ref, kk_ref, lw_ref, a_ref, ka_ref, s0_ref, y_ref, st_ref, s_scr):
    C = CHUNK
    d = pl.program_id(1)
    c = pl.program_id(2)

    @pl.when(c == 0)
    def _():
        s_scr[...] = s0_ref[0, 0]

    sgn = 1 - 2 * d
    r, k, v, kk = r_ref[0], k_ref[0], v_ref[0], kk_ref[0]
    lw, a = lw_ref[0, 0], a_ref[0, 0]
    before, upto = _order_masks(C, sgn)
    cum = _mm_exact_l(upto.astype(BF16), lw, N_CUM)
    tot = jnp.where(d == 0, cum[C - 1:C], cum[0:1])
    p_in = jnp.exp(cum)
    p_ex = jnp.exp(cum - lw)
    p_inv = jnp.exp(-cum)
    p_rem = jnp.exp(tot - cum)
    p_all = jnp.exp(tot)
    bvec = kk * a
    kmod = k * (1.0 + (a - 1.0) * ka_ref[...])
    a_t = -kk * p_ex
    r_t = r * p_in
    b_t = bvec * p_inv
    k_t = kmod * p_inv
    b_p = bvec * p_rem
    k_p = kmod * p_rem

    P = 2 * RW_HEAD_DIM
    lane = lax.broadcasted_iota(jnp.int32, (C, P), 1)
    h0 = lane < RW_HEAD_DIM
    ri = lax.broadcasted_iota(jnp.int32, (P, P), 0)
    ci = lax.broadcasted_iota(jnp.int32, (P, P), 1)
    same = (ri // C) == (ci // C)
    rel = ((ci % C) - (ri % C)) * sgn
    strict = same & (rel < 0)
    eye = (ri == ci).astype(F32)

    def stack2(x):
        return jnp.concatenate([jnp.where(h0, x, 0.0), jnp.where(h0, 0.0, x)], axis=0)

    pairs = range(RW_HEADS // 2)
    sls = [slice(p * P, (p + 1) * P) for p in pairs]
    a_st = [stack2(a_t[:, sl]) for sl in sls]
    v_st = [stack2(v[:, sl]) for sl in sls]
    g = [_mm(jnp.concatenate([a_st[p], stack2(r_t[:, sls[p]])], axis=0),
             jnp.concatenate([stack2(b_t[:, sls[p]]), stack2(k_t[:, sls[p]])], axis=0), N_RW, tb=True)
         for p in pairs]
    nmat = [jnp.where(strict, g[p][0:P, 0:P], 0.0) for p in pairs]
    a_ak = [jnp.where(strict, g[p][0:P, P:], 0.0) for p in pairs]
    ri2 = lax.broadcasted_iota(jnp.int32, (P, 2 * P), 0)
    ci2 = lax.broadcasted_iota(jnp.int32, (P, 2 * P), 1)
    incl2 = ((ri2 // C) == ((ci2 % P) // C)) & ((((ci2 % C) - (ri2 % C)) * sgn) <= 0)
    a_r = [jnp.where(incl2, g[p][P:, :], 0.0) for p in pairs]
    akv = [_mm(a_ak[p], v_st[p], N_RW) for p in pairs]
    t = [eye + nmat[p] for p in pairs]
    npow = nmat
    for _ in range(max(C.bit_length() - 2, 0)):
        npow = [_mm(npow[p], npow[p], N_RW) for p in pairs]
        t = [t[p] + _mm(t[p], npow[p], N_RW) for p in pairs]
    x = [_mm(t[p], jnp.concatenate([a_st[p], akv[p]], axis=1), N_RW) for p in pairs]
    s = [s_scr[p] for p in pairs]
    ur = [_mm(jnp.concatenate([x[p][0:C, 0:P] + x[p][C:, 0:P], r_t[:, sls[p]]], axis=0), s[p], N_RW, tb=True)
          for p in pairs]
    u = [ur[p][0:C] + x[p][0:C, P:] + x[p][C:, P:] for p in pairs]
    y_st = [_mm(a_r[p], jnp.concatenate([stack2(u[p]), v_st[p]], axis=0), N_RW) for p in pairs]
    for p in pairs:
        y_ref[0, 0, :, sls[p]] = ur[p][C:] + y_st[p][0:C] + y_st[p][C:]
    upd = [_mm(jnp.concatenate([u[p], v[:, sls[p]]], axis=0).T,
               jnp.concatenate([b_p[:, sls[p]], k_p[:, sls[p]]], axis=0), N_RW) for p in pairs]
    for p in pairs:
        s_new = s[p] * p_all[:, sls[p]] + jnp.where(same, upd[p], 0.0)
        s_scr[p] = s_new
        st_ref[0, 0, p] = s_new


def _rwkv_scan(r, k, v, kk, lw, a, ka, s0):
    B, L, W = r.shape
    nc = L // CHUNK
    cidx = _chunk_index(nc)
    shared = pl.BlockSpec((1, CHUNK, W), lambda b, d, c: (b, cidx(b, d, c), 0))
    perdir = pl.BlockSpec((1, 1, CHUNK, W), lambda b, d, c: (d, b, cidx(b, d, c), 0))
    npair = RW_HEADS // 2
    P = 2 * RW_HEAD_DIM
    sspec = pl.BlockSpec((1, 1, npair, P, P), lambda b, d, c: (b, d, 0, 0, 0))
    return pl.pallas_call(
        _rwkv_body,
        grid=(B, 2, nc),
        in_specs=[shared, shared, shared, shared, perdir, perdir,
                  pl.BlockSpec((1, W), lambda b, d, c: (0, 0)), sspec],
        out_specs=[pl.BlockSpec((1, 1, CHUNK, W), lambda b, d, c: (d, b, cidx(b, d, c), 0)), sspec],
        out_shape=[jax.ShapeDtypeStruct((2, B, L, W), F32),
                   jax.ShapeDtypeStruct((B, 2, npair, P, P), F32)],
        scratch_shapes=[pltpu.VMEM((npair, P, P), F32)],
        compiler_params=_cparams(("parallel", "parallel", "arbitrary")),
        name="rwkv_scan",
    )(r, k, v, kk, lw, a, ka, s0)


def _gla_body(qk_ref, v_ref, g_ref, hm_ref, s0_ref, o_ref, st_ref, s_scr):
    C = CHUNK
    SB = GLA_SUB
    H = GLA_HEADS
    d = pl.program_id(1)
    c = pl.program_id(2)

    @pl.when(c == 0)
    def _():
        s_scr[...] = s0_ref[0, 0]

    sgn = 1 - 2 * d
    kw, vw = GLA_KEY_WIDTH, GLA_VAL_WIDTH
    q = qk_ref[0][:, 0:kw]
    k = qk_ref[0][:, kw:]
    v = v_ref[0]
    g = g_ref[0, 0]
    _, upto = _order_masks(C, sgn)
    b = _mm_exact_l(upto.astype(BF16), g, N_CUM)
    pos = lax.broadcasted_iota(jnp.int32, (C, 1), 0) * sgn + d * (C - 1)

    def b_at(p):
        return jnp.sum(jnp.where(pos == p, b, 0.0), axis=0, keepdims=True)

    NEG = -jnp.inf
    tot = b_at(C - 1)
    s = s_scr[...]
    inter = _mm(q * jnp.exp(b), s, N_GLA, tb=True)

    half = pos >= C // 2
    odd = (pos // SB) % 2 == 1
    r1 = b_at(C // 2 - 1)
    r2 = jnp.where(half, b_at(3 * SB - 1), b_at(SB - 1))
    q1 = q * jnp.exp(jnp.where(half, b - r1, NEG))
    k1 = k * jnp.exp(jnp.where(half, NEG, r1 - b))
    q2 = q * jnp.exp(jnp.where(odd, b - r2, NEG))
    k2 = k * jnp.exp(jnp.where(odd, NEG, r2 - b))
    lane_h = lax.broadcasted_iota(jnp.int32, (C, kw), 1) // GLA_KEY_DIM

    def stack_heads(x):
        return jnp.concatenate([jnp.where(lane_h == h, x, 0.0) for h in range(H)], axis=0)

    att1 = _mm(stack_heads(q1), k1, N_GLA, tb=True)
    att2 = _mm(stack_heads(q2), k2, N_GLA, tb=True)
    pr = (lax.broadcasted_iota(jnp.int32, (H * C, C), 0) % C) * sgn + d * (C - 1)
    pc = lax.broadcasted_iota(jnp.int32, (H * C, C), 1) * sgn + d * (C - 1)
    att = att1 + jnp.where((pr >= C // 2) == (pc >= C // 2), att2, 0.0)
    res = _mm(att, v, N_GLA)
    lane_hv = lax.broadcasted_iota(jnp.int32, (C, vw), 1) // GLA_VAL_DIM
    off = None
    for h in range(H):
        t = jnp.where(lane_hv == h, res[h * C:(h + 1) * C], 0.0)
        off = t if off is None else off + t

    hm = hm_ref[...]
    il = lax.broadcasted_iota(jnp.int32, (SB, 1), 0)
    diag = []
    for blk in range(C // SB):
        rs = slice(blk * SB, (blk + 1) * SB)
        bb, qb, kb, vb = b[rs], q[rs], k[rs], v[rs]
        pieces = []
        for j in range(SB):
            valid = (il - j) * sgn >= 0
            pieces.append(jnp.exp(jnp.where(valid, bb - bb[j:j + 1], NEG)) * qb * kb[j:j + 1])
        a = _mm_exact_r(jnp.concatenate(pieces, axis=0), hm, N_GLA_INTRA)
        acc = None
        for j in range(SB):
            t = a[j * SB:(j + 1) * SB] * vb[j:j + 1]
            acc = t if acc is None else acc + t
        diag.append(acc)
    o_ref[0, 0] = inter + off + jnp.concatenate(diag, axis=0)

    upd = _mm(v.T, k * jnp.exp(tot - b), N_GLA)
    ri = lax.broadcasted_iota(jnp.int32, upd.shape, 0) // GLA_VAL_DIM
    ci = lax.broadcasted_iota(jnp.int32, upd.shape, 1) // GLA_KEY_DIM
    s_new = s * jnp.exp(tot) + jnp.where(ri == ci, upd, 0.0)
    s_scr[...] = s_new
    st_ref[0, 0] = s_new


def _gla_scan(qk, v, la, hm, s0):
    B, L, _ = qk.shape
    kw, vw = GLA_KEY_WIDTH, GLA_VAL_WIDTH
    nc = L // CHUNK
    cidx = _chunk_index(nc)
    sspec = pl.BlockSpec((1, 1, vw, kw), lambda b, d, c: (b, d, 0, 0))
    return pl.pallas_call(
        _gla_body,
        grid=(B, 2, nc),
        in_specs=[pl.BlockSpec((1, CHUNK, 2 * kw), lambda b, d, c: (b, cidx(b, d, c), 0)),
                  pl.BlockSpec((1, CHUNK, vw), lambda b, d, c: (b, cidx(b, d, c), 0)),
                  pl.BlockSpec((1, 1, CHUNK, kw), lambda b, d, c: (d, b, cidx(b, d, c), 0)),
                  pl.BlockSpec((kw, vw), lambda b, d, c: (0, 0)), sspec],
        out_specs=[pl.BlockSpec((1, 1, CHUNK, vw), lambda b, d, c: (d, b, cidx(b, d, c), 0)), sspec],
        out_shape=[jax.ShapeDtypeStruct((2, B, L, vw), F32),
                   jax.ShapeDtypeStruct((B, 2, vw, kw), F32)],
        scratch_shapes=[pltpu.VMEM((vw, kw), F32)],
        compiler_params=_cparams(("parallel", "parallel", "arbitrary")),
        name="gla_scan",
    )(qk, v, la, hm, s0)


def _seg_norm(y, bd, dim, eps):
    mu = _mm_exact_r(y, bd, N_SEG) * (1.0 / dim)
    dlt = y - mu
    var = _mm_exact_r(dlt * dlt, bd, N_SEG) * (1.0 / dim)
    return dlt * lax.rsqrt(var + eps)


def _layer_norm(x, w, b):
    mu = jnp.mean(x, axis=-1, keepdims=True)
    dlt = x - mu
    var = jnp.mean(dlt * dlt, axis=-1, keepdims=True)
    return dlt * lax.rsqrt(var + LN_EPS) * w + b


def _merge_body(x_ref, mod_ref, yrw_ref, bonus_ref, g_ref, ygla_ref, og_ref, gate_ref,
                bd64_ref, bd128_ref, rgw_ref, rgb_ref, ggw_ref, ggb_ref, *refs):
    n = N_MERGE
    wrw = [r[...] for r in refs[0:n]]
    wgla = [r[...] for r in refs[n:2 * n]]
    wout = [r[...] for r in refs[2 * n:3 * n]]
    ln_w_ref, ln_b_ref, x1_ref, h2_ref, h2p_ref = refs[3 * n:]
    y = _seg_norm(yrw_ref[0, 0] + yrw_ref[1, 0], bd64_ref[...], RW_HEAD_DIM, RW_GN_EPS)
    y = (y * rgw_ref[...] + rgb_ref[...] + bonus_ref[0]) * g_ref[0]
    y_rw = _mmp(_split(y, n), wrw)
    y = _seg_norm(ygla_ref[0, 0] + ygla_ref[1, 0], bd128_ref[...], GLA_VAL_DIM, LN_EPS)
    y = (y * ggw_ref[...] + ggb_ref[...]) * _silu(og_ref[0])
    y_gla = _mmp(_split(y, n), wgla)
    gate = gate_ref[0]
    mixed = gate[:, 0:D_MODEL] * y_rw + gate[:, D_MODEL:] * y_gla
    mix = _mmp(_split(mixed, n), wout)
    g1 = mod_ref[0, 2:3, :]
    x1 = _layer_norm(ALPHA * x_ref[0] + g1 * mix, ln_w_ref[...], ln_b_ref[...])
    x1_ref[0] = x1
    h2 = x1 * (1.0 + mod_ref[0, 4:5, :]) + mod_ref[0, 3:4, :]
    h2_ref[0] = h2
    h2p_ref[0] = _pack_bf16_pairs(h2)


def _merge(x, mod, yrw, bonus, g, ygla, p_gla, gate, consts, weights, ln_w, ln_b, tm):
    B, L, D = x.shape
    W = RW_WIDTH
    tok = lambda w: pl.BlockSpec((1, tm, w), lambda b, i: (b, i, 0))
    dirs = pl.BlockSpec((2, 1, tm, W), lambda b, i: (0, b, i, 0))
    wl = [w for ws in weights for w in ws]
    return pl.pallas_call(
        _merge_body,
        grid=(B, L // tm),
        in_specs=[tok(D), pl.BlockSpec((1, 6, D), lambda b, i: (b, 0, 0)), dirs, tok(W), tok(W), dirs,
                  pl.BlockSpec((1, tm, W), lambda b, i: (b, i, GLA_QKV_COLS // W)), tok(2 * D)]
                 + [_full(c.shape) for c in consts] + [_full(w.shape) for w in wl]
                 + [_full(ln_w.shape), _full(ln_b.shape)],
        out_specs=[tok(D), tok(D), tok(D // 2)],
        out_shape=[jax.ShapeDtypeStruct((B, L, D), F32)] * 2 + [jax.ShapeDtypeStruct((B, L, D // 2), jnp.int32)],
        compiler_params=_cparams(("parallel", "arbitrary")),
        name="merge",
    )(x, mod, yrw, bonus, g, ygla, p_gla, gate, *consts, *wl, ln_w, ln_b)


def _router_body(h_ref, *refs):
    n = N_ROUTER
    rt = [r[...] for r in refs[0:n]]
    bias_ref, e_out, g_out, cnt_out = refs[n:]
    tm = h_ref.shape[0]
    E, G, PG = N_EXPERTS, N_GROUPS, N_EXPERTS // N_GROUPS
    logits = _mmp(rt, _split(h_ref[...], n), tb=True)
    scores = _sigmoid(logits)
    sel = scores + bias_ref[:, 0:tm]
    NEG = -jnp.inf
    ip = lax.broadcasted_iota(jnp.int32, (PG, tm), 0)
    group_rows = []
    for gidx in range(G):
        sg = sel[gidx * PG:(gidx + 1) * PG]
        m1 = jnp.max(sg, axis=0, keepdims=True)
        first = jnp.min(jnp.where(sg == m1, ip, PG), axis=0, keepdims=True)
        m2 = jnp.max(jnp.where(ip == first, NEG, sg), axis=0, keepdims=True)
        group_rows.append(m1 + m2)
    gs = jnp.concatenate(group_rows, axis=0)
    gi = lax.broadcasted_iota(jnp.int32, (G, tm), 0)
    keep = jnp.zeros((G, tm), F32)
    for _ in range(TOPK_GROUPS):
        m = jnp.max(gs, axis=0, keepdims=True)
        idx = jnp.min(jnp.where(gs == m, gi, G), axis=0, keepdims=True)
        hit = gi == idx
        keep = jnp.where(hit, 1.0, keep)
        gs = jnp.where(hit, NEG, gs)
    cur = jnp.concatenate(
        [jnp.where(keep[gidx:gidx + 1] > 0.5, sel[gidx * PG:(gidx + 1) * PG], NEG) for gidx in range(G)],
        axis=0)
    ei = lax.broadcasted_iota(jnp.int32, (E, tm), 0)
    idxs, gates = [], []
    picked = jnp.zeros((E, tm), F32)
    for _ in range(TOP_K):
        m = jnp.max(cur, axis=0, keepdims=True)
        idx = jnp.min(jnp.where(cur == m, ei, E), axis=0, keepdims=True)
        hit = ei == idx
        idxs.append(idx)
        gates.append(jnp.sum(jnp.where(hit, scores, 0.0), axis=0, keepdims=True))
        cur = jnp.where(hit, NEG, cur)
        picked = jnp.where(hit, 1.0, picked)
    gate = jnp.concatenate(gates, axis=0)
    e_out[...] = jnp.concatenate(idxs, axis=0)
    g_out[...] = gate / jnp.sum(gate, axis=0, keepdims=True) * ROUTED_SCALE

    @pl.when(pl.program_id(0) == 0)
    def _():
        cnt_out[...] = jnp.zeros(cnt_out.shape, F32)

    part = picked[:, 0:128]
    for j in range(1, tm // 128):
        part = part + picked[:, j * 128:(j + 1) * 128]
    cnt_out[...] += part


def _router(h, router_t_parts, bias_b, tm):
    n, D = h.shape
    E = N_EXPERTS
    return pl.pallas_call(
        _router_body,
        grid=(n // tm,),
        in_specs=[pl.BlockSpec((tm, D), lambda i: (i, 0))]
                 + [pl.BlockSpec((E, D), lambda i: (0, 0))] * len(router_t_parts)
                 + [pl.BlockSpec(bias_b.shape, lambda i: (0, 0))],
        out_specs=[pl.BlockSpec((TOP_K, tm), lambda i: (0, i))] * 2
                  + [pl.BlockSpec((E, 128), lambda i: (0, 0))],
        out_shape=[jax.ShapeDtypeStruct((TOP_K, n), jnp.int32), jax.ShapeDtypeStruct((TOP_K, n), F32),
                   jax.ShapeDtypeStruct((E, 128), F32)],
        compiler_params=_cparams(("arbitrary",)),
        name="router",
    )(h, *router_t_parts, bias_b)


def _sc_gather_body(per_w, table_hbm, idx_hbm, out_hbm, idx_v, rows_v, sem_g, sem_o):
    wid = lax.axis_index("s") * SC_CORES + lax.axis_index("c")
    base = wid * per_w
    win = SC_WINDOW
    pltpu.sync_copy(idx_hbm.at[wid], idx_v)

    @pl.loop(0, per_w // win, step=2)
    def _(j):
        g0 = pltpu.async_copy(table_hbm.at[idx_v.at[j]], rows_v.at[0], sem_g.at[0])
        g1 = pltpu.async_copy(table_hbm.at[idx_v.at[j + 1]], rows_v.at[1], sem_g.at[1])
        g0.wait()
        o0 = pltpu.async_copy(rows_v.at[0], out_hbm.at[pl.ds(base + j * win, win)], sem_o.at[0])
        g1.wait()
        o1 = pltpu.async_copy(rows_v.at[1], out_hbm.at[pl.ds(base + (j + 1) * win, win)], sem_o.at[1])
        o0.wait()
        o1.wait()


def _sc_gather(table, idx):
    nrows = idx.shape[0]
    D = table.shape[1]
    per_w = nrows // SC_WORKERS
    mesh = plsc.VectorSubcoreMesh(core_axis_name="c", subcore_axis_name="s",
                                  num_cores=SC_CORES, num_subcores=SC_SUBCORES)
    return pl.kernel(
        functools.partial(_sc_gather_body, per_w),
        out_type=jax.ShapeDtypeStruct((nrows, D), table.dtype),
        mesh=mesh,
        scratch_types=[pltpu.VMEM((per_w // SC_WINDOW, SC_WINDOW), jnp.int32),
                       pltpu.VMEM((2, SC_WINDOW, D), table.dtype),
                       pltpu.SemaphoreType.DMA((2,)), pltpu.SemaphoreType.DMA((2,))],
        name="sc_gather",
    )(table, idx.reshape(SC_WORKERS, per_w // SC_WINDOW, SC_WINDOW))


def _experts_body(be_ref, nu_ref, x_ref, wgu_ref, wd_ref, y_ref):
    i = pl.program_id(0)

    @pl.when(i < nu_ref[0])
    def _():
        F = EXPERT_DIM
        gu = _mm(_unpack_bf16_pairs(x_ref[...]), wgu_ref[0], N_EXPERT)
        act = _silu(gu[:, 0:F]) * gu[:, F:]
        y_ref[...] = _pack_bf16_pairs(_mm(act, wd_ref[0], N_EXPERT))

    @pl.when(i >= nu_ref[0])
    def _():
        y_ref[...] = jnp.zeros(y_ref.shape, jnp.int32)


def _experts(block_e, nused, xs, w_gate_up, w_down):
    nb = block_e.shape[0]
    DP = xs.shape[1]
    D = 2 * DP
    R = ROW_BLOCK
    F2 = w_gate_up.shape[2]
    last = lambda i, nu: jnp.minimum(i, nu[0] - 1)
    grid_spec = pltpu.PrefetchScalarGridSpec(
        num_scalar_prefetch=2,
        grid=(nb,),
        in_specs=[pl.BlockSpec((R, DP), lambda i, be, nu: (last(i, nu), 0)),
                  pl.BlockSpec((1, D, F2), lambda i, be, nu: (be[last(i, nu)], 0, 0)),
                  pl.BlockSpec((1, F2 // 2, D), lambda i, be, nu: (be[last(i, nu)], 0, 0))],
        out_specs=pl.BlockSpec((R, DP), lambda i, be, nu: (i, 0)),
    )
    return pl.pallas_call(
        _experts_body,
        grid_spec=grid_spec,
        out_shape=jax.ShapeDtypeStruct((nb * R, DP), jnp.int32),
        compiler_params=_cparams(("arbitrary",)),
        name="experts",
    )(block_e, nused, xs, w_gate_up, w_down)


def _final_body(yg_ref, x1_ref, h2_ref, mod_ref, gate_ref, *refs):
    n = N_SHARED
    sgu = [r[...] for r in refs[0:n]]
    sd = [r[...] for r in refs[n:2 * n]]
    ln_w_ref, ln_b_ref, o_ref = refs[2 * n:]
    tm = x1_ref.shape[1]
    gate = gate_ref[0]
    routed = None
    for kk in range(TOP_K):
        t = _unpack_bf16_pairs(yg_ref[kk * tm:(kk + 1) * tm, :]) * gate[:, kk:kk + 1]
        routed = t if routed is None else routed + t
    h2 = h2_ref[0]
    F = sgu[0].shape[1] // 2
    gu = _mmp(_split(h2, n), sgu)
    act = _silu(gu[:, 0:F]) * gu[:, F:]
    shared = _mmp(_split(act, n), sd)
    g2 = mod_ref[0, 5:6, :]
    o_ref[0] = _layer_norm(ALPHA * x1_ref[0] + g2 * (routed + shared), ln_w_ref[...], ln_b_ref[...])


def _final(yg, x1, h2, mod, gate, sgu, sd, ln_w, ln_b, tm):
    B, L, D = x1.shape
    ni = L // tm
    rows = tm * TOP_K
    tok = lambda w: pl.BlockSpec((1, tm, w), lambda b, i: (b, i, 0))
    ws = list(sgu) + list(sd)
    return pl.pallas_call(
        _final_body,
        grid=(B, ni),
        in_specs=[pl.BlockSpec((rows, D // 2), lambda b, i: (b * ni + i, 0)),
                  tok(D), tok(D), pl.BlockSpec((1, 6, D), lambda b, i: (b, 0, 0)), tok(TOP_K)]
                 + [_full(w.shape) for w in ws] + [_full(ln_w.shape), _full(ln_b.shape)],
        out_specs=tok(D),
        out_shape=jax.ShapeDtypeStruct((B, L, D), F32),
        compiler_params=_cparams(("parallel", "arbitrary")),
        name="final",
    )(yg, x1, h2, mod, gate, *ws, ln_w, ln_b)


def _dispatch_plan(experts_t, sizes, tm):
    R, E = ROW_BLOCK, N_EXPERTS
    K, n = experts_t.shape
    nk = K * n
    rows = nk + E * (R - 1)
    rows = -(-rows // math.lcm(R, SC_ROW_ALIGN)) * math.lcm(R, SC_ROW_ALIGN)
    nb = rows // R
    ncand = nb * R - nk
    padded = (sizes + R - 1) // R * R
    pad_end = jnp.cumsum(padded)
    tok = jnp.broadcast_to(jnp.arange(n, dtype=jnp.int32)[None, :], (K, n))
    slot = jnp.arange(K, dtype=jnp.int32)[:, None]
    aid = (tok // tm) * (tm * K) + slot * tm + tok % tm
    cand = jnp.arange(ncand, dtype=jnp.int32)
    ce, cp = cand // (R - 1), cand % (R - 1)
    need = jnp.broadcast_to((padded - sizes)[:, None], (E, R - 1)).reshape(-1)
    need = jnp.pad(need, (0, ncand - E * (R - 1)))
    ckey = jnp.where(cp < need, 2 * ce + 1, 2 * E + 1)
    keys = jnp.concatenate([2 * experts_t.reshape(-1), ckey])
    toks = jnp.concatenate([tok.reshape(-1), cand % n])
    aids = jnp.concatenate([aid.reshape(-1), nk + cand])
    _, row_tok, row_aid = lax.sort((keys, toks, aids), num_keys=1)
    _, pos = lax.sort((row_aid, jnp.arange(nb * R, dtype=jnp.int32)), num_keys=1)
    block_e = jnp.minimum(jnp.searchsorted(pad_end, jnp.arange(nb) * R, side='right'), E - 1).astype(jnp.int32)
    nused = (pad_end[-1] // R).astype(jnp.int32).reshape(1)
    return row_tok, pos[:nk], block_e, nused


def _block_diag_ones(n, blk):
    i = jnp.arange(n) // blk
    return (i[:, None] == i[None, :]).astype(BF16)


def kernel(x, c, ctx, c_ctx, w_ada, b_ada, w_in, rw_mu, rw_w0, rw_w2, rw_a0, rw_a2, rw_g2, rw_k_k, rw_k_a,
           rw_r_k, rw_gn_w, rw_gn_b, gla_conv, gla_g2, gla_gb, gla_gn_w, gla_gn_b, w_br_rw, w_br_gla, w_out,
           ln1_w, ln1_b, router, router_bias, w_gate_up, w_down, sh_gate_up, sh_down, ln2_w, ln2_b):
    B, L, D = x.shape
    CT = ctx.shape[1]
    l = 0
    W = RW_WIDTH
    row = lambda t: t.reshape(1, -1)

    rows = -(-(B + 1) // 8) * 8
    cc = jnp.zeros((rows, D), F32).at[:B].set(c).at[B].set(c_ctx)
    mod = _ada(cc, w_ada[l], b_ada[l])
    mod_lat = mod[:B].reshape(B, 6, D)
    mod_ctx = jnp.broadcast_to(mod[B].reshape(1, 6, D), (B, 6, D))

    w = w_in[l]
    g0 = RW_COLS
    w_rw = jnp.pad(w[:, :RW_COLS], ((0, 0), (0, RW_SEG - RW_COLS)))
    w_gla = jnp.concatenate([w[:, g0:g0 + GLA_QKV_COLS],
                             w[:, g0 + GLA_QKV_COLS + GLA_GATE_RANK:g0 + GLA_COLS],
                             w[:, g0 + GLA_QKV_COLS:g0 + GLA_QKV_COLS + GLA_GATE_RANK],
                             jnp.zeros((D, GLA_SEG - GLA_COLS), F32)], axis=1)
    w_gate = w[:, MIX_COLS:]
    tm_p = min(512, L)
    p_rw = _proj(x, mod_lat, w_rw, False, tm_p)
    p_gla = _proj(x, mod_lat, w_gla, False, tm_p)
    gate = _proj(x, mod_lat, w_gate, True, tm_p)
    pc_rw = _proj(ctx, mod_ctx, w_rw, False, CT)
    pc_gla = _proj(ctx, mod_ctx, w_gla, False, CT)

    mu = jnp.pad(rw_mu[l], (0, RW_SEG - RW_COLS)).reshape(1, -1)
    lora = jnp.zeros((256, 5 * W), F32)
    e1 = RW_DECAY_RANK
    e2 = e1 + RW_AAA_RANK
    e3 = e2 + RW_GATE_RANK
    for d in range(2):
        lora = lora.at[0:e1, d * W:(d + 1) * W].set(rw_w2[l, d])
        lora = lora.at[e1:e2, (2 + d) * W:(3 + d) * W].set(rw_a2[l, d])
    lora = lora.at[e2:e3, 4 * W:].set(rw_g2[l])
    lora_p = (_split(lora, N_LORA) + [jnp.zeros_like(lora, BF16)])[:2]
    bd64 = _block_diag_ones(W, RW_HEAD_DIM)
    rw_consts = [mu, lora_p[0], lora_p[1], rw_w0[l].reshape(1, -1), rw_a0[l].reshape(1, -1),
                 row(rw_k_k[l]), row(rw_k_a[l]), row(rw_r_k[l]), bd64]
    tm_f = min(256, L)
    r, k, v, kkn, g, bonus, lwd, a = _feat_rw(p_rw, True, tm_f, rw_consts)
    rc, kc, vc, kknc, _, _, lwdc, ac = _feat_rw(pc_rw, False, CT, rw_consts)

    g2 = jnp.zeros((GLA_SEG - GLA_QKV_COLS - GLA_VAL_WIDTH, 2 * GLA_KEY_WIDTH), F32)
    g2 = g2.at[:GLA_GATE_RANK].set(jnp.concatenate([gla_g2[l, 0], gla_g2[l, 1]], axis=1))
    g2_p = (_split(g2, N_LORA) + [jnp.zeros_like(g2, BF16)])[:2]
    cw = jnp.pad(gla_conv[l].reshape(9, GLA_QKV_COLS), ((0, 7), (0, 0)))
    gla_consts = [cw, g2_p[0], g2_p[1], gla_gb[l].reshape(1, -1)]
    qk, vg, la = _feat_gla(p_gla, True, tm_f, gla_consts)
    qkc, vgc, lac = _feat_gla(pc_gla, False, CT, gla_consts)

    ka = row(rw_k_a[l])
    P = 2 * RW_HEAD_DIM
    s0 = jnp.zeros((B, 2, RW_HEADS // 2, P, P), F32)
    _, s_ctx = _rwkv_scan(rc, kc, vc, kknc, lwdc, ac, ka, s0)
    y_rw, _ = _rwkv_scan(r, k, v, kkn, lwd, a, ka, s_ctx)
    hi = jnp.arange(GLA_KEY_WIDTH) // GLA_KEY_DIM
    hj = jnp.arange(GLA_VAL_WIDTH) // GLA_VAL_DIM
    hm = (hi[:, None] == hj[None, :]).astype(BF16)
    g0s = jnp.zeros((B, 2, GLA_VAL_WIDTH, GLA_KEY_WIDTH), F32)
    _, gs_ctx = _gla_scan(qkc, vgc, lac, hm, g0s)
    y_gla, _ = _gla_scan(qk, vg, la, hm, gs_ctx)

    bd128 = _block_diag_ones(GLA_VAL_WIDTH, GLA_VAL_DIM)
    m_consts = [bd64, bd128, row(rw_gn_w[l]), row(rw_gn_b[l]), row(gla_gn_w[l]), row(gla_gn_b[l])]
    m_weights = [_split(w_br_rw[l], N_MERGE), _split(w_br_gla[l], N_MERGE), _split(w_out[l], N_MERGE)]
    x1, h2, h2p = _merge(x, mod_lat, y_rw, bonus, g, y_gla, p_gla, gate, m_consts, m_weights,
                    row(ln1_w[l]), row(ln1_b[l]), min(256, L))

    n = B * L
    h2f = h2.reshape(n, D)
    tm_r = min(256, n)
    bias_b = jnp.broadcast_to(router_bias[l].reshape(-1, 1), (N_EXPERTS, tm_r))
    e_t, g_t, cnt = _router(h2f, _split(router[l].T, N_ROUTER), bias_b, tm_r)
    sizes = jnp.sum(cnt, axis=1).astype(jnp.int32)
    tm_c = min(64, L)
    row_tok, pos_t, block_e, nused = _dispatch_plan(e_t, sizes, tm_c)
    xs = _sc_gather(h2p.reshape(n, D // 2), row_tok)
    y = _experts(block_e, nused, xs, w_gate_up[l], w_down[l])
    yg = _sc_gather(y, pos_t)
    gate_tok = g_t.T.reshape(B, L, TOP_K)
    return _final(yg, x1, h2, mod_lat, gate_tok, _split(sh_gate_up[l], N_SHARED),
                  _split(sh_down[l], N_SHARED), row(ln2_w[l]), row(ln2_b[l]), tm_c)
```

```python
import functools
import math

import jax
import jax.numpy as jnp
from jax import lax
from jax.experimental import pallas as pl
from jax.experimental.pallas import tpu as pltpu
from jax.experimental.pallas import tpu_sc as plsc

F32 = jnp.float32
BF16 = jnp.bfloat16

D_MODEL = 1024
GRID_W = 64
RW_WIDTH = 512
RW_HEADS = 8
RW_HEAD_DIM = 64
RW_DECAY_RANK = 32
RW_AAA_RANK = 32
RW_GATE_RANK = 96
RW_GN_EPS = 64e-5
RW_COLS = 1696
RW_SEG = 1792
GLA_HEADS = 4
GLA_KEY_WIDTH = 256
GLA_VAL_WIDTH = 512
GLA_KEY_DIM = 64
GLA_VAL_DIM = 128
GLA_GATE_RANK = 16
GLA_TAU = 16.0
GLA_QKV_COLS = 1024
GLA_COLS = 1552
GLA_SEG = 1664
MIX_COLS = RW_COLS + GLA_COLS
N_EXPERTS = 256
TOP_K = 8
N_GROUPS = 8
TOPK_GROUPS = 4
EXPERT_DIM = 256
ROUTED_SCALE = 2.5
LN_EPS = 1e-5
DEPTH = 1
ALPHA = (2 * DEPTH) ** 0.25

CHUNK = 64
GLA_SUB = 16
ROW_BLOCK = 256
VMEM_LIMIT = 48 * 1024 * 1024
SC_CORES = 2
SC_SUBCORES = 16
SC_WORKERS = SC_CORES * SC_SUBCORES
SC_WINDOW = 32
SC_ROW_ALIGN = 2 * SC_WORKERS * SC_WINDOW

N_ADA = 3
N_PROJ = 1
N_LORA = 1
N_SEG = 2
N_CUM = 2
N_RW = 1
N_GLA = 1
N_GLA_INTRA = 1
N_MERGE = 1
N_ROUTER = 3
N_SHARED = 1


def _split(x, n):
    parts = []
    r = x
    for i in range(n):
        p = r.astype(BF16)
        parts.append(p)
        if i < n - 1:
            r = r - p.astype(F32)
    return parts


def _dot(a, b, ta=False, tb=False):
    dn = (((0 if ta else 1,), (1 if tb else 0,)), ((), ()))
    return lax.dot_general(a, b, dn, preferred_element_type=F32)


def _mmp(ap, bp, ta=False, tb=False):
    n = max(len(ap), len(bp))
    out = None
    for i in range(len(ap)):
        for j in range(len(bp)):
            if i + j <= n - 1:
                t = _dot(ap[i], bp[j], ta, tb)
                out = t if out is None else out + t
    return out


def _mm(a, b, n, ta=False, tb=False):
    return _mmp(_split(a, n), _split(b, n), ta, tb)


def _mm_exact_l(m_bf16, x, n):
    return _mmp([m_bf16], _split(x, n))


def _mm_exact_r(x, m_bf16, n):
    return _mmp(_split(x, n), [m_bf16])


def _pack_bf16_pairs(x):
    w = x.shape[1] // 2
    hi = lax.bitcast_convert_type(x[:, :w].astype(BF16).astype(F32), jnp.int32)
    lo = lax.bitcast_convert_type(x[:, w:].astype(BF16).astype(F32), jnp.int32)
    return hi | lax.shift_right_logical(lo, 16)


def _unpack_bf16_pairs(p):
    hi = lax.bitcast_convert_type(p & jnp.int32(-65536), F32)
    lo = lax.bitcast_convert_type(lax.shift_left(p, 16), F32)
    return jnp.concatenate([hi, lo], axis=1)


def _sigmoid(x):
    return 1.0 / (1.0 + jnp.exp(-x))


def _silu(x):
    return x * _sigmoid(x)


def _log_sigmoid(x):
    return jnp.minimum(x, 0.0) - jnp.log(1.0 + jnp.exp(-jnp.abs(x)))


def _cparams(sem):
    return pltpu.CompilerParams(dimension_semantics=sem, vmem_limit_bytes=VMEM_LIMIT)


def _full(shape):
    nd = len(shape)
    return pl.BlockSpec(shape, lambda *a: (0,) * nd)


def _ada_body(c_ref, w_ref, b_ref, o_ref):
    s = _silu(c_ref[...])
    o_ref[...] = _mm(s, w_ref[...], N_ADA) + b_ref[...]


def _ada(cc, w, b):
    rows, d = cc.shape
    n = w.shape[1] // d
    return pl.pallas_call(
        _ada_body,
        grid=(n,),
        in_specs=[pl.BlockSpec((rows, d), lambda j: (0, 0)),
                  pl.BlockSpec((d, d), lambda j: (0, j)),
                  pl.BlockSpec((1, d), lambda j: (0, j))],
        out_specs=pl.BlockSpec((rows, d), lambda j: (0, j)),
        out_shape=jax.ShapeDtypeStruct((rows, w.shape[1]), F32),
        compiler_params=_cparams(("arbitrary",)),
        name="ada",
    )(cc, w, b.reshape(1, -1))


def _proj_body(act, x_ref, mod_ref, *refs):
    w_refs, o_ref = refs[:-1], refs[-1]
    sh = mod_ref[0, 0:1, :]
    sc = mod_ref[0, 1:2, :]
    h = x_ref[0] * (1.0 + sc) + sh
    y = _mmp(_split(h, N_PROJ), [w[...] for w in w_refs])
    if act:
        y = _sigmoid(y)
    o_ref[0] = y


def _proj(x, mod, w, act, tm):
    B, L, D = x.shape
    n = w.shape[1]
    wp = _split(w, N_PROJ)
    return pl.pallas_call(
        functools.partial(_proj_body, act),
        grid=(B, L // tm),
        in_specs=[pl.BlockSpec((1, tm, D), lambda b, i: (b, i, 0)),
                  pl.BlockSpec((1, 6, D), lambda b, i: (b, 0, 0))]
                 + [pl.BlockSpec((D, n), lambda b, i: (0, 0))] * N_PROJ,
        out_specs=pl.BlockSpec((1, tm, n), lambda b, i: (b, i, 0)),
        out_shape=jax.ShapeDtypeStruct((B, L, n), F32),
        compiler_params=_cparams(("parallel", "arbitrary")),
        name="proj",
    )(x, mod, *wp)


def _neighbours(vertical, width, tm, cur, prev_ref, next_ref):
    if vertical:
        i = pl.program_id(1)
        n = pl.num_programs(1)
        prev = jnp.where(i > 0, prev_ref[0], 0.0)
        nxt = jnp.where(i < n - 1, next_ref[0], 0.0)
        pad = jnp.zeros((8, cur.shape[1]), F32)
        ext = jnp.concatenate([pad, prev, cur, nxt, pad], axis=0)
        off = GRID_W + 8
    else:
        pad = jnp.zeros((8, cur.shape[1]), F32)
        ext = jnp.concatenate([pad, cur, pad], axis=0)
        off = 8
    col = lax.broadcasted_iota(jnp.int32, (tm, 1), 0) % width

    def get(dr, dc):
        s = off + GRID_W * dr + dc
        v = ext[s:s + tm]
        if dc == -1:
            v = jnp.where(col == 0, 0.0, v)
        elif dc == 1:
            v = jnp.where(col == width - 1, 0.0, v)
        return v

    return get


def _feat_rw_body(vertical, width, tm, *refs):
    if vertical:
        prev_ref, cur_ref, next_ref = refs[:3]
        refs = refs[3:]
    else:
        cur_ref = refs[0]
        prev_ref = next_ref = None
        refs = refs[1:]
    (mu_ref, lw_hi_ref, lw_lo_ref, w0_ref, a0_ref, kk_ref, ka_ref, rk_ref, bd_ref,
     r_out, k_out, v_out, kkn_out, g_out, bonus_out, lwd_out, a_out) = refs
    cur = cur_ref[0]
    get = _neighbours(vertical, width, tm, cur, prev_ref, next_ref)
    left, right = get(0, -1), get(0, 1)
    up, down = (get(-1, 0), get(1, 0)) if vertical else (left, right)
    l4 = lax.broadcasted_iota(jnp.int32, (1, cur.shape[1]), 1) % 4
    shifted = jnp.where(l4 == 0, left, jnp.where(l4 == 1, right, jnp.where(l4 == 2, up, down)))
    p = cur + mu_ref[...] * (shifted - cur)

    r = p[:, 0:RW_WIDTH]
    k = p[:, RW_WIDTH:2 * RW_WIDTH]
    v = p[:, 2 * RW_WIDTH:3 * RW_WIDTH]
    slab = p[:, 3 * RW_WIDTH:3 * RW_WIDTH + 256]
    ln = lax.broadcasted_iota(jnp.int32, (1, 256), 1)
    e1 = RW_DECAY_RANK
    e2 = e1 + RW_AAA_RANK
    e3 = e2 + RW_GATE_RANK
    slab = jnp.where(ln < e1, jnp.tanh(slab),
                     jnp.where(ln < e2, slab, jnp.where(ln < e3, _sigmoid(slab), 0.0)))
    lo = _mmp(_split(slab, N_LORA), [lw_hi_ref[...], lw_lo_ref[...]][:N_LORA])
    W = RW_WIDTH
    a_sum = None
    for d in range(2):
        z = w0_ref[:, d * W:(d + 1) * W] + lo[:, d * W:(d + 1) * W]
        lwd_out[d, 0] = -_sigmoid(z) * math.exp(-0.5)
        a = _sigmoid(a0_ref[:, d * W:(d + 1) * W] + lo[:, (2 + d) * W:(3 + d) * W])
        a_out[d, 0] = a
        a_sum = a if a_sum is None else a_sum + a
    g_out[0] = lo[:, 4 * W:5 * W]
    bd = bd_ref[...]
    kk = k * kk_ref[...]
    ss = _mm_exact_r(kk * kk, bd, N_SEG)
    kkn_out[0] = kk / jnp.maximum(jnp.sqrt(ss), 1e-12)
    kmod_sum = k * (2.0 + (a_sum - 2.0) * ka_ref[...])
    bonus_out[0] = _mm_exact_r(r * kmod_sum * rk_ref[...], bd, N_SEG) * v
    r_out[0] = r
    k_out[0] = k
    v_out[0] = v


def _feat_rw(p, vertical, tm, consts):
    B, L, S = p.shape
    width = GRID_W if vertical else tm
    W = RW_WIDTH
    hb = tm // GRID_W
    nhb = L // GRID_W
    tile = pl.BlockSpec((1, tm, S), lambda b, i: (b, i, 0))
    if vertical:
        in_specs = [pl.BlockSpec((1, GRID_W, S), lambda b, i: (b, jnp.maximum(i * hb - 1, 0), 0)),
                    tile,
                    pl.BlockSpec((1, GRID_W, S), lambda b, i: (b, jnp.minimum((i + 1) * hb, nhb - 1), 0))]
        args = [p, p, p]
    else:
        in_specs = [tile]
        args = [p]
    in_specs += [_full(c.shape) for c in consts]
    o1 = pl.BlockSpec((1, tm, W), lambda b, i: (b, i, 0))
    o2 = pl.BlockSpec((2, 1, tm, W), lambda b, i: (0, b, i, 0))
    s1 = jax.ShapeDtypeStruct((B, L, W), F32)
    s2 = jax.ShapeDtypeStruct((2, B, L, W), F32)
    return pl.pallas_call(
        functools.partial(_feat_rw_body, vertical, width, tm),
        grid=(B, L // tm),
        in_specs=in_specs,
        out_specs=[o1] * 6 + [o2] * 2,
        out_shape=[s1] * 6 + [s2] * 2,
        compiler_params=_cparams(("parallel", "arbitrary")),
        name="feat_rw",
    )(*args, *consts)


def _feat_gla_body(vertical, width, tm, *refs):
    if vertical:
        prev_ref, cur_ref, next_ref = refs[:3]
        refs = refs[3:]
    else:
        cur_ref = refs[0]
        prev_ref = next_ref = None
        refs = refs[1:]
    cw_ref, g2_hi_ref, g2_lo_ref, gb_ref, qk_out, v_out, la_out = refs
    cur = cur_ref[0][:, 0:GLA_QKV_COLS]
    if vertical:
        class _Slice:
            def __init__(self, ref):
                self.ref = ref

            def __getitem__(self, idx):
                return self.ref[idx][:, 0:GLA_QKV_COLS]
        get = _neighbours(True, width, tm, cur, _Slice(prev_ref), _Slice(next_ref))
    else:
        get = _neighbours(False, width, tm, cur, None, None)
    acc = None
    for dr in ((-1, 0, 1) if vertical else (0,)):
        for dc in (-1, 0, 1):
            t = get(dr, dc) * cw_ref[(dr + 1) * 3 + (dc + 1):(dr + 1) * 3 + (dc + 1) + 1, :]
            acc = t if acc is None else acc + t
    qkv = _silu(acc)
    kw = GLA_KEY_WIDTH
    lane = lax.broadcasted_iota(jnp.int32, (1, 2 * kw), 1)
    qk_out[0] = qkv[:, 0:2 * kw] * jnp.where(lane < kw, GLA_KEY_DIM ** -0.5, 1.0)
    v_out[0] = qkv[:, 2 * kw:]
    pgl = cur_ref[0][:, GLA_QKV_COLS + GLA_VAL_WIDTH:GLA_SEG]
    z = _mmp(_split(pgl, N_LORA), [g2_hi_ref[...], g2_lo_ref[...]][:N_LORA]) + gb_ref[...]
    la = _log_sigmoid(z) * (1.0 / GLA_TAU)
    la_out[0, 0] = la[:, 0:kw]
    la_out[1, 0] = la[:, kw:]


def _feat_gla(p, vertical, tm, consts):
    B, L, S = p.shape
    width = GRID_W if vertical else tm
    hb = tm // GRID_W
    nhb = L // GRID_W
    tile = pl.BlockSpec((1, tm, S), lambda b, i: (b, i, 0))
    if vertical:
        in_specs = [pl.BlockSpec((1, GRID_W, S), lambda b, i: (b, jnp.maximum(i * hb - 1, 0), 0)),
                    tile,
                    pl.BlockSpec((1, GRID_W, S), lambda b, i: (b, jnp.minimum((i + 1) * hb, nhb - 1), 0))]
        args = [p, p, p]
    else:
        in_specs = [tile]
        args = [p]
    in_specs += [_full(c.shape) for c in consts]
    kw, vw = GLA_KEY_WIDTH, GLA_VAL_WIDTH
    return pl.pallas_call(
        functools.partial(_feat_gla_body, vertical, width, tm),
        grid=(B, L // tm),
        in_specs=in_specs,
        out_specs=[pl.BlockSpec((1, tm, 2 * kw), lambda b, i: (b, i, 0)),
                   pl.BlockSpec((1, tm, vw), lambda b, i: (b, i, 0)),
                   pl.BlockSpec((2, 1, tm, kw), lambda b, i: (0, b, i, 0))],
        out_shape=[jax.ShapeDtypeStruct((B, L, 2 * kw), F32),
                   jax.ShapeDtypeStruct((B, L, vw), F32),
                   jax.ShapeDtypeStruct((2, B, L, kw), F32)],
        compiler_params=_cparams(("parallel", "arbitrary")),
        name="feat_gla",
    )(*args, *consts)


def _order_masks(n, sgn):
    ti = lax.broadcasted_iota(jnp.int32, (n, n), 0)
    si = lax.broadcasted_iota(jnp.int32, (n, n), 1)
    rel = (si - ti) * sgn
    return rel < 0, rel <= 0


def _chunk_index(nc):
    return lambda b, d, c: c + d * (nc - 1 - 2 * c)


def _rwkv_body(r_ref, k_ref, v_ref, kk_ref, lw_ref, a_ref, ka_ref, s0_ref, y_ref, st_ref, s_scr):
    C = CHUNK
    d = pl.program_id(1)
    c = pl.program_id(2)

    @pl.when(c == 0)
    def _():
        s_scr[...] = s0_ref[0, 0]

    sgn = 1 - 2 * d
    r, k, v, kk = r_ref[0], k_ref[0], v_ref[0], kk_ref[0]
    lw, a = lw_ref[0, 0], a_ref[0, 0]
    before, upto = _order_masks(C, sgn)
    cum = _mm_exact_l(upto.astype(BF16), lw, N_CUM)
    tot = jnp.where(d == 0, cum[C - 1:C], cum[0:1])
    p_in = jnp.exp(cum)
    p_ex = jnp.exp(cum - lw)
    p_inv = jnp.exp(-cum)
    p_rem = jnp.exp(tot - cum)
    p_all = jnp.exp(tot)
    bvec = kk * a
    kmod = k * (1.0 + (a - 1.0) * ka_ref[...])
    a_t = -kk * p_ex
    r_t = r * p_in
    b_t = bvec * p_inv
    k_t = kmod * p_inv
    b_p = bvec * p_rem
    k_p = kmod * p_rem

    P = 2 * RW_HEAD_DIM
    lane = lax.broadcasted_iota(jnp.int32, (C, P), 1)
    h0 = lane < RW_HEAD_DIM
    ri = lax.broadcasted_iota(jnp.int32, (P, P), 0)
    ci = lax.broadcasted_iota(jnp.int32, (P, P), 1)
    same = (ri // C) == (ci // C)
    rel = ((ci % C) - (ri % C)) * sgn
    strict = same & (rel < 0)
    eye = (ri == ci).astype(F32)

    def stack2(x):
        return jnp.concatenate([jnp.where(h0, x, 0.0), jnp.where(h0, 0.0, x)], axis=0)

    pairs = range(RW_HEADS // 2)
    sls = [slice(p * P, (p + 1) * P) for p in pairs]
    a_st = [stack2(a_t[:, sl]) for sl in sls]
    v_st = [stack2(v[:, sl]) for sl in sls]
    g = [_mm(jnp.concatenate([a_st[p], stack2(r_t[:, sls[p]])], axis=0),
             jnp.concatenate([stack2(b_t[:, sls[p]]), stack2(k_t[:, sls[p]])], axis=0), N_RW, tb=True)
         for p in pairs]
    nmat = [jnp.where(strict, g[p][0:P, 0:P], 0.0) for p in pairs]
    a_ak = [jnp.where(strict, g[p][0:P, P:], 0.0) for p in pairs]
    ri2 = lax.broadcasted_iota(jnp.int32, (P, 2 * P), 0)
    ci2 = lax.broadcasted_iota(jnp.int32, (P, 2 * P), 1)
    incl2 = ((ri2 // C) == ((ci2 % P) // C)) & ((((ci2 % C) - (ri2 % C)) * sgn) <= 0)
    a_r = [jnp.where(incl2, g[p][P:, :], 0.0) for p in pairs]
    akv = [_mm(a_ak[p], v_st[p], N_RW) for p in pairs]
    t = [eye + nmat[p] for p in pairs]
    npow = nmat
    for _ in range(max(C.bit_length() - 2, 0)):
        npow = [_mm(npow[p], npow[p], N_RW) for p in pairs]
        t = [t[p] + _mm(t[p], npow[p], N_RW) for p in pairs]
    x = [_mm(t[p], jnp.concatenate([a_st[p], akv[p]], axis=1), N_RW) for p in pairs]
    s = [s_scr[p] for p in pairs]
    ur = [_mm(jnp.concatenate([x[p][0:C, 0:P] + x[p][C:, 0:P], r_t[:, sls[p]]], axis=0), s[p], N_RW, tb=True)
          for p in pairs]
    u = [ur[p][0:C] + x[p][0:C, P:] + x[p][C:, P:] for p in pairs]
    y_st = [_mm(a_r[p], jnp.concatenate([stack2(u[p]), v_st[p]], axis=0), N_RW) for p in pairs]
    for p in pairs:
        y_ref[0, 0, :, sls[p]] = ur[p][C:] + y_st[p][0:C] + y_st[p][C:]
    upd = [_mm(jnp.concatenate([u[p], v[:, sls[p]]], axis=0).T,
               jnp.concatenate([b_p[:, sls[p]], k_p[:, sls[p]]], axis=0), N_RW) for p in pairs]
    for p in pairs:
        s_new = s[p] * p_all[:, sls[p]] + jnp.where(same, upd[p], 0.0)
        s_scr[p] = s_new
        st_ref[0, 0, p] = s_new


def _rwkv_scan(r, k, v, kk, lw, a, ka, s0):
    B, L, W = r.shape
    nc = L // CHUNK
    cidx = _chunk_index(nc)
    shared = pl.BlockSpec((1, CHUNK, W), lambda b, d, c: (b, cidx(b, d, c), 0))
    perdir = pl.BlockSpec((1, 1, CHUNK, W), lambda b, d, c: (d, b, cidx(b, d, c), 0))
    npair = RW_HEADS // 2
    P = 2 * RW_HEAD_DIM
    sspec = pl.BlockSpec((1, 1, npair, P, P), lambda b, d, c: (b, d, 0, 0, 0))
    return pl.pallas_call(
        _rwkv_body,
        grid=(B, 2, nc),
        in_specs=[shared, shared, shared, shared, perdir, perdir,
                  pl.BlockSpec((1, W), lambda b, d, c: (0, 0)), sspec],
        out_specs=[pl.BlockSpec((1, 1, CHUNK, W), lambda b, d, c: (d, b, cidx(b, d, c), 0)), sspec],
        out_shape=[jax.ShapeDtypeStruct((2, B, L, W), F32),
                   jax.ShapeDtypeStruct((B, 2, npair, P, P), F32)],
        scratch_shapes=[pltpu.VMEM((npair, P, P), F32)],
        compiler_params=_cparams(("parallel", "parallel", "arbitrary")),
        name="rwkv_scan",
    )(r, k, v, kk, lw, a, ka, s0)


def _gla_body(qk_ref, v_ref, g_ref, hm_ref, s0_ref, o_ref, st_ref, s_scr):
    C = CHUNK
    SB = GLA_SUB
    H = GLA_HEADS
    d = pl.program_id(1)
    c = pl.program_id(2)

    @pl.when(c == 0)
    def _():
        s_scr[...] = s0_ref[0, 0]

    sgn = 1 - 2 * d
    kw, vw = GLA_KEY_WIDTH, GLA_VAL_WIDTH
    q = qk_ref[0][:, 0:kw]
    k = qk_ref[0][:, kw:]
    v = v_ref[0]
    g = g_ref[0, 0]
    _, upto = _order_masks(C, sgn)
    b = _mm_exact_l(upto.astype(BF16), g, N_CUM)
    pos = lax.broadcasted_iota(jnp.int32, (C, 1), 0) * sgn + d * (C - 1)

    def b_at(p):
        return jnp.sum(jnp.where(pos == p, b, 0.0), axis=0, keepdims=True)

    NEG = -jnp.inf
    tot = b_at(C - 1)
    s = s_scr[...]
    inter = _mm(q * jnp.exp(b), s, N_GLA, tb=True)

    half = pos >= C // 2
    odd = (pos // SB) % 2 == 1
    r1 = b_at(C // 2 - 1)
    r2 = jnp.where(half, b_at(3 * SB - 1), b_at(SB - 1))
    q1 = q * jnp.exp(jnp.where(half, b - r1, NEG))
    k1 = k * jnp.exp(jnp.where(half, NEG, r1 - b))
    q2 = q * jnp.exp(jnp.where(odd, b - r2, NEG))
    k2 = k * jnp.exp(jnp.where(odd, NEG, r2 - b))
    lane_h = lax.broadcasted_iota(jnp.int32, (C, kw), 1) // GLA_KEY_DIM

    def stack_heads(x):
        return jnp.concatenate([jnp.where(lane_h == h, x, 0.0) for h in range(H)], axis=0)

    att1 = _mm(stack_heads(q1), k1, N_GLA, tb=True)
    att2 = _mm(stack_heads(q2), k2, N_GLA, tb=True)
    pr = (lax.broadcasted_iota(jnp.int32, (H * C, C), 0) % C) * sgn + d * (C - 1)
    pc = lax.broadcasted_iota(jnp.int32, (H * C, C), 1) * sgn + d * (C - 1)
    att = att1 + jnp.where((pr >= C // 2) == (pc >= C // 2), att2, 0.0)
    res = _mm(att, v, N_GLA)
    lane_hv = lax.broadcasted_iota(jnp.int32, (C, vw), 1) // GLA_VAL_DIM
    off = None
    for h in range(H):
        t = jnp.where(lane_hv == h, res[h * C:(h + 1) * C], 0.0)
        off = t if off is None else off + t

    hm = hm_ref[...]
    il = lax.broadcasted_iota(jnp.int32, (SB, 1), 0)
    diag = []
    for blk in range(C // SB):
        rs = slice(blk * SB, (blk + 1) * SB)
        bb, qb, kb, vb = b[rs], q[rs], k[rs], v[rs]
        pieces = []
        for j in range(SB):
            valid = (il - j) * sgn >= 0
            pieces.append(jnp.exp(jnp.where(valid, bb - bb[j:j + 1], NEG)) * qb * kb[j:j + 1])
        a = _mm_exact_r(jnp.concatenate(pieces, axis=0), hm, N_GLA_INTRA)
        acc = None
        for j in range(SB):
            t = a[j * SB:(j + 1) * SB] * vb[j:j + 1]
            acc = t if acc is None else acc + t
        diag.append(acc)
    o_ref[0, 0] = inter + off + jnp.concatenate(diag, axis=0)

    upd = _mm(v.T, k * jnp.exp(tot - b), N_GLA)
    ri = lax.broadcasted_iota(jnp.int32, upd.shape, 0) // GLA_VAL_DIM
    ci = lax.broadcasted_iota(jnp.int32, upd.shape, 1) // GLA_KEY_DIM
    s_new = s * jnp.exp(tot) + jnp.where(ri == ci, upd, 0.0)
    s_scr[...] = s_new
    st_ref[0, 0] = s_new


def _gla_scan(qk, v, la, hm, s0):
    B, L, _ = qk.shape
    kw, vw = GLA_KEY_WIDTH, GLA_VAL_WIDTH
    nc = L // CHUNK
    cidx = _chunk_index(nc)
    sspec = pl.BlockSpec((1, 1, vw, kw), lambda b, d, c: (b, d, 0, 0))
    return pl.pallas_call(
        _gla_body,
        grid=(B, 2, nc),
        in_specs=[pl.BlockSpec((1, CHUNK, 2 * kw), lambda b, d, c: (b, cidx(b, d, c), 0)),
                  pl.BlockSpec((1, CHUNK, vw), lambda b, d, c: (b, cidx(b, d, c), 0)),
                  pl.BlockSpec((1, 1, CHUNK, kw), lambda b, d, c: (d, b, cidx(b, d, c), 0)),
                  pl.BlockSpec((kw, vw), lambda b, d, c: (0, 0)), sspec],
        out_specs=[pl.BlockSpec((1, 1, CHUNK, vw), lambda b, d, c: (d, b, cidx(b, d, c), 0)), sspec],
        out_shape=[jax.ShapeDtypeStruct((2, B, L, vw), F32),
                   jax.ShapeDtypeStruct((B, 2, vw, kw), F32)],
        scratch_shapes=[pltpu.VMEM((vw, kw), F32)],
        compiler_params=_cparams(("parallel", "parallel", "arbitrary")),
        name="gla_scan",
    )(qk, v, la, hm, s0)


def _seg_norm(y, bd, dim, eps):
    mu = _mm_exact_r(y, bd, N_SEG) * (1.0 / dim)
    dlt = y - mu
    var = _mm_exact_r(dlt * dlt, bd, N_SEG) * (1.0 / dim)
    return dlt * lax.rsqrt(var + eps)


def _layer_norm(x, w, b):
    mu = jnp.mean(x, axis=-1, keepdims=True)
    dlt = x - mu
    var = jnp.mean(dlt * dlt, axis=-1, keepdims=True)
    return dlt * lax.rsqrt(var + LN_EPS) * w + b


def _merge_body(x_ref, mod_ref, yrw_ref, bonus_ref, g_ref, ygla_ref, og_ref, gate_ref,
                bd64_ref, bd128_ref, rgw_ref, rgb_ref, ggw_ref, ggb_ref, *refs):
    n = N_MERGE
    wrw = [r[...] for r in refs[0:n]]
    wgla = [r[...] for r in refs[n:2 * n]]
    wout = [r[...] for r in refs[2 * n:3 * n]]
    ln_w_ref, ln_b_ref, x1_ref, h2p_ref = refs[3 * n:]
    y = _seg_norm(yrw_ref[0, 0] + yrw_ref[1, 0], bd64_ref[...], RW_HEAD_DIM, RW_GN_EPS)
    y = (y * rgw_ref[...] + rgb_ref[...] + bonus_ref[0]) * g_ref[0]
    y_rw = _mmp(_split(y, n), wrw)
    y = _seg_norm(ygla_ref[0, 0] + ygla_ref[1, 0], bd128_ref[...], GLA_VAL_DIM, LN_EPS)
    y = (y * ggw_ref[...] + ggb_ref[...]) * _silu(og_ref[0])
    y_gla = _mmp(_split(y, n), wgla)
    gate = gate_ref[0]
    mixed = gate[:, 0:D_MODEL] * y_rw + gate[:, D_MODEL:] * y_gla
    mix = _mmp(_split(mixed, n), wout)
    g1 = mod_ref[0, 2:3, :]
    x1 = _layer_norm(ALPHA * x_ref[0] + g1 * mix, ln_w_ref[...], ln_b_ref[...])
    x1_ref[0] = x1
    h2p_ref[0] = _pack_bf16_pairs(_moe_input(x1, mod_ref))


def _merge(x, mod, yrw, bonus, g, ygla, p_gla, gate, consts, weights, ln_w, ln_b, tm):
    B, L, D = x.shape
    W = RW_WIDTH
    tok = lambda w: pl.BlockSpec((1, tm, w), lambda b, i: (b, i, 0))
    dirs = pl.BlockSpec((2, 1, tm, W), lambda b, i: (0, b, i, 0))
    wl = [w for ws in weights for w in ws]
    return pl.pallas_call(
        _merge_body,
        grid=(B, L // tm),
        in_specs=[tok(D), pl.BlockSpec((1, 6, D), lambda b, i: (b, 0, 0)), dirs, tok(W), tok(W), dirs,
                  pl.BlockSpec((1, tm, W), lambda b, i: (b, i, GLA_QKV_COLS // W)), tok(2 * D)]
                 + [_full(c.shape) for c in consts] + [_full(w.shape) for w in wl]
                 + [_full(ln_w.shape), _full(ln_b.shape)],
        out_specs=[tok(D), tok(D // 2)],
        out_shape=[jax.ShapeDtypeStruct((B, L, D), F32), jax.ShapeDtypeStruct((B, L, D // 2), jnp.int32)],
        compiler_params=_cparams(("parallel", "arbitrary")),
        name="merge",
    )(x, mod, yrw, bonus, g, ygla, p_gla, gate, *consts, *wl, ln_w, ln_b)


def _moe_input(x1, mod_ref):
    return x1 * (1.0 + mod_ref[0, 4:5, :]) + mod_ref[0, 3:4, :]


def _router_body(x1_ref, mod_ref, *refs):
    n = N_ROUTER
    rt = [r[...] for r in refs[0:n]]
    bias_ref, e_out, g_out, cnt_out = refs[n:]
    tm = x1_ref.shape[0]
    E, G, PG = N_EXPERTS, N_GROUPS, N_EXPERTS // N_GROUPS
    h = _moe_input(x1_ref[...], mod_ref)
    logits = _mmp(rt, _split(h, n), tb=True)
    scores = _sigmoid(logits)
    sel = scores + bias_ref[:, 0:tm]
    NEG = -jnp.inf
    ip = lax.broadcasted_iota(jnp.int32, (PG, tm), 0)
    group_rows = []
    for gidx in range(G):
        sg = sel[gidx * PG:(gidx + 1) * PG]
        m1 = jnp.max(sg, axis=0, keepdims=True)
        first = jnp.min(jnp.where(sg == m1, ip, PG), axis=0, keepdims=True)
        m2 = jnp.max(jnp.where(ip == first, NEG, sg), axis=0, keepdims=True)
        group_rows.append(m1 + m2)
    gs = jnp.concatenate(group_rows, axis=0)
    gi = lax.broadcasted_iota(jnp.int32, (G, tm), 0)
    keep = jnp.zeros((G, tm), F32)
    for _ in range(TOPK_GROUPS):
        m = jnp.max(gs, axis=0, keepdims=True)
        idx = jnp.min(jnp.where(gs == m, gi, G), axis=0, keepdims=True)
        hit = gi == idx
        keep = jnp.where(hit, 1.0, keep)
        gs = jnp.where(hit, NEG, gs)
    cur = jnp.concatenate(
        [jnp.where(keep[gidx:gidx + 1] > 0.5, sel[gidx * PG:(gidx + 1) * PG], NEG) for gidx in range(G)],
        axis=0)
    ei = lax.broadcasted_iota(jnp.int32, (E, tm), 0)
    idxs, gates = [], []
    picked = jnp.zeros((E, tm), F32)
    for _ in range(TOP_K):
        m = jnp.max(cur, axis=0, keepdims=True)
        idx = jnp.min(jnp.where(cur == m, ei, E), axis=0, keepdims=True)
        hit = ei == idx
        idxs.append(idx)
        gates.append(jnp.sum(jnp.where(hit, scores, 0.0), axis=0, keepdims=True))
        cur = jnp.where(hit, NEG, cur)
        picked = jnp.where(hit, 1.0, picked)
    gate = jnp.concatenate(gates, axis=0)
    e_out[...] = jnp.concatenate(idxs, axis=0)
    g_out[...] = gate / jnp.sum(gate, axis=0, keepdims=True) * ROUTED_SCALE

    @pl.when(pl.program_id(0) == 0)
    def _():
        cnt_out[...] = jnp.zeros(cnt_out.shape, F32)

    part = picked[:, 0:128]
    for j in range(1, tm // 128):
        part = part + picked[:, j * 128:(j + 1) * 128]
    cnt_out[...] += part


def _router(x1f, mod, seq, router_t_parts, bias_b, tm):
    n, D = x1f.shape
    E = N_EXPERTS
    return pl.pallas_call(
        _router_body,
        grid=(n // tm,),
        in_specs=[pl.BlockSpec((tm, D), lambda i: (i, 0)),
                  pl.BlockSpec((1, 6, D), lambda i: (i * tm // seq, 0, 0))]
                 + [pl.BlockSpec((E, D), lambda i: (0, 0))] * len(router_t_parts)
                 + [pl.BlockSpec(bias_b.shape, lambda i: (0, 0))],
        out_specs=[pl.BlockSpec((TOP_K, tm), lambda i: (0, i))] * 2
                  + [pl.BlockSpec((E, 128), lambda i: (0, 0))],
        out_shape=[jax.ShapeDtypeStruct((TOP_K, n), jnp.int32), jax.ShapeDtypeStruct((TOP_K, n), F32),
                   jax.ShapeDtypeStruct((E, 128), F32)],
        compiler_params=_cparams(("arbitrary",)),
        name="router",
    )(x1f, mod, *router_t_parts, bias_b)


def _sc_gather_body(per_w, table_hbm, idx_hbm, out_hbm, idx_v, rows_v, sem_g, sem_o):
    wid = lax.axis_index("s") * SC_CORES + lax.axis_index("c")
    base = wid * per_w
    win = SC_WINDOW
    pltpu.sync_copy(idx_hbm.at[wid], idx_v)

    @pl.loop(0, per_w // win, step=2)
    def _(j):
        g0 = pltpu.async_copy(table_hbm.at[idx_v.at[j]], rows_v.at[0], sem_g.at[0])
        g1 = pltpu.async_copy(table_hbm.at[idx_v.at[j + 1]], rows_v.at[1], sem_g.at[1])
        g0.wait()
        o0 = pltpu.async_copy(rows_v.at[0], out_hbm.at[pl.ds(base + j * win, win)], sem_o.at[0])
        g1.wait()
        o1 = pltpu.async_copy(rows_v.at[1], out_hbm.at[pl.ds(base + (j + 1) * win, win)], sem_o.at[1])
        o0.wait()
        o1.wait()


def _sc_gather(table, idx):
    nrows = idx.shape[0]
    D = table.shape[1]
    per_w = nrows // SC_WORKERS
    mesh = plsc.VectorSubcoreMesh(core_axis_name="c", subcore_axis_name="s",
                                  num_cores=SC_CORES, num_subcores=SC_SUBCORES)
    return pl.kernel(
        functools.partial(_sc_gather_body, per_w),
        out_type=jax.ShapeDtypeStruct((nrows, D), table.dtype),
        mesh=mesh,
        scratch_types=[pltpu.VMEM((per_w // SC_WINDOW, SC_WINDOW), jnp.int32),
                       pltpu.VMEM((2, SC_WINDOW, D), table.dtype),
                       pltpu.SemaphoreType.DMA((2,)), pltpu.SemaphoreType.DMA((2,))],
        name="sc_gather",
    )(table, idx.reshape(SC_WORKERS, per_w // SC_WINDOW, SC_WINDOW))


def _experts_body(be_ref, nu_ref, x_ref, wgu_ref, wd_ref, y_ref, wgu_bf, wd_bf):
    i = pl.program_id(0)

    @pl.when(i < nu_ref[0])
    def _():
        @pl.when((i == 0) | (be_ref[i] != be_ref[jnp.maximum(i - 1, 0)]))
        def _():
            wgu_bf[...] = wgu_ref[0].astype(BF16)
            wd_bf[...] = wd_ref[0].astype(BF16)

        F = EXPERT_DIM
        gu = _dot(_unpack_bf16_pairs(x_ref[...]).astype(BF16), wgu_bf[...])
        act = _silu(gu[:, 0:F]) * gu[:, F:]
        y_ref[...] = _pack_bf16_pairs(_dot(act.astype(BF16), wd_bf[...]))

    @pl.when(i >= nu_ref[0])
    def _():
        y_ref[...] = jnp.zeros(y_ref.shape, jnp.int32)


def _experts(block_e, nused, xs, w_gate_up, w_down):
    nb = block_e.shape[0]
    DP = xs.shape[1]
    D = 2 * DP
    R = ROW_BLOCK
    F2 = w_gate_up.shape[2]
    last = lambda i, nu: jnp.minimum(i, nu[0] - 1)
    grid_spec = pltpu.PrefetchScalarGridSpec(
        num_scalar_prefetch=2,
        grid=(nb,),
        in_specs=[pl.BlockSpec((R, DP), lambda i, be, nu: (last(i, nu), 0)),
                  pl.BlockSpec((1, D, F2), lambda i, be, nu: (be[last(i, nu)], 0, 0)),
                  pl.BlockSpec((1, F2 // 2, D), lambda i, be, nu: (be[last(i, nu)], 0, 0))],
        out_specs=pl.BlockSpec((R, DP), lambda i, be, nu: (i, 0)),
        scratch_shapes=[pltpu.VMEM((D, F2), BF16), pltpu.VMEM((F2 // 2, D), BF16)],
    )
    return pl.pallas_call(
        _experts_body,
        grid_spec=grid_spec,
        out_shape=jax.ShapeDtypeStruct((nb * R, DP), jnp.int32),
        compiler_params=_cparams(("arbitrary",)),
        name="experts",
    )(block_e, nused, xs, w_gate_up, w_down)


def _final_body(yg_ref, x1_ref, mod_ref, gate_ref, *refs):
    n = N_SHARED
    sgu = [r[...] for r in refs[0:n]]
    sd = [r[...] for r in refs[n:2 * n]]
    ln_w_ref, ln_b_ref, o_ref = refs[2 * n:]
    tm = x1_ref.shape[1]
    gate = gate_ref[0]
    routed = None
    for kk in range(TOP_K):
        t = _unpack_bf16_pairs(yg_ref[kk * tm:(kk + 1) * tm, :]) * gate[:, kk:kk + 1]
        routed = t if routed is None else routed + t
    h2 = _moe_input(x1_ref[0], mod_ref)
    F = sgu[0].shape[1] // 2
    gu = _mmp(_split(h2, n), sgu)
    act = _silu(gu[:, 0:F]) * gu[:, F:]
    shared = _mmp(_split(act, n), sd)
    g2 = mod_ref[0, 5:6, :]
    o_ref[0] = _layer_norm(ALPHA * x1_ref[0] + g2 * (routed + shared), ln_w_ref[...], ln_b_ref[...])


def _final(yg, x1, mod, gate, sgu, sd, ln_w, ln_b, tm):
    B, L, D = x1.shape
    ni = L // tm
    rows = tm * TOP_K
    tok = lambda w: pl.BlockSpec((1, tm, w), lambda b, i: (b, i, 0))
    ws = list(sgu) + list(sd)
    return pl.pallas_call(
        _final_body,
        grid=(B, ni),
        in_specs=[pl.BlockSpec((rows, D // 2), lambda b, i: (b * ni + i, 0)),
                  tok(D), pl.BlockSpec((1, 6, D), lambda b, i: (b, 0, 0)), tok(TOP_K)]
                 + [_full(w.shape) for w in ws] + [_full(ln_w.shape), _full(ln_b.shape)],
        out_specs=tok(D),
        out_shape=jax.ShapeDtypeStruct((B, L, D), F32),
        compiler_params=_cparams(("parallel", "arbitrary")),
        name="final",
    )(yg, x1, mod, gate, *ws, ln_w, ln_b)


def _dispatch_plan(experts_t, sizes, tm):
    R, E = ROW_BLOCK, N_EXPERTS
    K, n = experts_t.shape
    nk = K * n
    rows = nk + E * (R - 1)
    rows = -(-rows // math.lcm(R, SC_ROW_ALIGN)) * math.lcm(R, SC_ROW_ALIGN)
    nb = rows // R
    ncand = nb * R - nk
    padded = (sizes + R - 1) // R * R
    pad_end = jnp.cumsum(padded)
    tok = jnp.broadcast_to(jnp.arange(n, dtype=jnp.int32)[None, :], (K, n))
    slot = jnp.arange(K, dtype=jnp.int32)[:, None]
    aid = (tok // tm) * (tm * K) + slot * tm + tok % tm
    cand = jnp.arange(ncand, dtype=jnp.int32)
    ce, cp = cand // (R - 1), cand % (R - 1)
    need = jnp.broadcast_to((padded - sizes)[:, None], (E, R - 1)).reshape(-1)
    need = jnp.pad(need, (0, ncand - E * (R - 1)))
    ckey = jnp.where(cp < need, 2 * ce + 1, 2 * E + 1)
    keys = jnp.concatenate([2 * experts_t.reshape(-1), ckey])
    toks = jnp.concatenate([tok.reshape(-1), cand % n])
    aids = jnp.concatenate([aid.reshape(-1), nk + cand])
    _, row_tok, row_aid = lax.sort((keys, toks, aids), num_keys=1)
    _, pos = lax.sort((row_aid, jnp.arange(nb * R, dtype=jnp.int32)), num_keys=1)
    block_e = jnp.minimum(jnp.searchsorted(pad_end, jnp.arange(nb) * R, side='right'), E - 1).astype(jnp.int32)
    nused = (pad_end[-1] // R).astype(jnp.int32).reshape(1)
    return row_tok, pos[:nk], block_e, nused


def _block_diag_ones(n, blk):
    i = jnp.arange(n) // blk
    return (i[:, None] == i[None, :]).astype(BF16)


def kernel(x, c, ctx, c_ctx, w_ada, b_ada, w_in, rw_mu, rw_w0, rw_w2, rw_a0, rw_a2, rw_g2, rw_k_k, rw_k_a,
           rw_r_k, rw_gn_w, rw_gn_b, gla_conv, gla_g2, gla_gb, gla_gn_w, gla_gn_b, w_br_rw, w_br_gla, w_out,
           ln1_w, ln1_b, router, router_bias, w_gate_up, w_down, sh_gate_up, sh_down, ln2_w, ln2_b):
    B, L, D = x.shape
    CT = ctx.shape[1]
    l = 0
    W = RW_WIDTH
    row = lambda t: t.reshape(1, -1)

    rows = -(-(B + 1) // 8) * 8
    cc = jnp.zeros((rows, D), F32).at[:B].set(c).at[B].set(c_ctx)
    mod = _ada(cc, w_ada[l], b_ada[l])
    mod_lat = mod[:B].reshape(B, 6, D)
    mod_ctx = jnp.broadcast_to(mod[B].reshape(1, 6, D), (B, 6, D))

    w = w_in[l]
    g0 = RW_COLS
    w_rw = jnp.pad(w[:, :RW_COLS], ((0, 0), (0, RW_SEG - RW_COLS)))
    w_gla = jnp.concatenate([w[:, g0:g0 + GLA_QKV_COLS],
                             w[:, g0 + GLA_QKV_COLS + GLA_GATE_RANK:g0 + GLA_COLS],
                             w[:, g0 + GLA_QKV_COLS:g0 + GLA_QKV_COLS + GLA_GATE_RANK],
                             jnp.zeros((D, GLA_SEG - GLA_COLS), F32)], axis=1)
    w_gate = w[:, MIX_COLS:]
    tm_p = min(512, L)
    p_rw = _proj(x, mod_lat, w_rw, False, tm_p)
    p_gla = _proj(x, mod_lat, w_gla, False, tm_p)
    gate = _proj(x, mod_lat, w_gate, True, tm_p)
    pc_rw = _proj(ctx, mod_ctx, w_rw, False, CT)
    pc_gla = _proj(ctx, mod_ctx, w_gla, False, CT)

    mu = jnp.pad(rw_mu[l], (0, RW_SEG - RW_COLS)).reshape(1, -1)
    lora = jnp.zeros((256, 5 * W), F32)
    e1 = RW_DECAY_RANK
    e2 = e1 + RW_AAA_RANK
    e3 = e2 + RW_GATE_RANK
    for d in range(2):
        lora = lora.at[0:e1, d * W:(d + 1) * W].set(rw_w2[l, d])
        lora = lora.at[e1:e2, (2 + d) * W:(3 + d) * W].set(rw_a2[l, d])
    lora = lora.at[e2:e3, 4 * W:].set(rw_g2[l])
    lora_p = (_split(lora, N_LORA) + [jnp.zeros_like(lora, BF16)])[:2]
    bd64 = _block_diag_ones(W, RW_HEAD_DIM)
    rw_consts = [mu, lora_p[0], lora_p[1], rw_w0[l].reshape(1, -1), rw_a0[l].reshape(1, -1),
                 row(rw_k_k[l]), row(rw_k_a[l]), row(rw_r_k[l]), bd64]
    tm_f = min(256, L)
    r, k, v, kkn, g, bonus, lwd, a = _feat_rw(p_rw, True, tm_f, rw_consts)
    rc, kc, vc, kknc, _, _, lwdc, ac = _feat_rw(pc_rw, False, CT, rw_consts)

    g2 = jnp.zeros((GLA_SEG - GLA_QKV_COLS - GLA_VAL_WIDTH, 2 * GLA_KEY_WIDTH), F32)
    g2 = g2.at[:GLA_GATE_RANK].set(jnp.concatenate([gla_g2[l, 0], gla_g2[l, 1]], axis=1))
    g2_p = (_split(g2, N_LORA) + [jnp.zeros_like(g2, BF16)])[:2]
    cw = jnp.pad(gla_conv[l].reshape(9, GLA_QKV_COLS), ((0, 7), (0, 0)))
    gla_consts = [cw, g2_p[0], g2_p[1], gla_gb[l].reshape(1, -1)]
    qk, vg, la = _feat_gla(p_gla, True, tm_f, gla_consts)
    qkc, vgc, lac = _feat_gla(pc_gla, False, CT, gla_consts)

    ka = row(rw_k_a[l])
    P = 2 * RW_HEAD_DIM
    s0 = jnp.zeros((B, 2, RW_HEADS // 2, P, P), F32)
    _, s_ctx = _rwkv_scan(rc, kc, vc, kknc, lwdc, ac, ka, s0)
    y_rw, _ = _rwkv_scan(r, k, v, kkn, lwd, a, ka, s_ctx)
    hi = jnp.arange(GLA_KEY_WIDTH) // GLA_KEY_DIM
    hj = jnp.arange(GLA_VAL_WIDTH) // GLA_VAL_DIM
    hm = (hi[:, None] == hj[None, :]).astype(BF16)
    g0s = jnp.zeros((B, 2, GLA_VAL_WIDTH, GLA_KEY_WIDTH), F32)
    _, gs_ctx = _gla_scan(qkc, vgc, lac, hm, g0s)
    y_gla, _ = _gla_scan(qk, vg, la, hm, gs_ctx)

    bd128 = _block_diag_ones(GLA_VAL_WIDTH, GLA_VAL_DIM)
    m_consts = [bd64, bd128, row(rw_gn_w[l]), row(rw_gn_b[l]), row(gla_gn_w[l]), row(gla_gn_b[l])]
    m_weights = [_split(w_br_rw[l], N_MERGE), _split(w_br_gla[l], N_MERGE), _split(w_out[l], N_MERGE)]
    x1, h2p = _merge(x, mod_lat, y_rw, bonus, g, y_gla, p_gla, gate, m_consts, m_weights,
                     row(ln1_w[l]), row(ln1_b[l]), min(256, L))

    n = B * L
    tm_r = min(256, L)
    bias_b = jnp.broadcast_to(router_bias[l].reshape(-1, 1), (N_EXPERTS, tm_r))
    e_t, g_t, cnt = _router(x1.reshape(n, D), mod_lat, L, _split(router[l].T, N_ROUTER), bias_b, tm_r)
    sizes = jnp.sum(cnt, axis=1).astype(jnp.int32)
    tm_c = min(128, L)
    row_tok, pos_t, block_e, nused = _dispatch_plan(e_t, sizes, tm_c)
    xs = _sc_gather(h2p.reshape(n, D // 2), row_tok)
    y = _experts(block_e, nused, xs, w_gate_up[l], w_down[l])
    yg = _sc_gather(y, pos_t)
    gate_tok = g_t.T.reshape(B, L, TOP_K)
    return _final(yg, x1, mod_lat, gate_tok, _split(sh_gate_up[l], N_SHARED),
                  _split(sh_down[l], N_SHARED), row(ln2_w[l]), row(ln2_b[l]), tm_c)
```

```python
import functools
import math

import jax
import jax.numpy as jnp
from jax import lax
from jax.experimental import pallas as pl
from jax.experimental.pallas import tpu as pltpu
from jax.experimental.pallas import tpu_sc as plsc

F32 = jnp.float32
BF16 = jnp.bfloat16

D_MODEL = 1024
GRID_W = 64
RW_WIDTH = 512
RW_HEADS = 8
RW_HEAD_DIM = 64
RW_DECAY_RANK = 32
RW_AAA_RANK = 32
RW_GATE_RANK = 96
RW_GN_EPS = 64e-5
RW_COLS = 1696
RW_SEG = 1792
GLA_HEADS = 4
GLA_KEY_WIDTH = 256
GLA_VAL_WIDTH = 512
GLA_KEY_DIM = 64
GLA_VAL_DIM = 128
GLA_GATE_RANK = 16
GLA_TAU = 16.0
GLA_QKV_COLS = 1024
GLA_COLS = 1552
GLA_SEG = 1664
MIX_COLS = RW_COLS + GLA_COLS
N_EXPERTS = 256
TOP_K = 8
N_GROUPS = 8
TOPK_GROUPS = 4
EXPERT_DIM = 256
ROUTED_SCALE = 2.5
LN_EPS = 1e-5
DEPTH = 1
ALPHA = (2 * DEPTH) ** 0.25

CHUNK = 64
GLA_SUB = 16
ROW_BLOCK = 256
VMEM_LIMIT = 48 * 1024 * 1024
SC_CORES = 2
SC_SUBCORES = 16
SC_WORKERS = SC_CORES * SC_SUBCORES
SC_WINDOW = 32
SC_ROW_ALIGN = 2 * SC_WORKERS * SC_WINDOW

N_ADA = 3
N_PROJ = 1
N_LORA = 1
N_SEG = 2
N_CUM = 2
N_RW = 1
N_GLA = 1
N_GLA_INTRA = 1
N_MERGE = 1
N_ROUTER = 3
N_SHARED = 1


def _split(x, n):
    parts = []
    r = x
    for i in range(n):
        p = r.astype(BF16)
        parts.append(p)
        if i < n - 1:
            r = r - p.astype(F32)
    return parts


def _dot(a, b, ta=False, tb=False):
    dn = (((0 if ta else 1,), (1 if tb else 0,)), ((), ()))
    return lax.dot_general(a, b, dn, preferred_element_type=F32)


def _mmp(ap, bp, ta=False, tb=False):
    n = max(len(ap), len(bp))
    out = None
    for i in range(len(ap)):
        for j in range(len(bp)):
            if i + j <= n - 1:
                t = _dot(ap[i], bp[j], ta, tb)
                out = t if out is None else out + t
    return out


def _mm(a, b, n, ta=False, tb=False):
    return _mmp(_split(a, n), _split(b, n), ta, tb)


def _mm_exact_l(m_bf16, x, n):
    return _mmp([m_bf16], _split(x, n))


def _mm_exact_r(x, m_bf16, n):
    return _mmp(_split(x, n), [m_bf16])


def _pack_bf16_pairs(x):
    w = x.shape[1] // 2
    hi = lax.bitcast_convert_type(x[:, :w].astype(BF16).astype(F32), jnp.int32)
    lo = lax.bitcast_convert_type(x[:, w:].astype(BF16).astype(F32), jnp.int32)
    return hi | lax.shift_right_logical(lo, 16)


def _unpack_bf16_pairs(p):
    hi = lax.bitcast_convert_type(p & jnp.int32(-65536), F32)
    lo = lax.bitcast_convert_type(lax.shift_left(p, 16), F32)
    return jnp.concatenate([hi, lo], axis=1)


def _sigmoid(x):
    return 1.0 / (1.0 + jnp.exp(-x))


def _silu(x):
    return x * _sigmoid(x)


def _log_sigmoid(x):
    return jnp.minimum(x, 0.0) - jnp.log(1.0 + jnp.exp(-jnp.abs(x)))


def _cparams(sem):
    return pltpu.CompilerParams(dimension_semantics=sem, vmem_limit_bytes=VMEM_LIMIT)


def _full(shape):
    nd = len(shape)
    return pl.BlockSpec(shape, lambda *a: (0,) * nd)


def _ada_body(c_ref, w_ref, b_ref, o_ref):
    s = _silu(c_ref[...])
    o_ref[...] = _mm(s, w_ref[...], N_ADA) + b_ref[...]


def _ada(cc, w, b):
    rows, d = cc.shape
    n = w.shape[1] // d
    return pl.pallas_call(
        _ada_body,
        grid=(n,),
        in_specs=[pl.BlockSpec((rows, d), lambda j: (0, 0)),
                  pl.BlockSpec((d, d), lambda j: (0, j)),
                  pl.BlockSpec((1, d), lambda j: (0, j))],
        out_specs=pl.BlockSpec((rows, d), lambda j: (0, j)),
        out_shape=jax.ShapeDtypeStruct((rows, w.shape[1]), F32),
        compiler_params=_cparams(("arbitrary",)),
        name="ada",
    )(cc, w, b.reshape(1, -1))


def _proj_body(x_ref, mod_ref, *refs):
    w_refs, o_ref = refs[:-1], refs[-1]
    sh = mod_ref[0, 0:1, :]
    sc = mod_ref[0, 1:2, :]
    h = x_ref[0] * (1.0 + sc) + sh
    o_ref[0] = _mmp(_split(h, N_PROJ), [w[...] for w in w_refs])


def _proj(x, mod, w, tm):
    B, L, D = x.shape
    n = w.shape[1]
    wp = _split(w, N_PROJ)
    return pl.pallas_call(
        _proj_body,
        grid=(B, L // tm),
        in_specs=[pl.BlockSpec((1, tm, D), lambda b, i: (b, i, 0)),
                  pl.BlockSpec((1, 6, D), lambda b, i: (b, 0, 0))]
                 + [pl.BlockSpec((D, n), lambda b, i: (0, 0))] * N_PROJ,
        out_specs=pl.BlockSpec((1, tm, n), lambda b, i: (b, i, 0)),
        out_shape=jax.ShapeDtypeStruct((B, L, n), F32),
        compiler_params=_cparams(("parallel", "arbitrary")),
        name="proj",
    )(x, mod, *wp)


def _neighbours(vertical, width, tm, cur, prev_ref, next_ref):
    if vertical:
        i = pl.program_id(1)
        n = pl.num_programs(1)
        prev = jnp.where(i > 0, prev_ref[0], 0.0)
        nxt = jnp.where(i < n - 1, next_ref[0], 0.0)
        pad = jnp.zeros((8, cur.shape[1]), F32)
        ext = jnp.concatenate([pad, prev, cur, nxt, pad], axis=0)
        off = GRID_W + 8
    else:
        pad = jnp.zeros((8, cur.shape[1]), F32)
        ext = jnp.concatenate([pad, cur, pad], axis=0)
        off = 8
    col = lax.broadcasted_iota(jnp.int32, (tm, 1), 0) % width

    def get(dr, dc):
        s = off + GRID_W * dr + dc
        v = ext[s:s + tm]
        if dc == -1:
            v = jnp.where(col == 0, 0.0, v)
        elif dc == 1:
            v = jnp.where(col == width - 1, 0.0, v)
        return v

    return get


def _feat_rw_body(vertical, width, tm, *refs):
    if vertical:
        prev_ref, cur_ref, next_ref = refs[:3]
        refs = refs[3:]
    else:
        cur_ref = refs[0]
        prev_ref = next_ref = None
        refs = refs[1:]
    (mu_ref, lw_hi_ref, lw_lo_ref, w0_ref, a0_ref, kk_ref, ka_ref, rk_ref, bd_ref,
     r_out, k_out, v_out, kkn_out, g_out, bonus_out, lwd_out, a_out) = refs
    cur = cur_ref[0]
    get = _neighbours(vertical, width, tm, cur, prev_ref, next_ref)
    left, right = get(0, -1), get(0, 1)
    up, down = (get(-1, 0), get(1, 0)) if vertical else (left, right)
    l4 = lax.broadcasted_iota(jnp.int32, (1, cur.shape[1]), 1) % 4
    shifted = jnp.where(l4 == 0, left, jnp.where(l4 == 1, right, jnp.where(l4 == 2, up, down)))
    p = cur + mu_ref[...] * (shifted - cur)

    r = p[:, 0:RW_WIDTH]
    k = p[:, RW_WIDTH:2 * RW_WIDTH]
    v = p[:, 2 * RW_WIDTH:3 * RW_WIDTH]
    slab = p[:, 3 * RW_WIDTH:3 * RW_WIDTH + 256]
    ln = lax.broadcasted_iota(jnp.int32, (1, 256), 1)
    e1 = RW_DECAY_RANK
    e2 = e1 + RW_AAA_RANK
    e3 = e2 + RW_GATE_RANK
    slab = jnp.where(ln < e1, jnp.tanh(slab),
                     jnp.where(ln < e2, slab, jnp.where(ln < e3, _sigmoid(slab), 0.0)))
    lo = _mmp(_split(slab, N_LORA), [lw_hi_ref[...], lw_lo_ref[...]][:N_LORA])
    W = RW_WIDTH
    a_sum = None
    for d in range(2):
        z = w0_ref[:, d * W:(d + 1) * W] + lo[:, d * W:(d + 1) * W]
        lwd_out[d, 0] = -_sigmoid(z) * math.exp(-0.5)
        a = _sigmoid(a0_ref[:, d * W:(d + 1) * W] + lo[:, (2 + d) * W:(3 + d) * W])
        a_out[d, 0] = a
        a_sum = a if a_sum is None else a_sum + a
    g_out[0] = lo[:, 4 * W:5 * W]
    bd = bd_ref[...]
    kk = k * kk_ref[...]
    ss = _mm_exact_r(kk * kk, bd, N_SEG)
    kkn_out[0] = kk / jnp.maximum(jnp.sqrt(ss), 1e-12)
    kmod_sum = k * (2.0 + (a_sum - 2.0) * ka_ref[...])
    bonus_out[0] = _mm_exact_r(r * kmod_sum * rk_ref[...], bd, N_SEG) * v
    r_out[0] = r
    k_out[0] = k
    v_out[0] = v


def _feat_rw(p, vertical, tm, consts):
    B, L, S = p.shape
    width = GRID_W if vertical else tm
    W = RW_WIDTH
    hb = tm // GRID_W
    nhb = L // GRID_W
    tile = pl.BlockSpec((1, tm, S), lambda b, i: (b, i, 0))
    if vertical:
        in_specs = [pl.BlockSpec((1, GRID_W, S), lambda b, i: (b, jnp.maximum(i * hb - 1, 0), 0)),
                    tile,
                    pl.BlockSpec((1, GRID_W, S), lambda b, i: (b, jnp.minimum((i + 1) * hb, nhb - 1), 0))]
        args = [p, p, p]
    else:
        in_specs = [tile]
        args = [p]
    in_specs += [_full(c.shape) for c in consts]
    o1 = pl.BlockSpec((1, tm, W), lambda b, i: (b, i, 0))
    o2 = pl.BlockSpec((2, 1, tm, W), lambda b, i: (0, b, i, 0))
    s1 = jax.ShapeDtypeStruct((B, L, W), F32)
    s2 = jax.ShapeDtypeStruct((2, B, L, W), F32)
    return pl.pallas_call(
        functools.partial(_feat_rw_body, vertical, width, tm),
        grid=(B, L // tm),
        in_specs=in_specs,
        out_specs=[o1] * 6 + [o2] * 2,
        out_shape=[s1] * 6 + [s2] * 2,
        compiler_params=_cparams(("parallel", "arbitrary")),
        name="feat_rw",
    )(*args, *consts)


def _feat_gla_body(vertical, width, tm, *refs):
    if vertical:
        prev_ref, cur_ref, next_ref = refs[:3]
        refs = refs[3:]
    else:
        cur_ref = refs[0]
        prev_ref = next_ref = None
        refs = refs[1:]
    cw_ref, g2_hi_ref, g2_lo_ref, gb_ref, qk_out, v_out, la_out = refs
    cur = cur_ref[0][:, 0:GLA_QKV_COLS]
    if vertical:
        class _Slice:
            def __init__(self, ref):
                self.ref = ref

            def __getitem__(self, idx):
                return self.ref[idx][:, 0:GLA_QKV_COLS]
        get = _neighbours(True, width, tm, cur, _Slice(prev_ref), _Slice(next_ref))
    else:
        get = _neighbours(False, width, tm, cur, None, None)
    acc = None
    for dr in ((-1, 0, 1) if vertical else (0,)):
        for dc in (-1, 0, 1):
            t = get(dr, dc) * cw_ref[(dr + 1) * 3 + (dc + 1):(dr + 1) * 3 + (dc + 1) + 1, :]
            acc = t if acc is None else acc + t
    qkv = _silu(acc)
    kw = GLA_KEY_WIDTH
    lane = lax.broadcasted_iota(jnp.int32, (1, 2 * kw), 1)
    qk_out[0] = qkv[:, 0:2 * kw] * jnp.where(lane < kw, GLA_KEY_DIM ** -0.5, 1.0)
    v_out[0] = qkv[:, 2 * kw:]
    pgl = cur_ref[0][:, GLA_QKV_COLS + GLA_VAL_WIDTH:GLA_SEG]
    z = _mmp(_split(pgl, N_LORA), [g2_hi_ref[...], g2_lo_ref[...]][:N_LORA]) + gb_ref[...]
    la = _log_sigmoid(z) * (1.0 / GLA_TAU)
    la_out[0, 0] = la[:, 0:kw]
    la_out[1, 0] = la[:, kw:]


def _feat_gla(p, vertical, tm, consts):
    B, L, S = p.shape
    width = GRID_W if vertical else tm
    hb = tm // GRID_W
    nhb = L // GRID_W
    tile = pl.BlockSpec((1, tm, S), lambda b, i: (b, i, 0))
    if vertical:
        in_specs = [pl.BlockSpec((1, GRID_W, S), lambda b, i: (b, jnp.maximum(i * hb - 1, 0), 0)),
                    tile,
                    pl.BlockSpec((1, GRID_W, S), lambda b, i: (b, jnp.minimum((i + 1) * hb, nhb - 1), 0))]
        args = [p, p, p]
    else:
        in_specs = [tile]
        args = [p]
    in_specs += [_full(c.shape) for c in consts]
    kw, vw = GLA_KEY_WIDTH, GLA_VAL_WIDTH
    return pl.pallas_call(
        functools.partial(_feat_gla_body, vertical, width, tm),
        grid=(B, L // tm),
        in_specs=in_specs,
        out_specs=[pl.BlockSpec((1, tm, 2 * kw), lambda b, i: (b, i, 0)),
                   pl.BlockSpec((1, tm, vw), lambda b, i: (b, i, 0)),
                   pl.BlockSpec((2, 1, tm, kw), lambda b, i: (0, b, i, 0))],
        out_shape=[jax.ShapeDtypeStruct((B, L, 2 * kw), F32),
                   jax.ShapeDtypeStruct((B, L, vw), F32),
                   jax.ShapeDtypeStruct((2, B, L, kw), F32)],
        compiler_params=_cparams(("parallel", "arbitrary")),
        name="feat_gla",
    )(*args, *consts)


def _order_masks(n, sgn):
    ti = lax.broadcasted_iota(jnp.int32, (n, n), 0)
    si = lax.broadcasted_iota(jnp.int32, (n, n), 1)
    rel = (si - ti) * sgn
    return rel < 0, rel <= 0


def _chunk_index(nc):
    return lambda b, d, c: c + d * (nc - 1 - 2 * c)


def _rwkv_stages(d, c, r_ref, k_ref, v_ref, kk_ref, lw_ref, a_ref, ka_ref, s0_ref, y_ref, st_ref, s_scr):
    C = CHUNK

    @pl.when(c == 0)
    def _():
        s_scr[...] = s0_ref[0, 0]

    sgn = 1 - 2 * d
    r, k, v, kk = r_ref[0], k_ref[0], v_ref[0], kk_ref[0]
    lw, a = lw_ref[0, 0], a_ref[0, 0]
    before, upto = _order_masks(C, sgn)
    cum = _mm_exact_l(upto.astype(BF16), lw, N_CUM)
    tot = jnp.where(d == 0, cum[C - 1:C], cum[0:1])
    p_in = jnp.exp(cum)
    p_ex = jnp.exp(cum - lw)
    p_inv = jnp.exp(-cum)
    p_rem = jnp.exp(tot - cum)
    p_all = jnp.exp(tot)
    bvec = kk * a
    kmod = k * (1.0 + (a - 1.0) * ka_ref[...])
    a_t = -kk * p_ex
    r_t = r * p_in
    b_t = bvec * p_inv
    k_t = kmod * p_inv
    b_p = bvec * p_rem
    k_p = kmod * p_rem
    yield

    P = 2 * RW_HEAD_DIM
    lane = lax.broadcasted_iota(jnp.int32, (C, P), 1)
    h0 = lane < RW_HEAD_DIM
    ri = lax.broadcasted_iota(jnp.int32, (P, P), 0)
    ci = lax.broadcasted_iota(jnp.int32, (P, P), 1)
    same = (ri // C) == (ci // C)
    rel = ((ci % C) - (ri % C)) * sgn
    strict = same & (rel < 0)
    eye = (ri == ci).astype(F32)

    def stack2(x):
        return jnp.concatenate([jnp.where(h0, x, 0.0), jnp.where(h0, 0.0, x)], axis=0)

    pairs = range(RW_HEADS // 2)
    sls = [slice(p * P, (p + 1) * P) for p in pairs]
    a_st = [stack2(a_t[:, sl]) for sl in sls]
    v_st = [stack2(v[:, sl]) for sl in sls]
    g = [_mm(jnp.concatenate([a_st[p], stack2(r_t[:, sls[p]])], axis=0),
             jnp.concatenate([stack2(b_t[:, sls[p]]), stack2(k_t[:, sls[p]])], axis=0), N_RW, tb=True)
         for p in pairs]
    yield
    nmat = [jnp.where(strict, g[p][0:P, 0:P], 0.0) for p in pairs]
    a_ak = [jnp.where(strict, g[p][0:P, P:], 0.0) for p in pairs]
    ri2 = lax.broadcasted_iota(jnp.int32, (P, 2 * P), 0)
    ci2 = lax.broadcasted_iota(jnp.int32, (P, 2 * P), 1)
    incl2 = ((ri2 // C) == ((ci2 % P) // C)) & ((((ci2 % C) - (ri2 % C)) * sgn) <= 0)
    a_r = [jnp.where(incl2, g[p][P:, :], 0.0) for p in pairs]
    akv = [_mm(a_ak[p], v_st[p], N_RW) for p in pairs]
    yield
    t = [eye + nmat[p] for p in pairs]
    npow = nmat
    for _ in range(max(C.bit_length() - 2, 0)):
        npow = [_mm(npow[p], npow[p], N_RW) for p in pairs]
        t = [t[p] + _mm(t[p], npow[p], N_RW) for p in pairs]
        yield
    x = [_mm(t[p], jnp.concatenate([a_st[p], akv[p]], axis=1), N_RW) for p in pairs]
    s = [s_scr[p] for p in pairs]
    ur = [_mm(jnp.concatenate([x[p][0:C, 0:P] + x[p][C:, 0:P], r_t[:, sls[p]]], axis=0), s[p], N_RW, tb=True)
          for p in pairs]
    yield
    u = [ur[p][0:C] + x[p][0:C, P:] + x[p][C:, P:] for p in pairs]
    y_st = [_mm(a_r[p], jnp.concatenate([stack2(u[p]), v_st[p]], axis=0), N_RW) for p in pairs]
    for p in pairs:
        y_ref[0, 0, :, sls[p]] = ur[p][C:] + y_st[p][0:C] + y_st[p][C:]
    yield
    upd = [_mm(jnp.concatenate([u[p], v[:, sls[p]]], axis=0).T,
               jnp.concatenate([b_p[:, sls[p]], k_p[:, sls[p]]], axis=0), N_RW) for p in pairs]
    for p in pairs:
        s_new = s[p] * p_all[:, sls[p]] + jnp.where(same, upd[p], 0.0)
        s_scr[p] = s_new
        st_ref[0, 0, p] = s_new


def _scan_body(r_ref, k_ref, v_ref, kk_ref, lw_ref, a_ref, ka_ref, rs0_ref, qk_ref, gv_ref, la_ref, hm_ref, gs0_ref,
               y_ref, rst_ref, o_ref, gst_ref, rs_scr, gs_scr):
    d = pl.program_id(1)
    c = pl.program_id(2)
    chains = [_rwkv_stages(d, c, r_ref, k_ref, v_ref, kk_ref, lw_ref, a_ref, ka_ref, rs0_ref, y_ref, rst_ref, rs_scr),
              _gla_stages(d, c, qk_ref, gv_ref, la_ref, hm_ref, gs0_ref, o_ref, gst_ref, gs_scr)]
    while chains:
        chains = [ch for ch in chains if next(ch, True) is None]


def _scan(r, k, v, kk, lw, a, ka, rs0, qk, gv, la, hm, gs0):
    B, L, W = r.shape
    kw, vw = GLA_KEY_WIDTH, GLA_VAL_WIDTH
    nc = L // CHUNK
    cidx = _chunk_index(nc)
    shared = lambda w: pl.BlockSpec((1, CHUNK, w), lambda b, d, c: (b, cidx(b, d, c), 0))
    perdir = lambda w: pl.BlockSpec((1, 1, CHUNK, w), lambda b, d, c: (d, b, cidx(b, d, c), 0))
    npair = RW_HEADS // 2
    P = 2 * RW_HEAD_DIM
    rspec = pl.BlockSpec((1, 1, npair, P, P), lambda b, d, c: (b, d, 0, 0, 0))
    gspec = pl.BlockSpec((1, 1, vw, kw), lambda b, d, c: (b, d, 0, 0))
    return pl.pallas_call(
        _scan_body,
        grid=(B, 2, nc),
        in_specs=[shared(W), shared(W), shared(W), shared(W), perdir(W), perdir(W),
                  pl.BlockSpec((1, W), lambda b, d, c: (0, 0)), rspec,
                  shared(2 * kw), shared(vw), perdir(kw), pl.BlockSpec((kw, vw), lambda b, d, c: (0, 0)), gspec],
        out_specs=[perdir(W), rspec, perdir(vw), gspec],
        out_shape=[jax.ShapeDtypeStruct((2, B, L, W), F32),
                   jax.ShapeDtypeStruct((B, 2, npair, P, P), F32),
                   jax.ShapeDtypeStruct((2, B, L, vw), F32),
                   jax.ShapeDtypeStruct((B, 2, vw, kw), F32)],
        scratch_shapes=[pltpu.VMEM((npair, P, P), F32), pltpu.VMEM((vw, kw), F32)],
        compiler_params=_cparams(("parallel", "parallel", "arbitrary")),
        name="scan",
    )(r, k, v, kk, lw, a, ka, rs0, qk, gv, la, hm, gs0)


def _gla_stages(d, c, qk_ref, v_ref, g_ref, hm_ref, s0_ref, o_ref, st_ref, s_scr):
    C = CHUNK
    SB = GLA_SUB
    H = GLA_HEADS

    @pl.when(c == 0)
    def _():
        s_scr[...] = s0_ref[0, 0]

    sgn = 1 - 2 * d
    kw, vw = GLA_KEY_WIDTH, GLA_VAL_WIDTH
    q = qk_ref[0][:, 0:kw]
    k = qk_ref[0][:, kw:]
    v = v_ref[0]
    g = g_ref[0, 0]
    _, upto = _order_masks(C, sgn)
    b = _mm_exact_l(upto.astype(BF16), g, N_CUM)
    pos = lax.broadcasted_iota(jnp.int32, (C, 1), 0) * sgn + d * (C - 1)

    def b_at(p):
        return jnp.sum(jnp.where(pos == p, b, 0.0), axis=0, keepdims=True)

    NEG = -jnp.inf
    tot = b_at(C - 1)
    s = s_scr[...]
    inter = _mm(q * jnp.exp(b), s, N_GLA, tb=True)
    yield

    half = pos >= C // 2
    odd = (pos // SB) % 2 == 1
    r1 = b_at(C // 2 - 1)
    r2 = jnp.where(half, b_at(3 * SB - 1), b_at(SB - 1))
    q1 = q * jnp.exp(jnp.where(half, b - r1, NEG))
    k1 = k * jnp.exp(jnp.where(half, NEG, r1 - b))
    q2 = q * jnp.exp(jnp.where(odd, b - r2, NEG))
    k2 = k * jnp.exp(jnp.where(odd, NEG, r2 - b))
    lane_h = lax.broadcasted_iota(jnp.int32, (C, kw), 1) // GLA_KEY_DIM

    def stack_heads(x):
        return jnp.concatenate([jnp.where(lane_h == h, x, 0.0) for h in range(H)], axis=0)

    att1 = _mm(stack_heads(q1), k1, N_GLA, tb=True)
    att2 = _mm(stack_heads(q2), k2, N_GLA, tb=True)
    yield
    pr =(lax.broadcasted_iota(jnp.int32, (H * C, C), 0) % C) * sgn + d * (C - 1)
    pc = lax.broadcasted_iota(jnp.int32, (H * C, C), 1) * sgn + d * (C - 1)
    att = att1 + jnp.where((pr >= C // 2) == (pc >= C // 2), att2, 0.0)
    res = _mm(att, v, N_GLA)
    lane_hv = lax.broadcasted_iota(jnp.int32, (C, vw), 1) // GLA_VAL_DIM
    off = None
    for h in range(H):
        t = jnp.where(lane_hv == h, res[h * C:(h + 1) * C], 0.0)
        off = t if off is None else off + t
    yield

    hm = hm_ref[...]
    il = lax.broadcasted_iota(jnp.int32, (SB, 1), 0)
    diag = []
    for blk in range(C // SB):
        rs = slice(blk * SB, (blk + 1) * SB)
        bb, qb, kb, vb = b[rs], q[rs], k[rs], v[rs]
        pieces = []
        for j in range(SB):
            valid = (il - j) * sgn >= 0
            pieces.append(jnp.exp(jnp.where(valid, bb - bb[j:j + 1], NEG)) * qb * kb[j:j + 1])
        a = _mm_exact_r(jnp.concatenate(pieces, axis=0), hm, N_GLA_INTRA)
        acc = None
        for j in range(SB):
            t = a[j * SB:(j + 1) * SB] * vb[j:j + 1]
            acc = t if acc is None else acc + t
        diag.append(acc)
        yield
    o_ref[0, 0] = inter + off + jnp.concatenate(diag, axis=0)
    yield

    upd = _mm(v.T, k * jnp.exp(tot - b), N_GLA)
    ri = lax.broadcasted_iota(jnp.int32, upd.shape, 0) // GLA_VAL_DIM
    ci = lax.broadcasted_iota(jnp.int32, upd.shape, 1) // GLA_KEY_DIM
    s_new = s * jnp.exp(tot) + jnp.where(ri == ci, upd, 0.0)
    s_scr[...] = s_new
    st_ref[0, 0] = s_new


def _seg_norm(y, bd, dim, eps):
    mu = _mm_exact_r(y, bd, N_SEG) * (1.0 / dim)
    dlt = y - mu
    var = _mm_exact_r(dlt * dlt, bd, N_SEG) * (1.0 / dim)
    return dlt * lax.rsqrt(var + eps)


def _layer_norm(x, w, b):
    mu = jnp.mean(x, axis=-1, keepdims=True)
    dlt = x - mu
    var = jnp.mean(dlt * dlt, axis=-1, keepdims=True)
    return dlt * lax.rsqrt(var + LN_EPS) * w + b


def _merge_body(x_ref, mod_ref, yrw_ref, bonus_ref, g_ref, ygla_ref, og_ref,
                bd64_ref, bd128_ref, rgw_ref, rgb_ref, ggw_ref, ggb_ref, *refs):
    n = N_MERGE
    wrw = [r[...] for r in refs[0:n]]
    wgla = [r[...] for r in refs[n:2 * n]]
    wout = [r[...] for r in refs[2 * n:3 * n]]
    wgate = [r[...] for r in refs[3 * n:3 * n + N_PROJ]]
    ln_w_ref, ln_b_ref, x1_ref, h2p_ref = refs[3 * n + N_PROJ:]
    y = _seg_norm(yrw_ref[0, 0] + yrw_ref[1, 0], bd64_ref[...], RW_HEAD_DIM, RW_GN_EPS)
    y = (y * rgw_ref[...] + rgb_ref[...] + bonus_ref[0]) * g_ref[0]
    y_rw = _mmp(_split(y, n), wrw)
    y = _seg_norm(ygla_ref[0, 0] + ygla_ref[1, 0], bd128_ref[...], GLA_VAL_DIM, LN_EPS)
    y = (y * ggw_ref[...] + ggb_ref[...]) * _silu(og_ref[0])
    y_gla = _mmp(_split(y, n), wgla)
    h = x_ref[0] * (1.0 + mod_ref[0, 1:2, :]) + mod_ref[0, 0:1, :]
    gate = _sigmoid(_mmp(_split(h, N_PROJ), wgate))
    mixed = gate[:, 0:D_MODEL] * y_rw + gate[:, D_MODEL:] * y_gla
    mix = _mmp(_split(mixed, n), wout)
    g1 = mod_ref[0, 2:3, :]
    x1 = _layer_norm(ALPHA * x_ref[0] + g1 * mix, ln_w_ref[...], ln_b_ref[...])
    x1_ref[0] = x1
    h2p_ref[0] = _pack_bf16_pairs(_moe_input(x1, mod_ref))


def _merge(x, mod, yrw, bonus, g, ygla, p_gla, consts, weights, ln_w, ln_b, tm):
    B, L, D = x.shape
    W = RW_WIDTH
    tok = lambda w: pl.BlockSpec((1, tm, w), lambda b, i: (b, i, 0))
    dirs = pl.BlockSpec((2, 1, tm, W), lambda b, i: (0, b, i, 0))
    wl = [w for ws in weights for w in ws]
    return pl.pallas_call(
        _merge_body,
        grid=(B, L // tm),
        in_specs=[tok(D), pl.BlockSpec((1, 6, D), lambda b, i: (b, 0, 0)), dirs, tok(W), tok(W), dirs,
                  pl.BlockSpec((1, tm, W), lambda b, i: (b, i, GLA_QKV_COLS // W))]
                 + [_full(c.shape) for c in consts] + [_full(w.shape) for w in wl]
                 + [_full(ln_w.shape), _full(ln_b.shape)],
        out_specs=[tok(D), tok(D // 2)],
        out_shape=[jax.ShapeDtypeStruct((B, L, D), F32), jax.ShapeDtypeStruct((B, L, D // 2), jnp.int32)],
        compiler_params=_cparams(("parallel", "arbitrary")),
        name="merge",
    )(x, mod, yrw, bonus, g, ygla, p_gla, *consts, *wl, ln_w, ln_b)


def _moe_input(x1, mod_ref):
    return x1 * (1.0 + mod_ref[0, 4:5, :]) + mod_ref[0, 3:4, :]


def _router_body(x1_ref, mod_ref, *refs):
    n = N_ROUTER
    rt = [r[...] for r in refs[0:n]]
    bias_ref, e_out, g_out, cnt_out = refs[n:]
    tm = x1_ref.shape[0]
    E, G, PG = N_EXPERTS, N_GROUPS, N_EXPERTS // N_GROUPS
    h = _moe_input(x1_ref[...], mod_ref)
    logits = _mmp(rt, _split(h, n), tb=True)
    scores = _sigmoid(logits)
    sel = scores + bias_ref[:, 0:tm]
    NEG = -jnp.inf
    ip = lax.broadcasted_iota(jnp.int32, (PG, tm), 0)
    group_rows = []
    for gidx in range(G):
        sg = sel[gidx * PG:(gidx + 1) * PG]
        m1 = jnp.max(sg, axis=0, keepdims=True)
        first = jnp.min(jnp.where(sg == m1, ip, PG), axis=0, keepdims=True)
        m2 = jnp.max(jnp.where(ip == first, NEG, sg), axis=0, keepdims=True)
        group_rows.append(m1 + m2)
    gs = jnp.concatenate(group_rows, axis=0)
    gi = lax.broadcasted_iota(jnp.int32, (G, tm), 0)
    keep = jnp.zeros((G, tm), F32)
    for _ in range(TOPK_GROUPS):
        m = jnp.max(gs, axis=0, keepdims=True)
        idx = jnp.min(jnp.where(gs == m, gi, G), axis=0, keepdims=True)
        hit = gi == idx
        keep = jnp.where(hit, 1.0, keep)
        gs = jnp.where(hit, NEG, gs)
    cur = jnp.concatenate(
        [jnp.where(keep[gidx:gidx + 1] > 0.5, sel[gidx * PG:(gidx + 1) * PG], NEG) for gidx in range(G)],
        axis=0)
    ei = lax.broadcasted_iota(jnp.int32, (E, tm), 0)
    idxs, gates = [], []
    picked = jnp.zeros((E, tm), F32)
    for _ in range(TOP_K):
        m = jnp.max(cur, axis=0, keepdims=True)
        idx = jnp.min(jnp.where(cur == m, ei, E), axis=0, keepdims=True)
        hit = ei == idx
        idxs.append(idx)
        gates.append(jnp.sum(jnp.where(hit, scores, 0.0), axis=0, keepdims=True))
        cur = jnp.where(hit, NEG, cur)
        picked = jnp.where(hit, 1.0, picked)
    gate = jnp.concatenate(gates, axis=0)
    e_out[...] = jnp.concatenate(idxs, axis=0)
    g_out[...] = gate / jnp.sum(gate, axis=0, keepdims=True) * ROUTED_SCALE

    @pl.when(pl.program_id(0) == 0)
    def _():
        cnt_out[...] = jnp.zeros(cnt_out.shape, F32)

    part = picked[:, 0:128]
    for j in range(1, tm // 128):
        part = part + picked[:, j * 128:(j + 1) * 128]
    cnt_out[...] += part


def _router(x1f, mod, seq, router_t_parts, bias_b, tm):
    n, D = x1f.shape
    E = N_EXPERTS
    return pl.pallas_call(
        _router_body,
        grid=(n // tm,),
        in_specs=[pl.BlockSpec((tm, D), lambda i: (i, 0)),
                  pl.BlockSpec((1, 6, D), lambda i: (i * tm // seq, 0, 0))]
                 + [pl.BlockSpec((E, D), lambda i: (0, 0))] * len(router_t_parts)
                 + [pl.BlockSpec(bias_b.shape, lambda i: (0, 0))],
        out_specs=[pl.BlockSpec((TOP_K, tm), lambda i: (0, i))] * 2
                  + [pl.BlockSpec((E, 128), lambda i: (0, 0))],
        out_shape=[jax.ShapeDtypeStruct((TOP_K, n), jnp.int32), jax.ShapeDtypeStruct((TOP_K, n), F32),
                   jax.ShapeDtypeStruct((E, 128), F32)],
        compiler_params=_cparams(("arbitrary",)),
        name="router",
    )(x1f, mod, *router_t_parts, bias_b)


def _sc_gather_body(per_w, table_hbm, idx_hbm, out_hbm, idx_v, rows_v, sem_g, sem_o):
    wid = lax.axis_index("s") * SC_CORES + lax.axis_index("c")
    base = wid * per_w
    win = SC_WINDOW
    pltpu.sync_copy(idx_hbm.at[wid], idx_v)

    @pl.loop(0, per_w // win, step=2)
    def _(j):
        g0 = pltpu.async_copy(table_hbm.at[idx_v.at[j]], rows_v.at[0], sem_g.at[0])
        g1 = pltpu.async_copy(table_hbm.at[idx_v.at[j + 1]], rows_v.at[1], sem_g.at[1])
        g0.wait()
        o0 = pltpu.async_copy(rows_v.at[0], out_hbm.at[pl.ds(base + j * win, win)], sem_o.at[0])
        g1.wait()
        o1 = pltpu.async_copy(rows_v.at[1], out_hbm.at[pl.ds(base + (j + 1) * win, win)], sem_o.at[1])
        o0.wait()
        o1.wait()


def _sc_gather(table, idx):
    nrows = idx.shape[0]
    D = table.shape[1]
    per_w = nrows // SC_WORKERS
    mesh = plsc.VectorSubcoreMesh(core_axis_name="c", subcore_axis_name="s",
                                  num_cores=SC_CORES, num_subcores=SC_SUBCORES)
    return pl.kernel(
        functools.partial(_sc_gather_body, per_w),
        out_type=jax.ShapeDtypeStruct((nrows, D), table.dtype),
        mesh=mesh,
        scratch_types=[pltpu.VMEM((per_w // SC_WINDOW, SC_WINDOW), jnp.int32),
                       pltpu.VMEM((2, SC_WINDOW, D), table.dtype),
                       pltpu.SemaphoreType.DMA((2,)), pltpu.SemaphoreType.DMA((2,))],
        name="sc_gather",
    )(table, idx.reshape(SC_WORKERS, per_w // SC_WINDOW, SC_WINDOW))


def _experts_body(be_ref, nu_ref, x_ref, wgu_ref, wd_ref, y_ref, wgu_bf, wd_bf):
    i = pl.program_id(0)

    @pl.when(i < nu_ref[0])
    def _():
        @pl.when((i == 0) | (be_ref[i] != be_ref[jnp.maximum(i - 1, 0)]))
        def _():
            wgu_bf[...] = wgu_ref[0].astype(BF16)
            wd_bf[...] = wd_ref[0].astype(BF16)

        F = EXPERT_DIM
        gu = _dot(_unpack_bf16_pairs(x_ref[...]).astype(BF16), wgu_bf[...])
        act = _silu(gu[:, 0:F]) * gu[:, F:]
        y_ref[...] = _pack_bf16_pairs(_dot(act.astype(BF16), wd_bf[...]))

    @pl.when(i >= nu_ref[0])
    def _():
        y_ref[...] = jnp.zeros(y_ref.shape, jnp.int32)


def _experts(block_e, nused, xs, w_gate_up, w_down):
    nb = block_e.shape[0]
    DP = xs.shape[1]
    D = 2 * DP
    R = ROW_BLOCK
    F2 = w_gate_up.shape[2]
    last = lambda i, nu: jnp.minimum(i, nu[0] - 1)
    grid_spec = pltpu.PrefetchScalarGridSpec(
        num_scalar_prefetch=2,
        grid=(nb,),
        in_specs=[pl.BlockSpec((R, DP), lambda i, be, nu: (last(i, nu), 0)),
                  pl.BlockSpec((1, D, F2), lambda i, be, nu: (be[last(i, nu)], 0, 0)),
                  pl.BlockSpec((1, F2 // 2, D), lambda i, be, nu: (be[last(i, nu)], 0, 0))],
        out_specs=pl.BlockSpec((R, DP), lambda i, be, nu: (i, 0)),
        scratch_shapes=[pltpu.VMEM((D, F2), BF16), pltpu.VMEM((F2 // 2, D), BF16)],
    )
    return pl.pallas_call(
        _experts_body,
        grid_spec=grid_spec,
        out_shape=jax.ShapeDtypeStruct((nb * R, DP), jnp.int32),
        compiler_params=_cparams(("arbitrary",)),
        name="experts",
    )(block_e, nused, xs, w_gate_up, w_down)


def _final_body(yg_ref, x1_ref, mod_ref, gate_ref, *refs):
    n = N_SHARED
    sgu = [r[...] for r in refs[0:n]]
    sd = [r[...] for r in refs[n:2 * n]]
    ln_w_ref, ln_b_ref, o_ref = refs[2 * n:]
    tm = x1_ref.shape[1]
    gate = gate_ref[0]
    routed = None
    for kk in range(TOP_K):
        t = _unpack_bf16_pairs(yg_ref[kk * tm:(kk + 1) * tm, :]) * gate[:, kk:kk + 1]
        routed = t if routed is None else routed + t
    h2 = _moe_input(x1_ref[0], mod_ref)
    F = sgu[0].shape[1] // 2
    gu = _mmp(_split(h2, n), sgu)
    act = _silu(gu[:, 0:F]) * gu[:, F:]
    shared = _mmp(_split(act, n), sd)
    g2 = mod_ref[0, 5:6, :]
    o_ref[0] = _layer_norm(ALPHA * x1_ref[0] + g2 * (routed + shared), ln_w_ref[...], ln_b_ref[...])


def _final(yg, x1, mod, gate, sgu, sd, ln_w, ln_b, tm):
    B, L, D = x1.shape
    ni = L // tm
    rows = tm * TOP_K
    tok = lambda w: pl.BlockSpec((1, tm, w), lambda b, i: (b, i, 0))
    ws = list(sgu) + list(sd)
    return pl.pallas_call(
        _final_body,
        grid=(B, ni),
        in_specs=[pl.BlockSpec((rows, D // 2), lambda b, i: (b * ni + i, 0)),
                  tok(D), pl.BlockSpec((1, 6, D), lambda b, i: (b, 0, 0)), tok(TOP_K)]
                 + [_full(w.shape) for w in ws] + [_full(ln_w.shape), _full(ln_b.shape)],
        out_specs=tok(D),
        out_shape=jax.ShapeDtypeStruct((B, L, D), F32),
        compiler_params=_cparams(("parallel", "arbitrary")),
        name="final",
    )(yg, x1, mod, gate, *ws, ln_w, ln_b)


def _dispatch_plan(experts_t, sizes, tm):
    R, E = ROW_BLOCK, N_EXPERTS
    K, n = experts_t.shape
    nk = K * n
    rows = nk + E * (R - 1)
    rows = -(-rows // math.lcm(R, SC_ROW_ALIGN)) * math.lcm(R, SC_ROW_ALIGN)
    nb = rows // R
    ncand = nb * R - nk
    padded = (sizes + R - 1) // R * R
    pad_end = jnp.cumsum(padded)
    tok = jnp.broadcast_to(jnp.arange(n, dtype=jnp.int32)[None, :], (K, n))
    slot = jnp.arange(K, dtype=jnp.int32)[:, None]
    aid = (tok // tm) * (tm * K) + slot * tm + tok % tm
    cand = jnp.arange(ncand, dtype=jnp.int32)
    ce, cp = cand // (R - 1), cand % (R - 1)
    need = jnp.broadcast_to((padded - sizes)[:, None], (E, R - 1)).reshape(-1)
    need = jnp.pad(need, (0, ncand - E * (R - 1)))
    ckey = jnp.where(cp < need, 2 * ce + 1, 2 * E + 1)
    keys = jnp.concatenate([2 * experts_t.reshape(-1), ckey])
    toks = jnp.concatenate([tok.reshape(-1), cand % n])
    aids = jnp.concatenate([aid.reshape(-1), nk + cand])
    _, row_tok, row_aid = lax.sort((keys, toks, aids), num_keys=1)
    _, pos = lax.sort((row_aid, jnp.arange(nb * R, dtype=jnp.int32)), num_keys=1)
    block_e = jnp.minimum(jnp.searchsorted(pad_end, jnp.arange(nb) * R, side='right'), E - 1).astype(jnp.int32)
    nused = (pad_end[-1] // R).astype(jnp.int32).reshape(1)
    return row_tok, pos[:nk], block_e, nused


def _block_diag_ones(n, blk):
    i = jnp.arange(n) // blk
    return (i[:, None] == i[None, :]).astype(BF16)


def kernel(x, c, ctx, c_ctx, w_ada, b_ada, w_in, rw_mu, rw_w0, rw_w2, rw_a0, rw_a2, rw_g2, rw_k_k, rw_k_a,
           rw_r_k, rw_gn_w, rw_gn_b, gla_conv, gla_g2, gla_gb, gla_gn_w, gla_gn_b, w_br_rw, w_br_gla, w_out,
           ln1_w, ln1_b, router, router_bias, w_gate_up, w_down, sh_gate_up, sh_down, ln2_w, ln2_b):
    B, L, D = x.shape
    CT = ctx.shape[1]
    l = 0
    W = RW_WIDTH
    row = lambda t: t.reshape(1, -1)

    rows = -(-(B + 1) // 8) * 8
    cc = jnp.zeros((rows, D), F32).at[:B].set(c).at[B].set(c_ctx)
    mod = _ada(cc, w_ada[l], b_ada[l])
    mod_lat = mod[:B].reshape(B, 6, D)
    mod_ctx = jnp.broadcast_to(mod[B].reshape(1, 6, D), (B, 6, D))

    w = w_in[l]
    g0 = RW_COLS
    w_rw = jnp.pad(w[:, :RW_COLS], ((0, 0), (0, RW_SEG - RW_COLS)))
    w_gla = jnp.concatenate([w[:, g0:g0 + GLA_QKV_COLS],
                             w[:, g0 + GLA_QKV_COLS + GLA_GATE_RANK:g0 + GLA_COLS],
                             w[:, g0 + GLA_QKV_COLS:g0 + GLA_QKV_COLS + GLA_GATE_RANK],
                             jnp.zeros((D, GLA_SEG - GLA_COLS), F32)], axis=1)
    w_gate = w[:, MIX_COLS:]
    tm_p = min(512, L)
    p_rw = _proj(x, mod_lat, w_rw, tm_p)
    p_gla = _proj(x, mod_lat, w_gla, tm_p)
    pc_rw = _proj(ctx, mod_ctx, w_rw, CT)
    pc_gla = _proj(ctx, mod_ctx, w_gla, CT)

    mu = jnp.pad(rw_mu[l], (0, RW_SEG - RW_COLS)).reshape(1, -1)
    lora = jnp.zeros((256, 5 * W), F32)
    e1 = RW_DECAY_RANK
    e2 = e1 + RW_AAA_RANK
    e3 = e2 + RW_GATE_RANK
    for d in range(2):
        lora = lora.at[0:e1, d * W:(d + 1) * W].set(rw_w2[l, d])
        lora = lora.at[e1:e2, (2 + d) * W:(3 + d) * W].set(rw_a2[l, d])
    lora = lora.at[e2:e3, 4 * W:].set(rw_g2[l])
    lora_p = (_split(lora, N_LORA) + [jnp.zeros_like(lora, BF16)])[:2]
    bd64 = _block_diag_ones(W, RW_HEAD_DIM)
    rw_consts = [mu, lora_p[0], lora_p[1], rw_w0[l].reshape(1, -1), rw_a0[l].reshape(1, -1),
                 row(rw_k_k[l]), row(rw_k_a[l]), row(rw_r_k[l]), bd64]
    tm_f = min(256, L)
    r, k, v, kkn, g, bonus, lwd, a = _feat_rw(p_rw, True, tm_f, rw_consts)
    rc, kc, vc, kknc, _, _, lwdc, ac = _feat_rw(pc_rw, False, CT, rw_consts)

    g2 = jnp.zeros((GLA_SEG - GLA_QKV_COLS - GLA_VAL_WIDTH, 2 * GLA_KEY_WIDTH), F32)
    g2 = g2.at[:GLA_GATE_RANK].set(jnp.concatenate([gla_g2[l, 0], gla_g2[l, 1]], axis=1))
    g2_p = (_split(g2, N_LORA) + [jnp.zeros_like(g2, BF16)])[:2]
    cw = jnp.pad(gla_conv[l].reshape(9, GLA_QKV_COLS), ((0, 7), (0, 0)))
    gla_consts = [cw, g2_p[0], g2_p[1], gla_gb[l].reshape(1, -1)]
    qk, vg, la = _feat_gla(p_gla, True, tm_f, gla_consts)
    qkc, vgc, lac = _feat_gla(pc_gla, False, CT, gla_consts)

    ka = row(rw_k_a[l])
    P = 2 * RW_HEAD_DIM
    s0 = jnp.zeros((B, 2, RW_HEADS // 2, P, P), F32)
    hi = jnp.arange(GLA_KEY_WIDTH) // GLA_KEY_DIM
    hj = jnp.arange(GLA_VAL_WIDTH) // GLA_VAL_DIM
    hm = (hi[:, None] == hj[None, :]).astype(BF16)
    g0s = jnp.zeros((B, 2, GLA_VAL_WIDTH, GLA_KEY_WIDTH), F32)
    _, s_ctx, _, gs_ctx = _scan(rc, kc, vc, kknc, lwdc, ac, ka, s0, qkc, vgc, lac, hm, g0s)
    y_rw, _, y_gla, _ = _scan(r, k, v, kkn, lwd, a, ka, s_ctx, qk, vg, la, hm, gs_ctx)

    bd128 = _block_diag_ones(GLA_VAL_WIDTH, GLA_VAL_DIM)
    m_consts = [bd64, bd128, row(rw_gn_w[l]), row(rw_gn_b[l]), row(gla_gn_w[l]), row(gla_gn_b[l])]
    m_weights = [_split(w_br_rw[l], N_MERGE), _split(w_br_gla[l], N_MERGE), _split(w_out[l], N_MERGE),
                 _split(w_gate, N_PROJ)]
    x1, h2p = _merge(x, mod_lat, y_rw, bonus, g, y_gla, p_gla, m_consts, m_weights,
                     row(ln1_w[l]), row(ln1_b[l]), min(256, L))

    n = B * L
    tm_r = min(256, L)
    bias_b = jnp.broadcast_to(router_bias[l].reshape(-1, 1), (N_EXPERTS, tm_r))
    e_t, g_t, cnt = _router(x1.reshape(n, D), mod_lat, L, _split(router[l].T, N_ROUTER), bias_b, tm_r)
    sizes = jnp.sum(cnt, axis=1).astype(jnp.int32)
    tm_c = min(128, L)
    row_tok, pos_t, block_e, nused = _dispatch_plan(e_t, sizes, tm_c)
    xs = _sc_gather(h2p.reshape(n, D // 2), row_tok)
    y = _experts(block_e, nused, xs, w_gate_up[l], w_down[l])
    yg = _sc_gather(y, pos_t)
    gate_tok = g_t.T.reshape(B, L, TOP_K)
    return _final(yg, x1, mod_lat, gate_tok, _split(sh_gate_up[l], N_SHARED),
                  _split(sh_down[l], N_SHARED), row(ln2_w[l]), row(ln2_b[l]), tm_c)
```

```python
import functools
import math

import jax
import jax.numpy as jnp
from jax import lax
from jax.experimental import pallas as pl
from jax.experimental.pallas import tpu as pltpu
from jax.experimental.pallas import tpu_sc as plsc

F32 = jnp.float32
BF16 = jnp.bfloat16

D_MODEL = 1024
GRID_W = 64
RW_WIDTH = 512
RW_HEADS = 8
RW_HEAD_DIM = 64
RW_DECAY_RANK = 32
RW_AAA_RANK = 32
RW_GATE_RANK = 96
RW_GN_EPS = 64e-5
RW_COLS = 1696
RW_SEG = 1792
GLA_HEADS = 4
GLA_KEY_WIDTH = 256
GLA_VAL_WIDTH = 512
GLA_KEY_DIM = 64
GLA_VAL_DIM = 128
GLA_GATE_RANK = 16
GLA_TAU = 16.0
GLA_QKV_COLS = 1024
GLA_COLS = 1552
GLA_SEG = 1664
MIX_COLS = RW_COLS + GLA_COLS
N_EXPERTS = 256
TOP_K = 8
N_GROUPS = 8
TOPK_GROUPS = 4
EXPERT_DIM = 256
ROUTED_SCALE = 2.5
LN_EPS = 1e-5
DEPTH = 1
ALPHA = (2 * DEPTH) ** 0.25

CHUNK = 64
GLA_SUB = 16
ROW_BLOCK = 256
VMEM_LIMIT = 48 * 1024 * 1024
SC_CORES = 2
SC_SUBCORES = 16
SC_WORKERS = SC_CORES * SC_SUBCORES
SC_WINDOW = 32
SC_ROW_ALIGN = 2 * SC_WORKERS * SC_WINDOW

N_ADA = 3
N_PROJ = 1
N_LORA = 1
N_SEG = 2
N_CUM = 2
N_RW = 1
N_GLA = 1
N_GLA_INTRA = 1
N_MERGE = 1
N_ROUTER = 3
N_SHARED = 1


def _split(x, n):
    parts = []
    r = x
    for i in range(n):
        p = r.astype(BF16)
        parts.append(p)
        if i < n - 1:
            r = r - p.astype(F32)
    return parts


def _dot(a, b, ta=False, tb=False):
    dn = (((0 if ta else 1,), (1 if tb else 0,)), ((), ()))
    return lax.dot_general(a, b, dn, preferred_element_type=F32)


def _mmp(ap, bp, ta=False, tb=False):
    n = max(len(ap), len(bp))
    out = None
    for i in range(len(ap)):
        for j in range(len(bp)):
            if i + j <= n - 1:
                t = _dot(ap[i], bp[j], ta, tb)
                out = t if out is None else out + t
    return out


def _mm(a, b, n, ta=False, tb=False):
    return _mmp(_split(a, n), _split(b, n), ta, tb)


def _mm_exact_l(m_bf16, x, n):
    return _mmp([m_bf16], _split(x, n))


def _mm_exact_r(x, m_bf16, n):
    return _mmp(_split(x, n), [m_bf16])


def _pack_bf16_pairs(x):
    w = x.shape[1] // 2
    hi = lax.bitcast_convert_type(x[:, :w].astype(BF16).astype(F32), jnp.int32)
    lo = lax.bitcast_convert_type(x[:, w:].astype(BF16).astype(F32), jnp.int32)
    return hi | lax.shift_right_logical(lo, 16)


def _unpack_bf16_pairs(p):
    hi = lax.bitcast_convert_type(p & jnp.int32(-65536), F32)
    lo = lax.bitcast_convert_type(lax.shift_left(p, 16), F32)
    return jnp.concatenate([hi, lo], axis=1)


def _sigmoid(x):
    return 1.0 / (1.0 + jnp.exp(-x))


def _silu(x):
    return x * _sigmoid(x)


def _log_sigmoid(x):
    return jnp.minimum(x, 0.0) - jnp.log(1.0 + jnp.exp(-jnp.abs(x)))


def _cparams(sem):
    return pltpu.CompilerParams(dimension_semantics=sem, vmem_limit_bytes=VMEM_LIMIT)


def _full(shape):
    nd = len(shape)
    return pl.BlockSpec(shape, lambda *a: (0,) * nd)


def _ada_body(c_ref, w_ref, b_ref, o_ref):
    s = _silu(c_ref[...])
    o_ref[...] = _mm(s, w_ref[...], N_ADA) + b_ref[...]


def _ada(cc, w, b):
    rows, d = cc.shape
    n = w.shape[1] // d
    return pl.pallas_call(
        _ada_body,
        grid=(n,),
        in_specs=[pl.BlockSpec((rows, d), lambda j: (0, 0)),
                  pl.BlockSpec((d, d), lambda j: (0, j)),
                  pl.BlockSpec((1, d), lambda j: (0, j))],
        out_specs=pl.BlockSpec((rows, d), lambda j: (0, j)),
        out_shape=jax.ShapeDtypeStruct((rows, w.shape[1]), F32),
        compiler_params=_cparams(("arbitrary",)),
        name="ada",
    )(cc, w, b.reshape(1, -1))


def _proj_body(x_ref, mod_ref, *refs):
    w_refs, o_ref = refs[:-1], refs[-1]
    sh = mod_ref[0, 0:1, :]
    sc = mod_ref[0, 1:2, :]
    h = x_ref[0] * (1.0 + sc) + sh
    o_ref[0] = _mmp(_split(h, N_PROJ), [w[...] for w in w_refs])


def _proj(x, mod, w, tm):
    B, L, D = x.shape
    n = w.shape[1]
    wp = _split(w, N_PROJ)
    return pl.pallas_call(
        _proj_body,
        grid=(B, L // tm),
        in_specs=[pl.BlockSpec((1, tm, D), lambda b, i: (b, i, 0)),
                  pl.BlockSpec((1, 6, D), lambda b, i: (b, 0, 0))]
                 + [pl.BlockSpec((D, n), lambda b, i: (0, 0))] * N_PROJ,
        out_specs=pl.BlockSpec((1, tm, n), lambda b, i: (b, i, 0)),
        out_shape=jax.ShapeDtypeStruct((B, L, n), F32),
        compiler_params=_cparams(("parallel", "arbitrary")),
        name="proj",
    )(x, mod, *wp)


def _neighbours(vertical, width, tm, cur, prev_ref, next_ref):
    if vertical:
        i = pl.program_id(1)
        n = pl.num_programs(1)
        prev = jnp.where(i > 0, prev_ref[0], 0.0)
        nxt = jnp.where(i < n - 1, next_ref[0], 0.0)
        pad = jnp.zeros((8, cur.shape[1]), F32)
        ext = jnp.concatenate([pad, prev, cur, nxt, pad], axis=0)
        off = GRID_W + 8
    else:
        pad = jnp.zeros((8, cur.shape[1]), F32)
        ext = jnp.concatenate([pad, cur, pad], axis=0)
        off = 8
    col = lax.broadcasted_iota(jnp.int32, (tm, 1), 0) % width

    def get(dr, dc):
        s = off + GRID_W * dr + dc
        v = ext[s:s + tm]
        if dc == -1:
            v = jnp.where(col == 0, 0.0, v)
        elif dc == 1:
            v = jnp.where(col == width - 1, 0.0, v)
        return v

    return get


def _feat_rw_body(vertical, width, tm, *refs):
    if vertical:
        prev_ref, cur_ref, next_ref = refs[:3]
        refs = refs[3:]
    else:
        cur_ref = refs[0]
        prev_ref = next_ref = None
        refs = refs[1:]
    (mu_ref, lw_hi_ref, lw_lo_ref, w0_ref, a0_ref, kk_ref, ka_ref, rk_ref, bd_ref,
     r_out, k_out, v_out, kkn_out, g_out, bonus_out, lwd_out, a_out) = refs
    cur = cur_ref[0]
    get = _neighbours(vertical, width, tm, cur, prev_ref, next_ref)
    left, right = get(0, -1), get(0, 1)
    up, down = (get(-1, 0), get(1, 0)) if vertical else (left, right)
    l4 = lax.broadcasted_iota(jnp.int32, (1, cur.shape[1]), 1) % 4
    shifted = jnp.where(l4 == 0, left, jnp.where(l4 == 1, right, jnp.where(l4 == 2, up, down)))
    p = cur + mu_ref[...] * (shifted - cur)

    r = p[:, 0:RW_WIDTH]
    k = p[:, RW_WIDTH:2 * RW_WIDTH]
    v = p[:, 2 * RW_WIDTH:3 * RW_WIDTH]
    slab = p[:, 3 * RW_WIDTH:3 * RW_WIDTH + 256]
    ln = lax.broadcasted_iota(jnp.int32, (1, 256), 1)
    e1 = RW_DECAY_RANK
    e2 = e1 + RW_AAA_RANK
    e3 = e2 + RW_GATE_RANK
    slab = jnp.where(ln < e1, jnp.tanh(slab),
                     jnp.where(ln < e2, slab, jnp.where(ln < e3, _sigmoid(slab), 0.0)))
    lo = _mmp(_split(slab, N_LORA), [lw_hi_ref[...], lw_lo_ref[...]][:N_LORA])
    W = RW_WIDTH
    a_sum = None
    for d in range(2):
        z = w0_ref[:, d * W:(d + 1) * W] + lo[:, d * W:(d + 1) * W]
        lwd_out[d, 0] = -_sigmoid(z) * math.exp(-0.5)
        a = _sigmoid(a0_ref[:, d * W:(d + 1) * W] + lo[:, (2 + d) * W:(3 + d) * W])
        a_out[d, 0] = a
        a_sum = a if a_sum is None else a_sum + a
    g_out[0] = lo[:, 4 * W:5 * W]
    bd = bd_ref[...]
    kk = k * kk_ref[...]
    ss = _mm_exact_r(kk * kk, bd, N_SEG)
    kkn_out[0] = kk / jnp.maximum(jnp.sqrt(ss), 1e-12)
    kmod_sum = k * (2.0 + (a_sum - 2.0) * ka_ref[...])
    bonus_out[0] = _mm_exact_r(r * kmod_sum * rk_ref[...], bd, N_SEG) * v
    r_out[0] = r
    k_out[0] = k
    v_out[0] = v


def _feat_rw(p, vertical, tm, consts):
    B, L, S = p.shape
    width = GRID_W if vertical else tm
    W = RW_WIDTH
    hb = tm // GRID_W
    nhb = L // GRID_W
    tile = pl.BlockSpec((1, tm, S), lambda b, i: (b, i, 0))
    if vertical:
        in_specs = [pl.BlockSpec((1, GRID_W, S), lambda b, i: (b, jnp.maximum(i * hb - 1, 0), 0)),
                    tile,
                    pl.BlockSpec((1, GRID_W, S), lambda b, i: (b, jnp.minimum((i + 1) * hb, nhb - 1), 0))]
        args = [p, p, p]
    else:
        in_specs = [tile]
        args = [p]
    in_specs += [_full(c.shape) for c in consts]
    o1 = pl.BlockSpec((1, tm, W), lambda b, i: (b, i, 0))
    o2 = pl.BlockSpec((2, 1, tm, W), lambda b, i: (0, b, i, 0))
    s1 = jax.ShapeDtypeStruct((B, L, W), F32)
    s2 = jax.ShapeDtypeStruct((2, B, L, W), F32)
    return pl.pallas_call(
        functools.partial(_feat_rw_body, vertical, width, tm),
        grid=(B, L // tm),
        in_specs=in_specs,
        out_specs=[o1] * 6 + [o2] * 2,
        out_shape=[s1] * 6 + [s2] * 2,
        compiler_params=_cparams(("parallel", "arbitrary")),
        name="feat_rw",
    )(*args, *consts)


def _feat_gla_body(vertical, width, tm, *refs):
    if vertical:
        prev_ref, cur_ref, next_ref = refs[:3]
        refs = refs[3:]
    else:
        cur_ref = refs[0]
        prev_ref = next_ref = None
        refs = refs[1:]
    cw_ref, g2_hi_ref, g2_lo_ref, gb_ref, qk_out, v_out, la_out = refs
    cur = cur_ref[0][:, 0:GLA_QKV_COLS]
    if vertical:
        class _Slice:
            def __init__(self, ref):
                self.ref = ref

            def __getitem__(self, idx):
                return self.ref[idx][:, 0:GLA_QKV_COLS]
        get = _neighbours(True, width, tm, cur, _Slice(prev_ref), _Slice(next_ref))
    else:
        get = _neighbours(False, width, tm, cur, None, None)
    acc = None
    for dr in ((-1, 0, 1) if vertical else (0,)):
        for dc in (-1, 0, 1):
            t = get(dr, dc) * cw_ref[(dr + 1) * 3 + (dc + 1):(dr + 1) * 3 + (dc + 1) + 1, :]
            acc = t if acc is None else acc + t
    qkv = _silu(acc)
    kw = GLA_KEY_WIDTH
    lane = lax.broadcasted_iota(jnp.int32, (1, 2 * kw), 1)
    qk_out[0] = qkv[:, 0:2 * kw] * jnp.where(lane < kw, GLA_KEY_DIM ** -0.5, 1.0)
    v_out[0] = qkv[:, 2 * kw:]
    pgl = cur_ref[0][:, GLA_QKV_COLS + GLA_VAL_WIDTH:GLA_SEG]
    z = _mmp(_split(pgl, N_LORA), [g2_hi_ref[...], g2_lo_ref[...]][:N_LORA]) + gb_ref[...]
    la = _log_sigmoid(z) * (1.0 / GLA_TAU)
    la_out[0, 0] = la[:, 0:kw]
    la_out[1, 0] = la[:, kw:]


def _feat_gla(p, vertical, tm, consts):
    B, L, S = p.shape
    width = GRID_W if vertical else tm
    hb = tm // GRID_W
    nhb = L // GRID_W
    tile = pl.BlockSpec((1, tm, S), lambda b, i: (b, i, 0))
    if vertical:
        in_specs = [pl.BlockSpec((1, GRID_W, S), lambda b, i: (b, jnp.maximum(i * hb - 1, 0), 0)),
                    tile,
                    pl.BlockSpec((1, GRID_W, S), lambda b, i: (b, jnp.minimum((i + 1) * hb, nhb - 1), 0))]
        args = [p, p, p]
    else:
        in_specs = [tile]
        args = [p]
    in_specs += [_full(c.shape) for c in consts]
    kw, vw = GLA_KEY_WIDTH, GLA_VAL_WIDTH
    return pl.pallas_call(
        functools.partial(_feat_gla_body, vertical, width, tm),
        grid=(B, L // tm),
        in_specs=in_specs,
        out_specs=[pl.BlockSpec((1, tm, 2 * kw), lambda b, i: (b, i, 0)),
                   pl.BlockSpec((1, tm, vw), lambda b, i: (b, i, 0)),
                   pl.BlockSpec((2, 1, tm, kw), lambda b, i: (0, b, i, 0))],
        out_shape=[jax.ShapeDtypeStruct((B, L, 2 * kw), F32),
                   jax.ShapeDtypeStruct((B, L, vw), F32),
                   jax.ShapeDtypeStruct((2, B, L, kw), F32)],
        compiler_params=_cparams(("parallel", "arbitrary")),
        name="feat_gla",
    )(*args, *consts)


def _order_masks(n, sgn):
    ti = lax.broadcasted_iota(jnp.int32, (n, n), 0)
    si = lax.broadcasted_iota(jnp.int32, (n, n), 1)
    rel = (si - ti) * sgn
    return rel < 0, rel <= 0


def _chunk_index(nc):
    return lambda b, d, c: c + d * (nc - 1 - 2 * c)


def _rwkv_stages(d, c, r_ref, k_ref, v_ref, kk_ref, lw_ref, a_ref, ka_ref, s0_ref, y_ref, st_ref, s_scr):
    C = CHUNK

    @pl.when(c == 0)
    def _():
        s_scr[...] = s0_ref[0, 0]

    sgn = 1 - 2 * d
    r, k, v, kk = r_ref[0], k_ref[0], v_ref[0], kk_ref[0]
    lw, a = lw_ref[0, 0], a_ref[0, 0]
    before, upto = _order_masks(C, sgn)
    cum = _mm_exact_l(upto.astype(BF16), lw, N_CUM)
    tot = jnp.where(d == 0, cum[C - 1:C], cum[0:1])
    p_in = jnp.exp(cum)
    p_ex = jnp.exp(cum - lw)
    p_inv = jnp.exp(-cum)
    p_rem = jnp.exp(tot - cum)
    p_all = jnp.exp(tot)
    bvec = kk * a
    kmod = k * (1.0 + (a - 1.0) * ka_ref[...])
    a_t = -kk * p_ex
    r_t = r * p_in
    b_t = bvec * p_inv
    k_t = kmod * p_inv
    b_p = bvec * p_rem
    k_p = kmod * p_rem
    yield

    P = 2 * RW_HEAD_DIM
    lane = lax.broadcasted_iota(jnp.int32, (C, P), 1)
    h0 = lane < RW_HEAD_DIM
    ri = lax.broadcasted_iota(jnp.int32, (P, P), 0)
    ci = lax.broadcasted_iota(jnp.int32, (P, P), 1)
    same = (ri // C) == (ci // C)
    rel = ((ci % C) - (ri % C)) * sgn
    strict = same & (rel < 0)
    eye = (ri == ci).astype(F32)

    def stack2(x):
        return jnp.concatenate([jnp.where(h0, x, 0.0), jnp.where(h0, 0.0, x)], axis=0)

    pairs = range(RW_HEADS // 2)
    sls = [slice(p * P, (p + 1) * P) for p in pairs]
    a_st = [stack2(a_t[:, sl]) for sl in sls]
    v_st = [stack2(v[:, sl]) for sl in sls]
    g = [_mm(jnp.concatenate([a_st[p], stack2(r_t[:, sls[p]])], axis=0),
             jnp.concatenate([stack2(b_t[:, sls[p]]), stack2(k_t[:, sls[p]])], axis=0), N_RW, tb=True)
         for p in pairs]
    yield
    nmat = [jnp.where(strict, g[p][0:P, 0:P], 0.0) for p in pairs]
    a_ak = [jnp.where(strict, g[p][0:P, P:], 0.0) for p in pairs]
    ri2 = lax.broadcasted_iota(jnp.int32, (P, 2 * P), 0)
    ci2 = lax.broadcasted_iota(jnp.int32, (P, 2 * P), 1)
    incl2 = ((ri2 // C) == ((ci2 % P) // C)) & ((((ci2 % C) - (ri2 % C)) * sgn) <= 0)
    a_r = [jnp.where(incl2, g[p][P:, :], 0.0) for p in pairs]
    akv = [_mm(a_ak[p], v_st[p], N_RW) for p in pairs]
    yield
    t = [eye + nmat[p] for p in pairs]
    npow = nmat
    for _ in range(max(C.bit_length() - 2, 0)):
        npow = [_mm(npow[p], npow[p], N_RW) for p in pairs]
        t = [t[p] + _mm(t[p], npow[p], N_RW) for p in pairs]
        yield
    x = [_mm(t[p], jnp.concatenate([a_st[p], akv[p]], axis=1), N_RW) for p in pairs]
    s = [s_scr[p] for p in pairs]
    ur = [_mm(jnp.concatenate([x[p][0:C, 0:P] + x[p][C:, 0:P], r_t[:, sls[p]]], axis=0), s[p], N_RW, tb=True)
          for p in pairs]
    yield
    u = [ur[p][0:C] + x[p][0:C, P:] + x[p][C:, P:] for p in pairs]
    y_st = [_mm(a_r[p], jnp.concatenate([stack2(u[p]), v_st[p]], axis=0), N_RW) for p in pairs]
    for p in pairs:
        y_ref[0, 0, :, sls[p]] = ur[p][C:] + y_st[p][0:C] + y_st[p][C:]
    yield
    upd = [_mm(jnp.concatenate([u[p], v[:, sls[p]]], axis=0).T,
               jnp.concatenate([b_p[:, sls[p]], k_p[:, sls[p]]], axis=0), N_RW) for p in pairs]
    for p in pairs:
        s_new = s[p] * p_all[:, sls[p]] + jnp.where(same, upd[p], 0.0)
        s_scr[p] = s_new
        st_ref[0, 0, p] = s_new


def _scan_body(r_ref, k_ref, v_ref, kk_ref, lw_ref, a_ref, ka_ref, rs0_ref, qk_ref, gv_ref, la_ref, hm_ref, gs0_ref,
               y_ref, rst_ref, o_ref, gst_ref, rs_scr, gs_scr):
    d = pl.program_id(1)
    c = pl.program_id(2)
    chains = [_rwkv_stages(d, c, r_ref, k_ref, v_ref, kk_ref, lw_ref, a_ref, ka_ref, rs0_ref, y_ref, rst_ref, rs_scr),
              _gla_stages(d, c, qk_ref, gv_ref, la_ref, hm_ref, gs0_ref, o_ref, gst_ref, gs_scr)]
    while chains:
        chains = [ch for ch in chains if next(ch, True) is None]


def _scan(r, k, v, kk, lw, a, ka, rs0, qk, gv, la, hm, gs0):
    B, L, W = r.shape
    kw, vw = GLA_KEY_WIDTH, GLA_VAL_WIDTH
    nc = L // CHUNK
    cidx = _chunk_index(nc)
    shared = lambda w: pl.BlockSpec((1, CHUNK, w), lambda b, d, c: (b, cidx(b, d, c), 0))
    perdir = lambda w: pl.BlockSpec((1, 1, CHUNK, w), lambda b, d, c: (d, b, cidx(b, d, c), 0))
    npair = RW_HEADS // 2
    P = 2 * RW_HEAD_DIM
    rspec = pl.BlockSpec((1, 1, npair, P, P), lambda b, d, c: (b, d, 0, 0, 0))
    gspec = pl.BlockSpec((1, 1, vw, kw), lambda b, d, c: (b, d, 0, 0))
    return pl.pallas_call(
        _scan_body,
        grid=(B, 2, nc),
        in_specs=[shared(W), shared(W), shared(W), shared(W), perdir(W), perdir(W),
                  pl.BlockSpec((1, W), lambda b, d, c: (0, 0)), rspec,
                  shared(2 * kw), shared(vw), perdir(kw), pl.BlockSpec((kw, vw), lambda b, d, c: (0, 0)), gspec],
        out_specs=[perdir(W), rspec, perdir(vw), gspec],
        out_shape=[jax.ShapeDtypeStruct((2, B, L, W), F32),
                   jax.ShapeDtypeStruct((B, 2, npair, P, P), F32),
                   jax.ShapeDtypeStruct((2, B, L, vw), F32),
                   jax.ShapeDtypeStruct((B, 2, vw, kw), F32)],
        scratch_shapes=[pltpu.VMEM((npair, P, P), F32), pltpu.VMEM((vw, kw), F32)],
        compiler_params=_cparams(("parallel", "parallel", "arbitrary")),
        name="scan",
    )(r, k, v, kk, lw, a, ka, rs0, qk, gv, la, hm, gs0)


def _gla_stages(d, c, qk_ref, v_ref, g_ref, hm_ref, s0_ref, o_ref, st_ref, s_scr):
    C = CHUNK
    SB = GLA_SUB
    H = GLA_HEADS

    @pl.when(c == 0)
    def _():
        s_scr[...] = s0_ref[0, 0]

    sgn = 1 - 2 * d
    kw, vw = GLA_KEY_WIDTH, GLA_VAL_WIDTH
    q = qk_ref[0][:, 0:kw]
    k = qk_ref[0][:, kw:]
    v = v_ref[0]
    g = g_ref[0, 0]
    _, upto = _order_masks(C, sgn)
    b = _mm_exact_l(upto.astype(BF16), g, N_CUM)
    pos = lax.broadcasted_iota(jnp.int32, (C, 1), 0) * sgn + d * (C - 1)

    def b_at(p):
        return jnp.sum(jnp.where(pos == p, b, 0.0), axis=0, keepdims=True)

    NEG = -jnp.inf
    tot = b_at(C - 1)
    s = s_scr[...]
    inter = _mm(q * jnp.exp(b), s, N_GLA, tb=True)
    yield

    half = pos >= C // 2
    odd = (pos // SB) % 2 == 1
    r1 = b_at(C // 2 - 1)
    r2 = jnp.where(half, b_at(3 * SB - 1), b_at(SB - 1))
    q1 = q * jnp.exp(jnp.where(half, b - r1, NEG))
    k1 = k * jnp.exp(jnp.where(half, NEG, r1 - b))
    q2 = q * jnp.exp(jnp.where(odd, b - r2, NEG))
    k2 = k * jnp.exp(jnp.where(odd, NEG, r2 - b))
    lane_h = lax.broadcasted_iota(jnp.int32, (C, kw), 1) // GLA_KEY_DIM

    def stack_heads(x):
        return jnp.concatenate([jnp.where(lane_h == h, x, 0.0) for h in range(H)], axis=0)

    att1 = _mm(stack_heads(q1), k1, N_GLA, tb=True)
    att2 = _mm(stack_heads(q2), k2, N_GLA, tb=True)
    yield
    pr =(lax.broadcasted_iota(jnp.int32, (H * C, C), 0) % C) * sgn + d * (C - 1)
    pc = lax.broadcasted_iota(jnp.int32, (H * C, C), 1) * sgn + d * (C - 1)
    att = att1 + jnp.where((pr >= C // 2) == (pc >= C // 2), att2, 0.0)
    res = _mm(att, v, N_GLA)
    lane_hv = lax.broadcasted_iota(jnp.int32, (C, vw), 1) // GLA_VAL_DIM
    off = None
    for h in range(H):
        t = jnp.where(lane_hv == h, res[h * C:(h + 1) * C], 0.0)
        off = t if off is None else off + t
    yield

    hm = hm_ref[...]
    il = lax.broadcasted_iota(jnp.int32, (SB, 1), 0)
    diag = []
    for blk in range(C // SB):
        rs = slice(blk * SB, (blk + 1) * SB)
        bb, qb, kb, vb = b[rs], q[rs], k[rs], v[rs]
        pieces = []
        for j in range(SB):
            valid = (il - j) * sgn >= 0
            pieces.append(jnp.exp(jnp.where(valid, bb - bb[j:j + 1], NEG)) * qb * kb[j:j + 1])
        a = _mm_exact_r(jnp.concatenate(pieces, axis=0), hm, N_GLA_INTRA)
        acc = None
        for j in range(SB):
            t = a[j * SB:(j + 1) * SB] * vb[j:j + 1]
            acc = t if acc is None else acc + t
        diag.append(acc)
        yield
    o_ref[0, 0] = inter + off + jnp.concatenate(diag, axis=0)
    yield

    upd = _mm(v.T, k * jnp.exp(tot - b), N_GLA)
    ri = lax.broadcasted_iota(jnp.int32, upd.shape, 0) // GLA_VAL_DIM
    ci = lax.broadcasted_iota(jnp.int32, upd.shape, 1) // GLA_KEY_DIM
    s_new = s * jnp.exp(tot) + jnp.where(ri == ci, upd, 0.0)
    s_scr[...] = s_new
    st_ref[0, 0] = s_new


def _seg_norm(y, bd, dim, eps):
    mu = _mm_exact_r(y, bd, N_SEG) * (1.0 / dim)
    dlt = y - mu
    var = _mm_exact_r(dlt * dlt, bd, N_SEG) * (1.0 / dim)
    return dlt * lax.rsqrt(var + eps)


def _layer_norm(x, w, b):
    mu = jnp.mean(x, axis=-1, keepdims=True)
    dlt = x - mu
    var = jnp.mean(dlt * dlt, axis=-1, keepdims=True)
    return dlt * lax.rsqrt(var + LN_EPS) * w + b


def _merge_body(x_ref, mod_ref, yrw_ref, bonus_ref, g_ref, ygla_ref, og_ref,
                bd64_ref, bd128_ref, rgw_ref, rgb_ref, ggw_ref, ggb_ref, *refs):
    n = N_MERGE
    wrw = [r[...] for r in refs[0:n]]
    wgla = [r[...] for r in refs[n:2 * n]]
    wout = [r[...] for r in refs[2 * n:3 * n]]
    wgate = [r[...] for r in refs[3 * n:3 * n + N_PROJ]]
    ln_w_ref, ln_b_ref, x1_ref, h2p_ref = refs[3 * n + N_PROJ:]
    y = _seg_norm(yrw_ref[0, 0] + yrw_ref[1, 0], bd64_ref[...], RW_HEAD_DIM, RW_GN_EPS)
    y = (y * rgw_ref[...] + rgb_ref[...] + bonus_ref[0]) * g_ref[0]
    y_rw = _mmp(_split(y, n), wrw)
    y = _seg_norm(ygla_ref[0, 0] + ygla_ref[1, 0], bd128_ref[...], GLA_VAL_DIM, LN_EPS)
    y = (y * ggw_ref[...] + ggb_ref[...]) * _silu(og_ref[0])
    y_gla = _mmp(_split(y, n), wgla)
    h = x_ref[0] * (1.0 + mod_ref[0, 1:2, :]) + mod_ref[0, 0:1, :]
    gate = _sigmoid(_mmp(_split(h, N_PROJ), wgate))
    mixed = gate[:, 0:D_MODEL] * y_rw + gate[:, D_MODEL:] * y_gla
    mix = _mmp(_split(mixed, n), wout)
    g1 = mod_ref[0, 2:3, :]
    x1 = _layer_norm(ALPHA * x_ref[0] + g1 * mix, ln_w_ref[...], ln_b_ref[...])
    x1_ref[0] = x1
    h2p_ref[0] = _pack_bf16_pairs(_moe_input(x1, mod_ref))


def _merge(x, mod, yrw, bonus, g, ygla, p_gla, consts, weights, ln_w, ln_b, tm):
    B, L, D = x.shape
    W = RW_WIDTH
    tok = lambda w: pl.BlockSpec((1, tm, w), lambda b, i: (b, i, 0))
    dirs = pl.BlockSpec((2, 1, tm, W), lambda b, i: (0, b, i, 0))
    wl = [w for ws in weights for w in ws]
    return pl.pallas_call(
        _merge_body,
        grid=(B, L // tm),
        in_specs=[tok(D), pl.BlockSpec((1, 6, D), lambda b, i: (b, 0, 0)), dirs, tok(W), tok(W), dirs,
                  pl.BlockSpec((1, tm, W), lambda b, i: (b, i, GLA_QKV_COLS // W))]
                 + [_full(c.shape) for c in consts] + [_full(w.shape) for w in wl]
                 + [_full(ln_w.shape), _full(ln_b.shape)],
        out_specs=[tok(D), tok(D // 2)],
        out_shape=[jax.ShapeDtypeStruct((B, L, D), F32), jax.ShapeDtypeStruct((B, L, D // 2), jnp.int32)],
        compiler_params=_cparams(("parallel", "arbitrary")),
        name="merge",
    )(x, mod, yrw, bonus, g, ygla, p_gla, *consts, *wl, ln_w, ln_b)


def _moe_input(x1, mod_ref):
    return x1 * (1.0 + mod_ref[0, 4:5, :]) + mod_ref[0, 3:4, :]


def _router_body(x1_ref, mod_ref, *refs):
    n = N_ROUTER
    rt = [r[...] for r in refs[0:n]]
    bias_ref, e_out, g_out, cnt_out = refs[n:]
    tm = x1_ref.shape[0]
    E, G, PG = N_EXPERTS, N_GROUPS, N_EXPERTS // N_GROUPS
    h = _moe_input(x1_ref[...], mod_ref)
    logits = _mmp(rt, _split(h, n), tb=True)
    scores = _sigmoid(logits)
    sel = scores + bias_ref[:, 0:tm]
    NEG = -jnp.inf
    ip = lax.broadcasted_iota(jnp.int32, (PG, tm), 0)
    group_rows = []
    for gidx in range(G):
        sg = sel[gidx * PG:(gidx + 1) * PG]
        m1 = jnp.max(sg, axis=0, keepdims=True)
        first = jnp.min(jnp.where(sg == m1, ip, PG), axis=0, keepdims=True)
        m2 = jnp.max(jnp.where(ip == first, NEG, sg), axis=0, keepdims=True)
        group_rows.append(m1 + m2)
    gs = jnp.concatenate(group_rows, axis=0)
    gi = lax.broadcasted_iota(jnp.int32, (G, tm), 0)
    keep = jnp.zeros((G, tm), F32)
    for _ in range(TOPK_GROUPS):
        m = jnp.max(gs, axis=0, keepdims=True)
        idx = jnp.min(jnp.where(gs == m, gi, G), axis=0, keepdims=True)
        hit = gi == idx
        keep = jnp.where(hit, 1.0, keep)
        gs = jnp.where(hit, NEG, gs)
    cur = jnp.concatenate(
        [jnp.where(keep[gidx:gidx + 1] > 0.5, sel[gidx * PG:(gidx + 1) * PG], NEG) for gidx in range(G)],
        axis=0)
    ei = lax.broadcasted_iota(jnp.int32, (E, tm), 0)
    idxs, gates = [], []
    picked = jnp.zeros((E, tm), F32)
    for _ in range(TOP_K):
        m = jnp.max(cur, axis=0, keepdims=True)
        idx = jnp.min(jnp.where(cur == m, ei, E), axis=0, keepdims=True)
        hit = ei == idx
        idxs.append(idx)
        gates.append(jnp.sum(jnp.where(hit, scores, 0.0), axis=0, keepdims=True))
        cur = jnp.where(hit, NEG, cur)
        picked = jnp.where(hit, 1.0, picked)
    gate = jnp.concatenate(gates, axis=0)
    e_out[...] = jnp.concatenate(idxs, axis=0)
    g_out[...] = gate / jnp.sum(gate, axis=0, keepdims=True) * ROUTED_SCALE

    @pl.when(pl.program_id(0) == 0)
    def _():
        cnt_out[...] = jnp.zeros(cnt_out.shape, F32)

    part = picked[:, 0:128]
    for j in range(1, tm // 128):
        part = part + picked[:, j * 128:(j + 1) * 128]
    cnt_out[...] += part


def _router(x1f, mod, seq, router_t_parts, bias_b, tm):
    n, D = x1f.shape
    E = N_EXPERTS
    return pl.pallas_call(
        _router_body,
        grid=(n // tm,),
        in_specs=[pl.BlockSpec((tm, D), lambda i: (i, 0)),
                  pl.BlockSpec((1, 6, D), lambda i: (i * tm // seq, 0, 0))]
                 + [pl.BlockSpec((E, D), lambda i: (0, 0))] * len(router_t_parts)
                 + [pl.BlockSpec(bias_b.shape, lambda i: (0, 0))],
        out_specs=[pl.BlockSpec((TOP_K, tm), lambda i: (0, i))] * 2
                  + [pl.BlockSpec((E, 128), lambda i: (0, 0))],
        out_shape=[jax.ShapeDtypeStruct((TOP_K, n), jnp.int32), jax.ShapeDtypeStruct((TOP_K, n), F32),
                   jax.ShapeDtypeStruct((E, 128), F32)],
        compiler_params=_cparams(("arbitrary",)),
        name="router",
    )(x1f, mod, *router_t_parts, bias_b)


def _sc_permute_body(scatter, per_w, src_hbm, idx_hbm, out_hbm, idx_v, rows_v, sem_i, sem_o):
    wid = lax.axis_index("s") * SC_CORES + lax.axis_index("c")
    base = wid * per_w
    win = SC_WINDOW
    pltpu.sync_copy(idx_hbm.at[wid], idx_v)

    def linear(ref, j):
        return ref.at[pl.ds(base + j * win, win)]

    def indexed(ref, j):
        return ref.at[idx_v.at[j]]

    src_at, out_at = (linear, indexed) if scatter else (indexed, linear)

    @pl.loop(0, per_w // win, step=2)
    def _(j):
        i0 = pltpu.async_copy(src_at(src_hbm, j), rows_v.at[0], sem_i.at[0])
        i1 = pltpu.async_copy(src_at(src_hbm, j + 1), rows_v.at[1], sem_i.at[1])
        i0.wait()
        o0 = pltpu.async_copy(rows_v.at[0], out_at(out_hbm, j), sem_o.at[0])
        i1.wait()
        o1 = pltpu.async_copy(rows_v.at[1], out_at(out_hbm, j + 1), sem_o.at[1])
        o0.wait()
        o1.wait()


def _sc_permute(src, idx, scatter):
    nrows = idx.shape[0]
    D = src.shape[1]
    per_w = nrows // SC_WORKERS
    mesh = plsc.VectorSubcoreMesh(core_axis_name="c", subcore_axis_name="s",
                                  num_cores=SC_CORES, num_subcores=SC_SUBCORES)
    return pl.kernel(
        functools.partial(_sc_permute_body, scatter, per_w),
        out_type=jax.ShapeDtypeStruct((nrows, D), src.dtype),
        mesh=mesh,
        scratch_types=[pltpu.VMEM((per_w // SC_WINDOW, SC_WINDOW), jnp.int32),
                       pltpu.VMEM((2, SC_WINDOW, D), src.dtype),
                       pltpu.SemaphoreType.DMA((2,)), pltpu.SemaphoreType.DMA((2,))],
        name="sc_scatter" if scatter else "sc_gather",
    )(src, idx.reshape(SC_WORKERS, per_w // SC_WINDOW, SC_WINDOW))


def _experts_body(be_ref, nu_ref, x_ref, wgu_ref, wd_ref, y_ref, wgu_bf, wd_bf):
    i = pl.program_id(0)

    @pl.when(i < nu_ref[0])
    def _():
        @pl.when((i == 0) | (be_ref[i] != be_ref[jnp.maximum(i - 1, 0)]))
        def _():
            wgu_bf[...] = wgu_ref[0].astype(BF16)
            wd_bf[...] = wd_ref[0].astype(BF16)

        F = EXPERT_DIM
        halves = [slice(h * (ROW_BLOCK // 2), (h + 1) * (ROW_BLOCK // 2)) for h in range(2)]
        xb = [_unpack_bf16_pairs(x_ref[rs, :]).astype(BF16) for rs in halves]
        gu = [_dot(t, wgu_bf[...]) for t in xb]
        act = [(_silu(t[:, 0:F]) * t[:, F:]).astype(BF16) for t in gu]
        yb = [_dot(t, wd_bf[...]) for t in act]
        for rs, t in zip(halves, yb):
            y_ref[rs, :] = _pack_bf16_pairs(t)

    @pl.when(i >= nu_ref[0])
    def _():
        y_ref[...] = jnp.zeros(y_ref.shape, jnp.int32)


def _experts(block_e, nused, xs, w_gate_up, w_down):
    nb = block_e.shape[0]
    DP = xs.shape[1]
    D = 2 * DP
    R = ROW_BLOCK
    F2 = w_gate_up.shape[2]
    last = lambda i, nu: jnp.minimum(i, nu[0] - 1)
    grid_spec = pltpu.PrefetchScalarGridSpec(
        num_scalar_prefetch=2,
        grid=(nb,),
        in_specs=[pl.BlockSpec((R, DP), lambda i, be, nu: (last(i, nu), 0)),
                  pl.BlockSpec((1, D, F2), lambda i, be, nu: (be[last(i, nu)], 0, 0)),
                  pl.BlockSpec((1, F2 // 2, D), lambda i, be, nu: (be[last(i, nu)], 0, 0))],
        out_specs=pl.BlockSpec((R, DP), lambda i, be, nu: (i, 0)),
        scratch_shapes=[pltpu.VMEM((D, F2), BF16), pltpu.VMEM((F2 // 2, D), BF16)],
    )
    return pl.pallas_call(
        _experts_body,
        grid_spec=grid_spec,
        out_shape=jax.ShapeDtypeStruct((nb * R, DP), jnp.int32),
        compiler_params=_cparams(("arbitrary",)),
        name="experts",
    )(block_e, nused, xs, w_gate_up, w_down)


def _final_body(yg_ref, x1_ref, mod_ref, gate_ref, *refs):
    n = N_SHARED
    sgu = [r[...] for r in refs[0:n]]
    sd = [r[...] for r in refs[n:2 * n]]
    ln_w_ref, ln_b_ref, o_ref = refs[2 * n:]
    tm = x1_ref.shape[1]
    gate = gate_ref[0]
    routed = None
    for kk in range(TOP_K):
        t = _unpack_bf16_pairs(yg_ref[kk * tm:(kk + 1) * tm, :]) * gate[:, kk:kk + 1]
        routed = t if routed is None else routed + t
    h2 = _moe_input(x1_ref[0], mod_ref)
    F = sgu[0].shape[1] // 2
    gu = _mmp(_split(h2, n), sgu)
    act = _silu(gu[:, 0:F]) * gu[:, F:]
    shared = _mmp(_split(act, n), sd)
    g2 = mod_ref[0, 5:6, :]
    o_ref[0] = _layer_norm(ALPHA * x1_ref[0] + g2 * (routed + shared), ln_w_ref[...], ln_b_ref[...])


def _final(yg, x1, mod, gate, sgu, sd, ln_w, ln_b, tm):
    B, L, D = x1.shape
    ni = L // tm
    rows = tm * TOP_K
    tok = lambda w: pl.BlockSpec((1, tm, w), lambda b, i: (b, i, 0))
    ws = list(sgu) + list(sd)
    return pl.pallas_call(
        _final_body,
        grid=(B, ni),
        in_specs=[pl.BlockSpec((rows, D // 2), lambda b, i: (b * ni + i, 0)),
                  tok(D), pl.BlockSpec((1, 6, D), lambda b, i: (b, 0, 0)), tok(TOP_K)]
                 + [_full(w.shape) for w in ws] + [_full(ln_w.shape), _full(ln_b.shape)],
        out_specs=tok(D),
        out_shape=jax.ShapeDtypeStruct((B, L, D), F32),
        compiler_params=_cparams(("parallel", "arbitrary")),
        name="final",
    )(yg, x1, mod, gate, *ws, ln_w, ln_b)


def _dispatch_plan(experts_t, sizes, tm):
    R, E = ROW_BLOCK, N_EXPERTS
    K, n = experts_t.shape
    nk = K * n
    rows = nk + E * (R - 1)
    rows = -(-rows // math.lcm(R, SC_ROW_ALIGN)) * math.lcm(R, SC_ROW_ALIGN)
    nb = rows // R
    ncand = nb * R - nk
    padded = (sizes + R - 1) // R * R
    pad_end = jnp.cumsum(padded)
    tok = jnp.broadcast_to(jnp.arange(n, dtype=jnp.int32)[None, :], (K, n))
    slot = jnp.arange(K, dtype=jnp.int32)[:, None]
    aid = (tok // tm) * (tm * K) + slot * tm + tok % tm
    cand = jnp.arange(ncand, dtype=jnp.int32)
    ce, cp = cand // (R - 1), cand % (R - 1)
    need = jnp.broadcast_to((padded - sizes)[:, None], (E, R - 1)).reshape(-1)
    need = jnp.pad(need, (0, ncand - E * (R - 1)))
    ckey = jnp.where(cp < need, 2 * ce + 1, 2 * E + 1)
    id_bits = (nb * R - 1).bit_length()
    assert (2 * E + 2) << id_bits < 2 ** 31
    keys = jnp.concatenate([2 * experts_t.reshape(-1), ckey]) << id_bits
    ids = jnp.concatenate([aid.reshape(-1), nk + cand])
    row_id = lax.sort(keys | ids) & ((1 << id_bits) - 1)
    row_tok = jnp.where(row_id < nk, (row_id // (tm * K)) * tm + row_id % tm, (row_id - nk) % n)
    block_e = jnp.minimum(jnp.searchsorted(pad_end, jnp.arange(nb) * R, side='right'), E - 1).astype(jnp.int32)
    nused = (pad_end[-1] // R).astype(jnp.int32).reshape(1)
    return row_tok, row_id, block_e, nused


def _block_diag_ones(n, blk):
    i = jnp.arange(n) // blk
    return (i[:, None] == i[None, :]).astype(BF16)


def kernel(x, c, ctx, c_ctx, w_ada, b_ada, w_in, rw_mu, rw_w0, rw_w2, rw_a0, rw_a2, rw_g2, rw_k_k, rw_k_a,
           rw_r_k, rw_gn_w, rw_gn_b, gla_conv, gla_g2, gla_gb, gla_gn_w, gla_gn_b, w_br_rw, w_br_gla, w_out,
           ln1_w, ln1_b, router, router_bias, w_gate_up, w_down, sh_gate_up, sh_down, ln2_w, ln2_b):
    B, L, D = x.shape
    CT = ctx.shape[1]
    l = 0
    W = RW_WIDTH
    row = lambda t: t.reshape(1, -1)

    rows = -(-(B + 1) // 8) * 8
    cc = jnp.zeros((rows, D), F32).at[:B].set(c).at[B].set(c_ctx)
    mod = _ada(cc, w_ada[l], b_ada[l])
    mod_lat = mod[:B].reshape(B, 6, D)
    mod_ctx = jnp.broadcast_to(mod[B].reshape(1, 6, D), (B, 6, D))

    w = w_in[l]
    g0 = RW_COLS
    w_rw = jnp.pad(w[:, :RW_COLS], ((0, 0), (0, RW_SEG - RW_COLS)))
    w_gla = jnp.concatenate([w[:, g0:g0 + GLA_QKV_COLS],
                             w[:, g0 + GLA_QKV_COLS + GLA_GATE_RANK:g0 + GLA_COLS],
                             w[:, g0 + GLA_QKV_COLS:g0 + GLA_QKV_COLS + GLA_GATE_RANK],
                             jnp.zeros((D, GLA_SEG - GLA_COLS), F32)], axis=1)
    w_gate = w[:, MIX_COLS:]
    tm_p = min(512, L)
    p_rw = _proj(x, mod_lat, w_rw, tm_p)
    p_gla = _proj(x, mod_lat, w_gla, tm_p)
    pc_rw = _proj(ctx, mod_ctx, w_rw, CT)
    pc_gla = _proj(ctx, mod_ctx, w_gla, CT)

    mu = jnp.pad(rw_mu[l], (0, RW_SEG - RW_COLS)).reshape(1, -1)
    lora = jnp.zeros((256, 5 * W), F32)
    e1 = RW_DECAY_RANK
    e2 = e1 + RW_AAA_RANK
    e3 = e2 + RW_GATE_RANK
    for d in range(2):
        lora = lora.at[0:e1, d * W:(d + 1) * W].set(rw_w2[l, d])
        lora = lora.at[e1:e2, (2 + d) * W:(3 + d) * W].set(rw_a2[l, d])
    lora = lora.at[e2:e3, 4 * W:].set(rw_g2[l])
    lora_p = (_split(lora, N_LORA) + [jnp.zeros_like(lora, BF16)])[:2]
    bd64 = _block_diag_ones(W, RW_HEAD_DIM)
    rw_consts = [mu, lora_p[0], lora_p[1], rw_w0[l].reshape(1, -1), rw_a0[l].reshape(1, -1),
                 row(rw_k_k[l]), row(rw_k_a[l]), row(rw_r_k[l]), bd64]
    tm_f = min(256, L)
    r, k, v, kkn, g, bonus, lwd, a = _feat_rw(p_rw, True, tm_f, rw_consts)
    rc, kc, vc, kknc, _, _, lwdc, ac = _feat_rw(pc_rw, False, CT, rw_consts)

    g2 = jnp.zeros((GLA_SEG - GLA_QKV_COLS - GLA_VAL_WIDTH, 2 * GLA_KEY_WIDTH), F32)
    g2 = g2.at[:GLA_GATE_RANK].set(jnp.concatenate([gla_g2[l, 0], gla_g2[l, 1]], axis=1))
    g2_p = (_split(g2, N_LORA) + [jnp.zeros_like(g2, BF16)])[:2]
    cw = jnp.pad(gla_conv[l].reshape(9, GLA_QKV_COLS), ((0, 7), (0, 0)))
    gla_consts = [cw, g2_p[0], g2_p[1], gla_gb[l].reshape(1, -1)]
    qk, vg, la = _feat_gla(p_gla, True, tm_f, gla_consts)
    qkc, vgc, lac = _feat_gla(pc_gla, False, CT, gla_consts)

    ka = row(rw_k_a[l])
    P = 2 * RW_HEAD_DIM
    s0 = jnp.zeros((B, 2, RW_HEADS // 2, P, P), F32)
    hi = jnp.arange(GLA_KEY_WIDTH) // GLA_KEY_DIM
    hj = jnp.arange(GLA_VAL_WIDTH) // GLA_VAL_DIM
    hm = (hi[:, None] == hj[None, :]).astype(BF16)
    g0s = jnp.zeros((B, 2, GLA_VAL_WIDTH, GLA_KEY_WIDTH), F32)
    _, s_ctx, _, gs_ctx = _scan(rc, kc, vc, kknc, lwdc, ac, ka, s0, qkc, vgc, lac, hm, g0s)
    y_rw, _, y_gla, _ = _scan(r, k, v, kkn, lwd, a, ka, s_ctx, qk, vg, la, hm, gs_ctx)

    bd128 = _block_diag_ones(GLA_VAL_WIDTH, GLA_VAL_DIM)
    m_consts = [bd64, bd128, row(rw_gn_w[l]), row(rw_gn_b[l]), row(gla_gn_w[l]), row(gla_gn_b[l])]
    m_weights = [_split(w_br_rw[l], N_MERGE), _split(w_br_gla[l], N_MERGE), _split(w_out[l], N_MERGE),
                 _split(w_gate, N_PROJ)]
    x1, h2p = _merge(x, mod_lat, y_rw, bonus, g, y_gla, p_gla, m_consts, m_weights,
                     row(ln1_w[l]), row(ln1_b[l]), min(256, L))

    n = B * L
    tm_r = min(256, L)
    bias_b = jnp.broadcast_to(router_bias[l].reshape(-1, 1), (N_EXPERTS, tm_r))
    e_t, g_t, cnt = _router(x1.reshape(n, D), mod_lat, L, _split(router[l].T, N_ROUTER), bias_b, tm_r)
    sizes = jnp.sum(cnt, axis=1).astype(jnp.int32)
    tm_c = min(128, L)
    row_tok, row_id, block_e, nused = _dispatch_plan(e_t, sizes, tm_c)
    xs = _sc_permute(h2p.reshape(n, D // 2), row_tok, scatter=False)
    y = _experts(block_e, nused, xs, w_gate_up[l], w_down[l])
    yg = _sc_permute(y, row_id, scatter=True)
    gate_tok = g_t.T.reshape(B, L, TOP_K)
    return _final(yg, x1, mod_lat, gate_tok, _split(sh_gate_up[l], N_SHARED),
                  _split(sh_down[l], N_SHARED), row(ln2_w[l]), row(ln2_b[l]), tm_c)
```

```python
import functools
import math

import jax
import jax.numpy as jnp
from jax import lax
from jax.experimental import pallas as pl
from jax.experimental.pallas import tpu as pltpu
from jax.experimental.pallas import tpu_sc as plsc

F32 = jnp.float32
BF16 = jnp.bfloat16

D_MODEL = 1024
GRID_W = 64
RW_WIDTH = 512
RW_HEADS = 8
RW_HEAD_DIM = 64
RW_DECAY_RANK = 32
RW_AAA_RANK = 32
RW_GATE_RANK = 96
RW_GN_EPS = 64e-5
RW_COLS = 1696
RW_SEG = 1792
GLA_HEADS = 4
GLA_KEY_WIDTH = 256
GLA_VAL_WIDTH = 512
GLA_KEY_DIM = 64
GLA_VAL_DIM = 128
GLA_GATE_RANK = 16
GLA_TAU = 16.0
GLA_QKV_COLS = 1024
GLA_COLS = 1552
GLA_SEG = 1664
MIX_COLS = RW_COLS + GLA_COLS
N_EXPERTS = 256
TOP_K = 8
N_GROUPS = 8
TOPK_GROUPS = 4
EXPERT_DIM = 256
ROUTED_SCALE = 2.5
LN_EPS = 1e-5
DEPTH = 1
ALPHA = (2 * DEPTH) ** 0.25

CHUNK = 64
GLA_SUB = 16
ROW_BLOCK = 512
VMEM_LIMIT = 48 * 1024 * 1024
SC_CORES = 2
SC_SUBCORES = 16
SC_WORKERS = SC_CORES * SC_SUBCORES
SC_WINDOW = 32
SC_ROW_ALIGN = 2 * SC_WORKERS * SC_WINDOW

N_ADA = 3
N_PROJ = 1
N_LORA = 1
N_SEG = 2
N_CUM = 2
N_RW = 1
N_GLA = 1
N_GLA_INTRA = 1
N_MERGE = 1
N_ROUTER = 3
N_SHARED = 1


def _split(x, n):
    parts = []
    r = x
    for i in range(n):
        p = r.astype(BF16)
        parts.append(p)
        if i < n - 1:
            r = r - p.astype(F32)
    return parts


def _dot(a, b, ta=False, tb=False):
    dn = (((0 if ta else 1,), (1 if tb else 0,)), ((), ()))
    return lax.dot_general(a, b, dn, preferred_element_type=F32)


def _mmp(ap, bp, ta=False, tb=False):
    n = max(len(ap), len(bp))
    out = None
    for i in range(len(ap)):
        for j in range(len(bp)):
            if i + j <= n - 1:
                t = _dot(ap[i], bp[j], ta, tb)
                out = t if out is None else out + t
    return out


def _mm(a, b, n, ta=False, tb=False):
    return _mmp(_split(a, n), _split(b, n), ta, tb)


def _mm_exact_l(m_bf16, x, n):
    return _mmp([m_bf16], _split(x, n))


def _mm_exact_r(x, m_bf16, n):
    return _mmp(_split(x, n), [m_bf16])


def _pack_bf16_pairs(x):
    w = x.shape[1] // 2
    hi = lax.bitcast_convert_type(x[:, :w].astype(BF16).astype(F32), jnp.int32)
    lo = lax.bitcast_convert_type(x[:, w:].astype(BF16).astype(F32), jnp.int32)
    return hi | lax.shift_right_logical(lo, 16)


def _unpack_bf16_pairs(p):
    hi = lax.bitcast_convert_type(p & jnp.int32(-65536), F32)
    lo = lax.bitcast_convert_type(lax.shift_left(p, 16), F32)
    return jnp.concatenate([hi, lo], axis=1)


def _sigmoid(x):
    return 1.0 / (1.0 + jnp.exp(-x))


def _silu(x):
    return x * _sigmoid(x)


def _log_sigmoid(x):
    return jnp.minimum(x, 0.0) - jnp.log(1.0 + jnp.exp(-jnp.abs(x)))


def _cparams(sem):
    return pltpu.CompilerParams(dimension_semantics=sem, vmem_limit_bytes=VMEM_LIMIT)


def _full(shape):
    nd = len(shape)
    return pl.BlockSpec(shape, lambda *a: (0,) * nd)


def _ada_body(c_ref, w_ref, b_ref, o_ref):
    s = _silu(c_ref[...])
    o_ref[...] = _mm(s, w_ref[...], N_ADA) + b_ref[...]


def _ada(cc, w, b):
    rows, d = cc.shape
    n = w.shape[1] // d
    return pl.pallas_call(
        _ada_body,
        grid=(n,),
        in_specs=[pl.BlockSpec((rows, d), lambda j: (0, 0)),
                  pl.BlockSpec((d, d), lambda j: (0, j)),
                  pl.BlockSpec((1, d), lambda j: (0, j))],
        out_specs=pl.BlockSpec((rows, d), lambda j: (0, j)),
        out_shape=jax.ShapeDtypeStruct((rows, w.shape[1]), F32),
        compiler_params=_cparams(("arbitrary",)),
        name="ada",
    )(cc, w, b.reshape(1, -1))


def _proj_body(x_ref, mod_ref, *refs):
    w_refs, o_ref = refs[:-1], refs[-1]
    sh = mod_ref[0, 0:1, :]
    sc = mod_ref[0, 1:2, :]
    h = x_ref[0] * (1.0 + sc) + sh
    o_ref[0] = _mmp(_split(h, N_PROJ), [w[...] for w in w_refs])


def _proj(x, mod, w, tm):
    B, L, D = x.shape
    n = w.shape[1]
    wp = _split(w, N_PROJ)
    return pl.pallas_call(
        _proj_body,
        grid=(B, L // tm),
        in_specs=[pl.BlockSpec((1, tm, D), lambda b, i: (b, i, 0)),
                  pl.BlockSpec((1, 6, D), lambda b, i: (b, 0, 0))]
                 + [pl.BlockSpec((D, n), lambda b, i: (0, 0))] * N_PROJ,
        out_specs=pl.BlockSpec((1, tm, n), lambda b, i: (b, i, 0)),
        out_shape=jax.ShapeDtypeStruct((B, L, n), F32),
        compiler_params=_cparams(("parallel", "arbitrary")),
        name="proj",
    )(x, mod, *wp)


def _neighbours(vertical, width, tm, cur, prev_ref, next_ref):
    if vertical:
        i = pl.program_id(1)
        n = pl.num_programs(1)
        prev = jnp.where(i > 0, prev_ref[0], 0.0)
        nxt = jnp.where(i < n - 1, next_ref[0], 0.0)
        pad = jnp.zeros((8, cur.shape[1]), F32)
        ext = jnp.concatenate([pad, prev, cur, nxt, pad], axis=0)
        off = GRID_W + 8
    else:
        pad = jnp.zeros((8, cur.shape[1]), F32)
        ext = jnp.concatenate([pad, cur, pad], axis=0)
        off = 8
    col = lax.broadcasted_iota(jnp.int32, (tm, 1), 0) % width

    def get(dr, dc):
        s = off + GRID_W * dr + dc
        v = ext[s:s + tm]
        if dc == -1:
            v = jnp.where(col == 0, 0.0, v)
        elif dc == 1:
            v = jnp.where(col == width - 1, 0.0, v)
        return v

    return get


def _feat_rw_body(vertical, width, tm, *refs):
    if vertical:
        prev_ref, cur_ref, next_ref = refs[:3]
        refs = refs[3:]
    else:
        cur_ref = refs[0]
        prev_ref = next_ref = None
        refs = refs[1:]
    (mu_ref, lw_hi_ref, lw_lo_ref, w0_ref, a0_ref, kk_ref, ka_ref, rk_ref, bd_ref,
     r_out, k_out, v_out, kkn_out, g_out, bonus_out, lwd_out, a_out) = refs
    cur = cur_ref[0]
    get = _neighbours(vertical, width, tm, cur, prev_ref, next_ref)
    left, right = get(0, -1), get(0, 1)
    up, down = (get(-1, 0), get(1, 0)) if vertical else (left, right)
    l4 = lax.broadcasted_iota(jnp.int32, (1, cur.shape[1]), 1) % 4
    shifted = jnp.where(l4 == 0, left, jnp.where(l4 == 1, right, jnp.where(l4 == 2, up, down)))
    p = cur + mu_ref[...] * (shifted - cur)

    r = p[:, 0:RW_WIDTH]
    k = p[:, RW_WIDTH:2 * RW_WIDTH]
    v = p[:, 2 * RW_WIDTH:3 * RW_WIDTH]
    slab = p[:, 3 * RW_WIDTH:3 * RW_WIDTH + 256]
    ln = lax.broadcasted_iota(jnp.int32, (1, 256), 1)
    e1 = RW_DECAY_RANK
    e2 = e1 + RW_AAA_RANK
    e3 = e2 + RW_GATE_RANK
    slab = jnp.where(ln < e1, jnp.tanh(slab),
                     jnp.where(ln < e2, slab, jnp.where(ln < e3, _sigmoid(slab), 0.0)))
    lo = _mmp(_split(slab, N_LORA), [lw_hi_ref[...], lw_lo_ref[...]][:N_LORA])
    W = RW_WIDTH
    a_sum = None
    for d in range(2):
        z = w0_ref[:, d * W:(d + 1) * W] + lo[:, d * W:(d + 1) * W]
        lwd_out[d, 0] = -_sigmoid(z) * math.exp(-0.5)
        a = _sigmoid(a0_ref[:, d * W:(d + 1) * W] + lo[:, (2 + d) * W:(3 + d) * W])
        a_out[d, 0] = a
        a_sum = a if a_sum is None else a_sum + a
    g_out[0] = lo[:, 4 * W:5 * W]
    bd = bd_ref[...]
    kk = k * kk_ref[...]
    ss = _mm_exact_r(kk * kk, bd, N_SEG)
    kkn_out[0] = kk / jnp.maximum(jnp.sqrt(ss), 1e-12)
    kmod_sum = k * (2.0 + (a_sum - 2.0) * ka_ref[...])
    bonus_out[0] = _mm_exact_r(r * kmod_sum * rk_ref[...], bd, N_SEG) * v
    r_out[0] = r
    k_out[0] = k
    v_out[0] = v


def _feat_rw(p, vertical, tm, consts):
    B, L, S = p.shape
    width = GRID_W if vertical else tm
    W = RW_WIDTH
    hb = tm // GRID_W
    nhb = L // GRID_W
    tile = pl.BlockSpec((1, tm, S), lambda b, i: (b, i, 0))
    if vertical:
        in_specs = [pl.BlockSpec((1, GRID_W, S), lambda b, i: (b, jnp.maximum(i * hb - 1, 0), 0)),
                    tile,
                    pl.BlockSpec((1, GRID_W, S), lambda b, i: (b, jnp.minimum((i + 1) * hb, nhb - 1), 0))]
        args = [p, p, p]
    else:
        in_specs = [tile]
        args = [p]
    in_specs += [_full(c.shape) for c in consts]
    o1 = pl.BlockSpec((1, tm, W), lambda b, i: (b, i, 0))
    o2 = pl.BlockSpec((2, 1, tm, W), lambda b, i: (0, b, i, 0))
    s1 = jax.ShapeDtypeStruct((B, L, W), F32)
    s2 = jax.ShapeDtypeStruct((2, B, L, W), F32)
    return pl.pallas_call(
        functools.partial(_feat_rw_body, vertical, width, tm),
        grid=(B, L // tm),
        in_specs=in_specs,
        out_specs=[o1] * 6 + [o2] * 2,
        out_shape=[s1] * 6 + [s2] * 2,
        compiler_params=_cparams(("parallel", "arbitrary")),
        name="feat_rw",
    )(*args, *consts)


def _feat_gla_body(vertical, width, tm, *refs):
    if vertical:
        prev_ref, cur_ref, next_ref = refs[:3]
        refs = refs[3:]
    else:
        cur_ref = refs[0]
        prev_ref = next_ref = None
        refs = refs[1:]
    cw_ref, g2_hi_ref, g2_lo_ref, gb_ref, qk_out, v_out, la_out = refs
    cur = cur_ref[0][:, 0:GLA_QKV_COLS]
    if vertical:
        class _Slice:
            def __init__(self, ref):
                self.ref = ref

            def __getitem__(self, idx):
                return self.ref[idx][:, 0:GLA_QKV_COLS]
        get = _neighbours(True, width, tm, cur, _Slice(prev_ref), _Slice(next_ref))
    else:
        get = _neighbours(False, width, tm, cur, None, None)
    acc = None
    for dr in ((-1, 0, 1) if vertical else (0,)):
        for dc in (-1, 0, 1):
            t = get(dr, dc) * cw_ref[(dr + 1) * 3 + (dc + 1):(dr + 1) * 3 + (dc + 1) + 1, :]
            acc = t if acc is None else acc + t
    qkv = _silu(acc)
    kw = GLA_KEY_WIDTH
    lane = lax.broadcasted_iota(jnp.int32, (1, 2 * kw), 1)
    qk_out[0] = qkv[:, 0:2 * kw] * jnp.where(lane < kw, GLA_KEY_DIM ** -0.5, 1.0)
    v_out[0] = qkv[:, 2 * kw:]
    pgl = cur_ref[0][:, GLA_QKV_COLS + GLA_VAL_WIDTH:GLA_SEG]
    z = _mmp(_split(pgl, N_LORA), [g2_hi_ref[...], g2_lo_ref[...]][:N_LORA]) + gb_ref[...]
    la = _log_sigmoid(z) * (1.0 / GLA_TAU)
    la_out[0, 0] = la[:, 0:kw]
    la_out[1, 0] = la[:, kw:]


def _feat_gla(p, vertical, tm, consts):
    B, L, S = p.shape
    width = GRID_W if vertical else tm
    hb = tm // GRID_W
    nhb = L // GRID_W
    tile = pl.BlockSpec((1, tm, S), lambda b, i: (b, i, 0))
    if vertical:
        in_specs = [pl.BlockSpec((1, GRID_W, S), lambda b, i: (b, jnp.maximum(i * hb - 1, 0), 0)),
                    tile,
                    pl.BlockSpec((1, GRID_W, S), lambda b, i: (b, jnp.minimum((i + 1) * hb, nhb - 1), 0))]
        args = [p, p, p]
    else:
        in_specs = [tile]
        args = [p]
    in_specs += [_full(c.shape) for c in consts]
    kw, vw = GLA_KEY_WIDTH, GLA_VAL_WIDTH
    return pl.pallas_call(
        functools.partial(_feat_gla_body, vertical, width, tm),
        grid=(B, L // tm),
        in_specs=in_specs,
        out_specs=[pl.BlockSpec((1, tm, 2 * kw), lambda b, i: (b, i, 0)),
                   pl.BlockSpec((1, tm, vw), lambda b, i: (b, i, 0)),
                   pl.BlockSpec((2, 1, tm, kw), lambda b, i: (0, b, i, 0))],
        out_shape=[jax.ShapeDtypeStruct((B, L, 2 * kw), F32),
                   jax.ShapeDtypeStruct((B, L, vw), F32),
                   jax.ShapeDtypeStruct((2, B, L, kw), F32)],
        compiler_params=_cparams(("parallel", "arbitrary")),
        name="feat_gla",
    )(*args, *consts)


def _order_masks(n, sgn):
    ti = lax.broadcasted_iota(jnp.int32, (n, n), 0)
    si = lax.broadcasted_iota(jnp.int32, (n, n), 1)
    rel = (si - ti) * sgn
    return rel < 0, rel <= 0


def _chunk_index(nc):
    return lambda b, d, c: c + d * (nc - 1 - 2 * c)


def _rwkv_stages(d, c, r_ref, k_ref, v_ref, kk_ref, lw_ref, a_ref, ka_ref, s0_ref, y_ref, st_ref, s_scr):
    C = CHUNK

    @pl.when(c == 0)
    def _():
        s_scr[...] = s0_ref[0, 0]

    sgn = 1 - 2 * d
    r, k, v, kk = r_ref[0], k_ref[0], v_ref[0], kk_ref[0]
    lw, a = lw_ref[0, 0], a_ref[0, 0]
    before, upto = _order_masks(C, sgn)
    cum = _mm_exact_l(upto.astype(BF16), lw, N_CUM)
    tot = jnp.where(d == 0, cum[C - 1:C], cum[0:1])
    p_in = jnp.exp(cum)
    p_ex = jnp.exp(cum - lw)
    p_inv = jnp.exp(-cum)
    p_rem = jnp.exp(tot - cum)
    p_all = jnp.exp(tot)
    bvec = kk * a
    kmod = k * (1.0 + (a - 1.0) * ka_ref[...])
    a_t = -kk * p_ex
    r_t = r * p_in
    b_t = bvec * p_inv
    k_t = kmod * p_inv
    b_p = bvec * p_rem
    k_p = kmod * p_rem
    yield

    P = 2 * RW_HEAD_DIM
    lane = lax.broadcasted_iota(jnp.int32, (C, P), 1)
    h0 = lane < RW_HEAD_DIM
    ri = lax.broadcasted_iota(jnp.int32, (P, P), 0)
    ci = lax.broadcasted_iota(jnp.int32, (P, P), 1)
    same = (ri // C) == (ci // C)
    rel = ((ci % C) - (ri % C)) * sgn
    strict = same & (rel < 0)
    eye = (ri == ci).astype(F32)

    def stack2(x):
        return jnp.concatenate([jnp.where(h0, x, 0.0), jnp.where(h0, 0.0, x)], axis=0)

    pairs = range(RW_HEADS // 2)
    sls = [slice(p * P, (p + 1) * P) for p in pairs]
    a_st = [stack2(a_t[:, sl]) for sl in sls]
    v_st = [stack2(v[:, sl]) for sl in sls]
    g = [_mm(jnp.concatenate([a_st[p], stack2(r_t[:, sls[p]])], axis=0),
             jnp.concatenate([stack2(b_t[:, sls[p]]), stack2(k_t[:, sls[p]])], axis=0), N_RW, tb=True)
         for p in pairs]
    yield
    nmat = [jnp.where(strict, g[p][0:P, 0:P], 0.0) for p in pairs]
    a_ak = [jnp.where(strict, g[p][0:P, P:], 0.0) for p in pairs]
    ri2 = lax.broadcasted_iota(jnp.int32, (P, 2 * P), 0)
    ci2 = lax.broadcasted_iota(jnp.int32, (P, 2 * P), 1)
    incl2 = ((ri2 // C) == ((ci2 % P) // C)) & ((((ci2 % C) - (ri2 % C)) * sgn) <= 0)
    a_r = [jnp.where(incl2, g[p][P:, :], 0.0) for p in pairs]
    akv = [_mm(a_ak[p], v_st[p], N_RW) for p in pairs]
    yield
    t = [eye + nmat[p] for p in pairs]
    npow = nmat
    for _ in range(max(C.bit_length() - 2, 0)):
        npow = [_mm(npow[p], npow[p], N_RW) for p in pairs]
        t = [t[p] + _mm(t[p], npow[p], N_RW) for p in pairs]
        yield
    x = [_mm(t[p], jnp.concatenate([a_st[p], akv[p]], axis=1), N_RW) for p in pairs]
    s = [s_scr[p] for p in pairs]
    ur = [_mm(jnp.concatenate([x[p][0:C, 0:P] + x[p][C:, 0:P], r_t[:, sls[p]]], axis=0), s[p], N_RW, tb=True)
          for p in pairs]
    yield
    u = [ur[p][0:C] + x[p][0:C, P:] + x[p][C:, P:] for p in pairs]
    y_st = [_mm(a_r[p], jnp.concatenate([stack2(u[p]), v_st[p]], axis=0), N_RW) for p in pairs]
    for p in pairs:
        y_ref[0, 0, :, sls[p]] = ur[p][C:] + y_st[p][0:C] + y_st[p][C:]
    yield
    upd = [_mm(jnp.concatenate([u[p], v[:, sls[p]]], axis=0).T,
               jnp.concatenate([b_p[:, sls[p]], k_p[:, sls[p]]], axis=0), N_RW) for p in pairs]
    for p in pairs:
        s_new = s[p] * p_all[:, sls[p]] + jnp.where(same, upd[p], 0.0)
        s_scr[p] = s_new
        st_ref[0, 0, p] = s_new


def _scan_body(r_ref, k_ref, v_ref, kk_ref, lw_ref, a_ref, ka_ref, rs0_ref, qk_ref, gv_ref, la_ref, hm_ref, gs0_ref,
               y_ref, rst_ref, o_ref, gst_ref, rs_scr, gs_scr):
    d = pl.program_id(1)
    c = pl.program_id(2)
    chains = [_rwkv_stages(d, c, r_ref, k_ref, v_ref, kk_ref, lw_ref, a_ref, ka_ref, rs0_ref, y_ref, rst_ref, rs_scr),
              _gla_stages(d, c, qk_ref, gv_ref, la_ref, hm_ref, gs0_ref, o_ref, gst_ref, gs_scr)]
    while chains:
        chains = [ch for ch in chains if next(ch, True) is None]


def _scan(r, k, v, kk, lw, a, ka, rs0, qk, gv, la, hm, gs0):
    B, L, W = r.shape
    kw, vw = GLA_KEY_WIDTH, GLA_VAL_WIDTH
    nc = L // CHUNK
    cidx = _chunk_index(nc)
    shared = lambda w: pl.BlockSpec((1, CHUNK, w), lambda b, d, c: (b, cidx(b, d, c), 0))
    perdir = lambda w: pl.BlockSpec((1, 1, CHUNK, w), lambda b, d, c: (d, b, cidx(b, d, c), 0))
    npair = RW_HEADS // 2
    P = 2 * RW_HEAD_DIM
    rspec = pl.BlockSpec((1, 1, npair, P, P), lambda b, d, c: (b, d, 0, 0, 0))
    gspec = pl.BlockSpec((1, 1, vw, kw), lambda b, d, c: (b, d, 0, 0))
    return pl.pallas_call(
        _scan_body,
        grid=(B, 2, nc),
        in_specs=[shared(W), shared(W), shared(W), shared(W), perdir(W), perdir(W),
                  pl.BlockSpec((1, W), lambda b, d, c: (0, 0)), rspec,
                  shared(2 * kw), shared(vw), perdir(kw), pl.BlockSpec((kw, vw), lambda b, d, c: (0, 0)), gspec],
        out_specs=[perdir(W), rspec, perdir(vw), gspec],
        out_shape=[jax.ShapeDtypeStruct((2, B, L, W), F32),
                   jax.ShapeDtypeStruct((B, 2, npair, P, P), F32),
                   jax.ShapeDtypeStruct((2, B, L, vw), F32),
                   jax.ShapeDtypeStruct((B, 2, vw, kw), F32)],
        scratch_shapes=[pltpu.VMEM((npair, P, P), F32), pltpu.VMEM((vw, kw), F32)],
        compiler_params=_cparams(("parallel", "parallel", "arbitrary")),
        name="scan",
    )(r, k, v, kk, lw, a, ka, rs0, qk, gv, la, hm, gs0)


def _gla_stages(d, c, qk_ref, v_ref, g_ref, hm_ref, s0_ref, o_ref, st_ref, s_scr):
    C = CHUNK
    SB = GLA_SUB
    H = GLA_HEADS

    @pl.when(c == 0)
    def _():
        s_scr[...] = s0_ref[0, 0]

    sgn = 1 - 2 * d
    kw, vw = GLA_KEY_WIDTH, GLA_VAL_WIDTH
    q = qk_ref[0][:, 0:kw]
    k = qk_ref[0][:, kw:]
    v = v_ref[0]
    g = g_ref[0, 0]
    _, upto = _order_masks(C, sgn)
    b = _mm_exact_l(upto.astype(BF16), g, N_CUM)
    pos = lax.broadcasted_iota(jnp.int32, (C, 1), 0) * sgn + d * (C - 1)

    def b_at(p):
        return jnp.sum(jnp.where(pos == p, b, 0.0), axis=0, keepdims=True)

    NEG = -jnp.inf
    tot = b_at(C - 1)
    s = s_scr[...]
    inter = _mm(q * jnp.exp(b), s, N_GLA, tb=True)
    yield

    half = pos >= C // 2
    odd = (pos // SB) % 2 == 1
    r1 = b_at(C // 2 - 1)
    r2 = jnp.where(half, b_at(3 * SB - 1), b_at(SB - 1))
    q1 = q * jnp.exp(jnp.where(half, b - r1, NEG))
    k1 = k * jnp.exp(jnp.where(half, NEG, r1 - b))
    q2 = q * jnp.exp(jnp.where(odd, b - r2, NEG))
    k2 = k * jnp.exp(jnp.where(odd, NEG, r2 - b))
    lane_h = lax.broadcasted_iota(jnp.int32, (C, kw), 1) // GLA_KEY_DIM

    def stack_heads(x):
        return jnp.concatenate([jnp.where(lane_h == h, x, 0.0) for h in range(H)], axis=0)

    att1 = _mm(stack_heads(q1), k1, N_GLA, tb=True)
    att2 = _mm(stack_heads(q2), k2, N_GLA, tb=True)
    yield
    pr =(lax.broadcasted_iota(jnp.int32, (H * C, C), 0) % C) * sgn + d * (C - 1)
    pc = lax.broadcasted_iota(jnp.int32, (H * C, C), 1) * sgn + d * (C - 1)
    att = att1 + jnp.where((pr >= C // 2) == (pc >= C // 2), att2, 0.0)
    res = _mm(att, v, N_GLA)
    lane_hv = lax.broadcasted_iota(jnp.int32, (C, vw), 1) // GLA_VAL_DIM
    off = None
    for h in range(H):
        t = jnp.where(lane_hv == h, res[h * C:(h + 1) * C], 0.0)
        off = t if off is None else off + t
    yield

    hm = hm_ref[...]
    il = lax.broadcasted_iota(jnp.int32, (SB, 1), 0)
    diag = []
    for blk in range(C // SB):
        rs = slice(blk * SB, (blk + 1) * SB)
        bb, qb, kb, vb = b[rs], q[rs], k[rs], v[rs]
        pieces = []
        for j in range(SB):
            valid = (il - j) * sgn >= 0
            pieces.append(jnp.exp(jnp.where(valid, bb - bb[j:j + 1], NEG)) * qb * kb[j:j + 1])
        a = _mm_exact_r(jnp.concatenate(pieces, axis=0), hm, N_GLA_INTRA)
        acc = None
        for j in range(SB):
            t = a[j * SB:(j + 1) * SB] * vb[j:j + 1]
            acc = t if acc is None else acc + t
        diag.append(acc)
        yield
    o_ref[0, 0] = inter + off + jnp.concatenate(diag, axis=0)
    yield

    upd = _mm(v.T, k * jnp.exp(tot - b), N_GLA)
    ri = lax.broadcasted_iota(jnp.int32, upd.shape, 0) // GLA_VAL_DIM
    ci = lax.broadcasted_iota(jnp.int32, upd.shape, 1) // GLA_KEY_DIM
    s_new = s * jnp.exp(tot) + jnp.where(ri == ci, upd, 0.0)
    s_scr[...] = s_new
    st_ref[0, 0] = s_new


def _seg_norm(y, bd, dim, eps):
    mu = _mm_exact_r(y, bd, N_SEG) * (1.0 / dim)
    dlt = y - mu
    var = _mm_exact_r(dlt * dlt, bd, N_SEG) * (1.0 / dim)
    return dlt * lax.rsqrt(var + eps)


def _layer_norm(x, w, b):
    mu = jnp.mean(x, axis=-1, keepdims=True)
    dlt = x - mu
    var = jnp.mean(dlt * dlt, axis=-1, keepdims=True)
    return dlt * lax.rsqrt(var + LN_EPS) * w + b


def _merge_body(x_ref, mod_ref, yrw_ref, bonus_ref, g_ref, ygla_ref, og_ref,
                bd64_ref, bd128_ref, rgw_ref, rgb_ref, ggw_ref, ggb_ref, *refs):
    n = N_MERGE
    wrw = [r[...] for r in refs[0:n]]
    wgla = [r[...] for r in refs[n:2 * n]]
    wout = [r[...] for r in refs[2 * n:3 * n]]
    wgate = [r[...] for r in refs[3 * n:3 * n + N_PROJ]]
    ln_w_ref, ln_b_ref, x1_ref, h2p_ref = refs[3 * n + N_PROJ:]
    y = _seg_norm(yrw_ref[0, 0] + yrw_ref[1, 0], bd64_ref[...], RW_HEAD_DIM, RW_GN_EPS)
    y = (y * rgw_ref[...] + rgb_ref[...] + bonus_ref[0]) * g_ref[0]
    y_rw = _mmp(_split(y, n), wrw)
    y = _seg_norm(ygla_ref[0, 0] + ygla_ref[1, 0], bd128_ref[...], GLA_VAL_DIM, LN_EPS)
    y = (y * ggw_ref[...] + ggb_ref[...]) * _silu(og_ref[0])
    y_gla = _mmp(_split(y, n), wgla)
    h = x_ref[0] * (1.0 + mod_ref[0, 1:2, :]) + mod_ref[0, 0:1, :]
    gate = _sigmoid(_mmp(_split(h, N_PROJ), wgate))
    mixed = gate[:, 0:D_MODEL] * y_rw + gate[:, D_MODEL:] * y_gla
    mix = _mmp(_split(mixed, n), wout)
    g1 = mod_ref[0, 2:3, :]
    x1 = _layer_norm(ALPHA * x_ref[0] + g1 * mix, ln_w_ref[...], ln_b_ref[...])
    x1_ref[0] = x1
    h2p_ref[0] = _pack_bf16_pairs(_moe_input(x1, mod_ref))


def _merge(x, mod, yrw, bonus, g, ygla, p_gla, consts, weights, ln_w, ln_b, tm):
    B, L, D = x.shape
    W = RW_WIDTH
    tok = lambda w: pl.BlockSpec((1, tm, w), lambda b, i: (b, i, 0))
    dirs = pl.BlockSpec((2, 1, tm, W), lambda b, i: (0, b, i, 0))
    wl = [w for ws in weights for w in ws]
    return pl.pallas_call(
        _merge_body,
        grid=(B, L // tm),
        in_specs=[tok(D), pl.BlockSpec((1, 6, D), lambda b, i: (b, 0, 0)), dirs, tok(W), tok(W), dirs,
                  pl.BlockSpec((1, tm, W), lambda b, i: (b, i, GLA_QKV_COLS // W))]
                 + [_full(c.shape) for c in consts] + [_full(w.shape) for w in wl]
                 + [_full(ln_w.shape), _full(ln_b.shape)],
        out_specs=[tok(D), tok(D // 2)],
        out_shape=[jax.ShapeDtypeStruct((B, L, D), F32), jax.ShapeDtypeStruct((B, L, D // 2), jnp.int32)],
        compiler_params=_cparams(("parallel", "arbitrary")),
        name="merge",
    )(x, mod, yrw, bonus, g, ygla, p_gla, *consts, *wl, ln_w, ln_b)


def _moe_input(x1, mod_ref):
    return x1 * (1.0 + mod_ref[0, 4:5, :]) + mod_ref[0, 3:4, :]


def _router_body(x1_ref, mod_ref, *refs):
    n = N_ROUTER
    rt = [r[...] for r in refs[0:n]]
    bias_ref, e_out, g_out, cnt_out = refs[n:]
    tm = x1_ref.shape[0]
    E, G, PG = N_EXPERTS, N_GROUPS, N_EXPERTS // N_GROUPS
    h = _moe_input(x1_ref[...], mod_ref)
    logits = _mmp(rt, _split(h, n), tb=True)
    scores = _sigmoid(logits)
    sel = scores + bias_ref[:, 0:tm]
    NEG = -jnp.inf
    ip = lax.broadcasted_iota(jnp.int32, (PG, tm), 0)
    group_rows = []
    for gidx in range(G):
        sg = sel[gidx * PG:(gidx + 1) * PG]
        m1 = jnp.max(sg, axis=0, keepdims=True)
        first = jnp.min(jnp.where(sg == m1, ip, PG), axis=0, keepdims=True)
        m2 = jnp.max(jnp.where(ip == first, NEG, sg), axis=0, keepdims=True)
        group_rows.append(m1 + m2)
    gs = jnp.concatenate(group_rows, axis=0)
    gi = lax.broadcasted_iota(jnp.int32, (G, tm), 0)
    keep = jnp.zeros((G, tm), F32)
    for _ in range(TOPK_GROUPS):
        m = jnp.max(gs, axis=0, keepdims=True)
        idx = jnp.min(jnp.where(gs == m, gi, G), axis=0, keepdims=True)
        hit = gi == idx
        keep = jnp.where(hit, 1.0, keep)
        gs = jnp.where(hit, NEG, gs)
    cur = jnp.concatenate(
        [jnp.where(keep[gidx:gidx + 1] > 0.5, sel[gidx * PG:(gidx + 1) * PG], NEG) for gidx in range(G)],
        axis=0)
    ei = lax.broadcasted_iota(jnp.int32, (E, tm), 0)
    idxs, gates = [], []
    picked = jnp.zeros((E, tm), F32)
    for _ in range(TOP_K):
        m = jnp.max(cur, axis=0, keepdims=True)
        idx = jnp.min(jnp.where(cur == m, ei, E), axis=0, keepdims=True)
        hit = ei == idx
        idxs.append(idx)
        gates.append(jnp.sum(jnp.where(hit, scores, 0.0), axis=0, keepdims=True))
        cur = jnp.where(hit, NEG, cur)
        picked = jnp.where(hit, 1.0, picked)
    gate = jnp.concatenate(gates, axis=0)
    e_out[...] = jnp.concatenate(idxs, axis=0)
    g_out[...] = gate / jnp.sum(gate, axis=0, keepdims=True) * ROUTED_SCALE

    @pl.when(pl.program_id(0) == 0)
    def _():
        cnt_out[...] = jnp.zeros(cnt_out.shape, F32)

    part = picked[:, 0:128]
    for j in range(1, tm // 128):
        part = part + picked[:, j * 128:(j + 1) * 128]
    cnt_out[...] += part


def _router(x1f, mod, seq, router_t_parts, bias_b, tm):
    n, D = x1f.shape
    E = N_EXPERTS
    return pl.pallas_call(
        _router_body,
        grid=(n // tm,),
        in_specs=[pl.BlockSpec((tm, D), lambda i: (i, 0)),
                  pl.BlockSpec((1, 6, D), lambda i: (i * tm // seq, 0, 0))]
                 + [pl.BlockSpec((E, D), lambda i: (0, 0))] * len(router_t_parts)
                 + [pl.BlockSpec(bias_b.shape, lambda i: (0, 0))],
        out_specs=[pl.BlockSpec((TOP_K, tm), lambda i: (0, i))] * 2
                  + [pl.BlockSpec((E, 128), lambda i: (0, 0))],
        out_shape=[jax.ShapeDtypeStruct((TOP_K, n), jnp.int32), jax.ShapeDtypeStruct((TOP_K, n), F32),
                   jax.ShapeDtypeStruct((E, 128), F32)],
        compiler_params=_cparams(("arbitrary",)),
        name="router",
    )(x1f, mod, *router_t_parts, bias_b)


def _sc_permute_body(scatter, per_w, src_hbm, idx_hbm, out_hbm, idx_v, rows_v, sem_i, sem_o):
    wid = lax.axis_index("s") * SC_CORES + lax.axis_index("c")
    base = wid * per_w
    win = SC_WINDOW
    pltpu.sync_copy(idx_hbm.at[wid], idx_v)

    def linear(ref, j):
        return ref.at[pl.ds(base + j * win, win)]

    def indexed(ref, j):
        return ref.at[idx_v.at[j]]

    src_at, out_at = (linear, indexed) if scatter else (indexed, linear)

    @pl.loop(0, per_w // win, step=2)
    def _(j):
        i0 = pltpu.async_copy(src_at(src_hbm, j), rows_v.at[0], sem_i.at[0])
        i1 = pltpu.async_copy(src_at(src_hbm, j + 1), rows_v.at[1], sem_i.at[1])
        i0.wait()
        o0 = pltpu.async_copy(rows_v.at[0], out_at(out_hbm, j), sem_o.at[0])
        i1.wait()
        o1 = pltpu.async_copy(rows_v.at[1], out_at(out_hbm, j + 1), sem_o.at[1])
        o0.wait()
        o1.wait()


def _sc_permute(src, idx, scatter):
    nrows = idx.shape[0]
    D = src.shape[1]
    per_w = nrows // SC_WORKERS
    mesh = plsc.VectorSubcoreMesh(core_axis_name="c", subcore_axis_name="s",
                                  num_cores=SC_CORES, num_subcores=SC_SUBCORES)
    return pl.kernel(
        functools.partial(_sc_permute_body, scatter, per_w),
        out_type=jax.ShapeDtypeStruct((nrows, D), src.dtype),
        mesh=mesh,
        scratch_types=[pltpu.VMEM((per_w // SC_WINDOW, SC_WINDOW), jnp.int32),
                       pltpu.VMEM((2, SC_WINDOW, D), src.dtype),
                       pltpu.SemaphoreType.DMA((2,)), pltpu.SemaphoreType.DMA((2,))],
        name="sc_scatter" if scatter else "sc_gather",
    )(src, idx.reshape(SC_WORKERS, per_w // SC_WINDOW, SC_WINDOW))


def _experts_body(be_ref, nu_ref, x_ref, wgu_ref, wd_ref, y_ref, wgu_bf, wd_bf):
    i = pl.program_id(0)

    @pl.when(i < nu_ref[0])
    def _():
        @pl.when((i == 0) | (be_ref[i] != be_ref[jnp.maximum(i - 1, 0)]))
        def _():
            wgu_bf[...] = wgu_ref[0].astype(BF16)
            wd_bf[...] = wd_ref[0].astype(BF16)

        F = EXPERT_DIM
        halves = [slice(h * (ROW_BLOCK // 2), (h + 1) * (ROW_BLOCK // 2)) for h in range(2)]
        xb = [_unpack_bf16_pairs(x_ref[rs, :]).astype(BF16) for rs in halves]
        gu = [_dot(t, wgu_bf[...]) for t in xb]
        act = [(_silu(t[:, 0:F]) * t[:, F:]).astype(BF16) for t in gu]
        yb = [_dot(t, wd_bf[...]) for t in act]
        for rs, t in zip(halves, yb):
            y_ref[rs, :] = _pack_bf16_pairs(t)

    @pl.when(i >= nu_ref[0])
    def _():
        y_ref[...] = jnp.zeros(y_ref.shape, jnp.int32)


def _experts(block_e, nused, xs, w_gate_up, w_down):
    nb = block_e.shape[0]
    DP = xs.shape[1]
    D = 2 * DP
    R = ROW_BLOCK
    F2 = w_gate_up.shape[2]
    last = lambda i, nu: jnp.minimum(i, nu[0] - 1)
    grid_spec = pltpu.PrefetchScalarGridSpec(
        num_scalar_prefetch=2,
        grid=(nb,),
        in_specs=[pl.BlockSpec((R, DP), lambda i, be, nu: (last(i, nu), 0)),
                  pl.BlockSpec((1, D, F2), lambda i, be, nu: (be[last(i, nu)], 0, 0)),
                  pl.BlockSpec((1, F2 // 2, D), lambda i, be, nu: (be[last(i, nu)], 0, 0))],
        out_specs=pl.BlockSpec((R, DP), lambda i, be, nu: (i, 0)),
        scratch_shapes=[pltpu.VMEM((D, F2), BF16), pltpu.VMEM((F2 // 2, D), BF16)],
    )
    return pl.pallas_call(
        _experts_body,
        grid_spec=grid_spec,
        out_shape=jax.ShapeDtypeStruct((nb * R, DP), jnp.int32),
        compiler_params=_cparams(("arbitrary",)),
        name="experts",
    )(block_e, nused, xs, w_gate_up, w_down)


def _final_body(yg_ref, x1_ref, mod_ref, gate_ref, *refs):
    n = N_SHARED
    sgu = [r[...] for r in refs[0:n]]
    sd = [r[...] for r in refs[n:2 * n]]
    ln_w_ref, ln_b_ref, o_ref = refs[2 * n:]
    tm = x1_ref.shape[1]
    gate = gate_ref[0]
    routed = None
    for kk in range(TOP_K):
        t = _unpack_bf16_pairs(yg_ref[kk * tm:(kk + 1) * tm, :]) * gate[:, kk:kk + 1]
        routed = t if routed is None else routed + t
    h2 = _moe_input(x1_ref[0], mod_ref)
    F = sgu[0].shape[1] // 2
    gu = _mmp(_split(h2, n), sgu)
    act = _silu(gu[:, 0:F]) * gu[:, F:]
    shared = _mmp(_split(act, n), sd)
    g2 = mod_ref[0, 5:6, :]
    o_ref[0] = _layer_norm(ALPHA * x1_ref[0] + g2 * (routed + shared), ln_w_ref[...], ln_b_ref[...])


def _final(yg, x1, mod, gate, sgu, sd, ln_w, ln_b, tm):
    B, L, D = x1.shape
    ni = L // tm
    rows = tm * TOP_K
    tok = lambda w: pl.BlockSpec((1, tm, w), lambda b, i: (b, i, 0))
    ws = list(sgu) + list(sd)
    return pl.pallas_call(
        _final_body,
        grid=(B, ni),
        in_specs=[pl.BlockSpec((rows, D // 2), lambda b, i: (b * ni + i, 0)),
                  tok(D), pl.BlockSpec((1, 6, D), lambda b, i: (b, 0, 0)), tok(TOP_K)]
                 + [_full(w.shape) for w in ws] + [_full(ln_w.shape), _full(ln_b.shape)],
        out_specs=tok(D),
        out_shape=jax.ShapeDtypeStruct((B, L, D), F32),
        compiler_params=_cparams(("parallel", "arbitrary")),
        name="final",
    )(yg, x1, mod, gate, *ws, ln_w, ln_b)


def _dispatch_plan(experts_t, sizes, tm):
    R, E = ROW_BLOCK, N_EXPERTS
    K, n = experts_t.shape
    nk = K * n
    rows = nk + E * (R - 1)
    rows = -(-rows // math.lcm(R, SC_ROW_ALIGN)) * math.lcm(R, SC_ROW_ALIGN)
    nb = rows // R
    ncand = nb * R - nk
    padded = (sizes + R - 1) // R * R
    pad_end = jnp.cumsum(padded)
    tok = jnp.broadcast_to(jnp.arange(n, dtype=jnp.int32)[None, :], (K, n))
    slot = jnp.arange(K, dtype=jnp.int32)[:, None]
    aid = (tok // tm) * (tm * K) + slot * tm + tok % tm
    cand = jnp.arange(ncand, dtype=jnp.int32)
    ce, cp = cand // (R - 1), cand % (R - 1)
    need = jnp.broadcast_to((padded - sizes)[:, None], (E, R - 1)).reshape(-1)
    need = jnp.pad(need, (0, ncand - E * (R - 1)))
    ckey = jnp.where(cp < need, 2 * ce + 1, 2 * E + 1)
    id_bits = (nb * R - 1).bit_length()
    assert (2 * E + 2) << id_bits < 2 ** 31
    keys = jnp.concatenate([2 * experts_t.reshape(-1), ckey]) << id_bits
    ids = jnp.concatenate([aid.reshape(-1), nk + cand])
    row_id = lax.sort(keys | ids) & ((1 << id_bits) - 1)
    row_tok = jnp.where(row_id < nk, (row_id // (tm * K)) * tm + row_id % tm, (row_id - nk) % n)
    block_e = jnp.minimum(jnp.searchsorted(pad_end, jnp.arange(nb) * R, side='right'), E - 1).astype(jnp.int32)
    nused = (pad_end[-1] // R).astype(jnp.int32).reshape(1)
    return row_tok, row_id, block_e, nused


def _block_diag_ones(n, blk):
    i = jnp.arange(n) // blk
    return (i[:, None] == i[None, :]).astype(BF16)


def kernel(x, c, ctx, c_ctx, w_ada, b_ada, w_in, rw_mu, rw_w0, rw_w2, rw_a0, rw_a2, rw_g2, rw_k_k, rw_k_a,
           rw_r_k, rw_gn_w, rw_gn_b, gla_conv, gla_g2, gla_gb, gla_gn_w, gla_gn_b, w_br_rw, w_br_gla, w_out,
           ln1_w, ln1_b, router, router_bias, w_gate_up, w_down, sh_gate_up, sh_down, ln2_w, ln2_b):
    B, L, D = x.shape
    CT = ctx.shape[1]
    l = 0
    W = RW_WIDTH
    row = lambda t: t.reshape(1, -1)

    rows = -(-(B + 1) // 8) * 8
    cc = jnp.zeros((rows, D), F32).at[:B].set(c).at[B].set(c_ctx)
    mod = _ada(cc, w_ada[l], b_ada[l])
    mod_lat = mod[:B].reshape(B, 6, D)
    mod_ctx = jnp.broadcast_to(mod[B].reshape(1, 6, D), (B, 6, D))

    w = w_in[l]
    g0 = RW_COLS
    w_rw = jnp.pad(w[:, :RW_COLS], ((0, 0), (0, RW_SEG - RW_COLS)))
    w_gla = jnp.concatenate([w[:, g0:g0 + GLA_QKV_COLS],
                             w[:, g0 + GLA_QKV_COLS + GLA_GATE_RANK:g0 + GLA_COLS],
                             w[:, g0 + GLA_QKV_COLS:g0 + GLA_QKV_COLS + GLA_GATE_RANK],
                             jnp.zeros((D, GLA_SEG - GLA_COLS), F32)], axis=1)
    w_gate = w[:, MIX_COLS:]
    tm_p = min(512, L)
    p_rw = _proj(x, mod_lat, w_rw, tm_p)
    p_gla = _proj(x, mod_lat, w_gla, tm_p)
    pc_rw = _proj(ctx, mod_ctx, w_rw, CT)
    pc_gla = _proj(ctx, mod_ctx, w_gla, CT)

    mu = jnp.pad(rw_mu[l], (0, RW_SEG - RW_COLS)).reshape(1, -1)
    lora = jnp.zeros((256, 5 * W), F32)
    e1 = RW_DECAY_RANK
    e2 = e1 + RW_AAA_RANK
    e3 = e2 + RW_GATE_RANK
    for d in range(2):
        lora = lora.at[0:e1, d * W:(d + 1) * W].set(rw_w2[l, d])
        lora = lora.at[e1:e2, (2 + d) * W:(3 + d) * W].set(rw_a2[l, d])
    lora = lora.at[e2:e3, 4 * W:].set(rw_g2[l])
    lora_p = (_split(lora, N_LORA) + [jnp.zeros_like(lora, BF16)])[:2]
    bd64 = _block_diag_ones(W, RW_HEAD_DIM)
    rw_consts = [mu, lora_p[0], lora_p[1], rw_w0[l].reshape(1, -1), rw_a0[l].reshape(1, -1),
                 row(rw_k_k[l]), row(rw_k_a[l]), row(rw_r_k[l]), bd64]
    tm_f = min(256, L)
    r, k, v, kkn, g, bonus, lwd, a = _feat_rw(p_rw, True, tm_f, rw_consts)
    rc, kc, vc, kknc, _, _, lwdc, ac = _feat_rw(pc_rw, False, CT, rw_consts)

    g2 = jnp.zeros((GLA_SEG - GLA_QKV_COLS - GLA_VAL_WIDTH, 2 * GLA_KEY_WIDTH), F32)
    g2 = g2.at[:GLA_GATE_RANK].set(jnp.concatenate([gla_g2[l, 0], gla_g2[l, 1]], axis=1))
    g2_p = (_split(g2, N_LORA) + [jnp.zeros_like(g2, BF16)])[:2]
    cw = jnp.pad(gla_conv[l].reshape(9, GLA_QKV_COLS), ((0, 7), (0, 0)))
    gla_consts = [cw, g2_p[0], g2_p[1], gla_gb[l].reshape(1, -1)]
    qk, vg, la = _feat_gla(p_gla, True, tm_f, gla_consts)
    qkc, vgc, lac = _feat_gla(pc_gla, False, CT, gla_consts)

    ka = row(rw_k_a[l])
    P = 2 * RW_HEAD_DIM
    s0 = jnp.zeros((B, 2, RW_HEADS // 2, P, P), F32)
    hi = jnp.arange(GLA_KEY_WIDTH) // GLA_KEY_DIM
    hj = jnp.arange(GLA_VAL_WIDTH) // GLA_VAL_DIM
    hm = (hi[:, None] == hj[None, :]).astype(BF16)
    g0s = jnp.zeros((B, 2, GLA_VAL_WIDTH, GLA_KEY_WIDTH), F32)
    _, s_ctx, _, gs_ctx = _scan(rc, kc, vc, kknc, lwdc, ac, ka, s0, qkc, vgc, lac, hm, g0s)
    y_rw, _, y_gla, _ = _scan(r, k, v, kkn, lwd, a, ka, s_ctx, qk, vg, la, hm, gs_ctx)

    bd128 = _block_diag_ones(GLA_VAL_WIDTH, GLA_VAL_DIM)
    m_consts = [bd64, bd128, row(rw_gn_w[l]), row(rw_gn_b[l]), row(gla_gn_w[l]), row(gla_gn_b[l])]
    m_weights = [_split(w_br_rw[l], N_MERGE), _split(w_br_gla[l], N_MERGE), _split(w_out[l], N_MERGE),
                 _split(w_gate, N_PROJ)]
    x1, h2p = _merge(x, mod_lat, y_rw, bonus, g, y_gla, p_gla, m_consts, m_weights,
                     row(ln1_w[l]), row(ln1_b[l]), min(256, L))

    n = B * L
    tm_r = min(256, L)
    bias_b = jnp.broadcast_to(router_bias[l].reshape(-1, 1), (N_EXPERTS, tm_r))
    e_t, g_t, cnt = _router(x1.reshape(n, D), mod_lat, L, _split(router[l].T, N_ROUTER), bias_b, tm_r)
    sizes = jnp.sum(cnt, axis=1).astype(jnp.int32)
    tm_c = min(128, L)
    row_tok, row_id, block_e, nused = _dispatch_plan(e_t, sizes, tm_c)
    xs = _sc_permute(h2p.reshape(n, D // 2), row_tok, scatter=False)
    y = _experts(block_e, nused, xs, w_gate_up[l], w_down[l])
    yg = _sc_permute(y, row_id, scatter=True)
    gate_tok = g_t.T.reshape(B, L, TOP_K)
    return _final(yg, x1, mod_lat, gate_tok, _split(sh_gate_up[l], N_SHARED),
                  _split(sh_down[l], N_SHARED), row(ln2_w[l]), row(ln2_b[l]), tm_c)
```

```python
import functools
import math

import jax
import jax.numpy as jnp
from jax import lax
from jax.experimental import pallas as pl
from jax.experimental.pallas import tpu as pltpu
from jax.experimental.pallas import tpu_sc as plsc

F32 = jnp.float32
BF16 = jnp.bfloat16

D_MODEL = 1024
GRID_W = 64
RW_WIDTH = 512
RW_HEADS = 8
RW_HEAD_DIM = 64
RW_DECAY_RANK = 32
RW_AAA_RANK = 32
RW_GATE_RANK = 96
RW_GN_EPS = 64e-5
RW_COLS = 1696
RW_SEG = 1792
GLA_HEADS = 4
GLA_KEY_WIDTH = 256
GLA_VAL_WIDTH = 512
GLA_KEY_DIM = 64
GLA_VAL_DIM = 128
GLA_GATE_RANK = 16
GLA_TAU = 16.0
GLA_QKV_COLS = 1024
GLA_COLS = 1552
GLA_SEG = 1664
MIX_COLS = RW_COLS + GLA_COLS
N_EXPERTS = 256
TOP_K = 8
N_GROUPS = 8
TOPK_GROUPS = 4
EXPERT_DIM = 256
ROUTED_SCALE = 2.5
LN_EPS = 1e-5
DEPTH = 1
ALPHA = (2 * DEPTH) ** 0.25

CHUNK = 64
GLA_SUB = 16
ROW_BLOCK = 512
VMEM_LIMIT = 48 * 1024 * 1024
SC_CORES = 2
SC_SUBCORES = 16
SC_WORKERS = SC_CORES * SC_SUBCORES
SC_WINDOW = 32
SC_ROW_ALIGN = 2 * SC_WORKERS * SC_WINDOW

N_ADA = 3
N_PROJ = 1
N_LORA = 1
N_SEG = 2
N_CUM = 2
N_RW = 1
N_GLA = 1
N_GLA_INTRA = 1
N_MERGE = 1
N_ROUTER = 3
N_SHARED = 1


def _split(x, n):
    parts = []
    r = x
    for i in range(n):
        p = r.astype(BF16)
        parts.append(p)
        if i < n - 1:
            r = r - p.astype(F32)
    return parts


def _dot(a, b, ta=False, tb=False):
    dn = (((0 if ta else 1,), (1 if tb else 0,)), ((), ()))
    return lax.dot_general(a, b, dn, preferred_element_type=F32)


def _mmp(ap, bp, ta=False, tb=False):
    n = max(len(ap), len(bp))
    out = None
    for i in range(len(ap)):
        for j in range(len(bp)):
            if i + j <= n - 1:
                t = _dot(ap[i], bp[j], ta, tb)
                out = t if out is None else out + t
    return out


def _mm(a, b, n, ta=False, tb=False):
    return _mmp(_split(a, n), _split(b, n), ta, tb)


def _mm_exact_l(m_bf16, x, n):
    return _mmp([m_bf16], _split(x, n))


def _mm_exact_r(x, m_bf16, n):
    return _mmp(_split(x, n), [m_bf16])


def _pack_bf16_pairs(x):
    w = x.shape[1] // 2
    hi = lax.bitcast_convert_type(x[:, :w].astype(BF16).astype(F32), jnp.int32)
    lo = lax.bitcast_convert_type(x[:, w:].astype(BF16).astype(F32), jnp.int32)
    return hi | lax.shift_right_logical(lo, 16)


def _unpack_bf16_pairs(p):
    hi = lax.bitcast_convert_type(p & jnp.int32(-65536), F32)
    lo = lax.bitcast_convert_type(lax.shift_left(p, 16), F32)
    return jnp.concatenate([hi, lo], axis=1)


def _sigmoid(x):
    return 1.0 / (1.0 + jnp.exp(-x))


def _silu(x):
    return x * _sigmoid(x)


def _log_sigmoid(x):
    return jnp.minimum(x, 0.0) - jnp.log(1.0 + jnp.exp(-jnp.abs(x)))


def _cparams(sem):
    return pltpu.CompilerParams(dimension_semantics=sem, vmem_limit_bytes=VMEM_LIMIT)


def _full(shape):
    nd = len(shape)
    return pl.BlockSpec(shape, lambda *a: (0,) * nd)


def _ada_body(c_ref, w_ref, b_ref, o_ref):
    s = _silu(c_ref[...])
    o_ref[...] = _mm(s, w_ref[...], N_ADA) + b_ref[...]


def _ada(cc, w, b):
    rows, d = cc.shape
    n = w.shape[1] // d
    return pl.pallas_call(
        _ada_body,
        grid=(n,),
        in_specs=[pl.BlockSpec((rows, d), lambda j: (0, 0)),
                  pl.BlockSpec((d, d), lambda j: (0, j)),
                  pl.BlockSpec((1, d), lambda j: (0, j))],
        out_specs=pl.BlockSpec((rows, d), lambda j: (0, j)),
        out_shape=jax.ShapeDtypeStruct((rows, w.shape[1]), F32),
        compiler_params=_cparams(("arbitrary",)),
        name="ada",
    )(cc, w, b.reshape(1, -1))


def _proj_body(x_ref, mod_ref, *refs):
    w_refs, o_ref = refs[:-1], refs[-1]
    sh = mod_ref[0, 0:1, :]
    sc = mod_ref[0, 1:2, :]
    h = x_ref[0] * (1.0 + sc) + sh
    o_ref[0] = _mmp(_split(h, N_PROJ), [w[...] for w in w_refs])


def _proj(x, mod, w, tm):
    B, L, D = x.shape
    n = w.shape[1]
    wp = _split(w, N_PROJ)
    return pl.pallas_call(
        _proj_body,
        grid=(B, L // tm),
        in_specs=[pl.BlockSpec((1, tm, D), lambda b, i: (b, i, 0)),
                  pl.BlockSpec((1, 6, D), lambda b, i: (b, 0, 0))]
                 + [pl.BlockSpec((D, n), lambda b, i: (0, 0))] * N_PROJ,
        out_specs=pl.BlockSpec((1, tm, n), lambda b, i: (b, i, 0)),
        out_shape=jax.ShapeDtypeStruct((B, L, n), F32),
        compiler_params=_cparams(("parallel", "arbitrary")),
        name="proj",
    )(x, mod, *wp)


def _neighbours(vertical, width, tm, cur, prev_ref, next_ref):
    if vertical:
        i = pl.program_id(1)
        n = pl.num_programs(1)
        prev = jnp.where(i > 0, prev_ref[0], 0.0)
        nxt = jnp.where(i < n - 1, next_ref[0], 0.0)
        pad = jnp.zeros((8, cur.shape[1]), F32)
        ext = jnp.concatenate([pad, prev, cur, nxt, pad], axis=0)
        off = GRID_W + 8
    else:
        pad = jnp.zeros((8, cur.shape[1]), F32)
        ext = jnp.concatenate([pad, cur, pad], axis=0)
        off = 8
    col = lax.broadcasted_iota(jnp.int32, (tm, 1), 0) % width

    def get(dr, dc):
        s = off + GRID_W * dr + dc
        v = ext[s:s + tm]
        if dc == -1:
            v = jnp.where(col == 0, 0.0, v)
        elif dc == 1:
            v = jnp.where(col == width - 1, 0.0, v)
        return v

    return get


def _feat_rw_body(vertical, width, tm, *refs):
    if vertical:
        prev_ref, cur_ref, next_ref = refs[:3]
        refs = refs[3:]
    else:
        cur_ref = refs[0]
        prev_ref = next_ref = None
        refs = refs[1:]
    (mu_ref, lw_hi_ref, lw_lo_ref, w0_ref, a0_ref, kk_ref, ka_ref, rk_ref, bd_ref,
     r_out, k_out, v_out, kkn_out, g_out, bonus_out, lwd_out, a_out) = refs
    cur = cur_ref[0]
    get = _neighbours(vertical, width, tm, cur, prev_ref, next_ref)
    left, right = get(0, -1), get(0, 1)
    up, down = (get(-1, 0), get(1, 0)) if vertical else (left, right)
    l4 = lax.broadcasted_iota(jnp.int32, (1, cur.shape[1]), 1) % 4
    shifted = jnp.where(l4 == 0, left, jnp.where(l4 == 1, right, jnp.where(l4 == 2, up, down)))
    p = cur + mu_ref[...] * (shifted - cur)

    r = p[:, 0:RW_WIDTH]
    k = p[:, RW_WIDTH:2 * RW_WIDTH]
    v = p[:, 2 * RW_WIDTH:3 * RW_WIDTH]
    slab = p[:, 3 * RW_WIDTH:3 * RW_WIDTH + 256]
    ln = lax.broadcasted_iota(jnp.int32, (1, 256), 1)
    e1 = RW_DECAY_RANK
    e2 = e1 + RW_AAA_RANK
    e3 = e2 + RW_GATE_RANK
    slab = jnp.where(ln < e1, jnp.tanh(slab),
                     jnp.where(ln < e2, slab, jnp.where(ln < e3, _sigmoid(slab), 0.0)))
    lo = _mmp(_split(slab, N_LORA), [lw_hi_ref[...], lw_lo_ref[...]][:N_LORA])
    W = RW_WIDTH
    a_sum = None
    for d in range(2):
        z = w0_ref[:, d * W:(d + 1) * W] + lo[:, d * W:(d + 1) * W]
        lwd_out[d, 0] = -_sigmoid(z) * math.exp(-0.5)
        a = _sigmoid(a0_ref[:, d * W:(d + 1) * W] + lo[:, (2 + d) * W:(3 + d) * W])
        a_out[d, 0] = a
        a_sum = a if a_sum is None else a_sum + a
    g_out[0] = lo[:, 4 * W:5 * W]
    bd = bd_ref[...]
    kk = k * kk_ref[...]
    ss = _mm_exact_r(kk * kk, bd, N_SEG)
    kkn_out[0] = kk / jnp.maximum(jnp.sqrt(ss), 1e-12)
    kmod_sum = k * (2.0 + (a_sum - 2.0) * ka_ref[...])
    bonus_out[0] = _mm_exact_r(r * kmod_sum * rk_ref[...], bd, N_SEG) * v
    r_out[0] = r
    k_out[0] = k
    v_out[0] = v


def _feat_rw(p, vertical, tm, consts):
    B, L, S = p.shape
    width = GRID_W if vertical else tm
    W = RW_WIDTH
    hb = tm // GRID_W
    nhb = L // GRID_W
    tile = pl.BlockSpec((1, tm, S), lambda b, i: (b, i, 0))
    if vertical:
        in_specs = [pl.BlockSpec((1, GRID_W, S), lambda b, i: (b, jnp.maximum(i * hb - 1, 0), 0)),
                    tile,
                    pl.BlockSpec((1, GRID_W, S), lambda b, i: (b, jnp.minimum((i + 1) * hb, nhb - 1), 0))]
        args = [p, p, p]
    else:
        in_specs = [tile]
        args = [p]
    in_specs += [_full(c.shape) for c in consts]
    o1 = pl.BlockSpec((1, tm, W), lambda b, i: (b, i, 0))
    o2 = pl.BlockSpec((2, 1, tm, W), lambda b, i: (0, b, i, 0))
    s1 = jax.ShapeDtypeStruct((B, L, W), F32)
    s2 = jax.ShapeDtypeStruct((2, B, L, W), F32)
    return pl.pallas_call(
        functools.partial(_feat_rw_body, vertical, width, tm),
        grid=(B, L // tm),
        in_specs=in_specs,
        out_specs=[o1] * 6 + [o2] * 2,
        out_shape=[s1] * 6 + [s2] * 2,
        compiler_params=_cparams(("parallel", "arbitrary")),
        name="feat_rw",
    )(*args, *consts)


def _feat_gla_body(vertical, width, tm, *refs):
    if vertical:
        prev_ref, cur_ref, next_ref = refs[:3]
        refs = refs[3:]
    else:
        cur_ref = refs[0]
        prev_ref = next_ref = None
        refs = refs[1:]
    cw_ref, g2_hi_ref, g2_lo_ref, gb_ref, qk_out, v_out, la_out = refs
    cur = cur_ref[0][:, 0:GLA_QKV_COLS]
    if vertical:
        class _Slice:
            def __init__(self, ref):
                self.ref = ref

            def __getitem__(self, idx):
                return self.ref[idx][:, 0:GLA_QKV_COLS]
        get = _neighbours(True, width, tm, cur, _Slice(prev_ref), _Slice(next_ref))
    else:
        get = _neighbours(False, width, tm, cur, None, None)
    acc = None
    for dr in ((-1, 0, 1) if vertical else (0,)):
        for dc in (-1, 0, 1):
            t = get(dr, dc) * cw_ref[(dr + 1) * 3 + (dc + 1):(dr + 1) * 3 + (dc + 1) + 1, :]
            acc = t if acc is None else acc + t
    qkv = _silu(acc)
    kw = GLA_KEY_WIDTH
    lane = lax.broadcasted_iota(jnp.int32, (1, 2 * kw), 1)
    qk_out[0] = qkv[:, 0:2 * kw] * jnp.where(lane < kw, GLA_KEY_DIM ** -0.5, 1.0)
    v_out[0] = qkv[:, 2 * kw:]
    pgl = cur_ref[0][:, GLA_QKV_COLS + GLA_VAL_WIDTH:GLA_SEG]
    z = _mmp(_split(pgl, N_LORA), [g2_hi_ref[...], g2_lo_ref[...]][:N_LORA]) + gb_ref[...]
    la = _log_sigmoid(z) * (1.0 / GLA_TAU)
    la_out[0, 0] = la[:, 0:kw]
    la_out[1, 0] = la[:, kw:]


def _feat_gla(p, vertical, tm, consts):
    B, L, S = p.shape
    width = GRID_W if vertical else tm
    hb = tm // GRID_W
    nhb = L // GRID_W
    tile = pl.BlockSpec((1, tm, S), lambda b, i: (b, i, 0))
    if vertical:
        in_specs = [pl.BlockSpec((1, GRID_W, S), lambda b, i: (b, jnp.maximum(i * hb - 1, 0), 0)),
                    tile,
                    pl.BlockSpec((1, GRID_W, S), lambda b, i: (b, jnp.minimum((i + 1) * hb, nhb - 1), 0))]
        args = [p, p, p]
    else:
        in_specs = [tile]
        args = [p]
    in_specs += [_full(c.shape) for c in consts]
    kw, vw = GLA_KEY_WIDTH, GLA_VAL_WIDTH
    return pl.pallas_call(
        functools.partial(_feat_gla_body, vertical, width, tm),
        grid=(B, L // tm),
        in_specs=in_specs,
        out_specs=[pl.BlockSpec((1, tm, 2 * kw), lambda b, i: (b, i, 0)),
                   pl.BlockSpec((1, tm, vw), lambda b, i: (b, i, 0)),
                   pl.BlockSpec((2, 1, tm, kw), lambda b, i: (0, b, i, 0))],
        out_shape=[jax.ShapeDtypeStruct((B, L, 2 * kw), F32),
                   jax.ShapeDtypeStruct((B, L, vw), F32),
                   jax.ShapeDtypeStruct((2, B, L, kw), F32)],
        compiler_params=_cparams(("parallel", "arbitrary")),
        name="feat_gla",
    )(*args, *consts)


def _order_masks(n, sgn):
    ti = lax.broadcasted_iota(jnp.int32, (n, n), 0)
    si = lax.broadcasted_iota(jnp.int32, (n, n), 1)
    rel = (si - ti) * sgn
    return rel < 0, rel <= 0


def _chunk_index(nc):
    return lambda b, d, c: c + d * (nc - 1 - 2 * c)


def _rwkv_stages(d, c, r_ref, k_ref, v_ref, kk_ref, lw_ref, a_ref, ka_ref, s0_ref, y_ref, st_ref, s_scr):
    C = CHUNK

    @pl.when(c == 0)
    def _():
        s_scr[...] = s0_ref[0, 0]

    sgn = 1 - 2 * d
    r, k, v, kk = r_ref[0], k_ref[0], v_ref[0], kk_ref[0]
    lw, a = lw_ref[0, 0], a_ref[0, 0]
    before, upto = _order_masks(C, sgn)
    cum = _mm_exact_l(upto.astype(BF16), lw, N_CUM)
    tot = jnp.where(d == 0, cum[C - 1:C], cum[0:1])
    p_in = jnp.exp(cum)
    p_ex = jnp.exp(cum - lw)
    p_inv = jnp.exp(-cum)
    p_rem = jnp.exp(tot - cum)
    p_all = jnp.exp(tot)
    bvec = kk * a
    kmod = k * (1.0 + (a - 1.0) * ka_ref[...])
    a_t = -kk * p_ex
    r_t = r * p_in
    b_t = bvec * p_inv
    k_t = kmod * p_inv
    b_p = bvec * p_rem
    k_p = kmod * p_rem
    yield

    P = 2 * RW_HEAD_DIM
    lane = lax.broadcasted_iota(jnp.int32, (C, P), 1)
    h0 = lane < RW_HEAD_DIM
    ri = lax.broadcasted_iota(jnp.int32, (P, P), 0)
    ci = lax.broadcasted_iota(jnp.int32, (P, P), 1)
    same = (ri // C) == (ci // C)
    rel = ((ci % C) - (ri % C)) * sgn
    strict = same & (rel < 0)
    eye = (ri == ci).astype(F32)

    def stack2(x):
        return jnp.concatenate([jnp.where(h0, x, 0.0), jnp.where(h0, 0.0, x)], axis=0)

    pairs = range(RW_HEADS // 2)
    sls = [slice(p * P, (p + 1) * P) for p in pairs]
    a_st = [stack2(a_t[:, sl]) for sl in sls]
    v_st = [stack2(v[:, sl]) for sl in sls]
    g = [_mm(jnp.concatenate([a_st[p], stack2(r_t[:, sls[p]])], axis=0),
             jnp.concatenate([stack2(b_t[:, sls[p]]), stack2(k_t[:, sls[p]])], axis=0), N_RW, tb=True)
         for p in pairs]
    yield
    nmat = [jnp.where(strict, g[p][0:P, 0:P], 0.0) for p in pairs]
    a_ak = [jnp.where(strict, g[p][0:P, P:], 0.0) for p in pairs]
    ri2 = lax.broadcasted_iota(jnp.int32, (P, 2 * P), 0)
    ci2 = lax.broadcasted_iota(jnp.int32, (P, 2 * P), 1)
    incl2 = ((ri2 // C) == ((ci2 % P) // C)) & ((((ci2 % C) - (ri2 % C)) * sgn) <= 0)
    a_r = [jnp.where(incl2, g[p][P:, :], 0.0) for p in pairs]
    akv = [_mm(a_ak[p], v_st[p], N_RW) for p in pairs]
    yield
    t = [eye + nmat[p] for p in pairs]
    npow = nmat
    for _ in range(max(C.bit_length() - 2, 0)):
        npow = [_mm(npow[p], npow[p], N_RW) for p in pairs]
        t = [t[p] + _mm(t[p], npow[p], N_RW) for p in pairs]
        yield
    x = [_mm(t[p], jnp.concatenate([a_st[p], akv[p]], axis=1), N_RW) for p in pairs]
    s = [s_scr[p] for p in pairs]
    ur = [_mm(jnp.concatenate([x[p][0:C, 0:P] + x[p][C:, 0:P], r_t[:, sls[p]]], axis=0), s[p], N_RW, tb=True)
          for p in pairs]
    yield
    u = [ur[p][0:C] + x[p][0:C, P:] + x[p][C:, P:] for p in pairs]
    y_st = [_mm(a_r[p], jnp.concatenate([stack2(u[p]), v_st[p]], axis=0), N_RW) for p in pairs]
    for p in pairs:
        y_ref[0, 0, :, sls[p]] = ur[p][C:] + y_st[p][0:C] + y_st[p][C:]
    yield
    upd = [_mm(jnp.concatenate([u[p], v[:, sls[p]]], axis=0).T,
               jnp.concatenate([b_p[:, sls[p]], k_p[:, sls[p]]], axis=0), N_RW) for p in pairs]
    for p in pairs:
        s_new = s[p] * p_all[:, sls[p]] + jnp.where(same, upd[p], 0.0)
        s_scr[p] = s_new
        st_ref[0, 0, p] = s_new


def _scan_body(r_ref, k_ref, v_ref, kk_ref, lw_ref, a_ref, ka_ref, rs0_ref, qk_ref, gv_ref, la_ref, hm_ref, gs0_ref,
               y_ref, rst_ref, o_ref, gst_ref, rs_scr, gs_scr):
    d = pl.program_id(1)
    c = pl.program_id(2)
    chains = [_rwkv_stages(d, c, r_ref, k_ref, v_ref, kk_ref, lw_ref, a_ref, ka_ref, rs0_ref, y_ref, rst_ref, rs_scr),
              _gla_stages(d, c, qk_ref, gv_ref, la_ref, hm_ref, gs0_ref, o_ref, gst_ref, gs_scr)]
    while chains:
        chains = [ch for ch in chains if next(ch, True) is None]


def _scan(r, k, v, kk, lw, a, ka, rs0, qk, gv, la, hm, gs0):
    B, L, W = r.shape
    kw, vw = GLA_KEY_WIDTH, GLA_VAL_WIDTH
    nc = L // CHUNK
    cidx = _chunk_index(nc)
    shared = lambda w: pl.BlockSpec((1, CHUNK, w), lambda b, d, c: (b, cidx(b, d, c), 0))
    perdir = lambda w: pl.BlockSpec((1, 1, CHUNK, w), lambda b, d, c: (d, b, cidx(b, d, c), 0))
    npair = RW_HEADS // 2
    P = 2 * RW_HEAD_DIM
    rspec = pl.BlockSpec((1, 1, npair, P, P), lambda b, d, c: (b, d, 0, 0, 0))
    gspec = pl.BlockSpec((1, 1, vw, kw), lambda b, d, c: (b, d, 0, 0))
    return pl.pallas_call(
        _scan_body,
        grid=(B, 2, nc),
        in_specs=[shared(W), shared(W), shared(W), shared(W), perdir(W), perdir(W),
                  pl.BlockSpec((1, W), lambda b, d, c: (0, 0)), rspec,
                  shared(2 * kw), shared(vw), perdir(kw), pl.BlockSpec((kw, vw), lambda b, d, c: (0, 0)), gspec],
        out_specs=[perdir(W), rspec, perdir(vw), gspec],
        out_shape=[jax.ShapeDtypeStruct((2, B, L, W), F32),
                   jax.ShapeDtypeStruct((B, 2, npair, P, P), F32),
                   jax.ShapeDtypeStruct((2, B, L, vw), F32),
                   jax.ShapeDtypeStruct((B, 2, vw, kw), F32)],
        scratch_shapes=[pltpu.VMEM((npair, P, P), F32), pltpu.VMEM((vw, kw), F32)],
        compiler_params=_cparams(("parallel", "parallel", "arbitrary")),
        name="scan",
    )(r, k, v, kk, lw, a, ka, rs0, qk, gv, la, hm, gs0)


def _gla_stages(d, c, qk_ref, v_ref, g_ref, hm_ref, s0_ref, o_ref, st_ref, s_scr):
    C = CHUNK
    SB = GLA_SUB
    H = GLA_HEADS

    @pl.when(c == 0)
    def _():
        s_scr[...] = s0_ref[0, 0]

    sgn = 1 - 2 * d
    kw, vw = GLA_KEY_WIDTH, GLA_VAL_WIDTH
    q = qk_ref[0][:, 0:kw]
    k = qk_ref[0][:, kw:]
    v = v_ref[0]
    g = g_ref[0, 0]
    _, upto = _order_masks(C, sgn)
    b = _mm_exact_l(upto.astype(BF16), g, N_CUM)
    pos = lax.broadcasted_iota(jnp.int32, (C, 1), 0) * sgn + d * (C - 1)

    def b_at(p):
        return jnp.sum(jnp.where(pos == p, b, 0.0), axis=0, keepdims=True)

    NEG = -jnp.inf
    tot = b_at(C - 1)
    s = s_scr[...]
    inter = _mm(q * jnp.exp(b), s, N_GLA, tb=True)
    yield

    half = pos >= C // 2
    odd = (pos // SB) % 2 == 1
    r1 = b_at(C // 2 - 1)
    r2 = jnp.where(half, b_at(3 * SB - 1), b_at(SB - 1))
    q1 = q * jnp.exp(jnp.where(half, b - r1, NEG))
    k1 = k * jnp.exp(jnp.where(half, NEG, r1 - b))
    q2 = q * jnp.exp(jnp.where(odd, b - r2, NEG))
    k2 = k * jnp.exp(jnp.where(odd, NEG, r2 - b))
    lane_h = lax.broadcasted_iota(jnp.int32, (C, kw), 1) // GLA_KEY_DIM

    def stack_heads(x):
        return jnp.concatenate([jnp.where(lane_h == h, x, 0.0) for h in range(H)], axis=0)

    att1 = _mm(stack_heads(q1), k1, N_GLA, tb=True)
    att2 = _mm(stack_heads(q2), k2, N_GLA, tb=True)
    yield
    pr =(lax.broadcasted_iota(jnp.int32, (H * C, C), 0) % C) * sgn + d * (C - 1)
    pc = lax.broadcasted_iota(jnp.int32, (H * C, C), 1) * sgn + d * (C - 1)
    att = att1 + jnp.where((pr >= C // 2) == (pc >= C // 2), att2, 0.0)
    res = _mm(att, v, N_GLA)
    lane_hv = lax.broadcasted_iota(jnp.int32, (C, vw), 1) // GLA_VAL_DIM
    off = None
    for h in range(H):
        t = jnp.where(lane_hv == h, res[h * C:(h + 1) * C], 0.0)
        off = t if off is None else off + t
    yield

    hm = hm_ref[...]
    il = lax.broadcasted_iota(jnp.int32, (SB, 1), 0)
    diag = []
    for blk in range(C // SB):
        rs = slice(blk * SB, (blk + 1) * SB)
        bb, qb, kb, vb = b[rs], q[rs], k[rs], v[rs]
        pieces = []
        for j in range(SB):
            valid = (il - j) * sgn >= 0
            pieces.append(jnp.exp(jnp.where(valid, bb - bb[j:j + 1], NEG)) * qb * kb[j:j + 1])
        a = _mm_exact_r(jnp.concatenate(pieces, axis=0), hm, N_GLA_INTRA)
        acc = None
        for j in range(SB):
            t = a[j * SB:(j + 1) * SB] * vb[j:j + 1]
            acc = t if acc is None else acc + t
        diag.append(acc)
        yield
    o_ref[0, 0] = inter + off + jnp.concatenate(diag, axis=0)
    yield

    upd = _mm(v.T, k * jnp.exp(tot - b), N_GLA)
    ri = lax.broadcasted_iota(jnp.int32, upd.shape, 0) // GLA_VAL_DIM
    ci = lax.broadcasted_iota(jnp.int32, upd.shape, 1) // GLA_KEY_DIM
    s_new = s * jnp.exp(tot) + jnp.where(ri == ci, upd, 0.0)
    s_scr[...] = s_new
    st_ref[0, 0] = s_new


def _seg_norm(y, bd, dim, eps):
    mu = _mm_exact_r(y, bd, N_SEG) * (1.0 / dim)
    dlt = y - mu
    var = _mm_exact_r(dlt * dlt, bd, N_SEG) * (1.0 / dim)
    return dlt * lax.rsqrt(var + eps)


def _layer_norm(x, w, b):
    mu = jnp.mean(x, axis=-1, keepdims=True)
    dlt = x - mu
    var = jnp.mean(dlt * dlt, axis=-1, keepdims=True)
    return dlt * lax.rsqrt(var + LN_EPS) * w + b


def _merge_body(x_ref, mod_ref, yrw_ref, bonus_ref, g_ref, ygla_ref, og_ref,
                bd64_ref, bd128_ref, rgw_ref, rgb_ref, ggw_ref, ggb_ref, *refs):
    n = N_MERGE
    wrw = [r[...] for r in refs[0:n]]
    wgla = [r[...] for r in refs[n:2 * n]]
    wout = [r[...] for r in refs[2 * n:3 * n]]
    wgate = [r[...] for r in refs[3 * n:3 * n + N_PROJ]]
    ln_w_ref, ln_b_ref, x1_ref, h2p_ref = refs[3 * n + N_PROJ:]
    y = _seg_norm(yrw_ref[0, 0] + yrw_ref[1, 0], bd64_ref[...], RW_HEAD_DIM, RW_GN_EPS)
    y = (y * rgw_ref[...] + rgb_ref[...] + bonus_ref[0]) * g_ref[0]
    y_rw = _mmp(_split(y, n), wrw)
    y = _seg_norm(ygla_ref[0, 0] + ygla_ref[1, 0], bd128_ref[...], GLA_VAL_DIM, LN_EPS)
    y = (y * ggw_ref[...] + ggb_ref[...]) * _silu(og_ref[0])
    y_gla = _mmp(_split(y, n), wgla)
    h = x_ref[0] * (1.0 + mod_ref[0, 1:2, :]) + mod_ref[0, 0:1, :]
    gate = _sigmoid(_mmp(_split(h, N_PROJ), wgate))
    mixed = gate[:, 0:D_MODEL] * y_rw + gate[:, D_MODEL:] * y_gla
    mix = _mmp(_split(mixed, n), wout)
    g1 = mod_ref[0, 2:3, :]
    x1 = _layer_norm(ALPHA * x_ref[0] + g1 * mix, ln_w_ref[...], ln_b_ref[...])
    x1_ref[0] = x1
    h2p_ref[0] = _pack_bf16_pairs(_moe_input(x1, mod_ref))


def _merge(x, mod, yrw, bonus, g, ygla, p_gla, consts, weights, ln_w, ln_b, tm):
    B, L, D = x.shape
    W = RW_WIDTH
    tok = lambda w: pl.BlockSpec((1, tm, w), lambda b, i: (b, i, 0))
    dirs = pl.BlockSpec((2, 1, tm, W), lambda b, i: (0, b, i, 0))
    wl = [w for ws in weights for w in ws]
    return pl.pallas_call(
        _merge_body,
        grid=(B, L // tm),
        in_specs=[tok(D), pl.BlockSpec((1, 6, D), lambda b, i: (b, 0, 0)), dirs, tok(W), tok(W), dirs,
                  pl.BlockSpec((1, tm, W), lambda b, i: (b, i, GLA_QKV_COLS // W))]
                 + [_full(c.shape) for c in consts] + [_full(w.shape) for w in wl]
                 + [_full(ln_w.shape), _full(ln_b.shape)],
        out_specs=[tok(D), tok(D // 2)],
        out_shape=[jax.ShapeDtypeStruct((B, L, D), F32), jax.ShapeDtypeStruct((B, L, D // 2), jnp.int32)],
        compiler_params=_cparams(("parallel", "arbitrary")),
        name="merge",
    )(x, mod, yrw, bonus, g, ygla, p_gla, *consts, *wl, ln_w, ln_b)


def _moe_input(x1, mod_ref):
    return x1 * (1.0 + mod_ref[0, 4:5, :]) + mod_ref[0, 3:4, :]


def _router_body(x1_ref, mod_ref, *refs):
    n = N_ROUTER
    rt = [r[...] for r in refs[0:n]]
    bias_ref, e_out, g_out, cnt_out = refs[n:]
    tm = x1_ref.shape[0]
    E, G, PG = N_EXPERTS, N_GROUPS, N_EXPERTS // N_GROUPS
    h = _moe_input(x1_ref[...], mod_ref)
    logits = _mmp(rt, _split(h, n), tb=True)
    scores = _sigmoid(logits)
    sel = scores + bias_ref[:, 0:tm]
    NEG = -jnp.inf
    ip = lax.broadcasted_iota(jnp.int32, (PG, tm), 0)
    group_rows = []
    for gidx in range(G):
        sg = sel[gidx * PG:(gidx + 1) * PG]
        m1 = jnp.max(sg, axis=0, keepdims=True)
        first = jnp.min(jnp.where(sg == m1, ip, PG), axis=0, keepdims=True)
        m2 = jnp.max(jnp.where(ip == first, NEG, sg), axis=0, keepdims=True)
        group_rows.append(m1 + m2)
    gs = jnp.concatenate(group_rows, axis=0)
    gi = lax.broadcasted_iota(jnp.int32, (G, tm), 0)
    keep = jnp.zeros((G, tm), F32)
    for _ in range(TOPK_GROUPS):
        m = jnp.max(gs, axis=0, keepdims=True)
        idx = jnp.min(jnp.where(gs == m, gi, G), axis=0, keepdims=True)
        hit = gi == idx
        keep = jnp.where(hit, 1.0, keep)
        gs = jnp.where(hit, NEG, gs)
    cur = jnp.concatenate(
        [jnp.where(keep[gidx:gidx + 1] > 0.5, sel[gidx * PG:(gidx + 1) * PG], NEG) for gidx in range(G)],
        axis=0)
    ei = lax.broadcasted_iota(jnp.int32, (E, tm), 0)
    idxs, gates = [], []
    picked = jnp.zeros((E, tm), F32)
    for _ in range(TOP_K):
        m = jnp.max(cur, axis=0, keepdims=True)
        idx = jnp.min(jnp.where(cur == m, ei, E), axis=0, keepdims=True)
        hit = ei == idx
        idxs.append(idx)
        gates.append(jnp.sum(jnp.where(hit, scores, 0.0), axis=0, keepdims=True))
        cur = jnp.where(hit, NEG, cur)
        picked = jnp.where(hit, 1.0, picked)
    gate = jnp.concatenate(gates, axis=0)
    e_out[...] = jnp.concatenate(idxs, axis=0)
    g_out[...] = gate / jnp.sum(gate, axis=0, keepdims=True) * ROUTED_SCALE

    @pl.when(pl.program_id(0) == 0)
    def _():
        cnt_out[...] = jnp.zeros(cnt_out.shape, F32)

    part = picked[:, 0:128]
    for j in range(1, tm // 128):
        part = part + picked[:, j * 128:(j + 1) * 128]
    cnt_out[...] += part


def _router(x1f, mod, seq, router_t_parts, bias_b, tm):
    n, D = x1f.shape
    E = N_EXPERTS
    return pl.pallas_call(
        _router_body,
        grid=(n // tm,),
        in_specs=[pl.BlockSpec((tm, D), lambda i: (i, 0)),
                  pl.BlockSpec((1, 6, D), lambda i: (i * tm // seq, 0, 0))]
                 + [pl.BlockSpec((E, D), lambda i: (0, 0))] * len(router_t_parts)
                 + [pl.BlockSpec(bias_b.shape, lambda i: (0, 0))],
        out_specs=[pl.BlockSpec((TOP_K, tm), lambda i: (0, i))] * 2
                  + [pl.BlockSpec((E, 128), lambda i: (0, 0))],
        out_shape=[jax.ShapeDtypeStruct((TOP_K, n), jnp.int32), jax.ShapeDtypeStruct((TOP_K, n), F32),
                   jax.ShapeDtypeStruct((E, 128), F32)],
        compiler_params=_cparams(("arbitrary",)),
        name="router",
    )(x1f, mod, *router_t_parts, bias_b)


def _sc_permute_body(scatter, per_w, src_hbm, idx_hbm, out_hbm, idx_v, rows_v, sem_i, sem_o):
    wid = lax.axis_index("s") * SC_CORES + lax.axis_index("c")
    base = wid * per_w
    win = SC_WINDOW
    pltpu.sync_copy(idx_hbm.at[wid], idx_v)

    def linear(ref, j):
        return ref.at[pl.ds(base + j * win, win)]

    def indexed(ref, j):
        return ref.at[idx_v.at[j]]

    src_at, out_at = (linear, indexed) if scatter else (indexed, linear)

    @pl.loop(0, per_w // win, step=2)
    def _(j):
        i0 = pltpu.async_copy(src_at(src_hbm, j), rows_v.at[0], sem_i.at[0])
        i1 = pltpu.async_copy(src_at(src_hbm, j + 1), rows_v.at[1], sem_i.at[1])
        i0.wait()
        o0 = pltpu.async_copy(rows_v.at[0], out_at(out_hbm, j), sem_o.at[0])
        i1.wait()
        o1 = pltpu.async_copy(rows_v.at[1], out_at(out_hbm, j + 1), sem_o.at[1])
        o0.wait()
        o1.wait()


def _sc_permute(src, idx, scatter):
    nrows = idx.shape[0]
    D = src.shape[1]
    per_w = nrows // SC_WORKERS
    mesh = plsc.VectorSubcoreMesh(core_axis_name="c", subcore_axis_name="s",
                                  num_cores=SC_CORES, num_subcores=SC_SUBCORES)
    return pl.kernel(
        functools.partial(_sc_permute_body, scatter, per_w),
        out_type=jax.ShapeDtypeStruct((nrows, D), src.dtype),
        mesh=mesh,
        scratch_types=[pltpu.VMEM((per_w // SC_WINDOW, SC_WINDOW), jnp.int32),
                       pltpu.VMEM((2, SC_WINDOW, D), src.dtype),
                       pltpu.SemaphoreType.DMA((2,)), pltpu.SemaphoreType.DMA((2,))],
        name="sc_scatter" if scatter else "sc_gather",
    )(src, idx.reshape(SC_WORKERS, per_w // SC_WINDOW, SC_WINDOW))


def _experts_body(ib_ref, ie_ref, lo_ref, hi_ref, x_ref, wgu_ref, wd_ref, y_ref, wgu_bf, wd_bf):
    i = pl.program_id(0)
    prev = jnp.maximum(i - 1, 0)
    R = ROW_BLOCK

    @pl.when(hi_ref[i] > lo_ref[i])
    def _():
        @pl.when((i == 0) | (ie_ref[i] != ie_ref[prev]))
        def _():
            wgu_bf[...] = wgu_ref[0].astype(BF16)
            wd_bf[...] = wd_ref[0].astype(BF16)

        F = EXPERT_DIM
        halves = [slice(h * (R // 2), (h + 1) * (R // 2)) for h in range(2)]
        xb = [_unpack_bf16_pairs(x_ref[rs, :]).astype(BF16) for rs in halves]
        gu = [_dot(t, wgu_bf[...]) for t in xb]
        act = [(_silu(t[:, 0:F]) * t[:, F:]).astype(BF16) for t in gu]
        yb = [_pack_bf16_pairs(_dot(t, wd_bf[...])) for t in act]
        first = (i == 0) | (ib_ref[i] != ib_ref[prev])
        for h, (rs, t) in enumerate(zip(halves, yb)):
            g = ib_ref[i] * R + h * (R // 2) + lax.broadcasted_iota(jnp.int32, (R // 2, 1), 0)
            mine = (g >= lo_ref[i]) & (g < hi_ref[i])
            keep = jnp.where(first, 0, y_ref[rs, :])
            y_ref[rs, :] = jnp.where(mine, t, keep)


def _experts(item_blk, item_e, row_lo, row_hi, xs, w_gate_up, w_down):
    nrows, DP = xs.shape
    D = 2 * DP
    R = ROW_BLOCK
    F2 = w_gate_up.shape[2]
    grid_spec = pltpu.PrefetchScalarGridSpec(
        num_scalar_prefetch=4,
        grid=(item_blk.shape[0],),
        in_specs=[pl.BlockSpec((R, DP), lambda i, ib, ie, lo, hi: (ib[i], 0)),
                  pl.BlockSpec((1, D, F2), lambda i, ib, ie, lo, hi: (ie[i], 0, 0)),
                  pl.BlockSpec((1, F2 // 2, D), lambda i, ib, ie, lo, hi: (ie[i], 0, 0))],
        out_specs=pl.BlockSpec((R, DP), lambda i, ib, ie, lo, hi: (ib[i], 0)),
        scratch_shapes=[pltpu.VMEM((D, F2), BF16), pltpu.VMEM((F2 // 2, D), BF16)],
    )
    return pl.pallas_call(
        _experts_body,
        grid_spec=grid_spec,
        out_shape=jax.ShapeDtypeStruct((nrows, DP), jnp.int32),
        compiler_params=_cparams(("arbitrary",)),
        name="experts",
    )(item_blk, item_e, row_lo, row_hi, xs, w_gate_up, w_down)


def _final_body(yg_ref, x1_ref, mod_ref, gate_ref, *refs):
    n = N_SHARED
    sgu = [r[...] for r in refs[0:n]]
    sd = [r[...] for r in refs[n:2 * n]]
    ln_w_ref, ln_b_ref, o_ref = refs[2 * n:]
    tm = x1_ref.shape[1]
    gate = gate_ref[0]
    routed = None
    for kk in range(TOP_K):
        t = _unpack_bf16_pairs(yg_ref[kk * tm:(kk + 1) * tm, :]) * gate[:, kk:kk + 1]
        routed = t if routed is None else routed + t
    h2 = _moe_input(x1_ref[0], mod_ref)
    F = sgu[0].shape[1] // 2
    gu = _mmp(_split(h2, n), sgu)
    act = _silu(gu[:, 0:F]) * gu[:, F:]
    shared = _mmp(_split(act, n), sd)
    g2 = mod_ref[0, 5:6, :]
    o_ref[0] = _layer_norm(ALPHA * x1_ref[0] + g2 * (routed + shared), ln_w_ref[...], ln_b_ref[...])


def _final(yg, x1, mod, gate, sgu, sd, ln_w, ln_b, tm):
    B, L, D = x1.shape
    ni = L // tm
    rows = tm * TOP_K
    tok = lambda w: pl.BlockSpec((1, tm, w), lambda b, i: (b, i, 0))
    ws = list(sgu) + list(sd)
    return pl.pallas_call(
        _final_body,
        grid=(B, ni),
        in_specs=[pl.BlockSpec((rows, D // 2), lambda b, i: (b * ni + i, 0)),
                  tok(D), pl.BlockSpec((1, 6, D), lambda b, i: (b, 0, 0)), tok(TOP_K)]
                 + [_full(w.shape) for w in ws] + [_full(ln_w.shape), _full(ln_b.shape)],
        out_specs=tok(D),
        out_shape=jax.ShapeDtypeStruct((B, L, D), F32),
        compiler_params=_cparams(("parallel", "arbitrary")),
        name="final",
    )(yg, x1, mod, gate, *ws, ln_w, ln_b)


def _dispatch_plan(experts_t, sizes, tm):
    R, E = ROW_BLOCK, N_EXPERTS
    K, n = experts_t.shape
    nk = K * n
    assert nk % R == 0 and nk % SC_ROW_ALIGN == 0
    tok = jnp.broadcast_to(jnp.arange(n, dtype=jnp.int32)[None, :], (K, n))
    slot = jnp.arange(K, dtype=jnp.int32)[:, None]
    aid = (tok // tm) * (tm * K) + slot * tm + tok % tm
    id_bits = (nk - 1).bit_length()
    assert E << id_bits < 2 ** 31
    row_id = lax.sort((experts_t << id_bits | aid).reshape(-1)) & ((1 << id_bits) - 1)
    row_tok = (row_id // (tm * K)) * tm + row_id % tm

    start = jnp.cumsum(sizes) - sizes
    first_blk = start // R
    tiles = jnp.where(sizes > 0, (start + sizes - 1) // R - first_blk + 1, 0)
    t_end = jnp.cumsum(tiles)
    nitems = nk // R + E - 1
    it = jnp.arange(nitems, dtype=jnp.int32)
    live = it < t_end[-1]
    itc = jnp.minimum(it, t_end[-1] - 1)
    item_e = jnp.minimum(jnp.searchsorted(t_end, itc, side='right'), E - 1).astype(jnp.int32)
    item_blk = (first_blk[item_e] + itc - (t_end - tiles)[item_e]).astype(jnp.int32)
    row_lo = jnp.where(live, start[item_e], 0).astype(jnp.int32)
    row_hi = jnp.where(live, (start + sizes)[item_e], 0).astype(jnp.int32)
    return row_tok, row_id, item_blk, item_e, row_lo, row_hi


def _block_diag_ones(n, blk):
    i = jnp.arange(n) // blk
    return (i[:, None] == i[None, :]).astype(BF16)


def kernel(x, c, ctx, c_ctx, w_ada, b_ada, w_in, rw_mu, rw_w0, rw_w2, rw_a0, rw_a2, rw_g2, rw_k_k, rw_k_a,
           rw_r_k, rw_gn_w, rw_gn_b, gla_conv, gla_g2, gla_gb, gla_gn_w, gla_gn_b, w_br_rw, w_br_gla, w_out,
           ln1_w, ln1_b, router, router_bias, w_gate_up, w_down, sh_gate_up, sh_down, ln2_w, ln2_b):
    B, L, D = x.shape
    CT = ctx.shape[1]
    l = 0
    W = RW_WIDTH
    row = lambda t: t.reshape(1, -1)

    rows = -(-(B + 1) // 8) * 8
    cc = jnp.zeros((rows, D), F32).at[:B].set(c).at[B].set(c_ctx)
    mod = _ada(cc, w_ada[l], b_ada[l])
    mod_lat = mod[:B].reshape(B, 6, D)
    mod_ctx = jnp.broadcast_to(mod[B].reshape(1, 6, D), (B, 6, D))

    w = w_in[l]
    g0 = RW_COLS
    w_rw = jnp.pad(w[:, :RW_COLS], ((0, 0), (0, RW_SEG - RW_COLS)))
    w_gla = jnp.concatenate([w[:, g0:g0 + GLA_QKV_COLS],
                             w[:, g0 + GLA_QKV_COLS + GLA_GATE_RANK:g0 + GLA_COLS],
                             w[:, g0 + GLA_QKV_COLS:g0 + GLA_QKV_COLS + GLA_GATE_RANK],
                             jnp.zeros((D, GLA_SEG - GLA_COLS), F32)], axis=1)
    w_gate = w[:, MIX_COLS:]
    tm_p = min(512, L)
    p_rw = _proj(x, mod_lat, w_rw, tm_p)
    p_gla = _proj(x, mod_lat, w_gla, tm_p)
    pc_rw = _proj(ctx, mod_ctx, w_rw, CT)
    pc_gla = _proj(ctx, mod_ctx, w_gla, CT)

    mu = jnp.pad(rw_mu[l], (0, RW_SEG - RW_COLS)).reshape(1, -1)
    lora = jnp.zeros((256, 5 * W), F32)
    e1 = RW_DECAY_RANK
    e2 = e1 + RW_AAA_RANK
    e3 = e2 + RW_GATE_RANK
    for d in range(2):
        lora = lora.at[0:e1, d * W:(d + 1) * W].set(rw_w2[l, d])
        lora = lora.at[e1:e2, (2 + d) * W:(3 + d) * W].set(rw_a2[l, d])
    lora = lora.at[e2:e3, 4 * W:].set(rw_g2[l])
    lora_p = (_split(lora, N_LORA) + [jnp.zeros_like(lora, BF16)])[:2]
    bd64 = _block_diag_ones(W, RW_HEAD_DIM)
    rw_consts = [mu, lora_p[0], lora_p[1], rw_w0[l].reshape(1, -1), rw_a0[l].reshape(1, -1),
                 row(rw_k_k[l]), row(rw_k_a[l]), row(rw_r_k[l]), bd64]
    tm_f = min(256, L)
    r, k, v, kkn, g, bonus, lwd, a = _feat_rw(p_rw, True, tm_f, rw_consts)
    rc, kc, vc, kknc, _, _, lwdc, ac = _feat_rw(pc_rw, False, CT, rw_consts)

    g2 = jnp.zeros((GLA_SEG - GLA_QKV_COLS - GLA_VAL_WIDTH, 2 * GLA_KEY_WIDTH), F32)
    g2 = g2.at[:GLA_GATE_RANK].set(jnp.concatenate([gla_g2[l, 0], gla_g2[l, 1]], axis=1))
    g2_p = (_split(g2, N_LORA) + [jnp.zeros_like(g2, BF16)])[:2]
    cw = jnp.pad(gla_conv[l].reshape(9, GLA_QKV_COLS), ((0, 7), (0, 0)))
    gla_consts = [cw, g2_p[0], g2_p[1], gla_gb[l].reshape(1, -1)]
    qk, vg, la = _feat_gla(p_gla, True, tm_f, gla_consts)
    qkc, vgc, lac = _feat_gla(pc_gla, False, CT, gla_consts)

    ka = row(rw_k_a[l])
    P = 2 * RW_HEAD_DIM
    s0 = jnp.zeros((B, 2, RW_HEADS // 2, P, P), F32)
    hi = jnp.arange(GLA_KEY_WIDTH) // GLA_KEY_DIM
    hj = jnp.arange(GLA_VAL_WIDTH) // GLA_VAL_DIM
    hm = (hi[:, None] == hj[None, :]).astype(BF16)
    g0s = jnp.zeros((B, 2, GLA_VAL_WIDTH, GLA_KEY_WIDTH), F32)
    _, s_ctx, _, gs_ctx = _scan(rc, kc, vc, kknc, lwdc, ac, ka, s0, qkc, vgc, lac, hm, g0s)
    y_rw, _, y_gla, _ = _scan(r, k, v, kkn, lwd, a, ka, s_ctx, qk, vg, la, hm, gs_ctx)

    bd128 = _block_diag_ones(GLA_VAL_WIDTH, GLA_VAL_DIM)
    m_consts = [bd64, bd128, row(rw_gn_w[l]), row(rw_gn_b[l]), row(gla_gn_w[l]), row(gla_gn_b[l])]
    m_weights = [_split(w_br_rw[l], N_MERGE), _split(w_br_gla[l], N_MERGE), _split(w_out[l], N_MERGE),
                 _split(w_gate, N_PROJ)]
    x1, h2p = _merge(x, mod_lat, y_rw, bonus, g, y_gla, p_gla, m_consts, m_weights,
                     row(ln1_w[l]), row(ln1_b[l]), min(256, L))

    n = B * L
    tm_r = min(256, L)
    bias_b = jnp.broadcast_to(router_bias[l].reshape(-1, 1), (N_EXPERTS, tm_r))
    e_t, g_t, cnt = _router(x1.reshape(n, D), mod_lat, L, _split(router[l].T, N_ROUTER), bias_b, tm_r)
    sizes = jnp.sum(cnt, axis=1).astype(jnp.int32)
    tm_c = min(128, L)
    row_tok, row_id, item_blk, item_e, row_lo, row_hi = _dispatch_plan(e_t, sizes, tm_c)
    xs = _sc_permute(h2p.reshape(n, D // 2), row_tok, scatter=False)
    y = _experts(item_blk, item_e, row_lo, row_hi, xs, w_gate_up[l], w_down[l])
    yg = _sc_permute(y, row_id, scatter=True)
    gate_tok = g_t.T.reshape(B, L, TOP_K)
    return _final(yg, x1, mod_lat, gate_tok, _split(sh_gate_up[l], N_SHARED),
                  _split(sh_down[l], N_SHARED), row(ln2_w[l]), row(ln2_b[l]), tm_c)
```

```python
import functools
import math

import jax
import jax.numpy as jnp
from jax import lax
from jax.experimental import pallas as pl
from jax.experimental.pallas import tpu as pltpu
from jax.experimental.pallas import tpu_sc as plsc

F32 = jnp.float32
BF16 = jnp.bfloat16

D_MODEL = 1024
GRID_W = 64
RW_WIDTH = 512
RW_HEADS = 8
RW_HEAD_DIM = 64
RW_DECAY_RANK = 32
RW_AAA_RANK = 32
RW_GATE_RANK = 96
RW_GN_EPS = 64e-5
RW_COLS = 1696
RW_SEG = 1792
GLA_HEADS = 4
GLA_KEY_WIDTH = 256
GLA_VAL_WIDTH = 512
GLA_KEY_DIM = 64
GLA_VAL_DIM = 128
GLA_GATE_RANK = 16
GLA_TAU = 16.0
GLA_QKV_COLS = 1024
GLA_COLS = 1552
GLA_SEG = 1664
MIX_COLS = RW_COLS + GLA_COLS
N_EXPERTS = 256
TOP_K = 8
N_GROUPS = 8
TOPK_GROUPS = 4
EXPERT_DIM = 256
ROUTED_SCALE = 2.5
LN_EPS = 1e-5
DEPTH = 1
ALPHA = (2 * DEPTH) ** 0.25

CHUNK = 64
GLA_SUB = 16
ROW_BLOCK = 512
VMEM_LIMIT = 48 * 1024 * 1024
SC_CORES = 2
SC_SUBCORES = 16
SC_WORKERS = SC_CORES * SC_SUBCORES
SC_WINDOW = 32
SC_ROW_ALIGN = 2 * SC_WORKERS * SC_WINDOW

N_ADA = 3
N_PROJ = 1
N_LORA = 1
N_SEG = 2
N_CUM = 2
N_RW = 1
N_GLA = 1
N_GLA_INTRA = 1
N_MERGE = 1
N_ROUTER = 3
N_SHARED = 1


def _split(x, n):
    parts = []
    r = x
    for i in range(n):
        p = r.astype(BF16)
        parts.append(p)
        if i < n - 1:
            r = r - p.astype(F32)
    return parts


def _dot(a, b, ta=False, tb=False):
    dn = (((0 if ta else 1,), (1 if tb else 0,)), ((), ()))
    return lax.dot_general(a, b, dn, preferred_element_type=F32)


def _mmp(ap, bp, ta=False, tb=False):
    n = max(len(ap), len(bp))
    out = None
    for i in range(len(ap)):
        for j in range(len(bp)):
            if i + j <= n - 1:
                t = _dot(ap[i], bp[j], ta, tb)
                out = t if out is None else out + t
    return out


def _mm(a, b, n, ta=False, tb=False):
    return _mmp(_split(a, n), _split(b, n), ta, tb)


def _mm_exact_l(m_bf16, x, n):
    return _mmp([m_bf16], _split(x, n))


def _mm_exact_r(x, m_bf16, n):
    return _mmp(_split(x, n), [m_bf16])


def _pack_bf16_pairs(x):
    w = x.shape[1] // 2
    hi = lax.bitcast_convert_type(x[:, :w].astype(BF16).astype(F32), jnp.int32)
    lo = lax.bitcast_convert_type(x[:, w:].astype(BF16).astype(F32), jnp.int32)
    return hi | lax.shift_right_logical(lo, 16)


def _unpack_bf16_pairs(p):
    hi = lax.bitcast_convert_type(p & jnp.int32(-65536), F32)
    lo = lax.bitcast_convert_type(lax.shift_left(p, 16), F32)
    return jnp.concatenate([hi, lo], axis=1)


def _sigmoid(x):
    return 1.0 / (1.0 + jnp.exp(-x))


def _silu(x):
    return x * _sigmoid(x)


def _log_sigmoid(x):
    return jnp.minimum(x, 0.0) - jnp.log(1.0 + jnp.exp(-jnp.abs(x)))


def _cparams(sem):
    return pltpu.CompilerParams(dimension_semantics=sem, vmem_limit_bytes=VMEM_LIMIT)


def _full(shape):
    nd = len(shape)
    return pl.BlockSpec(shape, lambda *a: (0,) * nd)


def _ada_body(c_ref, w_ref, b_ref, o_ref):
    s = _silu(c_ref[...])
    o_ref[...] = _mm(s, w_ref[...], N_ADA) + b_ref[...]


def _ada(cc, w, b):
    rows, d = cc.shape
    n = w.shape[1] // d
    return pl.pallas_call(
        _ada_body,
        grid=(n,),
        in_specs=[pl.BlockSpec((rows, d), lambda j: (0, 0)),
                  pl.BlockSpec((d, d), lambda j: (0, j)),
                  pl.BlockSpec((1, d), lambda j: (0, j))],
        out_specs=pl.BlockSpec((rows, d), lambda j: (0, j)),
        out_shape=jax.ShapeDtypeStruct((rows, w.shape[1]), F32),
        compiler_params=_cparams(("arbitrary",)),
        name="ada",
    )(cc, w, b.reshape(1, -1))


def _proj_body(x_ref, mod_ref, *refs):
    w_refs, o_ref = refs[:-1], refs[-1]
    sh = mod_ref[0, 0:1, :]
    sc = mod_ref[0, 1:2, :]
    h = x_ref[0] * (1.0 + sc) + sh
    o_ref[0] = _mmp(_split(h, N_PROJ), [w[...] for w in w_refs])


def _proj(x, mod, w, tm):
    B, L, D = x.shape
    n = w.shape[1]
    wp = _split(w, N_PROJ)
    return pl.pallas_call(
        _proj_body,
        grid=(B, L // tm),
        in_specs=[pl.BlockSpec((1, tm, D), lambda b, i: (b, i, 0)),
                  pl.BlockSpec((1, 6, D), lambda b, i: (b, 0, 0))]
                 + [pl.BlockSpec((D, n), lambda b, i: (0, 0))] * N_PROJ,
        out_specs=pl.BlockSpec((1, tm, n), lambda b, i: (b, i, 0)),
        out_shape=jax.ShapeDtypeStruct((B, L, n), F32),
        compiler_params=_cparams(("parallel", "arbitrary")),
        name="proj",
    )(x, mod, *wp)


def _neighbours(vertical, width, tm, cur, prev_ref, next_ref):
    if vertical:
        i = pl.program_id(1)
        n = pl.num_programs(1)
        prev = jnp.where(i > 0, prev_ref[0], 0.0)
        nxt = jnp.where(i < n - 1, next_ref[0], 0.0)
        pad = jnp.zeros((8, cur.shape[1]), F32)
        ext = jnp.concatenate([pad, prev, cur, nxt, pad], axis=0)
        off = GRID_W + 8
    else:
        pad = jnp.zeros((8, cur.shape[1]), F32)
        ext = jnp.concatenate([pad, cur, pad], axis=0)
        off = 8
    col = lax.broadcasted_iota(jnp.int32, (tm, 1), 0) % width

    def get(dr, dc):
        s = off + GRID_W * dr + dc
        v = ext[s:s + tm]
        if dc == -1:
            v = jnp.where(col == 0, 0.0, v)
        elif dc == 1:
            v = jnp.where(col == width - 1, 0.0, v)
        return v

    return get


def _feat_rw_body(vertical, width, tm, *refs):
    if vertical:
        prev_ref, cur_ref, next_ref = refs[:3]
        refs = refs[3:]
    else:
        cur_ref = refs[0]
        prev_ref = next_ref = None
        refs = refs[1:]
    (mu_ref, lw_hi_ref, lw_lo_ref, w0_ref, a0_ref, kk_ref, ka_ref, rk_ref, bd_ref,
     r_out, k_out, v_out, kkn_out, g_out, bonus_out, lwd_out, a_out) = refs
    cur = cur_ref[0]
    get = _neighbours(vertical, width, tm, cur, prev_ref, next_ref)
    left, right = get(0, -1), get(0, 1)
    up, down = (get(-1, 0), get(1, 0)) if vertical else (left, right)
    l4 = lax.broadcasted_iota(jnp.int32, (1, cur.shape[1]), 1) % 4
    shifted = jnp.where(l4 == 0, left, jnp.where(l4 == 1, right, jnp.where(l4 == 2, up, down)))
    p = cur + mu_ref[...] * (shifted - cur)

    r = p[:, 0:RW_WIDTH]
    k = p[:, RW_WIDTH:2 * RW_WIDTH]
    v = p[:, 2 * RW_WIDTH:3 * RW_WIDTH]
    slab = p[:, 3 * RW_WIDTH:3 * RW_WIDTH + 256]
    ln = lax.broadcasted_iota(jnp.int32, (1, 256), 1)
    e1 = RW_DECAY_RANK
    e2 = e1 + RW_AAA_RANK
    e3 = e2 + RW_GATE_RANK
    slab = jnp.where(ln < e1, jnp.tanh(slab),
                     jnp.where(ln < e2, slab, jnp.where(ln < e3, _sigmoid(slab), 0.0)))
    lo = _mmp(_split(slab, N_LORA), [lw_hi_ref[...], lw_lo_ref[...]][:N_LORA])
    W = RW_WIDTH
    a_sum = None
    for d in range(2):
        z = w0_ref[:, d * W:(d + 1) * W] + lo[:, d * W:(d + 1) * W]
        lwd_out[d, 0] = -_sigmoid(z) * math.exp(-0.5)
        a = _sigmoid(a0_ref[:, d * W:(d + 1) * W] + lo[:, (2 + d) * W:(3 + d) * W])
        a_out[d, 0] = a
        a_sum = a if a_sum is None else a_sum + a
    g_out[0] = lo[:, 4 * W:5 * W]
    bd = bd_ref[...]
    kk = k * kk_ref[...]
    ss = _mm_exact_r(kk * kk, bd, N_SEG)
    kkn_out[0] = kk / jnp.maximum(jnp.sqrt(ss), 1e-12)
    kmod_sum = k * (2.0 + (a_sum - 2.0) * ka_ref[...])
    bonus_out[0] = _mm_exact_r(r * kmod_sum * rk_ref[...], bd, N_SEG) * v
    r_out[0] = r
    k_out[0] = k
    v_out[0] = v


def _feat_rw(p, vertical, tm, consts):
    B, L, S = p.shape
    width = GRID_W if vertical else tm
    W = RW_WIDTH
    hb = tm // GRID_W
    nhb = L // GRID_W
    tile = pl.BlockSpec((1, tm, S), lambda b, i: (b, i, 0))
    if vertical:
        in_specs = [pl.BlockSpec((1, GRID_W, S), lambda b, i: (b, jnp.maximum(i * hb - 1, 0), 0)),
                    tile,
                    pl.BlockSpec((1, GRID_W, S), lambda b, i: (b, jnp.minimum((i + 1) * hb, nhb - 1), 0))]
        args = [p, p, p]
    else:
        in_specs = [tile]
        args = [p]
    in_specs += [_full(c.shape) for c in consts]
    o1 = pl.BlockSpec((1, tm, W), lambda b, i: (b, i, 0))
    o2 = pl.BlockSpec((2, 1, tm, W), lambda b, i: (0, b, i, 0))
    s1 = jax.ShapeDtypeStruct((B, L, W), F32)
    s2 = jax.ShapeDtypeStruct((2, B, L, W), F32)
    return pl.pallas_call(
        functools.partial(_feat_rw_body, vertical, width, tm),
        grid=(B, L // tm),
        in_specs=in_specs,
        out_specs=[o1] * 6 + [o2] * 2,
        out_shape=[s1] * 6 + [s2] * 2,
        compiler_params=_cparams(("parallel", "arbitrary")),
        name="feat_rw",
    )(*args, *consts)


def _feat_gla_body(vertical, width, tm, *refs):
    if vertical:
        prev_ref, cur_ref, next_ref = refs[:3]
        refs = refs[3:]
    else:
        cur_ref = refs[0]
        prev_ref = next_ref = None
        refs = refs[1:]
    cw_ref, g2_hi_ref, g2_lo_ref, gb_ref, qk_out, v_out, la_out = refs
    cur = cur_ref[0][:, 0:GLA_QKV_COLS]
    if vertical:
        class _Slice:
            def __init__(self, ref):
                self.ref = ref

            def __getitem__(self, idx):
                return self.ref[idx][:, 0:GLA_QKV_COLS]
        get = _neighbours(True, width, tm, cur, _Slice(prev_ref), _Slice(next_ref))
    else:
        get = _neighbours(False, width, tm, cur, None, None)
    acc = None
    for dr in ((-1, 0, 1) if vertical else (0,)):
        for dc in (-1, 0, 1):
            t = get(dr, dc) * cw_ref[(dr + 1) * 3 + (dc + 1):(dr + 1) * 3 + (dc + 1) + 1, :]
            acc = t if acc is None else acc + t
    qkv = _silu(acc)
    kw = GLA_KEY_WIDTH
    lane = lax.broadcasted_iota(jnp.int32, (1, 2 * kw), 1)
    qk_out[0] = qkv[:, 0:2 * kw] * jnp.where(lane < kw, GLA_KEY_DIM ** -0.5, 1.0)
    v_out[0] = qkv[:, 2 * kw:]
    pgl = cur_ref[0][:, GLA_QKV_COLS + GLA_VAL_WIDTH:GLA_SEG]
    z = _mmp(_split(pgl, N_LORA), [g2_hi_ref[...], g2_lo_ref[...]][:N_LORA]) + gb_ref[...]
    la = _log_sigmoid(z) * (1.0 / GLA_TAU)
    la_out[0, 0] = la[:, 0:kw]
    la_out[1, 0] = la[:, kw:]


def _feat_gla(p, vertical, tm, consts):
    B, L, S = p.shape
    width = GRID_W if vertical else tm
    hb = tm // GRID_W
    nhb = L // GRID_W
    tile = pl.BlockSpec((1, tm, S), lambda b, i: (b, i, 0))
    if vertical:
        in_specs = [pl.BlockSpec((1, GRID_W, S), lambda b, i: (b, jnp.maximum(i * hb - 1, 0), 0)),
                    tile,
                    pl.BlockSpec((1, GRID_W, S), lambda b, i: (b, jnp.minimum((i + 1) * hb, nhb - 1), 0))]
        args = [p, p, p]
    else:
        in_specs = [tile]
        args = [p]
    in_specs += [_full(c.shape) for c in consts]
    kw, vw = GLA_KEY_WIDTH, GLA_VAL_WIDTH
    return pl.pallas_call(
        functools.partial(_feat_gla_body, vertical, width, tm),
        grid=(B, L // tm),
        in_specs=in_specs,
        out_specs=[pl.BlockSpec((1, tm, 2 * kw), lambda b, i: (b, i, 0)),
                   pl.BlockSpec((1, tm, vw), lambda b, i: (b, i, 0)),
                   pl.BlockSpec((2, 1, tm, kw), lambda b, i: (0, b, i, 0))],
        out_shape=[jax.ShapeDtypeStruct((B, L, 2 * kw), F32),
                   jax.ShapeDtypeStruct((B, L, vw), F32),
                   jax.ShapeDtypeStruct((2, B, L, kw), F32)],
        compiler_params=_cparams(("parallel", "arbitrary")),
        name="feat_gla",
    )(*args, *consts)


def _order_masks(n, sgn):
    ti = lax.broadcasted_iota(jnp.int32, (n, n), 0)
    si = lax.broadcasted_iota(jnp.int32, (n, n), 1)
    rel = (si - ti) * sgn
    return rel < 0, rel <= 0


def _rwkv_stages(d, c, r_ref, k_ref, v_ref, kk_ref, lw_ref, a_ref, ka_ref, s0_ref, y_ref, st_ref, s_scr):
    C = CHUNK

    @pl.when(c == 0)
    def _():
        s_scr[...] = s0_ref[0, d]

    sgn = 1 - 2 * d
    r, k, v, kk = r_ref[0], k_ref[0], v_ref[0], kk_ref[0]
    lw, a = lw_ref[0, 0], a_ref[0, 0]
    before, upto = _order_masks(C, sgn)
    cum = _mm_exact_l(upto.astype(BF16), lw, N_CUM)
    tot = cum[C - 1:C] if d == 0 else cum[0:1]
    p_in = jnp.exp(cum)
    p_ex = jnp.exp(cum - lw)
    p_inv = jnp.exp(-cum)
    p_rem = jnp.exp(tot - cum)
    p_all = jnp.exp(tot)
    bvec = kk * a
    kmod = k * (1.0 + (a - 1.0) * ka_ref[...])
    a_t = -kk * p_ex
    r_t = r * p_in
    b_t = bvec * p_inv
    k_t = kmod * p_inv
    b_p = bvec * p_rem
    k_p = kmod * p_rem
    yield

    P = 2 * RW_HEAD_DIM
    lane = lax.broadcasted_iota(jnp.int32, (C, P), 1)
    h0 = lane < RW_HEAD_DIM
    ri = lax.broadcasted_iota(jnp.int32, (P, P), 0)
    ci = lax.broadcasted_iota(jnp.int32, (P, P), 1)
    same = (ri // C) == (ci // C)
    rel = ((ci % C) - (ri % C)) * sgn
    strict = same & (rel < 0)
    eye = (ri == ci).astype(F32)

    def stack2(x):
        return jnp.concatenate([jnp.where(h0, x, 0.0), jnp.where(h0, 0.0, x)], axis=0)

    pairs = range(RW_HEADS // 2)
    sls = [slice(p * P, (p + 1) * P) for p in pairs]
    a_st = [stack2(a_t[:, sl]) for sl in sls]
    v_st = [stack2(v[:, sl]) for sl in sls]
    g = [_mm(jnp.concatenate([a_st[p], stack2(r_t[:, sls[p]])], axis=0),
             jnp.concatenate([stack2(b_t[:, sls[p]]), stack2(k_t[:, sls[p]])], axis=0), N_RW, tb=True)
         for p in pairs]
    yield
    nmat = [jnp.where(strict, g[p][0:P, 0:P], 0.0) for p in pairs]
    a_ak = [jnp.where(strict, g[p][0:P, P:], 0.0) for p in pairs]
    ri2 = lax.broadcasted_iota(jnp.int32, (P, 2 * P), 0)
    ci2 = lax.broadcasted_iota(jnp.int32, (P, 2 * P), 1)
    incl2 = ((ri2 // C) == ((ci2 % P) // C)) & ((((ci2 % C) - (ri2 % C)) * sgn) <= 0)
    a_r = [jnp.where(incl2, g[p][P:, :], 0.0) for p in pairs]
    akv = [_mm(a_ak[p], v_st[p], N_RW) for p in pairs]
    yield
    t = [eye + nmat[p] for p in pairs]
    npow = nmat
    for _ in range(max(C.bit_length() - 2, 0)):
        npow = [_mm(npow[p], npow[p], N_RW) for p in pairs]
        t = [t[p] + _mm(t[p], npow[p], N_RW) for p in pairs]
        yield
    x = [_mm(t[p], jnp.concatenate([a_st[p], akv[p]], axis=1), N_RW) for p in pairs]
    s = [s_scr[p] for p in pairs]
    ur = [_mm(jnp.concatenate([x[p][0:C, 0:P] + x[p][C:, 0:P], r_t[:, sls[p]]], axis=0), s[p], N_RW, tb=True)
          for p in pairs]
    yield
    u = [ur[p][0:C] + x[p][0:C, P:] + x[p][C:, P:] for p in pairs]
    y_st = [_mm(a_r[p], jnp.concatenate([stack2(u[p]), v_st[p]], axis=0), N_RW) for p in pairs]
    for p in pairs:
        y_ref[0, :, sls[p]] = ur[p][C:] + y_st[p][0:C] + y_st[p][C:]
    yield
    upd = [_mm(jnp.concatenate([u[p], v[:, sls[p]]], axis=0).T,
               jnp.concatenate([b_p[:, sls[p]], k_p[:, sls[p]]], axis=0), N_RW) for p in pairs]
    for p in pairs:
        s_new = s[p] * p_all[:, sls[p]] + jnp.where(same, upd[p], 0.0)
        s_scr[p] = s_new
        st_ref[0, d, p] = s_new


def _scan_body(*refs):
    rw_in = [refs[0:6], refs[6:12]]
    ka_ref, rs0_ref = refs[12:14]
    gl_in = [refs[14:17], refs[17:20]]
    hm_ref, gs0_ref = refs[20:22]
    y_refs, rst_ref, o_refs, gst_ref = refs[22:24], refs[24], refs[25:27], refs[27]
    rs_scr, gs_scr = refs[28:30]
    c = pl.program_id(1)
    chains = []
    for d in range(2):
        chains.append(_rwkv_stages(d, c, *rw_in[d], ka_ref, rs0_ref, y_refs[d], rst_ref, rs_scr.at[d]))
        chains.append(_gla_stages(d, c, *gl_in[d], hm_ref, gs0_ref, o_refs[d], gst_ref, gs_scr.at[d]))
    while chains:
        chains = [ch for ch in chains if next(ch, True) is None]


def _scan(r, k, v, kk, lw, a, ka, rs0, qk, gv, la, hm, gs0):
    B, L, W = r.shape
    kw, vw = GLA_KEY_WIDTH, GLA_VAL_WIDTH
    nc = L // CHUNK
    chunk = [lambda b, c: c, lambda b, c: nc - 1 - c]
    shared = lambda d, w: pl.BlockSpec((1, CHUNK, w), lambda b, c: (b, chunk[d](b, c), 0))
    perdir = lambda d, w: pl.BlockSpec((1, 1, CHUNK, w), lambda b, c: (d, b, chunk[d](b, c), 0))
    npair = RW_HEADS // 2
    P = 2 * RW_HEAD_DIM
    rspec = pl.BlockSpec((1, 2, npair, P, P), lambda b, c: (b, 0, 0, 0, 0))
    gspec = pl.BlockSpec((1, 2, vw, kw), lambda b, c: (b, 0, 0, 0))
    rw_specs = [[shared(d, W)] * 4 + [perdir(d, W)] * 2 for d in range(2)]
    gl_specs = [[shared(d, 2 * kw), shared(d, vw), perdir(d, kw)] for d in range(2)]
    seq = lambda w: jax.ShapeDtypeStruct((B, L, w), F32)
    return pl.pallas_call(
        _scan_body,
        grid=(B, nc),
        in_specs=rw_specs[0] + rw_specs[1] + [pl.BlockSpec((1, W), lambda b, c: (0, 0)), rspec]
                 + gl_specs[0] + gl_specs[1] + [pl.BlockSpec((kw, vw), lambda b, c: (0, 0)), gspec],
        out_specs=[shared(0, W), shared(1, W), rspec, shared(0, vw), shared(1, vw), gspec],
        out_shape=[seq(W), seq(W), jax.ShapeDtypeStruct((B, 2, npair, P, P), F32),
                   seq(vw), seq(vw), jax.ShapeDtypeStruct((B, 2, vw, kw), F32)],
        scratch_shapes=[pltpu.VMEM((2, npair, P, P), F32), pltpu.VMEM((2, vw, kw), F32)],
        compiler_params=_cparams(("parallel", "arbitrary")),
        name="scan",
    )(r, k, v, kk, lw, a, r, k, v, kk, lw, a, ka, rs0, qk, gv, la, qk, gv, la, hm, gs0)


def _gla_stages(d, c, qk_ref, v_ref, g_ref, hm_ref, s0_ref, o_ref, st_ref, s_scr):
    C = CHUNK
    SB = GLA_SUB
    H = GLA_HEADS

    @pl.when(c == 0)
    def _():
        s_scr[...] = s0_ref[0, d]

    sgn = 1 - 2 * d
    kw, vw = GLA_KEY_WIDTH, GLA_VAL_WIDTH
    q = qk_ref[0][:, 0:kw]
    k = qk_ref[0][:, kw:]
    v = v_ref[0]
    g = g_ref[0, 0]
    _, upto = _order_masks(C, sgn)
    b = _mm_exact_l(upto.astype(BF16), g, N_CUM)
    pos = lax.broadcasted_iota(jnp.int32, (C, 1), 0) * sgn + d * (C - 1)

    def b_at(p):
        return jnp.sum(jnp.where(pos == p, b, 0.0), axis=0, keepdims=True)

    NEG = -jnp.inf
    tot = b_at(C - 1)
    s = s_scr[...]
    inter = _mm(q * jnp.exp(b), s, N_GLA, tb=True)
    yield

    half = pos >= C // 2
    odd = (pos // SB) % 2 == 1
    r1 = b_at(C // 2 - 1)
    r2 = jnp.where(half, b_at(3 * SB - 1), b_at(SB - 1))
    q1 = q * jnp.exp(jnp.where(half, b - r1, NEG))
    k1 = k * jnp.exp(jnp.where(half, NEG, r1 - b))
    q2 = q * jnp.exp(jnp.where(odd, b - r2, NEG))
    k2 = k * jnp.exp(jnp.where(odd, NEG, r2 - b))
    lane_h = lax.broadcasted_iota(jnp.int32, (C, kw), 1) // GLA_KEY_DIM

    def stack_heads(x):
        return jnp.concatenate([jnp.where(lane_h == h, x, 0.0) for h in range(H)], axis=0)

    att1 = _mm(stack_heads(q1), k1, N_GLA, tb=True)
    att2 = _mm(stack_heads(q2), k2, N_GLA, tb=True)
    yield
    pr =(lax.broadcasted_iota(jnp.int32, (H * C, C), 0) % C) * sgn + d * (C - 1)
    pc = lax.broadcasted_iota(jnp.int32, (H * C, C), 1) * sgn + d * (C - 1)
    att = att1 + jnp.where((pr >= C // 2) == (pc >= C // 2), att2, 0.0)
    res = _mm(att, v, N_GLA)
    lane_hv = lax.broadcasted_iota(jnp.int32, (C, vw), 1) // GLA_VAL_DIM
    off = None
    for h in range(H):
        t = jnp.where(lane_hv == h, res[h * C:(h + 1) * C], 0.0)
        off = t if off is None else off + t
    yield

    hm = hm_ref[...]
    il = lax.broadcasted_iota(jnp.int32, (SB, 1), 0)
    diag = []
    for blk in range(C // SB):
        rs = slice(blk * SB, (blk + 1) * SB)
        bb, qb, kb, vb = b[rs], q[rs], k[rs], v[rs]
        pieces = []
        for j in range(SB):
            valid = (il - j) * sgn >= 0
            pieces.append(jnp.exp(jnp.where(valid, bb - bb[j:j + 1], NEG)) * qb * kb[j:j + 1])
        a = _mm_exact_r(jnp.concatenate(pieces, axis=0), hm, N_GLA_INTRA)
        acc = None
        for j in range(SB):
            t = a[j * SB:(j + 1) * SB] * vb[j:j + 1]
            acc = t if acc is None else acc + t
        diag.append(acc)
        yield
    o_ref[0] = inter + off + jnp.concatenate(diag, axis=0)
    yield

    upd = _mm(v.T, k * jnp.exp(tot - b), N_GLA)
    ri = lax.broadcasted_iota(jnp.int32, upd.shape, 0) // GLA_VAL_DIM
    ci = lax.broadcasted_iota(jnp.int32, upd.shape, 1) // GLA_KEY_DIM
    s_new = s * jnp.exp(tot) + jnp.where(ri == ci, upd, 0.0)
    s_scr[...] = s_new
    st_ref[0, d] = s_new


def _seg_norm(y, bd, dim, eps):
    mu = _mm_exact_r(y, bd, N_SEG) * (1.0 / dim)
    dlt = y - mu
    var = _mm_exact_r(dlt * dlt, bd, N_SEG) * (1.0 / dim)
    return dlt * lax.rsqrt(var + eps)


def _layer_norm(x, w, b):
    mu = jnp.mean(x, axis=-1, keepdims=True)
    dlt = x - mu
    var = jnp.mean(dlt * dlt, axis=-1, keepdims=True)
    return dlt * lax.rsqrt(var + LN_EPS) * w + b


def _merge_body(x_ref, mod_ref, yrw0_ref, yrw1_ref, bonus_ref, g_ref, ygla0_ref, ygla1_ref, og_ref,
                bd64_ref, bd128_ref, rgw_ref, rgb_ref, ggw_ref, ggb_ref, *refs):
    n = N_MERGE
    wrw = [r[...] for r in refs[0:n]]
    wgla = [r[...] for r in refs[n:2 * n]]
    wout = [r[...] for r in refs[2 * n:3 * n]]
    wgate = [r[...] for r in refs[3 * n:3 * n + N_PROJ]]
    ln_w_ref, ln_b_ref, x1_ref, h2p_ref = refs[3 * n + N_PROJ:]
    y = _seg_norm(yrw0_ref[0] + yrw1_ref[0], bd64_ref[...], RW_HEAD_DIM, RW_GN_EPS)
    y = (y * rgw_ref[...] + rgb_ref[...] + bonus_ref[0]) * g_ref[0]
    y_rw = _mmp(_split(y, n), wrw)
    y = _seg_norm(ygla0_ref[0] + ygla1_ref[0], bd128_ref[...], GLA_VAL_DIM, LN_EPS)
    y = (y * ggw_ref[...] + ggb_ref[...]) * _silu(og_ref[0])
    y_gla = _mmp(_split(y, n), wgla)
    h = x_ref[0] * (1.0 + mod_ref[0, 1:2, :]) + mod_ref[0, 0:1, :]
    gate = _sigmoid(_mmp(_split(h, N_PROJ), wgate))
    mixed = gate[:, 0:D_MODEL] * y_rw + gate[:, D_MODEL:] * y_gla
    mix = _mmp(_split(mixed, n), wout)
    g1 = mod_ref[0, 2:3, :]
    x1 = _layer_norm(ALPHA * x_ref[0] + g1 * mix, ln_w_ref[...], ln_b_ref[...])
    x1_ref[0] = x1
    h2p_ref[0] = _pack_bf16_pairs(_moe_input(x1, mod_ref))


def _merge(x, mod, yrw, bonus, g, ygla, p_gla, consts, weights, ln_w, ln_b, tm):
    B, L, D = x.shape
    W = RW_WIDTH
    tok = lambda w: pl.BlockSpec((1, tm, w), lambda b, i: (b, i, 0))
    wl = [w for ws in weights for w in ws]
    return pl.pallas_call(
        _merge_body,
        grid=(B, L // tm),
        in_specs=[tok(D), pl.BlockSpec((1, 6, D), lambda b, i: (b, 0, 0)), tok(W), tok(W), tok(W), tok(W),
                  tok(W), tok(W),
                  pl.BlockSpec((1, tm, W), lambda b, i: (b, i, GLA_QKV_COLS // W))]
                 + [_full(c.shape) for c in consts] + [_full(w.shape) for w in wl]
                 + [_full(ln_w.shape), _full(ln_b.shape)],
        out_specs=[tok(D), tok(D // 2)],
        out_shape=[jax.ShapeDtypeStruct((B, L, D), F32), jax.ShapeDtypeStruct((B, L, D // 2), jnp.int32)],
        compiler_params=_cparams(("parallel", "arbitrary")),
        name="merge",
    )(x, mod, *yrw, bonus, g, *ygla, p_gla, *consts, *wl, ln_w, ln_b)


def _moe_input(x1, mod_ref):
    return x1 * (1.0 + mod_ref[0, 4:5, :]) + mod_ref[0, 3:4, :]


def _router_body(x1_ref, mod_ref, *refs):
    n = N_ROUTER
    rt = [r[...] for r in refs[0:n]]
    bias_ref, e_out, g_out, cnt_out = refs[n:]
    tm = x1_ref.shape[0]
    E, G, PG = N_EXPERTS, N_GROUPS, N_EXPERTS // N_GROUPS
    h = _moe_input(x1_ref[...], mod_ref)
    logits = _mmp(rt, _split(h, n), tb=True)
    scores = _sigmoid(logits)
    sel = scores + bias_ref[:, 0:tm]
    NEG = -jnp.inf
    ip = lax.broadcasted_iota(jnp.int32, (PG, tm), 0)
    group_rows = []
    for gidx in range(G):
        sg = sel[gidx * PG:(gidx + 1) * PG]
        m1 = jnp.max(sg, axis=0, keepdims=True)
        first = jnp.min(jnp.where(sg == m1, ip, PG), axis=0, keepdims=True)
        m2 = jnp.max(jnp.where(ip == first, NEG, sg), axis=0, keepdims=True)
        group_rows.append(m1 + m2)
    gs = jnp.concatenate(group_rows, axis=0)
    gi = lax.broadcasted_iota(jnp.int32, (G, tm), 0)
    keep = jnp.zeros((G, tm), F32)
    for _ in range(TOPK_GROUPS):
        m = jnp.max(gs, axis=0, keepdims=True)
        idx = jnp.min(jnp.where(gs == m, gi, G), axis=0, keepdims=True)
        hit = gi == idx
        keep = jnp.where(hit, 1.0, keep)
        gs = jnp.where(hit, NEG, gs)
    cur = jnp.concatenate(
        [jnp.where(keep[gidx:gidx + 1] > 0.5, sel[gidx * PG:(gidx + 1) * PG], NEG) for gidx in range(G)],
        axis=0)
    ei = lax.broadcasted_iota(jnp.int32, (E, tm), 0)
    idxs, gates = [], []
    picked = jnp.zeros((E, tm), F32)
    for _ in range(TOP_K):
        m = jnp.max(cur, axis=0, keepdims=True)
        idx = jnp.min(jnp.where(cur == m, ei, E), axis=0, keepdims=True)
        hit = ei == idx
        idxs.append(idx)
        gates.append(jnp.sum(jnp.where(hit, scores, 0.0), axis=0, keepdims=True))
        cur = jnp.where(hit, NEG, cur)
        picked = jnp.where(hit, 1.0, picked)
    gate = jnp.concatenate(gates, axis=0)
    e_out[...] = jnp.concatenate(idxs, axis=0)
    g_out[...] = gate / jnp.sum(gate, axis=0, keepdims=True) * ROUTED_SCALE

    @pl.when(pl.program_id(0) == 0)
    def _():
        cnt_out[...] = jnp.zeros(cnt_out.shape, F32)

    part = picked[:, 0:128]
    for j in range(1, tm // 128):
        part = part + picked[:, j * 128:(j + 1) * 128]
    cnt_out[...] += part


def _router(x1f, mod, seq, router_t_parts, bias_b, tm):
    n, D = x1f.shape
    E = N_EXPERTS
    return pl.pallas_call(
        _router_body,
        grid=(n // tm,),
        in_specs=[pl.BlockSpec((tm, D), lambda i: (i, 0)),
                  pl.BlockSpec((1, 6, D), lambda i: (i * tm // seq, 0, 0))]
                 + [pl.BlockSpec((E, D), lambda i: (0, 0))] * len(router_t_parts)
                 + [pl.BlockSpec(bias_b.shape, lambda i: (0, 0))],
        out_specs=[pl.BlockSpec((TOP_K, tm), lambda i: (0, i))] * 2
                  + [pl.BlockSpec((E, 128), lambda i: (0, 0))],
        out_shape=[jax.ShapeDtypeStruct((TOP_K, n), jnp.int32), jax.ShapeDtypeStruct((TOP_K, n), F32),
                   jax.ShapeDtypeStruct((E, 128), F32)],
        compiler_params=_cparams(("arbitrary",)),
        name="router",
    )(x1f, mod, *router_t_parts, bias_b)


def _sc_permute_body(scatter, per_w, src_hbm, idx_hbm, out_hbm, idx_v, rows_v, sem_i, sem_o):
    wid = lax.axis_index("s") * SC_CORES + lax.axis_index("c")
    base = wid * per_w
    win = SC_WINDOW
    pltpu.sync_copy(idx_hbm.at[wid], idx_v)

    def linear(ref, j):
        return ref.at[pl.ds(base + j * win, win)]

    def indexed(ref, j):
        return ref.at[idx_v.at[j]]

    src_at, out_at = (linear, indexed) if scatter else (indexed, linear)

    @pl.loop(0, per_w // win, step=2)
    def _(j):
        i0 = pltpu.async_copy(src_at(src_hbm, j), rows_v.at[0], sem_i.at[0])
        i1 = pltpu.async_copy(src_at(src_hbm, j + 1), rows_v.at[1], sem_i.at[1])
        i0.wait()
        o0 = pltpu.async_copy(rows_v.at[0], out_at(out_hbm, j), sem_o.at[0])
        i1.wait()
        o1 = pltpu.async_copy(rows_v.at[1], out_at(out_hbm, j + 1), sem_o.at[1])
        o0.wait()
        o1.wait()


def _sc_permute(src, idx, scatter):
    nrows = idx.shape[0]
    D = src.shape[1]
    per_w = nrows // SC_WORKERS
    mesh = plsc.VectorSubcoreMesh(core_axis_name="c", subcore_axis_name="s",
                                  num_cores=SC_CORES, num_subcores=SC_SUBCORES)
    return pl.kernel(
        functools.partial(_sc_permute_body, scatter, per_w),
        out_type=jax.ShapeDtypeStruct((nrows, D), src.dtype),
        mesh=mesh,
        scratch_types=[pltpu.VMEM((per_w // SC_WINDOW, SC_WINDOW), jnp.int32),
                       pltpu.VMEM((2, SC_WINDOW, D), src.dtype),
                       pltpu.SemaphoreType.DMA((2,)), pltpu.SemaphoreType.DMA((2,))],
        name="sc_scatter" if scatter else "sc_gather",
    )(src, idx.reshape(SC_WORKERS, per_w // SC_WINDOW, SC_WINDOW))


def _experts_body(ib_ref, ie_ref, lo_ref, hi_ref, x_ref, wgu_ref, wd_ref, y_ref, wgu_bf, wd_bf):
    i = pl.program_id(0)
    prev = jnp.maximum(i - 1, 0)
    R = ROW_BLOCK

    @pl.when(hi_ref[i] > lo_ref[i])
    def _():
        @pl.when((i == 0) | (ie_ref[i] != ie_ref[prev]))
        def _():
            wgu_bf[...] = wgu_ref[0].astype(BF16)
            wd_bf[...] = wd_ref[0].astype(BF16)

        F = EXPERT_DIM
        halves = [slice(h * (R // 2), (h + 1) * (R // 2)) for h in range(2)]
        xb = [_unpack_bf16_pairs(x_ref[rs, :]).astype(BF16) for rs in halves]
        gu = [_dot(t, wgu_bf[...]) for t in xb]
        act = [(_silu(t[:, 0:F]) * t[:, F:]).astype(BF16) for t in gu]
        yb = [_pack_bf16_pairs(_dot(t, wd_bf[...])) for t in act]
        first = (i == 0) | (ib_ref[i] != ib_ref[prev])
        for h, (rs, t) in enumerate(zip(halves, yb)):
            g = ib_ref[i] * R + h * (R // 2) + lax.broadcasted_iota(jnp.int32, (R // 2, 1), 0)
            mine = (g >= lo_ref[i]) & (g < hi_ref[i])
            keep = jnp.where(first, 0, y_ref[rs, :])
            y_ref[rs, :] = jnp.where(mine, t, keep)


def _experts(item_blk, item_e, row_lo, row_hi, xs, w_gate_up, w_down):
    nrows, DP = xs.shape
    D = 2 * DP
    R = ROW_BLOCK
    F2 = w_gate_up.shape[2]
    grid_spec = pltpu.PrefetchScalarGridSpec(
        num_scalar_prefetch=4,
        grid=(item_blk.shape[0],),
        in_specs=[pl.BlockSpec((R, DP), lambda i, ib, ie, lo, hi: (ib[i], 0)),
                  pl.BlockSpec((1, D, F2), lambda i, ib, ie, lo, hi: (ie[i], 0, 0)),
                  pl.BlockSpec((1, F2 // 2, D), lambda i, ib, ie, lo, hi: (ie[i], 0, 0))],
        out_specs=pl.BlockSpec((R, DP), lambda i, ib, ie, lo, hi: (ib[i], 0)),
        scratch_shapes=[pltpu.VMEM((D, F2), BF16), pltpu.VMEM((F2 // 2, D), BF16)],
    )
    return pl.pallas_call(
        _experts_body,
        grid_spec=grid_spec,
        out_shape=jax.ShapeDtypeStruct((nrows, DP), jnp.int32),
        compiler_params=_cparams(("arbitrary",)),
        name="experts",
    )(item_blk, item_e, row_lo, row_hi, xs, w_gate_up, w_down)


def _final_body(yg_ref, x1_ref, mod_ref, gate_ref, *refs):
    n = N_SHARED
    sgu = [r[...] for r in refs[0:n]]
    sd = [r[...] for r in refs[n:2 * n]]
    ln_w_ref, ln_b_ref, o_ref = refs[2 * n:]
    tm = x1_ref.shape[1]
    gate = gate_ref[0]
    routed = None
    for kk in range(TOP_K):
        t = _unpack_bf16_pairs(yg_ref[kk * tm:(kk + 1) * tm, :]) * gate[:, kk:kk + 1]
        routed = t if routed is None else routed + t
    h2 = _moe_input(x1_ref[0], mod_ref)
    F = sgu[0].shape[1] // 2
    gu = _mmp(_split(h2, n), sgu)
    act = _silu(gu[:, 0:F]) * gu[:, F:]
    shared = _mmp(_split(act, n), sd)
    g2 = mod_ref[0, 5:6, :]
    o_ref[0] = _layer_norm(ALPHA * x1_ref[0] + g2 * (routed + shared), ln_w_ref[...], ln_b_ref[...])


def _final(yg, x1, mod, gate, sgu, sd, ln_w, ln_b, tm):
    B, L, D = x1.shape
    ni = L // tm
    rows = tm * TOP_K
    tok = lambda w: pl.BlockSpec((1, tm, w), lambda b, i: (b, i, 0))
    ws = list(sgu) + list(sd)
    return pl.pallas_call(
        _final_body,
        grid=(B, ni),
        in_specs=[pl.BlockSpec((rows, D // 2), lambda b, i: (b * ni + i, 0)),
                  tok(D), pl.BlockSpec((1, 6, D), lambda b, i: (b, 0, 0)), tok(TOP_K)]
                 + [_full(w.shape) for w in ws] + [_full(ln_w.shape), _full(ln_b.shape)],
        out_specs=tok(D),
        out_shape=jax.ShapeDtypeStruct((B, L, D), F32),
        compiler_params=_cparams(("parallel", "arbitrary")),
        name="final",
    )(yg, x1, mod, gate, *ws, ln_w, ln_b)


def _dispatch_plan(experts_t, sizes, tm):
    R, E = ROW_BLOCK, N_EXPERTS
    K, n = experts_t.shape
    nk = K * n
    assert nk % R == 0 and nk % SC_ROW_ALIGN == 0
    tok = jnp.broadcast_to(jnp.arange(n, dtype=jnp.int32)[None, :], (K, n))
    slot = jnp.arange(K, dtype=jnp.int32)[:, None]
    aid = (tok // tm) * (tm * K) + slot * tm + tok % tm
    id_bits = (nk - 1).bit_length()
    assert E << id_bits < 2 ** 31
    row_id = lax.sort((experts_t << id_bits | aid).reshape(-1)) & ((1 << id_bits) - 1)
    row_tok = (row_id // (tm * K)) * tm + row_id % tm

    start = jnp.cumsum(sizes) - sizes
    first_blk = start // R
    tiles = jnp.where(sizes > 0, (start + sizes - 1) // R - first_blk + 1, 0)
    t_end = jnp.cumsum(tiles)
    nitems = nk // R + E - 1
    it = jnp.arange(nitems, dtype=jnp.int32)
    live = it < t_end[-1]
    itc = jnp.minimum(it, t_end[-1] - 1)
    item_e = jnp.minimum(jnp.searchsorted(t_end, itc, side='right'), E - 1).astype(jnp.int32)
    item_blk = (first_blk[item_e] + itc - (t_end - tiles)[item_e]).astype(jnp.int32)
    row_lo = jnp.where(live, start[item_e], 0).astype(jnp.int32)
    row_hi = jnp.where(live, (start + sizes)[item_e], 0).astype(jnp.int32)
    return row_tok, row_id, item_blk, item_e, row_lo, row_hi


def _block_diag_ones(n, blk):
    i = jnp.arange(n) // blk
    return (i[:, None] == i[None, :]).astype(BF16)


def kernel(x, c, ctx, c_ctx, w_ada, b_ada, w_in, rw_mu, rw_w0, rw_w2, rw_a0, rw_a2, rw_g2, rw_k_k, rw_k_a,
           rw_r_k, rw_gn_w, rw_gn_b, gla_conv, gla_g2, gla_gb, gla_gn_w, gla_gn_b, w_br_rw, w_br_gla, w_out,
           ln1_w, ln1_b, router, router_bias, w_gate_up, w_down, sh_gate_up, sh_down, ln2_w, ln2_b):
    B, L, D = x.shape
    CT = ctx.shape[1]
    l = 0
    W = RW_WIDTH
    row = lambda t: t.reshape(1, -1)

    rows = -(-(B + 1) // 8) * 8
    cc = jnp.zeros((rows, D), F32).at[:B].set(c).at[B].set(c_ctx)
    mod = _ada(cc, w_ada[l], b_ada[l])
    mod_lat = mod[:B].reshape(B, 6, D)
    mod_ctx = jnp.broadcast_to(mod[B].reshape(1, 6, D), (B, 6, D))

    w = w_in[l]
    g0 = RW_COLS
    w_rw = jnp.pad(w[:, :RW_COLS], ((0, 0), (0, RW_SEG - RW_COLS)))
    w_gla = jnp.concatenate([w[:, g0:g0 + GLA_QKV_COLS],
                             w[:, g0 + GLA_QKV_COLS + GLA_GATE_RANK:g0 + GLA_COLS],
                             w[:, g0 + GLA_QKV_COLS:g0 + GLA_QKV_COLS + GLA_GATE_RANK],
                             jnp.zeros((D, GLA_SEG - GLA_COLS), F32)], axis=1)
    w_gate = w[:, MIX_COLS:]
    tm_p = min(512, L)
    p_rw = _proj(x, mod_lat, w_rw, tm_p)
    p_gla = _proj(x, mod_lat, w_gla, tm_p)
    pc_rw = _proj(ctx, mod_ctx, w_rw, CT)
    pc_gla = _proj(ctx, mod_ctx, w_gla, CT)

    mu = jnp.pad(rw_mu[l], (0, RW_SEG - RW_COLS)).reshape(1, -1)
    lora = jnp.zeros((256, 5 * W), F32)
    e1 = RW_DECAY_RANK
    e2 = e1 + RW_AAA_RANK
    e3 = e2 + RW_GATE_RANK
    for d in range(2):
        lora = lora.at[0:e1, d * W:(d + 1) * W].set(rw_w2[l, d])
        lora = lora.at[e1:e2, (2 + d) * W:(3 + d) * W].set(rw_a2[l, d])
    lora = lora.at[e2:e3, 4 * W:].set(rw_g2[l])
    lora_p = (_split(lora, N_LORA) + [jnp.zeros_like(lora, BF16)])[:2]
    bd64 = _block_diag_ones(W, RW_HEAD_DIM)
    rw_consts = [mu, lora_p[0], lora_p[1], rw_w0[l].reshape(1, -1), rw_a0[l].reshape(1, -1),
                 row(rw_k_k[l]), row(rw_k_a[l]), row(rw_r_k[l]), bd64]
    tm_f = min(256, L)
    r, k, v, kkn, g, bonus, lwd, a = _feat_rw(p_rw, True, tm_f, rw_consts)
    rc, kc, vc, kknc, _, _, lwdc, ac = _feat_rw(pc_rw, False, CT, rw_consts)

    g2 = jnp.zeros((GLA_SEG - GLA_QKV_COLS - GLA_VAL_WIDTH, 2 * GLA_KEY_WIDTH), F32)
    g2 = g2.at[:GLA_GATE_RANK].set(jnp.concatenate([gla_g2[l, 0], gla_g2[l, 1]], axis=1))
    g2_p = (_split(g2, N_LORA) + [jnp.zeros_like(g2, BF16)])[:2]
    cw = jnp.pad(gla_conv[l].reshape(9, GLA_QKV_COLS), ((0, 7), (0, 0)))
    gla_consts = [cw, g2_p[0], g2_p[1], gla_gb[l].reshape(1, -1)]
    qk, vg, la = _feat_gla(p_gla, True, tm_f, gla_consts)
    qkc, vgc, lac = _feat_gla(pc_gla, False, CT, gla_consts)

    ka = row(rw_k_a[l])
    P = 2 * RW_HEAD_DIM
    s0 = jnp.zeros((B, 2, RW_HEADS // 2, P, P), F32)
    hi = jnp.arange(GLA_KEY_WIDTH) // GLA_KEY_DIM
    hj = jnp.arange(GLA_VAL_WIDTH) // GLA_VAL_DIM
    hm = (hi[:, None] == hj[None, :]).astype(BF16)
    g0s = jnp.zeros((B, 2, GLA_VAL_WIDTH, GLA_KEY_WIDTH), F32)
    _, _, s_ctx, _, _, gs_ctx = _scan(rc, kc, vc, kknc, lwdc, ac, ka, s0, qkc, vgc, lac, hm, g0s)
    yr0, yr1, _, yg0, yg1, _ = _scan(r, k, v, kkn, lwd, a, ka, s_ctx, qk, vg, la, hm, gs_ctx)
    y_rw, y_gla = (yr0, yr1), (yg0, yg1)

    bd128 = _block_diag_ones(GLA_VAL_WIDTH, GLA_VAL_DIM)
    m_consts = [bd64, bd128, row(rw_gn_w[l]), row(rw_gn_b[l]), row(gla_gn_w[l]), row(gla_gn_b[l])]
    m_weights = [_split(w_br_rw[l], N_MERGE), _split(w_br_gla[l], N_MERGE), _split(w_out[l], N_MERGE),
                 _split(w_gate, N_PROJ)]
    x1, h2p = _merge(x, mod_lat, y_rw, bonus, g, y_gla, p_gla, m_consts, m_weights,
                     row(ln1_w[l]), row(ln1_b[l]), min(256, L))

    n = B * L
    tm_r = min(256, L)
    bias_b = jnp.broadcast_to(router_bias[l].reshape(-1, 1), (N_EXPERTS, tm_r))
    e_t, g_t, cnt = _router(x1.reshape(n, D), mod_lat, L, _split(router[l].T, N_ROUTER), bias_b, tm_r)
    sizes = jnp.sum(cnt, axis=1).astype(jnp.int32)
    tm_c = min(128, L)
    row_tok, row_id, item_blk, item_e, row_lo, row_hi = _dispatch_plan(e_t, sizes, tm_c)
    xs = _sc_permute(h2p.reshape(n, D // 2), row_tok, scatter=False)
    y = _experts(item_blk, item_e, row_lo, row_hi, xs, w_gate_up[l], w_down[l])
    yg = _sc_permute(y, row_id, scatter=True)
    gate_tok = g_t.T.reshape(B, L, TOP_K)
    return _final(yg, x1, mod_lat, gate_tok, _split(sh_gate_up[l], N_SHARED),
                  _split(sh_down[l], N_SHARED), row(ln2_w[l]), row(ln2_b[l]), tm_c)
```

```python
import functools
import math

import jax
import jax.numpy as jnp
from jax import lax
from jax.experimental import pallas as pl
from jax.experimental.pallas import tpu as pltpu
from jax.experimental.pallas import tpu_sc as plsc

F32 = jnp.float32
BF16 = jnp.bfloat16

D_MODEL = 1024
GRID_W = 64
RW_WIDTH = 512
RW_HEADS = 8
RW_HEAD_DIM = 64
RW_DECAY_RANK = 32
RW_AAA_RANK = 32
RW_GATE_RANK = 96
RW_GN_EPS = 64e-5
RW_COLS = 1696
RW_SEG = 1792
GLA_HEADS = 4
GLA_KEY_WIDTH = 256
GLA_VAL_WIDTH = 512
GLA_KEY_DIM = 64
GLA_VAL_DIM = 128
GLA_GATE_RANK = 16
GLA_TAU = 16.0
GLA_QKV_COLS = 1024
GLA_COLS = 1552
GLA_SEG = 1664
MIX_COLS = RW_COLS + GLA_COLS
N_EXPERTS = 256
TOP_K = 8
N_GROUPS = 8
TOPK_GROUPS = 4
EXPERT_DIM = 256
ROUTED_SCALE = 2.5
LN_EPS = 1e-5
DEPTH = 1
ALPHA = (2 * DEPTH) ** 0.25

CHUNK = 64
GLA_SUB = 16
ROW_BLOCK = 512
PROJ_TILE = 512
FEAT_TILE = 256
MERGE_TILE = 256
ROUTER_TILE = 256
COMBINE_TILE = 128
VMEM_LIMIT = 48 * 1024 * 1024
SC_CORES = 2
SC_SUBCORES = 16
SC_WORKERS = SC_CORES * SC_SUBCORES
SC_WINDOW = 32
SC_ROW_ALIGN = 2 * SC_WORKERS * SC_WINDOW

N_ADA = 3
N_PROJ = 1
N_LORA = 1
N_SEG = 2
N_CUM = 2
N_RW = 1
N_GLA = 1
N_GLA_INTRA = 1
N_MERGE = 1
N_ROUTER = 3
N_SHARED = 1


def _split(x, n):
    parts = []
    r = x
    for i in range(n):
        p = r.astype(BF16)
        parts.append(p)
        if i < n - 1:
            r = r - p.astype(F32)
    return parts


def _dot(a, b, ta=False, tb=False):
    dn = (((0 if ta else 1,), (1 if tb else 0,)), ((), ()))
    return lax.dot_general(a, b, dn, preferred_element_type=F32)


def _mmp(ap, bp, ta=False, tb=False):
    n = max(len(ap), len(bp))
    out = None
    for i in range(len(ap)):
        for j in range(len(bp)):
            if i + j <= n - 1:
                t = _dot(ap[i], bp[j], ta, tb)
                out = t if out is None else out + t
    return out


def _mm(a, b, n, ta=False, tb=False):
    return _mmp(_split(a, n), _split(b, n), ta, tb)


def _mm_exact_l(m_bf16, x, n):
    return _mmp([m_bf16], _split(x, n))


def _mm_exact_r(x, m_bf16, n):
    return _mmp(_split(x, n), [m_bf16])


def _pack_bf16_pairs(x):
    w = x.shape[1] // 2
    hi = lax.bitcast_convert_type(x[:, :w].astype(BF16).astype(F32), jnp.int32)
    lo = lax.bitcast_convert_type(x[:, w:].astype(BF16).astype(F32), jnp.int32)
    return hi | lax.shift_right_logical(lo, 16)


def _unpack_bf16_pairs(p):
    hi = lax.bitcast_convert_type(p & jnp.int32(-65536), F32)
    lo = lax.bitcast_convert_type(lax.shift_left(p, 16), F32)
    return jnp.concatenate([hi, lo], axis=1)


def _sigmoid(x):
    return 1.0 / (1.0 + jnp.exp(-x))


def _silu(x):
    return x * _sigmoid(x)


def _log_sigmoid(x):
    return jnp.minimum(x, 0.0) - jnp.log(1.0 + jnp.exp(-jnp.abs(x)))


def _cparams(sem):
    return pltpu.CompilerParams(dimension_semantics=sem, vmem_limit_bytes=VMEM_LIMIT)


def _full(shape):
    nd = len(shape)
    return pl.BlockSpec(shape, lambda *a: (0,) * nd)


def _ada_body(c_ref, w_ref, b_ref, o_ref):
    s = _silu(c_ref[...])
    o_ref[...] = _mm(s, w_ref[...], N_ADA) + b_ref[...]


def _ada(cc, w, b):
    rows, d = cc.shape
    n = w.shape[1] // d
    return pl.pallas_call(
        _ada_body,
        grid=(n,),
        in_specs=[pl.BlockSpec((rows, d), lambda j: (0, 0)),
                  pl.BlockSpec((d, d), lambda j: (0, j)),
                  pl.BlockSpec((1, d), lambda j: (0, j))],
        out_specs=pl.BlockSpec((rows, d), lambda j: (0, j)),
        out_shape=jax.ShapeDtypeStruct((rows, w.shape[1]), F32),
        compiler_params=_cparams(("arbitrary",)),
        name="ada",
    )(cc, w, b.reshape(1, -1))


def _proj_body(x_ref, mod_ref, *refs):
    w_refs, o_ref = refs[:-1], refs[-1]
    sh = mod_ref[0, 0:1, :]
    sc = mod_ref[0, 1:2, :]
    h = x_ref[0] * (1.0 + sc) + sh
    o_ref[0] = _mmp(_split(h, N_PROJ), [w[...] for w in w_refs])


def _proj(x, mod, w, tm):
    B, L, D = x.shape
    n = w.shape[1]
    wp = _split(w, N_PROJ)
    return pl.pallas_call(
        _proj_body,
        grid=(B, L // tm),
        in_specs=[pl.BlockSpec((1, tm, D), lambda b, i: (b, i, 0)),
                  pl.BlockSpec((1, 6, D), lambda b, i: (b, 0, 0))]
                 + [pl.BlockSpec((D, n), lambda b, i: (0, 0))] * N_PROJ,
        out_specs=pl.BlockSpec((1, tm, n), lambda b, i: (b, i, 0)),
        out_shape=jax.ShapeDtypeStruct((B, L, n), F32),
        compiler_params=_cparams(("parallel", "arbitrary")),
        name="proj",
    )(x, mod, *wp)


def _neighbours(vertical, width, tm, cur, prev_ref, next_ref):
    if vertical:
        i = pl.program_id(1)
        n = pl.num_programs(1)
        prev = jnp.where(i > 0, prev_ref[0], 0.0)
        nxt = jnp.where(i < n - 1, next_ref[0], 0.0)
        pad = jnp.zeros((8, cur.shape[1]), F32)
        ext = jnp.concatenate([pad, prev, cur, nxt, pad], axis=0)
        off = GRID_W + 8
    else:
        pad = jnp.zeros((8, cur.shape[1]), F32)
        ext = jnp.concatenate([pad, cur, pad], axis=0)
        off = 8
    col = lax.broadcasted_iota(jnp.int32, (tm, 1), 0) % width

    def get(dr, dc):
        s = off + GRID_W * dr + dc
        v = ext[s:s + tm]
        if dc == -1:
            v = jnp.where(col == 0, 0.0, v)
        elif dc == 1:
            v = jnp.where(col == width - 1, 0.0, v)
        return v

    return get


def _feat_rw_body(vertical, width, tm, *refs):
    if vertical:
        prev_ref, cur_ref, next_ref = refs[:3]
        refs = refs[3:]
    else:
        cur_ref = refs[0]
        prev_ref = next_ref = None
        refs = refs[1:]
    (mu_ref, lw_hi_ref, lw_lo_ref, w0_ref, a0_ref, kk_ref, ka_ref, rk_ref, bd_ref,
     r_out, k_out, v_out, kkn_out, g_out, bonus_out, lwd_out, a_out) = refs
    cur = cur_ref[0]
    get = _neighbours(vertical, width, tm, cur, prev_ref, next_ref)
    left, right = get(0, -1), get(0, 1)
    up, down = (get(-1, 0), get(1, 0)) if vertical else (left, right)
    l4 = lax.broadcasted_iota(jnp.int32, (1, cur.shape[1]), 1) % 4
    shifted = jnp.where(l4 == 0, left, jnp.where(l4 == 1, right, jnp.where(l4 == 2, up, down)))
    p = cur + mu_ref[...] * (shifted - cur)

    r = p[:, 0:RW_WIDTH]
    k = p[:, RW_WIDTH:2 * RW_WIDTH]
    v = p[:, 2 * RW_WIDTH:3 * RW_WIDTH]
    slab = p[:, 3 * RW_WIDTH:3 * RW_WIDTH + 256]
    ln = lax.broadcasted_iota(jnp.int32, (1, 256), 1)
    e1 = RW_DECAY_RANK
    e2 = e1 + RW_AAA_RANK
    e3 = e2 + RW_GATE_RANK
    slab = jnp.where(ln < e1, jnp.tanh(slab),
                     jnp.where(ln < e2, slab, jnp.where(ln < e3, _sigmoid(slab), 0.0)))
    lo = _mmp(_split(slab, N_LORA), [lw_hi_ref[...], lw_lo_ref[...]][:N_LORA])
    W = RW_WIDTH
    a_sum = None
    for d in range(2):
        z = w0_ref[:, d * W:(d + 1) * W] + lo[:, d * W:(d + 1) * W]
        lwd_out[d, 0] = -_sigmoid(z) * math.exp(-0.5)
        a = _sigmoid(a0_ref[:, d * W:(d + 1) * W] + lo[:, (2 + d) * W:(3 + d) * W])
        a_out[d, 0] = a
        a_sum = a if a_sum is None else a_sum + a
    g_out[0] = lo[:, 4 * W:5 * W]
    bd = bd_ref[...]
    kk = k * kk_ref[...]
    ss = _mm_exact_r(kk * kk, bd, N_SEG)
    kkn_out[0] = kk / jnp.maximum(jnp.sqrt(ss), 1e-12)
    kmod_sum = k * (2.0 + (a_sum - 2.0) * ka_ref[...])
    bonus_out[0] = _mm_exact_r(r * kmod_sum * rk_ref[...], bd, N_SEG) * v
    r_out[0] = r
    k_out[0] = k
    v_out[0] = v


def _feat_rw(p, vertical, tm, consts):
    B, L, S = p.shape
    width = GRID_W if vertical else tm
    W = RW_WIDTH
    hb = tm // GRID_W
    nhb = L // GRID_W
    tile = pl.BlockSpec((1, tm, S), lambda b, i: (b, i, 0))
    if vertical:
        in_specs = [pl.BlockSpec((1, GRID_W, S), lambda b, i: (b, jnp.maximum(i * hb - 1, 0), 0)),
                    tile,
                    pl.BlockSpec((1, GRID_W, S), lambda b, i: (b, jnp.minimum((i + 1) * hb, nhb - 1), 0))]
        args = [p, p, p]
    else:
        in_specs = [tile]
        args = [p]
    in_specs += [_full(c.shape) for c in consts]
    o1 = pl.BlockSpec((1, tm, W), lambda b, i: (b, i, 0))
    o2 = pl.BlockSpec((2, 1, tm, W), lambda b, i: (0, b, i, 0))
    s1 = jax.ShapeDtypeStruct((B, L, W), F32)
    s2 = jax.ShapeDtypeStruct((2, B, L, W), F32)
    return pl.pallas_call(
        functools.partial(_feat_rw_body, vertical, width, tm),
        grid=(B, L // tm),
        in_specs=in_specs,
        out_specs=[o1] * 6 + [o2] * 2,
        out_shape=[s1] * 6 + [s2] * 2,
        compiler_params=_cparams(("parallel", "arbitrary")),
        name="feat_rw",
    )(*args, *consts)


def _feat_gla_body(vertical, width, tm, *refs):
    if vertical:
        prev_ref, cur_ref, next_ref = refs[:3]
        refs = refs[3:]
    else:
        cur_ref = refs[0]
        prev_ref = next_ref = None
        refs = refs[1:]
    cw_ref, g2_hi_ref, g2_lo_ref, gb_ref, qk_out, v_out, la_out = refs
    cur = cur_ref[0][:, 0:GLA_QKV_COLS]
    if vertical:
        class _Slice:
            def __init__(self, ref):
                self.ref = ref

            def __getitem__(self, idx):
                return self.ref[idx][:, 0:GLA_QKV_COLS]
        get = _neighbours(True, width, tm, cur, _Slice(prev_ref), _Slice(next_ref))
    else:
        get = _neighbours(False, width, tm, cur, None, None)
    acc = None
    for dr in ((-1, 0, 1) if vertical else (0,)):
        for dc in (-1, 0, 1):
            t = get(dr, dc) * cw_ref[(dr + 1) * 3 + (dc + 1):(dr + 1) * 3 + (dc + 1) + 1, :]
            acc = t if acc is None else acc + t
    qkv = _silu(acc)
    kw = GLA_KEY_WIDTH
    lane = lax.broadcasted_iota(jnp.int32, (1, 2 * kw), 1)
    qk_out[0] = qkv[:, 0:2 * kw] * jnp.where(lane < kw, GLA_KEY_DIM ** -0.5, 1.0)
    v_out[0] = qkv[:, 2 * kw:]
    pgl = cur_ref[0][:, GLA_QKV_COLS + GLA_VAL_WIDTH:GLA_SEG]
    z = _mmp(_split(pgl, N_LORA), [g2_hi_ref[...], g2_lo_ref[...]][:N_LORA]) + gb_ref[...]
    la = _log_sigmoid(z) * (1.0 / GLA_TAU)
    la_out[0, 0] = la[:, 0:kw]
    la_out[1, 0] = la[:, kw:]


def _feat_gla(p, vertical, tm, consts):
    B, L, S = p.shape
    width = GRID_W if vertical else tm
    hb = tm // GRID_W
    nhb = L // GRID_W
    tile = pl.BlockSpec((1, tm, S), lambda b, i: (b, i, 0))
    if vertical:
        in_specs = [pl.BlockSpec((1, GRID_W, S), lambda b, i: (b, jnp.maximum(i * hb - 1, 0), 0)),
                    tile,
                    pl.BlockSpec((1, GRID_W, S), lambda b, i: (b, jnp.minimum((i + 1) * hb, nhb - 1), 0))]
        args = [p, p, p]
    else:
        in_specs = [tile]
        args = [p]
    in_specs += [_full(c.shape) for c in consts]
    kw, vw = GLA_KEY_WIDTH, GLA_VAL_WIDTH
    return pl.pallas_call(
        functools.partial(_feat_gla_body, vertical, width, tm),
        grid=(B, L // tm),
        in_specs=in_specs,
        out_specs=[pl.BlockSpec((1, tm, 2 * kw), lambda b, i: (b, i, 0)),
                   pl.BlockSpec((1, tm, vw), lambda b, i: (b, i, 0)),
                   pl.BlockSpec((2, 1, tm, kw), lambda b, i: (0, b, i, 0))],
        out_shape=[jax.ShapeDtypeStruct((B, L, 2 * kw), F32),
                   jax.ShapeDtypeStruct((B, L, vw), F32),
                   jax.ShapeDtypeStruct((2, B, L, kw), F32)],
        compiler_params=_cparams(("parallel", "arbitrary")),
        name="feat_gla",
    )(*args, *consts)


def _order_masks(n, sgn):
    ti = lax.broadcasted_iota(jnp.int32, (n, n), 0)
    si = lax.broadcasted_iota(jnp.int32, (n, n), 1)
    rel = (si - ti) * sgn
    return rel < 0, rel <= 0


def _rwkv_stages(d, c, r_ref, k_ref, v_ref, kk_ref, lw_ref, a_ref, ka_ref, s0_ref, y_ref, st_ref, s_scr):
    C = CHUNK

    @pl.when(c == 0)
    def _():
        s_scr[...] = s0_ref[0, d]

    sgn = 1 - 2 * d
    r, k, v, kk = r_ref[0], k_ref[0], v_ref[0], kk_ref[0]
    lw, a = lw_ref[0, 0], a_ref[0, 0]
    before, upto = _order_masks(C, sgn)
    cum = _mm_exact_l(upto.astype(BF16), lw, N_CUM)
    tot = cum[C - 1:C] if d == 0 else cum[0:1]
    p_in = jnp.exp(cum)
    p_ex = jnp.exp(cum - lw)
    p_inv = jnp.exp(-cum)
    p_rem = jnp.exp(tot - cum)
    p_all = jnp.exp(tot)
    bvec = kk * a
    kmod = k * (1.0 + (a - 1.0) * ka_ref[...])
    a_t = -kk * p_ex
    r_t = r * p_in
    b_t = bvec * p_inv
    k_t = kmod * p_inv
    b_p = bvec * p_rem
    k_p = kmod * p_rem
    yield

    P = 2 * RW_HEAD_DIM
    lane = lax.broadcasted_iota(jnp.int32, (C, P), 1)
    h0 = lane < RW_HEAD_DIM
    ri = lax.broadcasted_iota(jnp.int32, (P, P), 0)
    ci = lax.broadcasted_iota(jnp.int32, (P, P), 1)
    same = (ri // C) == (ci // C)
    rel = ((ci % C) - (ri % C)) * sgn
    strict = same & (rel < 0)
    eye = (ri == ci).astype(F32)

    def stack2(x):
        return jnp.concatenate([jnp.where(h0, x, 0.0), jnp.where(h0, 0.0, x)], axis=0)

    pairs = range(RW_HEADS // 2)
    sls = [slice(p * P, (p + 1) * P) for p in pairs]
    a_st = [stack2(a_t[:, sl]) for sl in sls]
    v_st = [stack2(v[:, sl]) for sl in sls]
    g = [_mm(jnp.concatenate([a_st[p], stack2(r_t[:, sls[p]])], axis=0),
             jnp.concatenate([stack2(b_t[:, sls[p]]), stack2(k_t[:, sls[p]])], axis=0), N_RW, tb=True)
         for p in pairs]
    yield
    nmat = [jnp.where(strict, g[p][0:P, 0:P], 0.0) for p in pairs]
    a_ak = [jnp.where(strict, g[p][0:P, P:], 0.0) for p in pairs]
    ri2 = lax.broadcasted_iota(jnp.int32, (P, 2 * P), 0)
    ci2 = lax.broadcasted_iota(jnp.int32, (P, 2 * P), 1)
    incl2 = ((ri2 // C) == ((ci2 % P) // C)) & ((((ci2 % C) - (ri2 % C)) * sgn) <= 0)
    a_r = [jnp.where(incl2, g[p][P:, :], 0.0) for p in pairs]
    akv = [_mm(a_ak[p], v_st[p], N_RW) for p in pairs]
    yield
    t = [eye + nmat[p] for p in pairs]
    npow = nmat
    for _ in range(max(C.bit_length() - 2, 0)):
        npow = [_mm(npow[p], npow[p], N_RW) for p in pairs]
        t = [t[p] + _mm(t[p], npow[p], N_RW) for p in pairs]
        yield
    x = [_mm(t[p], jnp.concatenate([a_st[p], akv[p]], axis=1), N_RW) for p in pairs]
    s = [s_scr[p] for p in pairs]
    ur = [_mm(jnp.concatenate([x[p][0:C, 0:P] + x[p][C:, 0:P], r_t[:, sls[p]]], axis=0), s[p], N_RW, tb=True)
          for p in pairs]
    yield
    u = [ur[p][0:C] + x[p][0:C, P:] + x[p][C:, P:] for p in pairs]
    y_st = [_mm(a_r[p], jnp.concatenate([stack2(u[p]), v_st[p]], axis=0), N_RW) for p in pairs]
    for p in pairs:
        y_ref[0, :, sls[p]] = ur[p][C:] + y_st[p][0:C] + y_st[p][C:]
    yield
    upd = [_mm(jnp.concatenate([u[p], v[:, sls[p]]], axis=0).T,
               jnp.concatenate([b_p[:, sls[p]], k_p[:, sls[p]]], axis=0), N_RW) for p in pairs]
    for p in pairs:
        s_new = s[p] * p_all[:, sls[p]] + jnp.where(same, upd[p], 0.0)
        s_scr[p] = s_new
        st_ref[0, d, p] = s_new


def _scan_body(*refs):
    rw_in = [refs[0:6], refs[6:12]]
    ka_ref, rs0_ref = refs[12:14]
    gl_in = [refs[14:17], refs[17:20]]
    hm_ref, gs0_ref = refs[20:22]
    y_refs, rst_ref, o_refs, gst_ref = refs[22:24], refs[24], refs[25:27], refs[27]
    rs_scr, gs_scr = refs[28:30]
    c = pl.program_id(1)
    chains = []
    for d in range(2):
        chains.append(_rwkv_stages(d, c, *rw_in[d], ka_ref, rs0_ref, y_refs[d], rst_ref, rs_scr.at[d]))
        chains.append(_gla_stages(d, c, *gl_in[d], hm_ref, gs0_ref, o_refs[d], gst_ref, gs_scr.at[d]))
    while chains:
        chains = [ch for ch in chains if next(ch, True) is None]


def _scan(r, k, v, kk, lw, a, ka, rs0, qk, gv, la, hm, gs0):
    B, L, W = r.shape
    kw, vw = GLA_KEY_WIDTH, GLA_VAL_WIDTH
    nc = L // CHUNK
    chunk = [lambda b, c: c, lambda b, c: nc - 1 - c]
    shared = lambda d, w: pl.BlockSpec((1, CHUNK, w), lambda b, c: (b, chunk[d](b, c), 0))
    perdir = lambda d, w: pl.BlockSpec((1, 1, CHUNK, w), lambda b, c: (d, b, chunk[d](b, c), 0))
    npair = RW_HEADS // 2
    P = 2 * RW_HEAD_DIM
    rspec = pl.BlockSpec((1, 2, npair, P, P), lambda b, c: (b, 0, 0, 0, 0))
    gspec = pl.BlockSpec((1, 2, vw, kw), lambda b, c: (b, 0, 0, 0))
    rw_specs = [[shared(d, W)] * 4 + [perdir(d, W)] * 2 for d in range(2)]
    gl_specs = [[shared(d, 2 * kw), shared(d, vw), perdir(d, kw)] for d in range(2)]
    seq = lambda w: jax.ShapeDtypeStruct((B, L, w), F32)
    return pl.pallas_call(
        _scan_body,
        grid=(B, nc),
        in_specs=rw_specs[0] + rw_specs[1] + [pl.BlockSpec((1, W), lambda b, c: (0, 0)), rspec]
                 + gl_specs[0] + gl_specs[1] + [pl.BlockSpec((kw, vw), lambda b, c: (0, 0)), gspec],
        out_specs=[shared(0, W), shared(1, W), rspec, shared(0, vw), shared(1, vw), gspec],
        out_shape=[seq(W), seq(W), jax.ShapeDtypeStruct((B, 2, npair, P, P), F32),
                   seq(vw), seq(vw), jax.ShapeDtypeStruct((B, 2, vw, kw), F32)],
        scratch_shapes=[pltpu.VMEM((2, npair, P, P), F32), pltpu.VMEM((2, vw, kw), F32)],
        compiler_params=_cparams(("parallel", "arbitrary")),
        name="scan",
    )(r, k, v, kk, lw, a, r, k, v, kk, lw, a, ka, rs0, qk, gv, la, qk, gv, la, hm, gs0)


def _gla_stages(d, c, qk_ref, v_ref, g_ref, hm_ref, s0_ref, o_ref, st_ref, s_scr):
    C = CHUNK
    SB = GLA_SUB
    H = GLA_HEADS

    @pl.when(c == 0)
    def _():
        s_scr[...] = s0_ref[0, d]

    sgn = 1 - 2 * d
    kw, vw = GLA_KEY_WIDTH, GLA_VAL_WIDTH
    q = qk_ref[0][:, 0:kw]
    k = qk_ref[0][:, kw:]
    v = v_ref[0]
    g = g_ref[0, 0]
    _, upto = _order_masks(C, sgn)
    b = _mm_exact_l(upto.astype(BF16), g, N_CUM)
    pos = lax.broadcasted_iota(jnp.int32, (C, 1), 0) * sgn + d * (C - 1)

    def b_at(p):
        return jnp.sum(jnp.where(pos == p, b, 0.0), axis=0, keepdims=True)

    NEG = -jnp.inf
    tot = b_at(C - 1)
    s = s_scr[...]
    inter = _mm(q * jnp.exp(b), s, N_GLA, tb=True)
    yield

    half = pos >= C // 2
    odd = (pos // SB) % 2 == 1
    r1 = b_at(C // 2 - 1)
    r2 = jnp.where(half, b_at(3 * SB - 1), b_at(SB - 1))
    q1 = q * jnp.exp(jnp.where(half, b - r1, NEG))
    k1 = k * jnp.exp(jnp.where(half, NEG, r1 - b))
    q2 = q * jnp.exp(jnp.where(odd, b - r2, NEG))
    k2 = k * jnp.exp(jnp.where(odd, NEG, r2 - b))
    lane_h = lax.broadcasted_iota(jnp.int32, (C, kw), 1) // GLA_KEY_DIM

    def stack_heads(x):
        return jnp.concatenate([jnp.where(lane_h == h, x, 0.0) for h in range(H)], axis=0)

    att1 = _mm(stack_heads(q1), k1, N_GLA, tb=True)
    att2 = _mm(stack_heads(q2), k2, N_GLA, tb=True)
    yield
    pr =(lax.broadcasted_iota(jnp.int32, (H * C, C), 0) % C) * sgn + d * (C - 1)
    pc = lax.broadcasted_iota(jnp.int32, (H * C, C), 1) * sgn + d * (C - 1)
    att = att1 + jnp.where((pr >= C // 2) == (pc >= C // 2), att2, 0.0)
    res = _mm(att, v, N_GLA)
    lane_hv = lax.broadcasted_iota(jnp.int32, (C, vw), 1) // GLA_VAL_DIM
    off = None
    for h in range(H):
        t = jnp.where(lane_hv == h, res[h * C:(h + 1) * C], 0.0)
        off = t if off is None else off + t
    yield

    hm = hm_ref[...]
    il = lax.broadcasted_iota(jnp.int32, (SB, 1), 0)
    diag = []
    for blk in range(C // SB):
        rs = slice(blk * SB, (blk + 1) * SB)
        bb, qb, kb, vb = b[rs], q[rs], k[rs], v[rs]
        pieces = []
        for j in range(SB):
            valid = (il - j) * sgn >= 0
            pieces.append(jnp.exp(jnp.where(valid, bb - bb[j:j + 1], NEG)) * qb * kb[j:j + 1])
        a = _mm_exact_r(jnp.concatenate(pieces, axis=0), hm, N_GLA_INTRA)
        acc = None
        for j in range(SB):
            t = a[j * SB:(j + 1) * SB] * vb[j:j + 1]
            acc = t if acc is None else acc + t
        diag.append(acc)
        yield
    o_ref[0] = inter + off + jnp.concatenate(diag, axis=0)
    yield

    upd = _mm(v.T, k * jnp.exp(tot - b), N_GLA)
    ri = lax.broadcasted_iota(jnp.int32, upd.shape, 0) // GLA_VAL_DIM
    ci = lax.broadcasted_iota(jnp.int32, upd.shape, 1) // GLA_KEY_DIM
    s_new = s * jnp.exp(tot) + jnp.where(ri == ci, upd, 0.0)
    s_scr[...] = s_new
    st_ref[0, d] = s_new


def _seg_norm(y, bd, dim, eps):
    mu = _mm_exact_r(y, bd, N_SEG) * (1.0 / dim)
    dlt = y - mu
    var = _mm_exact_r(dlt * dlt, bd, N_SEG) * (1.0 / dim)
    return dlt * lax.rsqrt(var + eps)


def _layer_norm(x, w, b):
    mu = jnp.mean(x, axis=-1, keepdims=True)
    dlt = x - mu
    var = jnp.mean(dlt * dlt, axis=-1, keepdims=True)
    return dlt * lax.rsqrt(var + LN_EPS) * w + b


def _merge_body(x_ref, mod_ref, yrw0_ref, yrw1_ref, bonus_ref, g_ref, ygla0_ref, ygla1_ref, og_ref,
                bd64_ref, bd128_ref, rgw_ref, rgb_ref, ggw_ref, ggb_ref, *refs):
    n = N_MERGE
    wrw = [r[...] for r in refs[0:n]]
    wgla = [r[...] for r in refs[n:2 * n]]
    wout = [r[...] for r in refs[2 * n:3 * n]]
    wgate = [r[...] for r in refs[3 * n:3 * n + N_PROJ]]
    ln_w_ref, ln_b_ref, x1_ref, h2p_ref = refs[3 * n + N_PROJ:]
    y = _seg_norm(yrw0_ref[0] + yrw1_ref[0], bd64_ref[...], RW_HEAD_DIM, RW_GN_EPS)
    y = (y * rgw_ref[...] + rgb_ref[...] + bonus_ref[0]) * g_ref[0]
    y_rw = _mmp(_split(y, n), wrw)
    y = _seg_norm(ygla0_ref[0] + ygla1_ref[0], bd128_ref[...], GLA_VAL_DIM, LN_EPS)
    y = (y * ggw_ref[...] + ggb_ref[...]) * _silu(og_ref[0])
    y_gla = _mmp(_split(y, n), wgla)
    h = x_ref[0] * (1.0 + mod_ref[0, 1:2, :]) + mod_ref[0, 0:1, :]
    gate = _sigmoid(_mmp(_split(h, N_PROJ), wgate))
    mixed = gate[:, 0:D_MODEL] * y_rw + gate[:, D_MODEL:] * y_gla
    mix = _mmp(_split(mixed, n), wout)
    g1 = mod_ref[0, 2:3, :]
    x1 = _layer_norm(ALPHA * x_ref[0] + g1 * mix, ln_w_ref[...], ln_b_ref[...])
    x1_ref[0] = x1
    h2p_ref[0] = _pack_bf16_pairs(_moe_input(x1, mod_ref))


def _merge(x, mod, yrw, bonus, g, ygla, p_gla, consts, weights, ln_w, ln_b, tm):
    B, L, D = x.shape
    W = RW_WIDTH
    tok = lambda w: pl.BlockSpec((1, tm, w), lambda b, i: (b, i, 0))
    wl = [w for ws in weights for w in ws]
    return pl.pallas_call(
        _merge_body,
        grid=(B, L // tm),
        in_specs=[tok(D), pl.BlockSpec((1, 6, D), lambda b, i: (b, 0, 0)), tok(W), tok(W), tok(W), tok(W),
                  tok(W), tok(W),
                  pl.BlockSpec((1, tm, W), lambda b, i: (b, i, GLA_QKV_COLS // W))]
                 + [_full(c.shape) for c in consts] + [_full(w.shape) for w in wl]
                 + [_full(ln_w.shape), _full(ln_b.shape)],
        out_specs=[tok(D), tok(D // 2)],
        out_shape=[jax.ShapeDtypeStruct((B, L, D), F32), jax.ShapeDtypeStruct((B, L, D // 2), jnp.int32)],
        compiler_params=_cparams(("parallel", "arbitrary")),
        name="merge",
    )(x, mod, *yrw, bonus, g, *ygla, p_gla, *consts, *wl, ln_w, ln_b)


def _moe_input(x1, mod_ref):
    return x1 * (1.0 + mod_ref[0, 4:5, :]) + mod_ref[0, 3:4, :]


def _router_body(x1_ref, mod_ref, *refs):
    n = N_ROUTER
    rt = [r[...] for r in refs[0:n]]
    bias_ref, e_out, g_out, cnt_out = refs[n:]
    tm = x1_ref.shape[0]
    E, G, PG = N_EXPERTS, N_GROUPS, N_EXPERTS // N_GROUPS
    h = _moe_input(x1_ref[...], mod_ref)
    logits = _mmp(rt, _split(h, n), tb=True)
    scores = _sigmoid(logits)
    sel = scores + bias_ref[:, 0:tm]
    NEG = -jnp.inf
    ip = lax.broadcasted_iota(jnp.int32, (PG, tm), 0)
    group_rows = []
    for gidx in range(G):
        sg = sel[gidx * PG:(gidx + 1) * PG]
        m1 = jnp.max(sg, axis=0, keepdims=True)
        first = jnp.min(jnp.where(sg == m1, ip, PG), axis=0, keepdims=True)
        m2 = jnp.max(jnp.where(ip == first, NEG, sg), axis=0, keepdims=True)
        group_rows.append(m1 + m2)
    gs = jnp.concatenate(group_rows, axis=0)
    gi = lax.broadcasted_iota(jnp.int32, (G, tm), 0)
    keep = jnp.zeros((G, tm), F32)
    for _ in range(TOPK_GROUPS):
        m = jnp.max(gs, axis=0, keepdims=True)
        idx = jnp.min(jnp.where(gs == m, gi, G), axis=0, keepdims=True)
        hit = gi == idx
        keep = jnp.where(hit, 1.0, keep)
        gs = jnp.where(hit, NEG, gs)
    cur = jnp.concatenate(
        [jnp.where(keep[gidx:gidx + 1] > 0.5, sel[gidx * PG:(gidx + 1) * PG], NEG) for gidx in range(G)],
        axis=0)
    ei = lax.broadcasted_iota(jnp.int32, (E, tm), 0)
    idxs, gates = [], []
    picked = jnp.zeros((E, tm), F32)
    for _ in range(TOP_K):
        m = jnp.max(cur, axis=0, keepdims=True)
        idx = jnp.min(jnp.where(cur == m, ei, E), axis=0, keepdims=True)
        hit = ei == idx
        idxs.append(idx)
        gates.append(jnp.sum(jnp.where(hit, scores, 0.0), axis=0, keepdims=True))
        cur = jnp.where(hit, NEG, cur)
        picked = jnp.where(hit, 1.0, picked)
    gate = jnp.concatenate(gates, axis=0)
    e_out[...] = jnp.concatenate(idxs, axis=0)
    g_out[...] = gate / jnp.sum(gate, axis=0, keepdims=True) * ROUTED_SCALE

    @pl.when(pl.program_id(0) == 0)
    def _():
        cnt_out[...] = jnp.zeros(cnt_out.shape, F32)

    part = picked[:, 0:128]
    for j in range(1, tm // 128):
        part = part + picked[:, j * 128:(j + 1) * 128]
    cnt_out[...] += part


def _router(x1f, mod, seq, router_t_parts, bias_b, tm):
    n, D = x1f.shape
    E = N_EXPERTS
    return pl.pallas_call(
        _router_body,
        grid=(n // tm,),
        in_specs=[pl.BlockSpec((tm, D), lambda i: (i, 0)),
                  pl.BlockSpec((1, 6, D), lambda i: (i * tm // seq, 0, 0))]
                 + [pl.BlockSpec((E, D), lambda i: (0, 0))] * len(router_t_parts)
                 + [pl.BlockSpec(bias_b.shape, lambda i: (0, 0))],
        out_specs=[pl.BlockSpec((TOP_K, tm), lambda i: (0, i))] * 2
                  + [pl.BlockSpec((E, 128), lambda i: (0, 0))],
        out_shape=[jax.ShapeDtypeStruct((TOP_K, n), jnp.int32), jax.ShapeDtypeStruct((TOP_K, n), F32),
                   jax.ShapeDtypeStruct((E, 128), F32)],
        compiler_params=_cparams(("arbitrary",)),
        name="router",
    )(x1f, mod, *router_t_parts, bias_b)


def _sc_permute_body(scatter, per_w, src_hbm, idx_hbm, out_hbm, idx_v, rows_v, sem_i, sem_o):
    wid = lax.axis_index("s") * SC_CORES + lax.axis_index("c")
    base = wid * per_w
    win = SC_WINDOW
    pltpu.sync_copy(idx_hbm.at[wid], idx_v)

    def linear(ref, j):
        return ref.at[pl.ds(base + j * win, win)]

    def indexed(ref, j):
        return ref.at[idx_v.at[j]]

    src_at, out_at = (linear, indexed) if scatter else (indexed, linear)

    @pl.loop(0, per_w // win, step=2)
    def _(j):
        i0 = pltpu.async_copy(src_at(src_hbm, j), rows_v.at[0], sem_i.at[0])
        i1 = pltpu.async_copy(src_at(src_hbm, j + 1), rows_v.at[1], sem_i.at[1])
        i0.wait()
        o0 = pltpu.async_copy(rows_v.at[0], out_at(out_hbm, j), sem_o.at[0])
        i1.wait()
        o1 = pltpu.async_copy(rows_v.at[1], out_at(out_hbm, j + 1), sem_o.at[1])
        o0.wait()
        o1.wait()


def _sc_permute(src, idx, scatter):
    nrows = idx.shape[0]
    D = src.shape[1]
    per_w = nrows // SC_WORKERS
    mesh = plsc.VectorSubcoreMesh(core_axis_name="c", subcore_axis_name="s",
                                  num_cores=SC_CORES, num_subcores=SC_SUBCORES)
    return pl.kernel(
        functools.partial(_sc_permute_body, scatter, per_w),
        out_type=jax.ShapeDtypeStruct((nrows, D), src.dtype),
        mesh=mesh,
        scratch_types=[pltpu.VMEM((per_w // SC_WINDOW, SC_WINDOW), jnp.int32),
                       pltpu.VMEM((2, SC_WINDOW, D), src.dtype),
                       pltpu.SemaphoreType.DMA((2,)), pltpu.SemaphoreType.DMA((2,))],
        name="sc_scatter" if scatter else "sc_gather",
    )(src, idx.reshape(SC_WORKERS, per_w // SC_WINDOW, SC_WINDOW))


def _experts_body(ib_ref, ie_ref, lo_ref, hi_ref, x_ref, wgu_ref, wd_ref, y_ref, wgu_bf, wd_bf):
    i = pl.program_id(0)
    prev = jnp.maximum(i - 1, 0)
    R = ROW_BLOCK

    @pl.when(hi_ref[i] > lo_ref[i])
    def _():
        @pl.when((i == 0) | (ie_ref[i] != ie_ref[prev]))
        def _():
            wgu_bf[...] = wgu_ref[0].astype(BF16)
            wd_bf[...] = wd_ref[0].astype(BF16)

        F = EXPERT_DIM
        H = R // 2
        lo, hi, base = lo_ref[i], hi_ref[i], ib_ref[i] * R
        first = (i == 0) | (ib_ref[i] != ib_ref[prev])
        need = [(lo < base + (h + 1) * H) & (hi > base + h * H) for h in range(2)]

        def run(which):
            rows = [slice(h * H, (h + 1) * H) for h in which]
            xb = [_unpack_bf16_pairs(x_ref[rs, :]).astype(BF16) for rs in rows]
            gu = [_dot(t, wgu_bf[...]) for t in xb]
            act = [(_silu(t[:, 0:F]) * t[:, F:]).astype(BF16) for t in gu]
            yb = [_pack_bf16_pairs(_dot(t, wd_bf[...])) for t in act]
            for h, rs, t in zip(which, rows, yb):
                g = base + h * H + lax.broadcasted_iota(jnp.int32, (H, 1), 0)
                mine = (g >= lo) & (g < hi)
                keep = jnp.where(first, 0, y_ref[rs, :])
                y_ref[rs, :] = jnp.where(mine, t, keep)

        pl.when(need[0] & need[1])(lambda: run((0, 1)))
        pl.when(need[0] & jnp.logical_not(need[1]))(lambda: run((0,)))
        pl.when(jnp.logical_not(need[0]) & need[1])(lambda: run((1,)))


def _experts(item_blk, item_e, row_lo, row_hi, xs, w_gate_up, w_down):
    nrows, DP = xs.shape
    D = 2 * DP
    R = ROW_BLOCK
    F2 = w_gate_up.shape[2]
    grid_spec = pltpu.PrefetchScalarGridSpec(
        num_scalar_prefetch=4,
        grid=(item_blk.shape[0],),
        in_specs=[pl.BlockSpec((R, DP), lambda i, ib, ie, lo, hi: (ib[i], 0)),
                  pl.BlockSpec((1, D, F2), lambda i, ib, ie, lo, hi: (ie[i], 0, 0)),
                  pl.BlockSpec((1, F2 // 2, D), lambda i, ib, ie, lo, hi: (ie[i], 0, 0))],
        out_specs=pl.BlockSpec((R, DP), lambda i, ib, ie, lo, hi: (ib[i], 0)),
        scratch_shapes=[pltpu.VMEM((D, F2), BF16), pltpu.VMEM((F2 // 2, D), BF16)],
    )
    return pl.pallas_call(
        _experts_body,
        grid_spec=grid_spec,
        out_shape=jax.ShapeDtypeStruct((nrows, DP), jnp.int32),
        compiler_params=_cparams(("arbitrary",)),
        name="experts",
    )(item_blk, item_e, row_lo, row_hi, xs, w_gate_up, w_down)


def _final_body(yg_ref, x1_ref, mod_ref, gate_ref, *refs):
    n = N_SHARED
    sgu = [r[...] for r in refs[0:n]]
    sd = [r[...] for r in refs[n:2 * n]]
    ln_w_ref, ln_b_ref, o_ref = refs[2 * n:]
    tm = x1_ref.shape[1]
    gate = gate_ref[0]
    routed = None
    for kk in range(TOP_K):
        t = _unpack_bf16_pairs(yg_ref[kk * tm:(kk + 1) * tm, :]) * gate[:, kk:kk + 1]
        routed = t if routed is None else routed + t
    h2 = _moe_input(x1_ref[0], mod_ref)
    F = sgu[0].shape[1] // 2
    gu = _mmp(_split(h2, n), sgu)
    act = _silu(gu[:, 0:F]) * gu[:, F:]
    shared = _mmp(_split(act, n), sd)
    g2 = mod_ref[0, 5:6, :]
    o_ref[0] = _layer_norm(ALPHA * x1_ref[0] + g2 * (routed + shared), ln_w_ref[...], ln_b_ref[...])


def _final(yg, x1, mod, gate, sgu, sd, ln_w, ln_b, tm):
    B, L, D = x1.shape
    ni = L // tm
    rows = tm * TOP_K
    tok = lambda w: pl.BlockSpec((1, tm, w), lambda b, i: (b, i, 0))
    ws = list(sgu) + list(sd)
    return pl.pallas_call(
        _final_body,
        grid=(B, ni),
        in_specs=[pl.BlockSpec((rows, D // 2), lambda b, i: (b * ni + i, 0)),
                  tok(D), pl.BlockSpec((1, 6, D), lambda b, i: (b, 0, 0)), tok(TOP_K)]
                 + [_full(w.shape) for w in ws] + [_full(ln_w.shape), _full(ln_b.shape)],
        out_specs=tok(D),
        out_shape=jax.ShapeDtypeStruct((B, L, D), F32),
        compiler_params=_cparams(("parallel", "arbitrary")),
        name="final",
    )(yg, x1, mod, gate, *ws, ln_w, ln_b)


def _dispatch_plan(experts_t, sizes, tm):
    R, E = ROW_BLOCK, N_EXPERTS
    K, n = experts_t.shape
    nk = K * n
    assert nk % R == 0 and nk % SC_ROW_ALIGN == 0
    tok = jnp.broadcast_to(jnp.arange(n, dtype=jnp.int32)[None, :], (K, n))
    slot = jnp.arange(K, dtype=jnp.int32)[:, None]
    aid = (tok // tm) * (tm * K) + slot * tm + tok % tm
    id_bits = (nk - 1).bit_length()
    assert E << id_bits < 2 ** 31
    row_id = lax.sort((experts_t << id_bits | aid).reshape(-1)) & ((1 << id_bits) - 1)
    row_tok = (row_id // (tm * K)) * tm + row_id % tm

    start = jnp.cumsum(sizes) - sizes
    first_blk = start // R
    tiles = jnp.where(sizes > 0, (start + sizes - 1) // R - first_blk + 1, 0)
    t_end = jnp.cumsum(tiles)
    nitems = nk // R + E - 1
    it = jnp.arange(nitems, dtype=jnp.int32)
    live = it < t_end[-1]
    itc = jnp.minimum(it, t_end[-1] - 1)
    item_e = jnp.minimum(jnp.searchsorted(t_end, itc, side='right'), E - 1).astype(jnp.int32)
    item_blk = (first_blk[item_e] + itc - (t_end - tiles)[item_e]).astype(jnp.int32)
    row_lo = jnp.where(live, start[item_e], 0).astype(jnp.int32)
    row_hi = jnp.where(live, (start + sizes)[item_e], 0).astype(jnp.int32)
    return row_tok, row_id, item_blk, item_e, row_lo, row_hi


def _block_diag_ones(n, blk):
    i = jnp.arange(n) // blk
    return (i[:, None] == i[None, :]).astype(BF16)


def kernel(x, c, ctx, c_ctx, w_ada, b_ada, w_in, rw_mu, rw_w0, rw_w2, rw_a0, rw_a2, rw_g2, rw_k_k, rw_k_a,
           rw_r_k, rw_gn_w, rw_gn_b, gla_conv, gla_g2, gla_gb, gla_gn_w, gla_gn_b, w_br_rw, w_br_gla, w_out,
           ln1_w, ln1_b, router, router_bias, w_gate_up, w_down, sh_gate_up, sh_down, ln2_w, ln2_b):
    B, L, D = x.shape
    CT = ctx.shape[1]
    l = 0
    W = RW_WIDTH
    row = lambda t: t.reshape(1, -1)

    rows = -(-(B + 1) // 8) * 8
    cc = jnp.zeros((rows, D), F32).at[:B].set(c).at[B].set(c_ctx)
    mod = _ada(cc, w_ada[l], b_ada[l])
    mod_lat = mod[:B].reshape(B, 6, D)
    mod_ctx = jnp.broadcast_to(mod[B].reshape(1, 6, D), (B, 6, D))

    w = w_in[l]
    g0 = RW_COLS
    w_rw = jnp.pad(w[:, :RW_COLS], ((0, 0), (0, RW_SEG - RW_COLS)))
    w_gla = jnp.concatenate([w[:, g0:g0 + GLA_QKV_COLS],
                             w[:, g0 + GLA_QKV_COLS + GLA_GATE_RANK:g0 + GLA_COLS],
                             w[:, g0 + GLA_QKV_COLS:g0 + GLA_QKV_COLS + GLA_GATE_RANK],
                             jnp.zeros((D, GLA_SEG - GLA_COLS), F32)], axis=1)
    w_gate = w[:, MIX_COLS:]
    tm_p = min(PROJ_TILE, L)
    p_rw = _proj(x, mod_lat, w_rw, tm_p)
    p_gla = _proj(x, mod_lat, w_gla, tm_p)
    pc_rw = _proj(ctx, mod_ctx, w_rw, CT)
    pc_gla = _proj(ctx, mod_ctx, w_gla, CT)

    mu = jnp.pad(rw_mu[l], (0, RW_SEG - RW_COLS)).reshape(1, -1)
    lora = jnp.zeros((256, 5 * W), F32)
    e1 = RW_DECAY_RANK
    e2 = e1 + RW_AAA_RANK
    e3 = e2 + RW_GATE_RANK
    for d in range(2):
        lora = lora.at[0:e1, d * W:(d + 1) * W].set(rw_w2[l, d])
        lora = lora.at[e1:e2, (2 + d) * W:(3 + d) * W].set(rw_a2[l, d])
    lora = lora.at[e2:e3, 4 * W:].set(rw_g2[l])
    lora_p = (_split(lora, N_LORA) + [jnp.zeros_like(lora, BF16)])[:2]
    bd64 = _block_diag_ones(W, RW_HEAD_DIM)
    rw_consts = [mu, lora_p[0], lora_p[1], rw_w0[l].reshape(1, -1), rw_a0[l].reshape(1, -1),
                 row(rw_k_k[l]), row(rw_k_a[l]), row(rw_r_k[l]), bd64]
    tm_f = min(FEAT_TILE, L)
    r, k, v, kkn, g, bonus, lwd, a = _feat_rw(p_rw, True, tm_f, rw_consts)
    rc, kc, vc, kknc, _, _, lwdc, ac = _feat_rw(pc_rw, False, CT, rw_consts)

    g2 = jnp.zeros((GLA_SEG - GLA_QKV_COLS - GLA_VAL_WIDTH, 2 * GLA_KEY_WIDTH), F32)
    g2 = g2.at[:GLA_GATE_RANK].set(jnp.concatenate([gla_g2[l, 0], gla_g2[l, 1]], axis=1))
    g2_p = (_split(g2, N_LORA) + [jnp.zeros_like(g2, BF16)])[:2]
    cw = jnp.pad(gla_conv[l].reshape(9, GLA_QKV_COLS), ((0, 7), (0, 0)))
    gla_consts = [cw, g2_p[0], g2_p[1], gla_gb[l].reshape(1, -1)]
    qk, vg, la = _feat_gla(p_gla, True, tm_f, gla_consts)
    qkc, vgc, lac = _feat_gla(pc_gla, False, CT, gla_consts)

    ka = row(rw_k_a[l])
    P = 2 * RW_HEAD_DIM
    s0 = jnp.zeros((B, 2, RW_HEADS // 2, P, P), F32)
    hi = jnp.arange(GLA_KEY_WIDTH) // GLA_KEY_DIM
    hj = jnp.arange(GLA_VAL_WIDTH) // GLA_VAL_DIM
    hm = (hi[:, None] == hj[None, :]).astype(BF16)
    g0s = jnp.zeros((B, 2, GLA_VAL_WIDTH, GLA_KEY_WIDTH), F32)
    _, _, s_ctx, _, _, gs_ctx = _scan(rc, kc, vc, kknc, lwdc, ac, ka, s0, qkc, vgc, lac, hm, g0s)
    yr0, yr1, _, yg0, yg1, _ = _scan(r, k, v, kkn, lwd, a, ka, s_ctx, qk, vg, la, hm, gs_ctx)
    y_rw, y_gla = (yr0, yr1), (yg0, yg1)

    bd128 = _block_diag_ones(GLA_VAL_WIDTH, GLA_VAL_DIM)
    m_consts = [bd64, bd128, row(rw_gn_w[l]), row(rw_gn_b[l]), row(gla_gn_w[l]), row(gla_gn_b[l])]
    m_weights = [_split(w_br_rw[l], N_MERGE), _split(w_br_gla[l], N_MERGE), _split(w_out[l], N_MERGE),
                 _split(w_gate, N_PROJ)]
    x1, h2p = _merge(x, mod_lat, y_rw, bonus, g, y_gla, p_gla, m_consts, m_weights,
                     row(ln1_w[l]), row(ln1_b[l]), min(MERGE_TILE, L))

    n = B * L
    tm_r = min(ROUTER_TILE, L)
    bias_b = jnp.broadcast_to(router_bias[l].reshape(-1, 1), (N_EXPERTS, tm_r))
    e_t, g_t, cnt = _router(x1.reshape(n, D), mod_lat, L, _split(router[l].T, N_ROUTER), bias_b, tm_r)
    sizes = jnp.sum(cnt, axis=1).astype(jnp.int32)
    tm_c = min(COMBINE_TILE, L)
    row_tok, row_id, item_blk, item_e, row_lo, row_hi = _dispatch_plan(e_t, sizes, tm_c)
    xs = _sc_permute(h2p.reshape(n, D // 2), row_tok, scatter=False)
    y = _experts(item_blk, item_e, row_lo, row_hi, xs, w_gate_up[l], w_down[l])
    yg = _sc_permute(y, row_id, scatter=True)
    gate_tok = g_t.T.reshape(B, L, TOP_K)
    return _final(yg, x1, mod_lat, gate_tok, _split(sh_gate_up[l], N_SHARED),
                  _split(sh_down[l], N_SHARED), row(ln2_w[l]), row(ln2_b[l]), tm_c)
```

```python
import functools
import math

import jax
import jax.numpy as jnp
from jax import lax
from jax.experimental import pallas as pl
from jax.experimental.pallas import tpu as pltpu
from jax.experimental.pallas import tpu_sc as plsc

F32 = jnp.float32
BF16 = jnp.bfloat16

D_MODEL = 1024
GRID_W = 64
RW_WIDTH = 512
RW_HEADS = 8
RW_HEAD_DIM = 64
RW_DECAY_RANK = 32
RW_AAA_RANK = 32
RW_GATE_RANK = 96
RW_GN_EPS = 64e-5
RW_COLS = 1696
RW_SEG = 1792
GLA_HEADS = 4
GLA_KEY_WIDTH = 256
GLA_VAL_WIDTH = 512
GLA_KEY_DIM = 64
GLA_VAL_DIM = 128
GLA_GATE_RANK = 16
GLA_TAU = 16.0
GLA_QKV_COLS = 1024
GLA_COLS = 1552
GLA_SEG = 1664
MIX_COLS = RW_COLS + GLA_COLS
N_EXPERTS = 256
TOP_K = 8
N_GROUPS = 8
TOPK_GROUPS = 4
EXPERT_DIM = 256
ROUTED_SCALE = 2.5
LN_EPS = 1e-5
DEPTH = 1
ALPHA = (2 * DEPTH) ** 0.25

CHUNK = 64
GLA_SUB = 16
ROW_BLOCK = 512
PROJ_TILE = 512
FEAT_TILE = 512
MERGE_TILE = 512
ROUTER_TILE = 256
COMBINE_TILE = 256
VMEM_LIMIT = 48 * 1024 * 1024
SC_CORES = 2
SC_SUBCORES = 16
SC_WORKERS = SC_CORES * SC_SUBCORES
SC_WINDOW = 32
SC_ROW_ALIGN = 2 * SC_WORKERS * SC_WINDOW

N_ADA = 3
N_PROJ = 1
N_LORA = 1
N_SEG = 2
N_CUM = 2
N_RW = 1
N_GLA = 1
N_GLA_INTRA = 1
N_MERGE = 1
N_ROUTER = 3
N_SHARED = 1


def _split(x, n):
    parts = []
    r = x
    for i in range(n):
        p = r.astype(BF16)
        parts.append(p)
        if i < n - 1:
            r = r - p.astype(F32)
    return parts


def _dot(a, b, ta=False, tb=False):
    dn = (((0 if ta else 1,), (1 if tb else 0,)), ((), ()))
    return lax.dot_general(a, b, dn, preferred_element_type=F32)


def _mmp(ap, bp, ta=False, tb=False):
    n = max(len(ap), len(bp))
    out = None
    for i in range(len(ap)):
        for j in range(len(bp)):
            if i + j <= n - 1:
                t = _dot(ap[i], bp[j], ta, tb)
                out = t if out is None else out + t
    return out


def _mm(a, b, n, ta=False, tb=False):
    return _mmp(_split(a, n), _split(b, n), ta, tb)


def _mm_exact_l(m_bf16, x, n):
    return _mmp([m_bf16], _split(x, n))


def _mm_exact_r(x, m_bf16, n):
    return _mmp(_split(x, n), [m_bf16])


def _pack_bf16_pairs(x):
    w = x.shape[1] // 2
    hi = lax.bitcast_convert_type(x[:, :w].astype(BF16).astype(F32), jnp.int32)
    lo = lax.bitcast_convert_type(x[:, w:].astype(BF16).astype(F32), jnp.int32)
    return hi | lax.shift_right_logical(lo, 16)


def _unpack_bf16_pairs(p):
    hi = lax.bitcast_convert_type(p & jnp.int32(-65536), F32)
    lo = lax.bitcast_convert_type(lax.shift_left(p, 16), F32)
    return jnp.concatenate([hi, lo], axis=1)


def _sigmoid(x):
    return 1.0 / (1.0 + jnp.exp(-x))


def _silu(x):
    return x * _sigmoid(x)


def _log_sigmoid(x):
    return jnp.minimum(x, 0.0) - jnp.log(1.0 + jnp.exp(-jnp.abs(x)))


def _cparams(sem):
    return pltpu.CompilerParams(dimension_semantics=sem, vmem_limit_bytes=VMEM_LIMIT)


def _full(shape):
    nd = len(shape)
    return pl.BlockSpec(shape, lambda *a: (0,) * nd)


def _ada_body(c_ref, w_ref, b_ref, o_ref):
    s = _silu(c_ref[...])
    o_ref[...] = _mm(s, w_ref[...], N_ADA) + b_ref[...]


def _ada(cc, w, b):
    rows, d = cc.shape
    n = w.shape[1] // d
    return pl.pallas_call(
        _ada_body,
        grid=(n,),
        in_specs=[pl.BlockSpec((rows, d), lambda j: (0, 0)),
                  pl.BlockSpec((d, d), lambda j: (0, j)),
                  pl.BlockSpec((1, d), lambda j: (0, j))],
        out_specs=pl.BlockSpec((rows, d), lambda j: (0, j)),
        out_shape=jax.ShapeDtypeStruct((rows, w.shape[1]), F32),
        compiler_params=_cparams(("arbitrary",)),
        name="ada",
    )(cc, w, b.reshape(1, -1))


def _proj_body(x_ref, mod_ref, *refs):
    w_refs, o_ref = refs[:-1], refs[-1]
    sh = mod_ref[0, 0:1, :]
    sc = mod_ref[0, 1:2, :]
    h = x_ref[0] * (1.0 + sc) + sh
    o_ref[0] = _mmp(_split(h, N_PROJ), [w[...] for w in w_refs])


def _proj(x, mod, w, tm):
    B, L, D = x.shape
    n = w.shape[1]
    wp = _split(w, N_PROJ)
    return pl.pallas_call(
        _proj_body,
        grid=(B, L // tm),
        in_specs=[pl.BlockSpec((1, tm, D), lambda b, i: (b, i, 0)),
                  pl.BlockSpec((1, 6, D), lambda b, i: (b, 0, 0))]
                 + [pl.BlockSpec((D, n), lambda b, i: (0, 0))] * N_PROJ,
        out_specs=pl.BlockSpec((1, tm, n), lambda b, i: (b, i, 0)),
        out_shape=jax.ShapeDtypeStruct((B, L, n), F32),
        compiler_params=_cparams(("parallel", "arbitrary")),
        name="proj",
    )(x, mod, *wp)


def _neighbours(vertical, width, tm, cur, prev_ref, next_ref):
    if vertical:
        i = pl.program_id(1)
        n = pl.num_programs(1)
        prev = jnp.where(i > 0, prev_ref[0], 0.0)
        nxt = jnp.where(i < n - 1, next_ref[0], 0.0)
        pad = jnp.zeros((8, cur.shape[1]), F32)
        ext = jnp.concatenate([pad, prev, cur, nxt, pad], axis=0)
        off = GRID_W + 8
    else:
        pad = jnp.zeros((8, cur.shape[1]), F32)
        ext = jnp.concatenate([pad, cur, pad], axis=0)
        off = 8
    col = lax.broadcasted_iota(jnp.int32, (tm, 1), 0) % width

    def get(dr, dc):
        s = off + GRID_W * dr + dc
        v = ext[s:s + tm]
        if dc == -1:
            v = jnp.where(col == 0, 0.0, v)
        elif dc == 1:
            v = jnp.where(col == width - 1, 0.0, v)
        return v

    return get


def _feat_rw_body(vertical, width, tm, *refs):
    if vertical:
        prev_ref, cur_ref, next_ref = refs[:3]
        refs = refs[3:]
    else:
        cur_ref = refs[0]
        prev_ref = next_ref = None
        refs = refs[1:]
    (mu_ref, lw_hi_ref, lw_lo_ref, w0_ref, a0_ref, kk_ref, ka_ref, rk_ref, bd_ref,
     r_out, k_out, v_out, kkn_out, g_out, bonus_out, lwd_out, a_out) = refs
    cur = cur_ref[0]
    get = _neighbours(vertical, width, tm, cur, prev_ref, next_ref)
    left, right = get(0, -1), get(0, 1)
    up, down = (get(-1, 0), get(1, 0)) if vertical else (left, right)
    l4 = lax.broadcasted_iota(jnp.int32, (1, cur.shape[1]), 1) % 4
    shifted = jnp.where(l4 == 0, left, jnp.where(l4 == 1, right, jnp.where(l4 == 2, up, down)))
    p = cur + mu_ref[...] * (shifted - cur)

    r = p[:, 0:RW_WIDTH]
    k = p[:, RW_WIDTH:2 * RW_WIDTH]
    v = p[:, 2 * RW_WIDTH:3 * RW_WIDTH]
    slab = p[:, 3 * RW_WIDTH:3 * RW_WIDTH + 256]
    ln = lax.broadcasted_iota(jnp.int32, (1, 256), 1)
    e1 = RW_DECAY_RANK
    e2 = e1 + RW_AAA_RANK
    e3 = e2 + RW_GATE_RANK
    slab = jnp.where(ln < e1, jnp.tanh(slab),
                     jnp.where(ln < e2, slab, jnp.where(ln < e3, _sigmoid(slab), 0.0)))
    lo = _mmp(_split(slab, N_LORA), [lw_hi_ref[...], lw_lo_ref[...]][:N_LORA])
    W = RW_WIDTH
    a_sum = None
    for d in range(2):
        z = w0_ref[:, d * W:(d + 1) * W] + lo[:, d * W:(d + 1) * W]
        lwd_out[d, 0] = -_sigmoid(z) * math.exp(-0.5)
        a = _sigmoid(a0_ref[:, d * W:(d + 1) * W] + lo[:, (2 + d) * W:(3 + d) * W])
        a_out[d, 0] = a
        a_sum = a if a_sum is None else a_sum + a
    g_out[0] = lo[:, 4 * W:5 * W]
    bd = bd_ref[...]
    kk = k * kk_ref[...]
    ss = _mm_exact_r(kk * kk, bd, N_SEG)
    kkn_out[0] = kk / jnp.maximum(jnp.sqrt(ss), 1e-12)
    kmod_sum = k * (2.0 + (a_sum - 2.0) * ka_ref[...])
    bonus_out[0] = _mm_exact_r(r * kmod_sum * rk_ref[...], bd, N_SEG) * v
    r_out[0] = r
    k_out[0] = k
    v_out[0] = v


def _feat_rw(p, vertical, tm, consts):
    B, L, S = p.shape
    width = GRID_W if vertical else tm
    W = RW_WIDTH
    hb = tm // GRID_W
    nhb = L // GRID_W
    tile = pl.BlockSpec((1, tm, S), lambda b, i: (b, i, 0))
    if vertical:
        in_specs = [pl.BlockSpec((1, GRID_W, S), lambda b, i: (b, jnp.maximum(i * hb - 1, 0), 0)),
                    tile,
                    pl.BlockSpec((1, GRID_W, S), lambda b, i: (b, jnp.minimum((i + 1) * hb, nhb - 1), 0))]
        args = [p, p, p]
    else:
        in_specs = [tile]
        args = [p]
    in_specs += [_full(c.shape) for c in consts]
    o1 = pl.BlockSpec((1, tm, W), lambda b, i: (b, i, 0))
    o2 = pl.BlockSpec((2, 1, tm, W), lambda b, i: (0, b, i, 0))
    s1 = jax.ShapeDtypeStruct((B, L, W), F32)
    s2 = jax.ShapeDtypeStruct((2, B, L, W), F32)
    return pl.pallas_call(
        functools.partial(_feat_rw_body, vertical, width, tm),
        grid=(B, L // tm),
        in_specs=in_specs,
        out_specs=[o1] * 6 + [o2] * 2,
        out_shape=[s1] * 6 + [s2] * 2,
        compiler_params=_cparams(("parallel", "arbitrary")),
        name="feat_rw",
    )(*args, *consts)


def _feat_gla_body(vertical, width, tm, *refs):
    if vertical:
        prev_ref, cur_ref, next_ref = refs[:3]
        refs = refs[3:]
    else:
        cur_ref = refs[0]
        prev_ref = next_ref = None
        refs = refs[1:]
    cw_ref, g2_hi_ref, g2_lo_ref, gb_ref, qk_out, v_out, la_out = refs
    cur = cur_ref[0][:, 0:GLA_QKV_COLS]
    if vertical:
        class _Slice:
            def __init__(self, ref):
                self.ref = ref

            def __getitem__(self, idx):
                return self.ref[idx][:, 0:GLA_QKV_COLS]
        get = _neighbours(True, width, tm, cur, _Slice(prev_ref), _Slice(next_ref))
    else:
        get = _neighbours(False, width, tm, cur, None, None)
    acc = None
    for dr in ((-1, 0, 1) if vertical else (0,)):
        for dc in (-1, 0, 1):
            t = get(dr, dc) * cw_ref[(dr + 1) * 3 + (dc + 1):(dr + 1) * 3 + (dc + 1) + 1, :]
            acc = t if acc is None else acc + t
    qkv = _silu(acc)
    kw = GLA_KEY_WIDTH
    lane = lax.broadcasted_iota(jnp.int32, (1, 2 * kw), 1)
    qk_out[0] = qkv[:, 0:2 * kw] * jnp.where(lane < kw, GLA_KEY_DIM ** -0.5, 1.0)
    v_out[0] = qkv[:, 2 * kw:]
    pgl = cur_ref[0][:, GLA_QKV_COLS + GLA_VAL_WIDTH:GLA_SEG]
    z = _mmp(_split(pgl, N_LORA), [g2_hi_ref[...], g2_lo_ref[...]][:N_LORA]) + gb_ref[...]
    la = _log_sigmoid(z) * (1.0 / GLA_TAU)
    la_out[0, 0] = la[:, 0:kw]
    la_out[1, 0] = la[:, kw:]


def _feat_gla(p, vertical, tm, consts):
    B, L, S = p.shape
    width = GRID_W if vertical else tm
    hb = tm // GRID_W
    nhb = L // GRID_W
    tile = pl.BlockSpec((1, tm, S), lambda b, i: (b, i, 0))
    if vertical:
        in_specs = [pl.BlockSpec((1, GRID_W, S), lambda b, i: (b, jnp.maximum(i * hb - 1, 0), 0)),
                    tile,
                    pl.BlockSpec((1, GRID_W, S), lambda b, i: (b, jnp.minimum((i + 1) * hb, nhb - 1), 0))]
        args = [p, p, p]
    else:
        in_specs = [tile]
        args = [p]
    in_specs += [_full(c.shape) for c in consts]
    kw, vw = GLA_KEY_WIDTH, GLA_VAL_WIDTH
    return pl.pallas_call(
        functools.partial(_feat_gla_body, vertical, width, tm),
        grid=(B, L // tm),
        in_specs=in_specs,
        out_specs=[pl.BlockSpec((1, tm, 2 * kw), lambda b, i: (b, i, 0)),
                   pl.BlockSpec((1, tm, vw), lambda b, i: (b, i, 0)),
                   pl.BlockSpec((2, 1, tm, kw), lambda b, i: (0, b, i, 0))],
        out_shape=[jax.ShapeDtypeStruct((B, L, 2 * kw), F32),
                   jax.ShapeDtypeStruct((B, L, vw), F32),
                   jax.ShapeDtypeStruct((2, B, L, kw), F32)],
        compiler_params=_cparams(("parallel", "arbitrary")),
        name="feat_gla",
    )(*args, *consts)


def _order_masks(n, sgn):
    ti = lax.broadcasted_iota(jnp.int32, (n, n), 0)
    si = lax.broadcasted_iota(jnp.int32, (n, n), 1)
    rel = (si - ti) * sgn
    return rel < 0, rel <= 0


def _rwkv_stages(d, c, r_ref, k_ref, v_ref, kk_ref, lw_ref, a_ref, ka_ref, s0_ref, y_ref, st_ref, s_scr):
    C = CHUNK

    @pl.when(c == 0)
    def _():
        s_scr[...] = s0_ref[0, d]

    sgn = 1 - 2 * d
    r, k, v, kk = r_ref[0], k_ref[0], v_ref[0], kk_ref[0]
    lw, a = lw_ref[0, 0], a_ref[0, 0]
    before, upto = _order_masks(C, sgn)
    cum = _mm_exact_l(upto.astype(BF16), lw, N_CUM)
    tot = cum[C - 1:C] if d == 0 else cum[0:1]
    p_in = jnp.exp(cum)
    p_ex = jnp.exp(cum - lw)
    p_inv = jnp.exp(-cum)
    p_rem = jnp.exp(tot - cum)
    p_all = jnp.exp(tot)
    bvec = kk * a
    kmod = k * (1.0 + (a - 1.0) * ka_ref[...])
    a_t = -kk * p_ex
    r_t = r * p_in
    b_t = bvec * p_inv
    k_t = kmod * p_inv
    b_p = bvec * p_rem
    k_p = kmod * p_rem
    yield

    P = 2 * RW_HEAD_DIM
    lane = lax.broadcasted_iota(jnp.int32, (C, P), 1)
    h0 = lane < RW_HEAD_DIM
    ri = lax.broadcasted_iota(jnp.int32, (P, P), 0)
    ci = lax.broadcasted_iota(jnp.int32, (P, P), 1)
    same = (ri // C) == (ci // C)
    rel = ((ci % C) - (ri % C)) * sgn
    strict = same & (rel < 0)
    eye = (ri == ci).astype(F32)

    def stack2(x):
        return jnp.concatenate([jnp.where(h0, x, 0.0), jnp.where(h0, 0.0, x)], axis=0)

    pairs = range(RW_HEADS // 2)
    sls = [slice(p * P, (p + 1) * P) for p in pairs]
    a_st = [stack2(a_t[:, sl]) for sl in sls]
    v_st = [stack2(v[:, sl]) for sl in sls]
    g = [_mm(jnp.concatenate([a_st[p], stack2(r_t[:, sls[p]])], axis=0),
             jnp.concatenate([stack2(b_t[:, sls[p]]), stack2(k_t[:, sls[p]])], axis=0), N_RW, tb=True)
         for p in pairs]
    yield
    nmat = [jnp.where(strict, g[p][0:P, 0:P], 0.0) for p in pairs]
    a_ak = [jnp.where(strict, g[p][0:P, P:], 0.0) for p in pairs]
    ri2 = lax.broadcasted_iota(jnp.int32, (P, 2 * P), 0)
    ci2 = lax.broadcasted_iota(jnp.int32, (P, 2 * P), 1)
    incl2 = ((ri2 // C) == ((ci2 % P) // C)) & ((((ci2 % C) - (ri2 % C)) * sgn) <= 0)
    a_r = [jnp.where(incl2, g[p][P:, :], 0.0) for p in pairs]
    akv = [_mm(a_ak[p], v_st[p], N_RW) for p in pairs]
    yield
    t = [eye + nmat[p] for p in pairs]
    npow = nmat
    for _ in range(max(C.bit_length() - 2, 0)):
        npow = [_mm(npow[p], npow[p], N_RW) for p in pairs]
        t = [t[p] + _mm(t[p], npow[p], N_RW) for p in pairs]
        yield
    x = [_mm(t[p], jnp.concatenate([a_st[p], akv[p]], axis=1), N_RW) for p in pairs]
    s = [s_scr[p] for p in pairs]
    ur = [_mm(jnp.concatenate([x[p][0:C, 0:P] + x[p][C:, 0:P], r_t[:, sls[p]]], axis=0), s[p], N_RW, tb=True)
          for p in pairs]
    yield
    u = [ur[p][0:C] + x[p][0:C, P:] + x[p][C:, P:] for p in pairs]
    y_st = [_mm(a_r[p], jnp.concatenate([stack2(u[p]), v_st[p]], axis=0), N_RW) for p in pairs]
    for p in pairs:
        y_ref[0, :, sls[p]] = ur[p][C:] + y_st[p][0:C] + y_st[p][C:]
    yield
    upd = [_mm(jnp.concatenate([u[p], v[:, sls[p]]], axis=0).T,
               jnp.concatenate([b_p[:, sls[p]], k_p[:, sls[p]]], axis=0), N_RW) for p in pairs]
    for p in pairs:
        s_new = s[p] * p_all[:, sls[p]] + jnp.where(same, upd[p], 0.0)
        s_scr[p] = s_new
        st_ref[0, d, p] = s_new


def _scan_body(*refs):
    rw_in = [refs[0:6], refs[6:12]]
    ka_ref, rs0_ref = refs[12:14]
    gl_in = [refs[14:17], refs[17:20]]
    hm_ref, gs0_ref = refs[20:22]
    y_refs, rst_ref, o_refs, gst_ref = refs[22:24], refs[24], refs[25:27], refs[27]
    rs_scr, gs_scr = refs[28:30]
    c = pl.program_id(1)
    chains = []
    for d in range(2):
        chains.append(_rwkv_stages(d, c, *rw_in[d], ka_ref, rs0_ref, y_refs[d], rst_ref, rs_scr.at[d]))
        chains.append(_gla_stages(d, c, *gl_in[d], hm_ref, gs0_ref, o_refs[d], gst_ref, gs_scr.at[d]))
    while chains:
        chains = [ch for ch in chains if next(ch, True) is None]


def _scan(r, k, v, kk, lw, a, ka, rs0, qk, gv, la, hm, gs0):
    B, L, W = r.shape
    kw, vw = GLA_KEY_WIDTH, GLA_VAL_WIDTH
    nc = L // CHUNK
    chunk = [lambda b, c: c, lambda b, c: nc - 1 - c]
    shared = lambda d, w: pl.BlockSpec((1, CHUNK, w), lambda b, c: (b, chunk[d](b, c), 0))
    perdir = lambda d, w: pl.BlockSpec((1, 1, CHUNK, w), lambda b, c: (d, b, chunk[d](b, c), 0))
    npair = RW_HEADS // 2
    P = 2 * RW_HEAD_DIM
    rspec = pl.BlockSpec((1, 2, npair, P, P), lambda b, c: (b, 0, 0, 0, 0))
    gspec = pl.BlockSpec((1, 2, vw, kw), lambda b, c: (b, 0, 0, 0))
    rw_specs = [[shared(d, W)] * 4 + [perdir(d, W)] * 2 for d in range(2)]
    gl_specs = [[shared(d, 2 * kw), shared(d, vw), perdir(d, kw)] for d in range(2)]
    seq = lambda w: jax.ShapeDtypeStruct((B, L, w), F32)
    return pl.pallas_call(
        _scan_body,
        grid=(B, nc),
        in_specs=rw_specs[0] + rw_specs[1] + [pl.BlockSpec((1, W), lambda b, c: (0, 0)), rspec]
                 + gl_specs[0] + gl_specs[1] + [pl.BlockSpec((kw, vw), lambda b, c: (0, 0)), gspec],
        out_specs=[shared(0, W), shared(1, W), rspec, shared(0, vw), shared(1, vw), gspec],
        out_shape=[seq(W), seq(W), jax.ShapeDtypeStruct((B, 2, npair, P, P), F32),
                   seq(vw), seq(vw), jax.ShapeDtypeStruct((B, 2, vw, kw), F32)],
        scratch_shapes=[pltpu.VMEM((2, npair, P, P), F32), pltpu.VMEM((2, vw, kw), F32)],
        compiler_params=_cparams(("parallel", "arbitrary")),
        name="scan",
    )(r, k, v, kk, lw, a, r, k, v, kk, lw, a, ka, rs0, qk, gv, la, qk, gv, la, hm, gs0)


def _gla_stages(d, c, qk_ref, v_ref, g_ref, hm_ref, s0_ref, o_ref, st_ref, s_scr):
    C = CHUNK
    SB = GLA_SUB
    H = GLA_HEADS

    @pl.when(c == 0)
    def _():
        s_scr[...] = s0_ref[0, d]

    sgn = 1 - 2 * d
    kw, vw = GLA_KEY_WIDTH, GLA_VAL_WIDTH
    q = qk_ref[0][:, 0:kw]
    k = qk_ref[0][:, kw:]
    v = v_ref[0]
    g = g_ref[0, 0]
    _, upto = _order_masks(C, sgn)
    b = _mm_exact_l(upto.astype(BF16), g, N_CUM)
    pos = lax.broadcasted_iota(jnp.int32, (C, 1), 0) * sgn + d * (C - 1)

    def b_at(p):
        return jnp.sum(jnp.where(pos == p, b, 0.0), axis=0, keepdims=True)

    NEG = -jnp.inf
    tot = b_at(C - 1)
    s = s_scr[...]
    inter = _mm(q * jnp.exp(b), s, N_GLA, tb=True)
    yield

    half = pos >= C // 2
    odd = (pos // SB) % 2 == 1
    r1 = b_at(C // 2 - 1)
    r2 = jnp.where(half, b_at(3 * SB - 1), b_at(SB - 1))
    q1 = q * jnp.exp(jnp.where(half, b - r1, NEG))
    k1 = k * jnp.exp(jnp.where(half, NEG, r1 - b))
    q2 = q * jnp.exp(jnp.where(odd, b - r2, NEG))
    k2 = k * jnp.exp(jnp.where(odd, NEG, r2 - b))
    lane_h = lax.broadcasted_iota(jnp.int32, (C, kw), 1) // GLA_KEY_DIM

    def stack_heads(x):
        return jnp.concatenate([jnp.where(lane_h == h, x, 0.0) for h in range(H)], axis=0)

    att1 = _mm(stack_heads(q1), k1, N_GLA, tb=True)
    att2 = _mm(stack_heads(q2), k2, N_GLA, tb=True)
    yield
    pr =(lax.broadcasted_iota(jnp.int32, (H * C, C), 0) % C) * sgn + d * (C - 1)
    pc = lax.broadcasted_iota(jnp.int32, (H * C, C), 1) * sgn + d * (C - 1)
    att = att1 + jnp.where((pr >= C // 2) == (pc >= C // 2), att2, 0.0)
    res = _mm(att, v, N_GLA)
    lane_hv = lax.broadcasted_iota(jnp.int32, (C, vw), 1) // GLA_VAL_DIM
    off = None
    for h in range(H):
        t = jnp.where(lane_hv == h, res[h * C:(h + 1) * C], 0.0)
        off = t if off is None else off + t
    yield

    hm = hm_ref[...]
    il = lax.broadcasted_iota(jnp.int32, (SB, 1), 0)
    diag = []
    for blk in range(C // SB):
        rs = slice(blk * SB, (blk + 1) * SB)
        bb, qb, kb, vb = b[rs], q[rs], k[rs], v[rs]
        pieces = []
        for j in range(SB):
            valid = (il - j) * sgn >= 0
            pieces.append(jnp.exp(jnp.where(valid, bb - bb[j:j + 1], NEG)) * qb * kb[j:j + 1])
        a = _mm_exact_r(jnp.concatenate(pieces, axis=0), hm, N_GLA_INTRA)
        acc = None
        for j in range(SB):
            t = a[j * SB:(j + 1) * SB] * vb[j:j + 1]
            acc = t if acc is None else acc + t
        diag.append(acc)
        yield
    o_ref[0] = inter + off + jnp.concatenate(diag, axis=0)
    yield

    upd = _mm(v.T, k * jnp.exp(tot - b), N_GLA)
    ri = lax.broadcasted_iota(jnp.int32, upd.shape, 0) // GLA_VAL_DIM
    ci = lax.broadcasted_iota(jnp.int32, upd.shape, 1) // GLA_KEY_DIM
    s_new = s * jnp.exp(tot) + jnp.where(ri == ci, upd, 0.0)
    s_scr[...] = s_new
    st_ref[0, d] = s_new


def _seg_norm(y, bd, dim, eps):
    mu = _mm_exact_r(y, bd, N_SEG) * (1.0 / dim)
    dlt = y - mu
    var = _mm_exact_r(dlt * dlt, bd, N_SEG) * (1.0 / dim)
    return dlt * lax.rsqrt(var + eps)


def _layer_norm(x, w, b):
    mu = jnp.mean(x, axis=-1, keepdims=True)
    dlt = x - mu
    var = jnp.mean(dlt * dlt, axis=-1, keepdims=True)
    return dlt * lax.rsqrt(var + LN_EPS) * w + b


def _merge_body(x_ref, mod_ref, yrw0_ref, yrw1_ref, bonus_ref, g_ref, ygla0_ref, ygla1_ref, og_ref,
                bd64_ref, bd128_ref, rgw_ref, rgb_ref, ggw_ref, ggb_ref, *refs):
    n = N_MERGE
    wrw = [r[...] for r in refs[0:n]]
    wgla = [r[...] for r in refs[n:2 * n]]
    wout = [r[...] for r in refs[2 * n:3 * n]]
    wgate = [r[...] for r in refs[3 * n:3 * n + N_PROJ]]
    ln_w_ref, ln_b_ref, x1_ref, h2p_ref = refs[3 * n + N_PROJ:]
    y = _seg_norm(yrw0_ref[0] + yrw1_ref[0], bd64_ref[...], RW_HEAD_DIM, RW_GN_EPS)
    y = (y * rgw_ref[...] + rgb_ref[...] + bonus_ref[0]) * g_ref[0]
    y_rw = _mmp(_split(y, n), wrw)
    y = _seg_norm(ygla0_ref[0] + ygla1_ref[0], bd128_ref[...], GLA_VAL_DIM, LN_EPS)
    y = (y * ggw_ref[...] + ggb_ref[...]) * _silu(og_ref[0])
    y_gla = _mmp(_split(y, n), wgla)
    h = x_ref[0] * (1.0 + mod_ref[0, 1:2, :]) + mod_ref[0, 0:1, :]
    gate = _sigmoid(_mmp(_split(h, N_PROJ), wgate))
    mixed = gate[:, 0:D_MODEL] * y_rw + gate[:, D_MODEL:] * y_gla
    mix = _mmp(_split(mixed, n), wout)
    g1 = mod_ref[0, 2:3, :]
    x1 = _layer_norm(ALPHA * x_ref[0] + g1 * mix, ln_w_ref[...], ln_b_ref[...])
    x1_ref[0] = x1
    h2p_ref[0] = _pack_bf16_pairs(_moe_input(x1, mod_ref))


def _merge(x, mod, yrw, bonus, g, ygla, p_gla, consts, weights, ln_w, ln_b, tm):
    B, L, D = x.shape
    W = RW_WIDTH
    tok = lambda w: pl.BlockSpec((1, tm, w), lambda b, i: (b, i, 0))
    wl = [w for ws in weights for w in ws]
    return pl.pallas_call(
        _merge_body,
        grid=(B, L // tm),
        in_specs=[tok(D), pl.BlockSpec((1, 6, D), lambda b, i: (b, 0, 0)), tok(W), tok(W), tok(W), tok(W),
                  tok(W), tok(W),
                  pl.BlockSpec((1, tm, W), lambda b, i: (b, i, GLA_QKV_COLS // W))]
                 + [_full(c.shape) for c in consts] + [_full(w.shape) for w in wl]
                 + [_full(ln_w.shape), _full(ln_b.shape)],
        out_specs=[tok(D), tok(D // 2)],
        out_shape=[jax.ShapeDtypeStruct((B, L, D), F32), jax.ShapeDtypeStruct((B, L, D // 2), jnp.int32)],
        compiler_params=_cparams(("parallel", "arbitrary")),
        name="merge",
    )(x, mod, *yrw, bonus, g, *ygla, p_gla, *consts, *wl, ln_w, ln_b)


def _moe_input(x1, mod_ref):
    return x1 * (1.0 + mod_ref[0, 4:5, :]) + mod_ref[0, 3:4, :]


def _router_body(x1_ref, mod_ref, *refs):
    n = N_ROUTER
    rt = [r[...] for r in refs[0:n]]
    bias_ref, e_out, g_out, cnt_out = refs[n:]
    tm = x1_ref.shape[0]
    E, G, PG = N_EXPERTS, N_GROUPS, N_EXPERTS // N_GROUPS
    h = _moe_input(x1_ref[...], mod_ref)
    logits = _mmp(rt, _split(h, n), tb=True)
    scores = _sigmoid(logits)
    sel = scores + bias_ref[:, 0:tm]
    NEG = -jnp.inf
    ip = lax.broadcasted_iota(jnp.int32, (PG, tm), 0)
    group_rows = []
    for gidx in range(G):
        sg = sel[gidx * PG:(gidx + 1) * PG]
        m1 = jnp.max(sg, axis=0, keepdims=True)
        first = jnp.min(jnp.where(sg == m1, ip, PG), axis=0, keepdims=True)
        m2 = jnp.max(jnp.where(ip == first, NEG, sg), axis=0, keepdims=True)
        group_rows.append(m1 + m2)
    gs = jnp.concatenate(group_rows, axis=0)
    gi = lax.broadcasted_iota(jnp.int32, (G, tm), 0)
    keep = jnp.zeros((G, tm), F32)
    for _ in range(TOPK_GROUPS):
        m = jnp.max(gs, axis=0, keepdims=True)
        idx = jnp.min(jnp.where(gs == m, gi, G), axis=0, keepdims=True)
        hit = gi == idx
        keep = jnp.where(hit, 1.0, keep)
        gs = jnp.where(hit, NEG, gs)
    cur = jnp.concatenate(
        [jnp.where(keep[gidx:gidx + 1] > 0.5, sel[gidx * PG:(gidx + 1) * PG], NEG) for gidx in range(G)],
        axis=0)
    ei = lax.broadcasted_iota(jnp.int32, (E, tm), 0)
    idxs, gates = [], []
    picked = jnp.zeros((E, tm), F32)
    for _ in range(TOP_K):
        m = jnp.max(cur, axis=0, keepdims=True)
        idx = jnp.min(jnp.where(cur == m, ei, E), axis=0, keepdims=True)
        hit = ei == idx
        idxs.append(idx)
        gates.append(jnp.sum(jnp.where(hit, scores, 0.0), axis=0, keepdims=True))
        cur = jnp.where(hit, NEG, cur)
        picked = jnp.where(hit, 1.0, picked)
    gate = jnp.concatenate(gates, axis=0)
    e_out[...] = jnp.concatenate(idxs, axis=0)
    g_out[...] = gate / jnp.sum(gate, axis=0, keepdims=True) * ROUTED_SCALE

    @pl.when(pl.program_id(0) == 0)
    def _():
        cnt_out[...] = jnp.zeros(cnt_out.shape, F32)

    part = picked[:, 0:128]
    for j in range(1, tm // 128):
        part = part + picked[:, j * 128:(j + 1) * 128]
    cnt_out[...] += part


def _router(x1f, mod, seq, router_t_parts, bias_b, tm):
    n, D = x1f.shape
    E = N_EXPERTS
    return pl.pallas_call(
        _router_body,
        grid=(n // tm,),
        in_specs=[pl.BlockSpec((tm, D), lambda i: (i, 0)),
                  pl.BlockSpec((1, 6, D), lambda i: (i * tm // seq, 0, 0))]
                 + [pl.BlockSpec((E, D), lambda i: (0, 0))] * len(router_t_parts)
                 + [pl.BlockSpec(bias_b.shape, lambda i: (0, 0))],
        out_specs=[pl.BlockSpec((TOP_K, tm), lambda i: (0, i))] * 2
                  + [pl.BlockSpec((E, 128), lambda i: (0, 0))],
        out_shape=[jax.ShapeDtypeStruct((TOP_K, n), jnp.int32), jax.ShapeDtypeStruct((TOP_K, n), F32),
                   jax.ShapeDtypeStruct((E, 128), F32)],
        compiler_params=_cparams(("arbitrary",)),
        name="router",
    )(x1f, mod, *router_t_parts, bias_b)


def _sc_permute_body(scatter, per_w, src_hbm, idx_hbm, out_hbm, idx_v, rows_v, sem_i, sem_o):
    wid = lax.axis_index("s") * SC_CORES + lax.axis_index("c")
    base = wid * per_w
    win = SC_WINDOW
    pltpu.sync_copy(idx_hbm.at[wid], idx_v)

    def linear(ref, j):
        return ref.at[pl.ds(base + j * win, win)]

    def indexed(ref, j):
        return ref.at[idx_v.at[j]]

    src_at, out_at = (linear, indexed) if scatter else (indexed, linear)

    @pl.loop(0, per_w // win, step=2)
    def _(j):
        i0 = pltpu.async_copy(src_at(src_hbm, j), rows_v.at[0], sem_i.at[0])
        i1 = pltpu.async_copy(src_at(src_hbm, j + 1), rows_v.at[1], sem_i.at[1])
        i0.wait()
        o0 = pltpu.async_copy(rows_v.at[0], out_at(out_hbm, j), sem_o.at[0])
        i1.wait()
        o1 = pltpu.async_copy(rows_v.at[1], out_at(out_hbm, j + 1), sem_o.at[1])
        o0.wait()
        o1.wait()


def _sc_permute(src, idx, scatter):
    nrows = idx.shape[0]
    D = src.shape[1]
    per_w = nrows // SC_WORKERS
    mesh = plsc.VectorSubcoreMesh(core_axis_name="c", subcore_axis_name="s",
                                  num_cores=SC_CORES, num_subcores=SC_SUBCORES)
    return pl.kernel(
        functools.partial(_sc_permute_body, scatter, per_w),
        out_type=jax.ShapeDtypeStruct((nrows, D), src.dtype),
        mesh=mesh,
        scratch_types=[pltpu.VMEM((per_w // SC_WINDOW, SC_WINDOW), jnp.int32),
                       pltpu.VMEM((2, SC_WINDOW, D), src.dtype),
                       pltpu.SemaphoreType.DMA((2,)), pltpu.SemaphoreType.DMA((2,))],
        name="sc_scatter" if scatter else "sc_gather",
    )(src, idx.reshape(SC_WORKERS, per_w // SC_WINDOW, SC_WINDOW))


def _experts_body(ib_ref, ie_ref, lo_ref, hi_ref, x_ref, wgu_ref, wd_ref, y_ref, wgu_bf, wd_bf):
    i = pl.program_id(0)
    prev = jnp.maximum(i - 1, 0)
    R = ROW_BLOCK

    @pl.when(hi_ref[i] > lo_ref[i])
    def _():
        @pl.when((i == 0) | (ie_ref[i] != ie_ref[prev]))
        def _():
            wgu_bf[...] = wgu_ref[0].astype(BF16)
            wd_bf[...] = wd_ref[0].astype(BF16)

        F = EXPERT_DIM
        H = R // 2
        lo, hi, base = lo_ref[i], hi_ref[i], ib_ref[i] * R
        first = (i == 0) | (ib_ref[i] != ib_ref[prev])
        need = [(lo < base + (h + 1) * H) & (hi > base + h * H) for h in range(2)]

        def run(which):
            rows = [slice(h * H, (h + 1) * H) for h in which]
            xb = [_unpack_bf16_pairs(x_ref[rs, :]).astype(BF16) for rs in rows]
            gu = [_dot(t, wgu_bf[...]) for t in xb]
            act = [(_silu(t[:, 0:F]) * t[:, F:]).astype(BF16) for t in gu]
            yb = [_pack_bf16_pairs(_dot(t, wd_bf[...])) for t in act]
            for h, rs, t in zip(which, rows, yb):
                g = base + h * H + lax.broadcasted_iota(jnp.int32, (H, 1), 0)
                mine = (g >= lo) & (g < hi)
                keep = jnp.where(first, 0, y_ref[rs, :])
                y_ref[rs, :] = jnp.where(mine, t, keep)

        pl.when(need[0] & need[1])(lambda: run((0, 1)))
        pl.when(need[0] & jnp.logical_not(need[1]))(lambda: run((0,)))
        pl.when(jnp.logical_not(need[0]) & need[1])(lambda: run((1,)))


def _experts(item_blk, item_e, row_lo, row_hi, xs, w_gate_up, w_down):
    nrows, DP = xs.shape
    D = 2 * DP
    R = ROW_BLOCK
    F2 = w_gate_up.shape[2]
    grid_spec = pltpu.PrefetchScalarGridSpec(
        num_scalar_prefetch=4,
        grid=(item_blk.shape[0],),
        in_specs=[pl.BlockSpec((R, DP), lambda i, ib, ie, lo, hi: (ib[i], 0)),
                  pl.BlockSpec((1, D, F2), lambda i, ib, ie, lo, hi: (ie[i], 0, 0)),
                  pl.BlockSpec((1, F2 // 2, D), lambda i, ib, ie, lo, hi: (ie[i], 0, 0))],
        out_specs=pl.BlockSpec((R, DP), lambda i, ib, ie, lo, hi: (ib[i], 0)),
        scratch_shapes=[pltpu.VMEM((D, F2), BF16), pltpu.VMEM((F2 // 2, D), BF16)],
    )
    return pl.pallas_call(
        _experts_body,
        grid_spec=grid_spec,
        out_shape=jax.ShapeDtypeStruct((nrows, DP), jnp.int32),
        compiler_params=_cparams(("arbitrary",)),
        name="experts",
    )(item_blk, item_e, row_lo, row_hi, xs, w_gate_up, w_down)


def _final_body(yg_ref, x1_ref, mod_ref, gate_ref, *refs):
    n = N_SHARED
    sgu = [r[...] for r in refs[0:n]]
    sd = [r[...] for r in refs[n:2 * n]]
    ln_w_ref, ln_b_ref, o_ref = refs[2 * n:]
    tm = x1_ref.shape[1]
    gate = gate_ref[0]
    routed = None
    for kk in range(TOP_K):
        t = _unpack_bf16_pairs(yg_ref[kk * tm:(kk + 1) * tm, :]) * gate[:, kk:kk + 1]
        routed = t if routed is None else routed + t
    h2 = _moe_input(x1_ref[0], mod_ref)
    F = sgu[0].shape[1] // 2
    gu = _mmp(_split(h2, n), sgu)
    act = _silu(gu[:, 0:F]) * gu[:, F:]
    shared = _mmp(_split(act, n), sd)
    g2 = mod_ref[0, 5:6, :]
    o_ref[0] = _layer_norm(ALPHA * x1_ref[0] + g2 * (routed + shared), ln_w_ref[...], ln_b_ref[...])


def _final(yg, x1, mod, gate, sgu, sd, ln_w, ln_b, tm):
    B, L, D = x1.shape
    ni = L // tm
    rows = tm * TOP_K
    tok = lambda w: pl.BlockSpec((1, tm, w), lambda b, i: (b, i, 0))
    ws = list(sgu) + list(sd)
    return pl.pallas_call(
        _final_body,
        grid=(B, ni),
        in_specs=[pl.BlockSpec((rows, D // 2), lambda b, i: (b * ni + i, 0)),
                  tok(D), pl.BlockSpec((1, 6, D), lambda b, i: (b, 0, 0)), tok(TOP_K)]
                 + [_full(w.shape) for w in ws] + [_full(ln_w.shape), _full(ln_b.shape)],
        out_specs=tok(D),
        out_shape=jax.ShapeDtypeStruct((B, L, D), F32),
        compiler_params=_cparams(("parallel", "arbitrary")),
        name="final",
    )(yg, x1, mod, gate, *ws, ln_w, ln_b)


def _dispatch_plan(experts_t, sizes, tm):
    R, E = ROW_BLOCK, N_EXPERTS
    K, n = experts_t.shape
    nk = K * n
    assert nk % R == 0 and nk % SC_ROW_ALIGN == 0
    tok = jnp.broadcast_to(jnp.arange(n, dtype=jnp.int32)[None, :], (K, n))
    slot = jnp.arange(K, dtype=jnp.int32)[:, None]
    aid = (tok // tm) * (tm * K) + slot * tm + tok % tm
    id_bits = (nk - 1).bit_length()
    assert E << id_bits < 2 ** 31
    row_id = lax.sort((experts_t << id_bits | aid).reshape(-1)) & ((1 << id_bits) - 1)
    row_tok = (row_id // (tm * K)) * tm + row_id % tm

    start = jnp.cumsum(sizes) - sizes
    first_blk = start // R
    tiles = jnp.where(sizes > 0, (start + sizes - 1) // R - first_blk + 1, 0)
    t_end = jnp.cumsum(tiles)
    nitems = nk // R + E - 1
    it = jnp.arange(nitems, dtype=jnp.int32)
    live = it < t_end[-1]
    itc = jnp.minimum(it, t_end[-1] - 1)
    item_e = jnp.minimum(jnp.searchsorted(t_end, itc, side='right'), E - 1).astype(jnp.int32)
    item_blk = (first_blk[item_e] + itc - (t_end - tiles)[item_e]).astype(jnp.int32)
    row_lo = jnp.where(live, start[item_e], 0).astype(jnp.int32)
    row_hi = jnp.where(live, (start + sizes)[item_e], 0).astype(jnp.int32)
    return row_tok, row_id, item_blk, item_e, row_lo, row_hi


def _block_diag_ones(n, blk):
    i = jnp.arange(n) // blk
    return (i[:, None] == i[None, :]).astype(BF16)


def kernel(x, c, ctx, c_ctx, w_ada, b_ada, w_in, rw_mu, rw_w0, rw_w2, rw_a0, rw_a2, rw_g2, rw_k_k, rw_k_a,
           rw_r_k, rw_gn_w, rw_gn_b, gla_conv, gla_g2, gla_gb, gla_gn_w, gla_gn_b, w_br_rw, w_br_gla, w_out,
           ln1_w, ln1_b, router, router_bias, w_gate_up, w_down, sh_gate_up, sh_down, ln2_w, ln2_b):
    B, L, D = x.shape
    CT = ctx.shape[1]
    l = 0
    W = RW_WIDTH
    row = lambda t: t.reshape(1, -1)

    rows = -(-(B + 1) // 8) * 8
    cc = jnp.zeros((rows, D), F32).at[:B].set(c).at[B].set(c_ctx)
    mod = _ada(cc, w_ada[l], b_ada[l])
    mod_lat = mod[:B].reshape(B, 6, D)
    mod_ctx = jnp.broadcast_to(mod[B].reshape(1, 6, D), (B, 6, D))

    w = w_in[l]
    g0 = RW_COLS
    w_rw = jnp.pad(w[:, :RW_COLS], ((0, 0), (0, RW_SEG - RW_COLS)))
    w_gla = jnp.concatenate([w[:, g0:g0 + GLA_QKV_COLS],
                             w[:, g0 + GLA_QKV_COLS + GLA_GATE_RANK:g0 + GLA_COLS],
                             w[:, g0 + GLA_QKV_COLS:g0 + GLA_QKV_COLS + GLA_GATE_RANK],
                             jnp.zeros((D, GLA_SEG - GLA_COLS), F32)], axis=1)
    w_gate = w[:, MIX_COLS:]
    tm_p = min(PROJ_TILE, L)
    p_rw = _proj(x, mod_lat, w_rw, tm_p)
    p_gla = _proj(x, mod_lat, w_gla, tm_p)
    pc_rw = _proj(ctx, mod_ctx, w_rw, CT)
    pc_gla = _proj(ctx, mod_ctx, w_gla, CT)

    mu = jnp.pad(rw_mu[l], (0, RW_SEG - RW_COLS)).reshape(1, -1)
    lora = jnp.zeros((256, 5 * W), F32)
    e1 = RW_DECAY_RANK
    e2 = e1 + RW_AAA_RANK
    e3 = e2 + RW_GATE_RANK
    for d in range(2):
        lora = lora.at[0:e1, d * W:(d + 1) * W].set(rw_w2[l, d])
        lora = lora.at[e1:e2, (2 + d) * W:(3 + d) * W].set(rw_a2[l, d])
    lora = lora.at[e2:e3, 4 * W:].set(rw_g2[l])
    lora_p = (_split(lora, N_LORA) + [jnp.zeros_like(lora, BF16)])[:2]
    bd64 = _block_diag_ones(W, RW_HEAD_DIM)
    rw_consts = [mu, lora_p[0], lora_p[1], rw_w0[l].reshape(1, -1), rw_a0[l].reshape(1, -1),
                 row(rw_k_k[l]), row(rw_k_a[l]), row(rw_r_k[l]), bd64]
    tm_f = min(FEAT_TILE, L)
    r, k, v, kkn, g, bonus, lwd, a = _feat_rw(p_rw, True, tm_f, rw_consts)
    rc, kc, vc, kknc, _, _, lwdc, ac = _feat_rw(pc_rw, False, CT, rw_consts)

    g2 = jnp.zeros((GLA_SEG - GLA_QKV_COLS - GLA_VAL_WIDTH, 2 * GLA_KEY_WIDTH), F32)
    g2 = g2.at[:GLA_GATE_RANK].set(jnp.concatenate([gla_g2[l, 0], gla_g2[l, 1]], axis=1))
    g2_p = (_split(g2, N_LORA) + [jnp.zeros_like(g2, BF16)])[:2]
    cw = jnp.pad(gla_conv[l].reshape(9, GLA_QKV_COLS), ((0, 7), (0, 0)))
    gla_consts = [cw, g2_p[0], g2_p[1], gla_gb[l].reshape(1, -1)]
    qk, vg, la = _feat_gla(p_gla, True, tm_f, gla_consts)
    qkc, vgc, lac = _feat_gla(pc_gla, False, CT, gla_consts)

    ka = row(rw_k_a[l])
    P = 2 * RW_HEAD_DIM
    s0 = jnp.zeros((B, 2, RW_HEADS // 2, P, P), F32)
    hi = jnp.arange(GLA_KEY_WIDTH) // GLA_KEY_DIM
    hj = jnp.arange(GLA_VAL_WIDTH) // GLA_VAL_DIM
    hm = (hi[:, None] == hj[None, :]).astype(BF16)
    g0s = jnp.zeros((B, 2, GLA_VAL_WIDTH, GLA_KEY_WIDTH), F32)
    _, _, s_ctx, _, _, gs_ctx = _scan(rc, kc, vc, kknc, lwdc, ac, ka, s0, qkc, vgc, lac, hm, g0s)
    yr0, yr1, _, yg0, yg1, _ = _scan(r, k, v, kkn, lwd, a, ka, s_ctx, qk, vg, la, hm, gs_ctx)
    y_rw, y_gla = (yr0, yr1), (yg0, yg1)

    bd128 = _block_diag_ones(GLA_VAL_WIDTH, GLA_VAL_DIM)
    m_consts = [bd64, bd128, row(rw_gn_w[l]), row(rw_gn_b[l]), row(gla_gn_w[l]), row(gla_gn_b[l])]
    m_weights = [_split(w_br_rw[l], N_MERGE), _split(w_br_gla[l], N_MERGE), _split(w_out[l], N_MERGE),
                 _split(w_gate, N_PROJ)]
    x1, h2p = _merge(x, mod_lat, y_rw, bonus, g, y_gla, p_gla, m_consts, m_weights,
                     row(ln1_w[l]), row(ln1_b[l]), min(MERGE_TILE, L))

    n = B * L
    tm_r = min(ROUTER_TILE, L)
    bias_b = jnp.broadcast_to(router_bias[l].reshape(-1, 1), (N_EXPERTS, tm_r))
    e_t, g_t, cnt = _router(x1.reshape(n, D), mod_lat, L, _split(router[l].T, N_ROUTER), bias_b, tm_r)
    sizes = jnp.sum(cnt, axis=1).astype(jnp.int32)
    tm_c = min(COMBINE_TILE, L)
    row_tok, row_id, item_blk, item_e, row_lo, row_hi = _dispatch_plan(e_t, sizes, tm_c)
    xs = _sc_permute(h2p.reshape(n, D // 2), row_tok, scatter=False)
    y = _experts(item_blk, item_e, row_lo, row_hi, xs, w_gate_up[l], w_down[l])
    yg = _sc_permute(y, row_id, scatter=True)
    gate_tok = g_t.T.reshape(B, L, TOP_K)
    return _final(yg, x1, mod_lat, gate_tok, _split(sh_gate_up[l], N_SHARED),
                  _split(sh_down[l], N_SHARED), row(ln2_w[l]), row(ln2_b[l]), tm_c)
```
